```python
import math
import jax, jax.numpy as jnp
from jax import lax
import numpy as np

D_MODEL = 1024
BATCH = 16
SEQ = 2048
DEPTH = 1

N_META = 16
D_MIX = D_MODEL
D_HYENA = D_MIX // 2
D_HGRN = D_MIX - D_HYENA
HYENA_ORDER = 2
SHORT_CONV = 3
FILTER_EMB = 33
FILTER_BANDS = (FILTER_EMB - 1) // 2
FILTER_HIDDEN = 64
DECAY_TARGET = 1e-2
FAST_DECAY_PCT = 0.3
SLOW_DECAY_PCT = 1.5
HGRN_HEAD_DIM = 128
HGRN_HEADS = D_HGRN // HGRN_HEAD_DIM
CHUNK = 64
N_GROUPS = 8
EXPERTS_PER_GROUP = 8
N_EXPERTS = N_GROUPS * EXPERTS_PER_GROUP
TOP_K = 2
D_EXPERT = D_MODEL // 2
MOE_BLOCK = 128
D_HYENA_PROJ = (HYENA_ORDER + 1) * D_HYENA
D_HGRN_PROJ = 5 * D_HGRN
D_IN_PROJ = D_HYENA_PROJ + D_HGRN_PROJ
EPS = 1e-6

kernel_name = 'hymba_hyena_hgrn2_hmoe_encoder'


def rms_norm(x, gain):
    xf = x.astype(jnp.float32)
    y = xf * lax.rsqrt(jnp.mean(xf * xf, axis=-1, keepdims=True) + EPS)
    return (y * gain.astype(jnp.float32)).astype(x.dtype)


def centred_short_conv(u, w, b):
    L = u.shape[1]
    half = SHORT_CONV // 2
    up = jnp.pad(u, ((0, 0), (half, SHORT_CONV - 1 - half), (0, 0)))
    y = b
    for j in range(SHORT_CONV):
        y = y + up[:, j:j + L] * w[j]
    return y


def hyena_filter_spectra(L, w1, b1, w2, b2, w3, freq):
    f32 = jnp.float32
    pos = jnp.arange(L, dtype=f32)
    t = pos / max(L - 1, 1)
    bands = jnp.linspace(1e-4, FILTER_BANDS - 1, FILTER_BANDS, dtype=f32)
    ang = (2.0 * math.pi / L) * pos[:, None] * bands[None, :]
    z = jnp.concatenate([t[:, None], jnp.cos(ang), -jnp.sin(ang)], axis=-1)
    fr = freq.astype(f32)
    hid = jnp.sin(fr * (z @ w1.astype(f32) + b1.astype(f32)))
    hid = jnp.sin(fr * (hid @ w2.astype(f32) + b2.astype(f32)))
    filt = (hid @ w3.astype(f32)).reshape(L, 2, HYENA_ORDER, D_HYENA)
    deltas = jnp.abs(jnp.linspace(math.log(DECAY_TARGET) / SLOW_DECAY_PCT,
                                  math.log(DECAY_TARGET) / FAST_DECAY_PCT, D_HYENA, dtype=f32))
    window = jnp.exp(-t[:, None] * deltas[None, :])
    filt = filt * window[:, None, None, :]
    h_fwd, h_bwd = filt[:, 0], filt[:, 1]
    two_sided = jnp.concatenate([h_fwd, jnp.zeros_like(h_fwd[:1]), h_bwd[:0:-1]], axis=0)
    return jnp.fft.rfft(two_sided, axis=0)


def fft_long_conv(u, spec, skip):
    L = u.shape[1]
    uf = u.astype(jnp.float32)
    y = jnp.fft.irfft(jnp.fft.rfft(uf, n=2 * L, axis=1) * spec[None], n=2 * L, axis=1)[:, :L]
    return (y + uf * skip.astype(jnp.float32)).astype(u.dtype)


def hyena_mixer(p, conv_w, conv_b, w1, b1, w2, b2, w3, freq, skip):
    L = p.shape[1]
    u = centred_short_conv(p, conv_w, conv_b)
    z = u[..., :D_HYENA]
    spectra = hyena_filter_spectra(L, w1, b1, w2, b2, w3, freq)
    for n in range(HYENA_ORDER):
        gate = u[..., (n + 1) * D_HYENA:(n + 2) * D_HYENA]
        z = gate * fft_long_conv(z, spectra[:, n], skip[n])
    return z


def chunked_gated_linear_scan(q, k, v, log_f):
    B, T, H, DK = q.shape
    DV = v.shape[-1]
    N = T // CHUNK
    q, k, v, log_f = [a.reshape(B, N, CHUNK, H, a.shape[-1]) for a in (q, k, v, log_f)]
    b = jnp.cumsum(log_f, axis=2)
    b_last = b[:, :, -1]
    b_mid = b[:, :, CHUNK // 2][:, :, None]
    scores = jnp.einsum('bnthd,bnshd->bnhts', q * jnp.exp(b - b_mid), k * jnp.exp(b_mid - b))
    lower_tri = jnp.tril(jnp.ones((CHUNK, CHUNK), dtype=bool))
    scores = jnp.where(lower_tri, scores, 0.0)
    o_intra = jnp.einsum('bnhts,bnshv->bnthv', scores, v)
    chunk_state = jnp.einsum('bnshd,bnshv->bnhdv', k * jnp.exp(b_last[:, :, None] - b), v)
    chunk_decay = jnp.exp(b_last)

    def step(S, inp):
        dec, U = inp
        return dec[..., None] * S + U, S

    S0 = jnp.zeros((B, H, DK, DV), q.dtype)
    _, S_in = lax.scan(step, S0, (jnp.moveaxis(chunk_decay, 1, 0), jnp.moveaxis(chunk_state, 1, 0)))
    S_in = jnp.moveaxis(S_in, 0, 1)
    o_inter = jnp.einsum('bnthd,bnhdv->bnthv', q * jnp.exp(b), S_in)
    return (o_intra + o_inter).reshape(B, T, H, DV)


def hgrn2_mixer(p, lb_f, lb_b, norm_w):
    f32 = jnp.float32
    B, L, _ = p.shape
    q, f_fwd, f_bwd, i_in, g = [p[..., j * D_HGRN:(j + 1) * D_HGRN] for j in range(5)]

    def heads(a):
        return a.reshape(B, L, HGRN_HEADS, HGRN_HEAD_DIM)

    qh = heads(jax.nn.silu(q.astype(f32)))
    vh = heads(i_in.astype(f32))

    def forget(logit, lb):
        lb = lb.astype(f32)
        f = lb + (1.0 - lb) * jax.nn.sigmoid(logit.astype(f32))
        return heads(1.0 - f), heads(jnp.log(f))

    k_f, lf_f = forget(f_fwd, lb_f)
    k_b, lf_b = forget(f_bwd, lb_b)
    n_pad = (-N_META) % CHUNK

    def pad(a):
        return jnp.pad(a, ((0, 0), (n_pad, 0), (0, 0), (0, 0)))

    def flip(a):
        return a[:, ::-1]

    qp, vp = pad(qh), pad(vh)
    o_f = chunked_gated_linear_scan(qp, pad(k_f), vp, pad(lf_f))
    o_b = flip(chunked_gated_linear_scan(flip(qp), flip(pad(k_b)), flip(vp), flip(pad(lf_b))))
    o = (o_f + o_b)[:, n_pad:]
    o = o * lax.rsqrt(jnp.mean(o * o, axis=-1, keepdims=True) + EPS)
    o = o.reshape(B, L, D_HGRN) * norm_w.astype(f32) * jax.nn.silu(g.astype(f32))
    return o.astype(p.dtype)


def hier_moe(h, w_rg, w_re, w_gate, w_up, w_down):
    f32 = jnp.float32
    B, L, D = h.shape
    n_tok = B * L
    hf = h.reshape(n_tok, D)
    g_logits = (hf @ w_rg).astype(f32)
    g_sel = jnp.argmax(g_logits, axis=-1)
    p_group = jnp.take_along_axis(jax.nn.softmax(g_logits, axis=-1), g_sel[:, None], axis=-1)
    e_logits = jnp.einsum('nd,dge->nge', hf, w_re).astype(f32)
    e_logits = e_logits[jnp.arange(n_tok), g_sel]
    p_top, e_top = lax.top_k(jax.nn.softmax(e_logits, axis=-1), TOP_K)
    gate = p_group * p_top / jnp.sum(p_top, axis=-1, keepdims=True)
    expert_id = (g_sel[:, None] * EXPERTS_PER_GROUP + e_top).reshape(-1).astype(jnp.int32)

    M = n_tok * TOP_K
    n_blocks = -(-M // MOE_BLOCK) + N_EXPERTS
    n_slots = n_blocks * MOE_BLOCK
    order = jnp.argsort(expert_id)
    eid_s = expert_id[order]
    tok_s = (jnp.arange(M, dtype=jnp.int32) // TOP_K)[order]
    gate_s = gate.reshape(-1)[order]
    counts = jnp.bincount(expert_id, length=N_EXPERTS)
    padded = (counts + MOE_BLOCK - 1) // MOE_BLOCK * MOE_BLOCK
    start = jnp.cumsum(counts) - counts
    pend = jnp.cumsum(padded)
    dest = (pend - padded)[eid_s] + jnp.arange(M, dtype=jnp.int32) - start[eid_s]
    tok_buf = jnp.zeros((n_slots,), jnp.int32).at[dest].set(tok_s)
    gate_buf = jnp.zeros((n_slots,), f32).at[dest].set(gate_s)
    block_eid = jnp.minimum(
        jnp.searchsorted(pend, jnp.arange(n_blocks, dtype=pend.dtype) * MOE_BLOCK, side='right'),
        N_EXPERTS - 1)
    xb = hf[tok_buf].reshape(n_blocks, MOE_BLOCK, D)

    def expert_block(args):
        xe, e = args
        return (jax.nn.silu(xe @ w_gate[e]) * (xe @ w_up[e])) @ w_down[e]

    yb = lax.map(expert_block, (xb, block_eid)).reshape(n_slots, D)
    y = jnp.zeros_like(hf).at[tok_buf].add(yb * gate_buf[:, None].astype(yb.dtype))
    return y.reshape(B, L, D)


def setup_inputs(seed: int = 0) -> dict:
    key = jax.random.key(seed)
    ks = jax.random.split(key, 25)

    def nrm(k, shape, scale):
        return scale * jax.random.normal(k, shape, jnp.float32)

    return {
        'x': nrm(ks[0], (BATCH, SEQ, D_MODEL), 1.0),
        'meta_tokens': nrm(ks[1], (N_META, D_MODEL), 1.0),
        'w_in': nrm(ks[2], (DEPTH, D_MODEL, D_IN_PROJ), D_MODEL ** -0.5),
        'conv_w': nrm(ks[3], (DEPTH, SHORT_CONV, D_HYENA_PROJ), SHORT_CONV ** -0.5),
        'conv_b': nrm(ks[4], (DEPTH, D_HYENA_PROJ), 0.01),
        'filt_w1': nrm(ks[5], (DEPTH, FILTER_EMB, FILTER_HIDDEN), FILTER_EMB ** -0.5),
        'filt_b1': nrm(ks[6], (DEPTH, FILTER_HIDDEN), 0.1),
        'filt_w2': nrm(ks[7], (DEPTH, FILTER_HIDDEN, FILTER_HIDDEN), FILTER_HIDDEN ** -0.5),
        'filt_b2': nrm(ks[8], (DEPTH, FILTER_HIDDEN), 0.1),
        'filt_w3': nrm(ks[9], (DEPTH, FILTER_HIDDEN, 2 * HYENA_ORDER * D_HYENA), 0.02),
        'filt_freq': 1.0 + nrm(ks[10], (DEPTH, FILTER_HIDDEN), 0.01),
        'filt_skip': nrm(ks[11], (DEPTH, HYENA_ORDER, D_HYENA), 1.0),
        'hyena_norm': 1.0 + nrm(ks[12], (DEPTH, D_HYENA), 0.01),
        'lb_fwd': nrm(ks[13], (DEPTH + 1, D_HGRN), 0.1),
        'lb_bwd': nrm(ks[14], (DEPTH + 1, D_HGRN), 0.1),
        'hgrn_norm': 1.0 + nrm(ks[15], (DEPTH, D_HGRN), 0.01),
        'w_out': nrm(ks[16], (DEPTH, D_MIX, D_MODEL), D_MIX ** -0.5),
        'norm_mix': 1.0 + nrm(ks[17], (DEPTH, D_MODEL), 0.01),
        'norm_ffn': 1.0 + nrm(ks[18], (DEPTH, D_MODEL), 0.01),
        'w_router_group': nrm(ks[19], (DEPTH, D_MODEL, N_GROUPS), D_MODEL ** -0.5),
        'w_router_expert': nrm(ks[20], (DEPTH, D_MODEL, N_GROUPS, EXPERTS_PER_GROUP), D_MODEL ** -0.5),
        'w_gate': nrm(ks[21], (DEPTH, N_EXPERTS, D_MODEL, D_EXPERT), D_MODEL ** -0.5),
        'w_up': nrm(ks[22], (DEPTH, N_EXPERTS, D_MODEL, D_EXPERT), D_MODEL ** -0.5),
        'w_down': nrm(ks[23], (DEPTH, N_EXPERTS, D_EXPERT, D_MODEL), D_EXPERT ** -0.5),
        'norm_final': 1.0 + nrm(ks[24], (D_MODEL,), 0.01),
    }


def reference(x, meta_tokens, w_in, conv_w, conv_b, filt_w1, filt_b1, filt_w2, filt_b2, filt_w3,
              filt_freq, filt_skip, hyena_norm, lb_fwd, lb_bwd, hgrn_norm, w_out, norm_mix, norm_ffn,
              w_router_group, w_router_expert, w_gate, w_up, w_down, norm_final):
    B = x.shape[0]
    meta = jnp.broadcast_to(meta_tokens[None].astype(x.dtype), (B, N_META, x.shape[-1]))
    h = jnp.concatenate([meta, x], axis=1)
    lbf_all = jnp.cumsum(jax.nn.softmax(lb_fwd.astype(jnp.float32), axis=0), axis=0)
    lbb_all = jnp.cumsum(jax.nn.softmax(lb_bwd.astype(jnp.float32), axis=0), axis=0)
    for l in range(DEPTH):
        a = rms_norm(h, norm_mix[l])
        p = a @ w_in[l]
        y_hy = hyena_mixer(p[..., :D_HYENA_PROJ], conv_w[l], conv_b[l], filt_w1[l], filt_b1[l],
                           filt_w2[l], filt_b2[l], filt_w3[l], filt_freq[l], filt_skip[l])
        y_hy = rms_norm(y_hy, hyena_norm[l])
        y_hg = hgrn2_mixer(p[..., D_HYENA_PROJ:], lbf_all[l], lbb_all[l], hgrn_norm[l])
        h = h + jnp.concatenate([y_hy, y_hg], axis=-1) @ w_out[l]
        h = h + hier_moe(rms_norm(h, norm_ffn[l]), w_router_group[l], w_router_expert[l],
                         w_gate[l], w_up[l], w_down[l])
    h = rms_norm(h, norm_final)
    return h[:, N_META:]
```

```python
import functools
import math

import jax
import jax.numpy as jnp
from jax import lax
from jax.experimental import pallas as pl
from jax.experimental.pallas import tpu as pltpu

D_MODEL = 1024
N_META = 16
D_HYENA = 512
D_HGRN = 512
HYENA_ORDER = 2
SHORT_CONV = 3
FILTER_EMB = 33
FILTER_BANDS = 16
DECAY_TARGET = 1e-2
FAST_DECAY_PCT = 0.3
SLOW_DECAY_PCT = 1.5
HGRN_HEAD_DIM = 128
HGRN_HEADS = D_HGRN // HGRN_HEAD_DIM
CHUNK = 64
N_GROUPS = 8
EXPERTS_PER_GROUP = 8
N_EXPERTS = 64
TOP_K = 2
D_EXPERT = 512
MOE_BLOCK = 128
D_HYENA_PROJ = 3 * D_HYENA
D_IN_PROJ = D_HYENA_PROJ + 5 * D_HGRN
EPS = 1e-6

F32 = jnp.float32
BF16 = jnp.bfloat16


def _rms(x, gain):
    return x * lax.rsqrt(jnp.mean(x * x, axis=-1, keepdims=True) + EPS) * gain


def _inproj_body(x_ref, g_ref, w_ref, o_ref, *, tn):
    a = _rms(x_ref[...], g_ref[...]).astype(BF16)
    n = o_ref.shape[-1]
    for j in range(n // tn):
        o_ref[:, j * tn:(j + 1) * tn] = jnp.dot(
            a, w_ref[:, j * tn:(j + 1) * tn], preferred_element_type=F32).astype(o_ref.dtype)


def _inproj(x, gain, w_bf16, tm, out_dtype=F32):
    m, d = x.shape
    n = w_bf16.shape[1]
    return pl.pallas_call(
        functools.partial(_inproj_body, tn=512),
        grid=(m // tm,),
        in_specs=[pl.BlockSpec((tm, d), lambda i: (i, 0)),
                  pl.BlockSpec((1, d), lambda i: (0, 0)),
                  pl.BlockSpec((d, n), lambda i: (0, 0))],
        out_specs=pl.BlockSpec((tm, n), lambda i: (i, 0)),
        out_shape=jax.ShapeDtypeStruct((m, n), out_dtype),
        compiler_params=pltpu.CompilerParams(
            dimension_semantics=("arbitrary",), vmem_limit_bytes=56 * 1024 * 1024),
        name="inproj",
    )(x, gain.reshape(1, d), w_bf16)


def _outproj_body(zhy_ref, yhg_ref, h0_ref, ghy_ref, gffn_ref, wo_ref, wr_ref,
                  h1_ref, a2_ref, lg_ref):
    yhy = _rms(zhy_ref[...], ghy_ref[...]).astype(BF16)
    yhg = yhg_ref[...].astype(BF16)
    acc = jnp.dot(yhy, wo_ref[:D_HYENA, :], preferred_element_type=F32)
    acc = acc + jnp.dot(yhg, wo_ref[D_HYENA:, :], preferred_element_type=F32)
    h1 = h0_ref[...] + acc
    h1_ref[...] = h1
    a2 = _rms(h1, gffn_ref[...])
    a2b = a2.astype(BF16)
    a2_ref[...] = a2b
    lg_ref[...] = jnp.dot(a2b, wr_ref[...], preferred_element_type=F32)


def _outproj(zhy, yhg, h0, g_hy, g_ffn, wo_bf16, wr_bf16, tm):
    m = h0.shape[0]
    nr = wr_bf16.shape[1]
    return pl.pallas_call(
        _outproj_body,
        grid=(m // tm,),
        in_specs=[pl.BlockSpec((tm, D_HYENA), lambda i: (i, 0)),
                  pl.BlockSpec((tm, D_HGRN), lambda i: (i, 0)),
                  pl.BlockSpec((tm, D_MODEL), lambda i: (i, 0)),
                  pl.BlockSpec((1, D_HYENA), lambda i: (0, 0)),
                  pl.BlockSpec((1, D_MODEL), lambda i: (0, 0)),
                  pl.BlockSpec((D_MODEL, D_MODEL), lambda i: (0, 0)),
                  pl.BlockSpec((D_MODEL, nr), lambda i: (0, 0))],
        out_specs=[pl.BlockSpec((tm, D_MODEL), lambda i: (i, 0)),
                   pl.BlockSpec((tm, D_MODEL), lambda i: (i, 0)),
                   pl.BlockSpec((tm, nr), lambda i: (i, 0))],
        out_shape=[jax.ShapeDtypeStruct((m, D_MODEL), F32),
                   jax.ShapeDtypeStruct((m, D_MODEL), BF16),
                   jax.ShapeDtypeStruct((m, nr), F32)],
        compiler_params=pltpu.CompilerParams(
            dimension_semantics=("arbitrary",), vmem_limit_bytes=48 * 1024 * 1024),
        name="outproj",
    )(zhy, yhg, h0, g_hy.reshape(1, -1), g_ffn.reshape(1, -1), wo_bf16, wr_bf16)


def _expert_body(eid_ref, xb_ref, gate_ref, wg_ref, wu_ref, wd_ref, o_ref):
    xb = xb_ref[...]
    g = jnp.dot(xb, wg_ref[0].astype(BF16), preferred_element_type=F32)
    u = jnp.dot(xb, wu_ref[0].astype(BF16), preferred_element_type=F32)
    hmid = (g * jax.nn.sigmoid(g) * u).astype(BF16)
    y = jnp.dot(hmid, wd_ref[0].astype(BF16), preferred_element_type=F32)
    o_ref[...] = y * gate_ref[...]


def _experts(block_eid, xb, gate_buf, w_gate, w_up, w_down, tb):
    n_slots = xb.shape[0]
    grid_spec = pltpu.PrefetchScalarGridSpec(
        num_scalar_prefetch=1,
        grid=(n_slots // tb,),
        in_specs=[pl.BlockSpec((tb, D_MODEL), lambda i, e: (i, 0)),
                  pl.BlockSpec((tb, 1), lambda i, e: (i, 0)),
                  pl.BlockSpec((1, D_MODEL, D_EXPERT), lambda i, e: (e[i], 0, 0)),
                  pl.BlockSpec((1, D_MODEL, D_EXPERT), lambda i, e: (e[i], 0, 0)),
                  pl.BlockSpec((1, D_EXPERT, D_MODEL), lambda i, e: (e[i], 0, 0))],
        out_specs=pl.BlockSpec((tb, D_MODEL), lambda i, e: (i, 0)),
    )
    return pl.pallas_call(
        _expert_body,
        grid_spec=grid_spec,
        out_shape=jax.ShapeDtypeStruct((n_slots, D_MODEL), F32),
        compiler_params=pltpu.CompilerParams(
            dimension_semantics=("arbitrary",), vmem_limit_bytes=48 * 1024 * 1024),
        name="experts",
    )(block_eid, xb, gate_buf.reshape(n_slots, 1), w_gate, w_up, w_down)


def _centred_short_conv(u, w, b):
    L = u.shape[1]
    half = SHORT_CONV // 2
    up = jnp.pad(u, ((0, 0), (half, SHORT_CONV - 1 - half), (0, 0)))
    y = b
    for j in range(SHORT_CONV):
        y = y + up[:, j:j + L] * w[j]
    return y


def _hyena_filters(L, w1, b1, w2, b2, w3, freq):
    pos = jnp.arange(L, dtype=F32)
    t = pos / max(L - 1, 1)
    bands = jnp.linspace(1e-4, FILTER_BANDS - 1, FILTER_BANDS, dtype=F32)
    ang = (2.0 * math.pi / L) * pos[:, None] * bands[None, :]
    z = jnp.concatenate([t[:, None], jnp.cos(ang), -jnp.sin(ang)], axis=-1)
    hp = lax.Precision.HIGHEST
    hid = jnp.sin(freq * (jnp.dot(z, w1, precision=hp) + b1))
    hid = jnp.sin(freq * (jnp.dot(hid, w2, precision=hp) + b2))
    filt = jnp.dot(hid, w3, precision=hp).reshape(L, 2, HYENA_ORDER, D_HYENA)
    deltas = jnp.abs(jnp.linspace(math.log(DECAY_TARGET) / SLOW_DECAY_PCT,
                                  math.log(DECAY_TARGET) / FAST_DECAY_PCT, D_HYENA, dtype=F32))
    window = jnp.exp(-t[:, None] * deltas[None, :])
    filt = filt * window[:, None, None, :]
    return filt[:, 0], filt[:, 1]


def _hyena_jnp(p, conv_w, conv_b, w1, b1, w2, b2, w3, freq, skip):
    L = p.shape[1]
    u = _centred_short_conv(p, conv_w, conv_b)
    z = u[..., :D_HYENA]
    h_fwd, h_bwd = _hyena_filters(L, w1, b1, w2, b2, w3, freq)
    two_sided = jnp.concatenate([h_fwd, jnp.zeros_like(h_fwd[:1]), h_bwd[:0:-1]], axis=0)
    spectra = jnp.fft.rfft(two_sided, axis=0)
    for n in range(HYENA_ORDER):
        gate = u[..., (n + 1) * D_HYENA:(n + 2) * D_HYENA]
        y = jnp.fft.irfft(jnp.fft.rfft(z, n=2 * L, axis=1) * spectra[None, :, n], n=2 * L, axis=1)[:, :L]
        z = gate * (y + z * skip[n])
    return z


def _scan_jnp(q, k, v, log_f):
    B, T, H, DK = q.shape
    DV = v.shape[-1]
    N = T // CHUNK
    q, k, v, log_f = [a.reshape(B, N, CHUNK, H, a.shape[-1]) for a in (q, k, v, log_f)]
    b = jnp.cumsum(log_f, axis=2)
    b_last = b[:, :, -1]
    b_mid = b[:, :, CHUNK // 2][:, :, None]
    scores = jnp.einsum('bnthd,bnshd->bnhts', q * jnp.exp(b - b_mid), k * jnp.exp(b_mid - b))
    lower_tri = jnp.tril(jnp.ones((CHUNK, CHUNK), dtype=bool))
    scores = jnp.where(lower_tri, scores, 0.0)
    o_intra = jnp.einsum('bnhts,bnshv->bnthv', scores, v)
    chunk_state = jnp.einsum('bnshd,bnshv->bnhdv', k * jnp.exp(b_last[:, :, None] - b), v)
    chunk_decay = jnp.exp(b_last)

    def step(S, inp):
        dec, U = inp
        return dec[..., None] * S + U, S

    S0 = jnp.zeros((B, H, DK, DV), q.dtype)
    _, S_in = lax.scan(step, S0, (jnp.moveaxis(chunk_decay, 1, 0), jnp.moveaxis(chunk_state, 1, 0)))
    S_in = jnp.moveaxis(S_in, 0, 1)
    o_inter = jnp.einsum('bnthd,bnhdv->bnthv', q * jnp.exp(b), S_in)
    return (o_intra + o_inter).reshape(B, T, H, DV)


def _hgrn_jnp(p, lb_f, lb_b, norm_w):
    B, L, _ = p.shape
    q, f_fwd, f_bwd, i_in, g = [p[..., j * D_HGRN:(j + 1) * D_HGRN] for j in range(5)]

    def heads(a):
        return a.reshape(B, L, HGRN_HEADS, HGRN_HEAD_DIM)

    qh = heads(jax.nn.silu(q))
    vh = heads(i_in)

    def forget(logit, lb):
        f = lb + (1.0 - lb) * jax.nn.sigmoid(logit)
        return heads(1.0 - f), heads(jnp.log(f))

    k_f, lf_f = forget(f_fwd, lb_f)
    k_b, lf_b = forget(f_bwd, lb_b)
    n_pad = (-N_META) % CHUNK

    def pad(a):
        return jnp.pad(a, ((0, 0), (n_pad, 0), (0, 0), (0, 0)))

    def flip(a):
        return a[:, ::-1]

    qp, vp = pad(qh), pad(vh)
    o_f = _scan_jnp(qp, pad(k_f), vp, pad(lf_f))
    o_b = flip(_scan_jnp(flip(qp), flip(pad(k_b)), flip(vp), flip(pad(lf_b))))
    o = (o_f + o_b)[:, n_pad:]
    o = o * lax.rsqrt(jnp.mean(o * o, axis=-1, keepdims=True) + EPS)
    return o.reshape(B, L, D_HGRN) * norm_w * jax.nn.silu(g)


def _route_jnp(logits):
    n_tok = logits.shape[0]
    g_logits = logits[:, :N_GROUPS]
    g_sel = jnp.argmax(g_logits, axis=-1)
    p_group = jnp.take_along_axis(jax.nn.softmax(g_logits, axis=-1), g_sel[:, None], axis=-1)
    e_logits = logits[:, N_GROUPS:N_GROUPS + N_EXPERTS].reshape(n_tok, N_GROUPS, EXPERTS_PER_GROUP)
    e_logits = e_logits[jnp.arange(n_tok), g_sel]
    p_top, e_top = lax.top_k(jax.nn.softmax(e_logits, axis=-1), TOP_K)
    gate = p_group * p_top / jnp.sum(p_top, axis=-1, keepdims=True)
    expert_id = (g_sel[:, None] * EXPERTS_PER_GROUP + e_top).reshape(-1).astype(jnp.int32)
    return gate, expert_id


def kernel(x, meta_tokens, w_in, conv_w, conv_b, filt_w1, filt_b1, filt_w2, filt_b2, filt_w3,
           filt_freq, filt_skip, hyena_norm, lb_fwd, lb_bwd, hgrn_norm, w_out, norm_mix, norm_ffn,
           w_router_group, w_router_expert, w_gate, w_up, w_down, norm_final):
    B, S, D = x.shape
    L = S + N_META
    lbf = jnp.cumsum(jax.nn.softmax(lb_fwd, axis=0), axis=0)[0]
    lbb = jnp.cumsum(jax.nn.softmax(lb_bwd, axis=0), axis=0)[0]

    w_in_b = w_in[0].astype(BF16)
    xf = x.reshape(B * S, D)
    p_x = _inproj(xf, norm_mix[0], w_in_b, tm=512).reshape(B, S, D_IN_PROJ)
    p_m = _inproj(meta_tokens, norm_mix[0], w_in_b, tm=N_META)
    p = jnp.concatenate([jnp.broadcast_to(p_m[None], (B, N_META, D_IN_PROJ)), p_x], axis=1)

    z_hy = _hyena_jnp(p[..., :D_HYENA_PROJ], conv_w[0], conv_b[0], filt_w1[0], filt_b1[0],
                      filt_w2[0], filt_b2[0], filt_w3[0], filt_freq[0], filt_skip[0])
    y_hg = _hgrn_jnp(p[..., D_HYENA_PROJ:], lbf, lbb, hgrn_norm[0])
    z_hy = z_hy[:, N_META:].reshape(B * S, D_HYENA)
    y_hg = y_hg[:, N_META:].reshape(B * S, D_HGRN)

    n_r = 128
    w_r = jnp.concatenate([w_router_group[0], w_router_expert[0].reshape(D, N_EXPERTS),
                           jnp.zeros((D, n_r - N_GROUPS - N_EXPERTS), F32)], axis=1).astype(BF16)
    h1, a2, logits = _outproj(z_hy, y_hg, xf, hyena_norm[0], norm_ffn[0],
                              w_out[0].astype(BF16), w_r, tm=512)

    n_tok = B * S
    gate, expert_id = _route_jnp(logits)
    M = n_tok * TOP_K
    n_blocks = -(-M // MOE_BLOCK) + N_EXPERTS
    n_slots = n_blocks * MOE_BLOCK
    order = jnp.argsort(expert_id)
    eid_s = expert_id[order]
    tok_s = (jnp.arange(M, dtype=jnp.int32) // TOP_K)[order]
    gate_s = gate.reshape(-1)[order]
    counts = jnp.bincount(expert_id, length=N_EXPERTS)
    padded = (counts + MOE_BLOCK - 1) // MOE_BLOCK * MOE_BLOCK
    start = jnp.cumsum(counts) - counts
    pend = jnp.cumsum(padded)
    dest = (pend - padded)[eid_s] + jnp.arange(M, dtype=jnp.int32) - start[eid_s]
    tok_buf = jnp.zeros((n_slots,), jnp.int32).at[dest].set(tok_s)
    gate_buf = jnp.zeros((n_slots,), F32).at[dest].set(gate_s)
    block_eid = jnp.minimum(
        jnp.searchsorted(pend, jnp.arange(n_blocks, dtype=pend.dtype) * MOE_BLOCK, side='right'),
        N_EXPERTS - 1).astype(jnp.int32)
    xb = a2[tok_buf]
    yb = _experts(block_eid, xb, gate_buf, w_gate[0], w_up[0], w_down[0], tb=MOE_BLOCK)
    y = jnp.zeros((n_tok, D), F32).at[tok_buf].add(yb)
    h2 = h1 + y
    out = _rms(h2, norm_final)
    return out.reshape(B, S, D)
```

```python
import functools
import math

import jax
import jax.numpy as jnp
from jax import lax
from jax.experimental import pallas as pl
from jax.experimental.pallas import tpu as pltpu

D_MODEL = 1024
N_META = 16
D_HYENA = 512
D_HGRN = 512
HYENA_ORDER = 2
SHORT_CONV = 3
FILTER_EMB = 33
FILTER_BANDS = 16
DECAY_TARGET = 1e-2
FAST_DECAY_PCT = 0.3
SLOW_DECAY_PCT = 1.5
HGRN_HEAD_DIM = 128
HGRN_HEADS = D_HGRN // HGRN_HEAD_DIM
CHUNK = 64
N_GROUPS = 8
EXPERTS_PER_GROUP = 8
N_EXPERTS = 64
TOP_K = 2
D_EXPERT = 512
MOE_BLOCK = 128
D_HYENA_PROJ = 3 * D_HYENA
D_IN_PROJ = D_HYENA_PROJ + 5 * D_HGRN
EPS = 1e-6

F32 = jnp.float32
BF16 = jnp.bfloat16


def _rms(x, gain):
    return x * lax.rsqrt(jnp.mean(x * x, axis=-1, keepdims=True) + EPS) * gain


def _inproj_body(x_ref, g_ref, w_ref, hy_ref, hg_ref, *, tn):
    a = _rms(x_ref[...], g_ref[...]).astype(BF16)
    for j in range(D_IN_PROJ // tn):
        acc = jnp.dot(a, w_ref[:, j * tn:(j + 1) * tn], preferred_element_type=F32)
        if j * tn < D_HYENA_PROJ:
            hy_ref[:, j * tn:(j + 1) * tn] = acc.astype(hy_ref.dtype)
        else:
            c0 = j * tn - D_HYENA_PROJ
            hg_ref[:, c0:c0 + tn] = acc.astype(hg_ref.dtype)


def _inproj(x, gain, w_bf16, tm):
    m, d = x.shape
    n_hg = D_IN_PROJ - D_HYENA_PROJ
    return pl.pallas_call(
        functools.partial(_inproj_body, tn=512),
        grid=(m // tm,),
        in_specs=[pl.BlockSpec((tm, d), lambda i: (i, 0)),
                  pl.BlockSpec((1, d), lambda i: (0, 0)),
                  pl.BlockSpec((d, D_IN_PROJ), lambda i: (0, 0))],
        out_specs=[pl.BlockSpec((tm, D_HYENA_PROJ), lambda i: (i, 0)),
                   pl.BlockSpec((tm, n_hg), lambda i: (i, 0))],
        out_shape=[jax.ShapeDtypeStruct((m, D_HYENA_PROJ), BF16),
                   jax.ShapeDtypeStruct((m, n_hg), F32)],
        compiler_params=pltpu.CompilerParams(
            dimension_semantics=("arbitrary",), vmem_limit_bytes=48 * 1024 * 1024),
        name="inproj",
    )(x, gain.reshape(1, d), w_bf16)


def _outproj_body(zhy_ref, yhg_ref, h0_ref, ghy_ref, gffn_ref, wo_ref, wr_ref,
                  h1_ref, a2_ref, lg_ref):
    yhy = _rms(zhy_ref[...], ghy_ref[...]).astype(BF16)
    yhg = yhg_ref[...].astype(BF16)
    acc = jnp.dot(yhy, wo_ref[:D_HYENA, :], preferred_element_type=F32)
    acc = acc + jnp.dot(yhg, wo_ref[D_HYENA:, :], preferred_element_type=F32)
    h1 = h0_ref[...] + acc
    h1_ref[...] = h1
    a2 = _rms(h1, gffn_ref[...])
    a2b = a2.astype(BF16)
    a2_ref[...] = a2b
    lg_ref[...] = jnp.dot(a2b, wr_ref[...], preferred_element_type=F32)


def _outproj(zhy, yhg, h0, g_hy, g_ffn, wo_bf16, wr_bf16, tm):
    m = h0.shape[0]
    nr = wr_bf16.shape[1]
    return pl.pallas_call(
        _outproj_body,
        grid=(m // tm,),
        in_specs=[pl.BlockSpec((tm, D_HYENA), lambda i: (i, 0)),
                  pl.BlockSpec((tm, D_HGRN), lambda i: (i, 0)),
                  pl.BlockSpec((tm, D_MODEL), lambda i: (i, 0)),
                  pl.BlockSpec((1, D_HYENA), lambda i: (0, 0)),
                  pl.BlockSpec((1, D_MODEL), lambda i: (0, 0)),
                  pl.BlockSpec((D_MODEL, D_MODEL), lambda i: (0, 0)),
                  pl.BlockSpec((D_MODEL, nr), lambda i: (0, 0))],
        out_specs=[pl.BlockSpec((tm, D_MODEL), lambda i: (i, 0)),
                   pl.BlockSpec((tm, D_MODEL), lambda i: (i, 0)),
                   pl.BlockSpec((tm, nr), lambda i: (i, 0))],
        out_shape=[jax.ShapeDtypeStruct((m, D_MODEL), F32),
                   jax.ShapeDtypeStruct((m, D_MODEL), BF16),
                   jax.ShapeDtypeStruct((m, nr), F32)],
        compiler_params=pltpu.CompilerParams(
            dimension_semantics=("arbitrary",), vmem_limit_bytes=48 * 1024 * 1024),
        name="outproj",
    )(zhy, yhg, h0, g_hy.reshape(1, -1), g_ffn.reshape(1, -1), wo_bf16, wr_bf16)


def _expert_body(eid_ref, xb_ref, gate_ref, wg_ref, wu_ref, wd_ref, o_ref):
    xb = xb_ref[...]
    g = jnp.dot(xb, wg_ref[0].astype(BF16), preferred_element_type=F32)
    u = jnp.dot(xb, wu_ref[0].astype(BF16), preferred_element_type=F32)
    hmid = (g * jax.nn.sigmoid(g) * u).astype(BF16)
    y = jnp.dot(hmid, wd_ref[0].astype(BF16), preferred_element_type=F32)
    o_ref[...] = y * gate_ref[...]


def _experts(block_eid, xb, gate_buf, w_gate, w_up, w_down, tb):
    n_slots = xb.shape[0]
    grid_spec = pltpu.PrefetchScalarGridSpec(
        num_scalar_prefetch=1,
        grid=(n_slots // tb,),
        in_specs=[pl.BlockSpec((tb, D_MODEL), lambda i, e: (i, 0)),
                  pl.BlockSpec((tb, 1), lambda i, e: (i, 0)),
                  pl.BlockSpec((1, D_MODEL, D_EXPERT), lambda i, e: (e[i], 0, 0)),
                  pl.BlockSpec((1, D_MODEL, D_EXPERT), lambda i, e: (e[i], 0, 0)),
                  pl.BlockSpec((1, D_EXPERT, D_MODEL), lambda i, e: (e[i], 0, 0))],
        out_specs=pl.BlockSpec((tb, D_MODEL), lambda i, e: (i, 0)),
    )
    return pl.pallas_call(
        _expert_body,
        grid_spec=grid_spec,
        out_shape=jax.ShapeDtypeStruct((n_slots, D_MODEL), F32),
        compiler_params=pltpu.CompilerParams(
            dimension_semantics=("arbitrary",), vmem_limit_bytes=48 * 1024 * 1024),
        name="experts",
    )(block_eid, xb, gate_buf.reshape(n_slots, 1), w_gate, w_up, w_down)


ND = 2176
NCIRC = 2 * ND
SEQ = 2048
L_TOT = SEQ + N_META
HY_CT = 256
HY_FB = 544
HY_TB = 688
HALO = 8


def _dft_tables():
    k = jnp.arange(ND, dtype=jnp.int32)
    a = 2 * k + 1
    t_hi = jnp.arange(ND // 64, dtype=jnp.int32) * 64
    t_lo = jnp.arange(64, dtype=jnp.int32)
    m_hi = (a[:, None] * t_hi[None, :]) % (2 * NCIRC)
    m_lo = (a[:, None] * (2 * t_lo + 1)[None, :]) % (4 * NCIRC)
    ang_hi = m_hi.astype(F32) * (math.pi / NCIRC)
    ang_lo = m_lo.astype(F32) * (math.pi / (2 * NCIRC))
    c_hi, s_hi = jnp.cos(ang_hi)[:, :, None], jnp.sin(ang_hi)[:, :, None]
    c_lo, s_lo = jnp.cos(ang_lo)[:, None, :], jnp.sin(ang_lo)[:, None, :]
    cs = (c_hi * c_lo - s_hi * s_lo).reshape(ND, ND)
    ss = (s_hi * c_lo + c_hi * s_lo).reshape(ND, ND)
    return cs, ss


def _hyena_body(pv_ref, px1_ref, px2_ref, mv_ref, mx1_ref, mx2_ref,
                wv_ref, wx1_ref, wx2_ref, bv_ref, bx1_ref, bx2_ref, skip_ref,
                cs_ref, ss_ref, hr_ref, hi_ref, o_ref,
                stage_s, z_s, g1_s, g2_s, zb_s, yr_s, yi_s):
    ct = o_ref.shape[-1]

    def short_conv(p_ref, m_ref, w_ref, b_ref, dst_ref):
        stage_s[0:HALO, :] = jnp.zeros((HALO, ct), F32)
        stage_s[HALO:HALO + N_META, :] = m_ref[...].astype(F32)
        stage_s[HALO + N_META:HALO + L_TOT, :] = p_ref[0].astype(F32)
        stage_s[HALO + L_TOT:HALO + L_TOT + HALO, :] = jnp.zeros((HALO, ct), F32)
        w = w_ref[...]
        b = b_ref[...]
        rb = 344
        for r0 in range(0, L_TOT, rb):
            prev = stage_s[HALO - 1 + r0:HALO - 1 + r0 + rb, :]
            cur = stage_s[HALO + r0:HALO + r0 + rb, :]
            nxt = stage_s[HALO + 1 + r0:HALO + 1 + r0 + rb, :]
            dst_ref[r0:r0 + rb, :] = b + prev * w[0:1] + cur * w[1:2] + nxt * w[2:3]

    short_conv(pv_ref, mv_ref, wv_ref, bv_ref, z_s)
    short_conv(px1_ref, mx1_ref, wx1_ref, bx1_ref, g1_s)
    short_conv(px2_ref, mx2_ref, wx2_ref, bx2_ref, g2_s)
    z_s[L_TOT:ND, :] = jnp.zeros((ND - L_TOT, ct), F32)

    for n, g_s in enumerate((g1_s, g2_s)):
        zb_s[...] = z_s[...].astype(BF16)
        for k0 in range(0, ND, HY_FB):
            zb = zb_s[...]
            p = jnp.dot(cs_ref[k0:k0 + HY_FB, :], zb, preferred_element_type=F32)
            q = jnp.dot(ss_ref[k0:k0 + HY_FB, :], zb, preferred_element_type=F32)
            hr = hr_ref[n, k0:k0 + HY_FB, :]
            hi = hi_ref[n, k0:k0 + HY_FB, :]
            yr_s[k0:k0 + HY_FB, :] = (p * hr + q * hi).astype(BF16)
            yi_s[k0:k0 + HY_FB, :] = (q * hr - p * hi).astype(BF16)
        skip = skip_ref[n:n + 1, :]
        for t0 in range(0, L_TOT, HY_TB):
            y = jnp.dot(cs_ref[t0:t0 + HY_TB, :], yr_s[...], preferred_element_type=F32)
            y = y + jnp.dot(ss_ref[t0:t0 + HY_TB, :], yi_s[...], preferred_element_type=F32)
            z = z_s[t0:t0 + HY_TB, :]
            znew = g_s[t0:t0 + HY_TB, :] * (y + z * skip)
            if n == 0:
                z_s[t0:t0 + HY_TB, :] = znew
            elif t0 == 0:
                o_ref[0, 0:HY_TB - N_META, :] = znew[N_META:, :]
            else:
                o_ref[0, t0 - N_META:t0 - N_META + HY_TB, :] = znew


def _hyena(p_hy, pm_hy, conv_w, conv_b, skip, cs, ss, hr, hi):
    bsz = p_hy.shape[0]
    nct = D_HYENA // HY_CT
    ct = HY_CT

    def part(j):
        return [pl.BlockSpec((1, SEQ, ct), lambda c, b, j=j: (b, 0, j * nct + c))]

    def mpart(j):
        return [pl.BlockSpec((N_META, ct), lambda c, b, j=j: (0, j * nct + c))]

    def wpart(rows, j):
        return [pl.BlockSpec((rows, ct), lambda c, b, j=j: (0, j * nct + c))]

    in_specs = (part(0) + part(1) + part(2) + mpart(0) + mpart(1) + mpart(2)
                + wpart(3, 0) + wpart(3, 1) + wpart(3, 2)
                + wpart(1, 0) + wpart(1, 1) + wpart(1, 2)
                + [pl.BlockSpec((HYENA_ORDER, ct), lambda c, b: (0, c)),
                   pl.BlockSpec((ND, ND), lambda c, b: (0, 0)),
                   pl.BlockSpec((ND, ND), lambda c, b: (0, 0)),
                   pl.BlockSpec((HYENA_ORDER, ND, ct), lambda c, b: (0, 0, c),
                                pipeline_mode=pl.Buffered(1)),
                   pl.BlockSpec((HYENA_ORDER, ND, ct), lambda c, b: (0, 0, c),
                                pipeline_mode=pl.Buffered(1))])
    cb = conv_b.reshape(1, -1)
    return pl.pallas_call(
        _hyena_body,
        grid=(nct, bsz),
        in_specs=in_specs,
        out_specs=pl.BlockSpec((1, SEQ, ct), lambda c, b: (b, 0, c)),
        out_shape=jax.ShapeDtypeStruct((bsz, SEQ, D_HYENA), F32),
        scratch_shapes=[pltpu.VMEM((L_TOT + 2 * HALO, ct), F32),
                        pltpu.VMEM((ND, ct), F32),
                        pltpu.VMEM((L_TOT, ct), F32),
                        pltpu.VMEM((L_TOT, ct), F32),
                        pltpu.VMEM((ND, ct), BF16),
                        pltpu.VMEM((ND, ct), BF16),
                        pltpu.VMEM((ND, ct), BF16)],
        compiler_params=pltpu.CompilerParams(
            dimension_semantics=("arbitrary", "arbitrary"), vmem_limit_bytes=60 * 1024 * 1024),
        name="hyena",
    )(p_hy, p_hy, p_hy, pm_hy, pm_hy, pm_hy, conv_w, conv_w, conv_w, cb, cb, cb, skip,
      cs, ss, hr, hi)


def _hyena_spectra(h_fwd, h_bwd, cs, ss):
    L = h_fwd.shape[0]
    hbs = jnp.concatenate([h_bwd[1:], jnp.zeros_like(h_bwd[:1])], axis=0)
    hp = lax.Precision.HIGHEST
    a_r = jnp.einsum('kt,toc->okc', cs[:, :L], h_fwd + hbs, precision=hp)
    a_i = -jnp.einsum('kt,toc->okc', ss[:, :L], h_fwd - hbs, precision=hp)
    half = (2 * jnp.arange(ND, dtype=F32) + 1) * (math.pi / (2 * NCIRC))
    c, s = jnp.cos(half)[None, :, None], jnp.sin(half)[None, :, None]
    scale = 2.0 / NCIRC
    return (c * a_r - s * a_i) * scale, (s * a_r + c * a_i) * scale


def _centred_short_conv(u, w, b):
    L = u.shape[1]
    half = SHORT_CONV // 2
    up = jnp.pad(u, ((0, 0), (half, SHORT_CONV - 1 - half), (0, 0)))
    y = b
    for j in range(SHORT_CONV):
        y = y + up[:, j:j + L] * w[j]
    return y


def _hyena_filters(L, w1, b1, w2, b2, w3, freq):
    pos = jnp.arange(L, dtype=F32)
    t = pos / max(L - 1, 1)
    bands = jnp.linspace(1e-4, FILTER_BANDS - 1, FILTER_BANDS, dtype=F32)
    ang = (2.0 * math.pi / L) * pos[:, None] * bands[None, :]
    z = jnp.concatenate([t[:, None], jnp.cos(ang), -jnp.sin(ang)], axis=-1)
    hp = lax.Precision.HIGHEST
    hid = jnp.sin(freq * (jnp.dot(z, w1, precision=hp) + b1))
    hid = jnp.sin(freq * (jnp.dot(hid, w2, precision=hp) + b2))
    filt = jnp.dot(hid, w3, precision=hp).reshape(L, 2, HYENA_ORDER, D_HYENA)
    deltas = jnp.abs(jnp.linspace(math.log(DECAY_TARGET) / SLOW_DECAY_PCT,
                                  math.log(DECAY_TARGET) / FAST_DECAY_PCT, D_HYENA, dtype=F32))
    window = jnp.exp(-t[:, None] * deltas[None, :])
    filt = filt * window[:, None, None, :]
    return filt[:, 0], filt[:, 1]


def _hyena_jnp(p, conv_w, conv_b, w1, b1, w2, b2, w3, freq, skip):
    L = p.shape[1]
    u = _centred_short_conv(p, conv_w, conv_b)
    z = u[..., :D_HYENA]
    h_fwd, h_bwd = _hyena_filters(L, w1, b1, w2, b2, w3, freq)
    two_sided = jnp.concatenate([h_fwd, jnp.zeros_like(h_fwd[:1]), h_bwd[:0:-1]], axis=0)
    spectra = jnp.fft.rfft(two_sided, axis=0)
    for n in range(HYENA_ORDER):
        gate = u[..., (n + 1) * D_HYENA:(n + 2) * D_HYENA]
        y = jnp.fft.irfft(jnp.fft.rfft(z, n=2 * L, axis=1) * spectra[None, :, n], n=2 * L, axis=1)[:, :L]
        z = gate * (y + z * skip[n])
    return z


def _scan_jnp(q, k, v, log_f):
    B, T, H, DK = q.shape
    DV = v.shape[-1]
    N = T // CHUNK
    q, k, v, log_f = [a.reshape(B, N, CHUNK, H, a.shape[-1]) for a in (q, k, v, log_f)]
    b = jnp.cumsum(log_f, axis=2)
    b_last = b[:, :, -1]
    b_mid = b[:, :, CHUNK // 2][:, :, None]
    scores = jnp.einsum('bnthd,bnshd->bnhts', q * jnp.exp(b - b_mid), k * jnp.exp(b_mid - b))
    lower_tri = jnp.tril(jnp.ones((CHUNK, CHUNK), dtype=bool))
    scores = jnp.where(lower_tri, scores, 0.0)
    o_intra = jnp.einsum('bnhts,bnshv->bnthv', scores, v)
    chunk_state = jnp.einsum('bnshd,bnshv->bnhdv', k * jnp.exp(b_last[:, :, None] - b), v)
    chunk_decay = jnp.exp(b_last)

    def step(S, inp):
        dec, U = inp
        return dec[..., None] * S + U, S

    S0 = jnp.zeros((B, H, DK, DV), q.dtype)
    _, S_in = lax.scan(step, S0, (jnp.moveaxis(chunk_decay, 1, 0), jnp.moveaxis(chunk_state, 1, 0)))
    S_in = jnp.moveaxis(S_in, 0, 1)
    o_inter = jnp.einsum('bnthd,bnhdv->bnthv', q * jnp.exp(b), S_in)
    return (o_intra + o_inter).reshape(B, T, H, DV)


def _hgrn_jnp(p, lb_f, lb_b, norm_w):
    B, L, _ = p.shape
    q, f_fwd, f_bwd, i_in, g = [p[..., j * D_HGRN:(j + 1) * D_HGRN] for j in range(5)]

    def heads(a):
        return a.reshape(B, L, HGRN_HEADS, HGRN_HEAD_DIM)

    qh = heads(jax.nn.silu(q))
    vh = heads(i_in)

    def forget(logit, lb):
        f = lb + (1.0 - lb) * jax.nn.sigmoid(logit)
        return heads(1.0 - f), heads(jnp.log(f))

    k_f, lf_f = forget(f_fwd, lb_f)
    k_b, lf_b = forget(f_bwd, lb_b)
    n_pad = (-N_META) % CHUNK

    def pad(a):
        return jnp.pad(a, ((0, 0), (n_pad, 0), (0, 0), (0, 0)))

    def flip(a):
        return a[:, ::-1]

    qp, vp = pad(qh), pad(vh)
    o_f = _scan_jnp(qp, pad(k_f), vp, pad(lf_f))
    o_b = flip(_scan_jnp(flip(qp), flip(pad(k_b)), flip(vp), flip(pad(lf_b))))
    o = (o_f + o_b)[:, n_pad:]
    o = o * lax.rsqrt(jnp.mean(o * o, axis=-1, keepdims=True) + EPS)
    return o.reshape(B, L, D_HGRN) * norm_w * jax.nn.silu(g)


def _route_jnp(logits):
    n_tok = logits.shape[0]
    g_logits = logits[:, :N_GROUPS]
    g_sel = jnp.argmax(g_logits, axis=-1)
    p_group = jnp.take_along_axis(jax.nn.softmax(g_logits, axis=-1), g_sel[:, None], axis=-1)
    e_logits = logits[:, N_GROUPS:N_GROUPS + N_EXPERTS].reshape(n_tok, N_GROUPS, EXPERTS_PER_GROUP)
    e_logits = e_logits[jnp.arange(n_tok), g_sel]
    p_top, e_top = lax.top_k(jax.nn.softmax(e_logits, axis=-1), TOP_K)
    gate = p_group * p_top / jnp.sum(p_top, axis=-1, keepdims=True)
    expert_id = (g_sel[:, None] * EXPERTS_PER_GROUP + e_top).reshape(-1).astype(jnp.int32)
    return gate, expert_id


def kernel(x, meta_tokens, w_in, conv_w, conv_b, filt_w1, filt_b1, filt_w2, filt_b2, filt_w3,
           filt_freq, filt_skip, hyena_norm, lb_fwd, lb_bwd, hgrn_norm, w_out, norm_mix, norm_ffn,
           w_router_group, w_router_expert, w_gate, w_up, w_down, norm_final):
    B, S, D = x.shape
    L = S + N_META
    lbf = jnp.cumsum(jax.nn.softmax(lb_fwd, axis=0), axis=0)[0]
    lbb = jnp.cumsum(jax.nn.softmax(lb_bwd, axis=0), axis=0)[0]

    w_in_b = w_in[0].astype(BF16)
    xf = x.reshape(B * S, D)
    phy_x, phg_x = _inproj(xf, norm_mix[0], w_in_b, tm=512)
    phy_m, phg_m = _inproj(meta_tokens, norm_mix[0], w_in_b, tm=N_META)

    cs, ss = _dft_tables()
    h_fwd, h_bwd = _hyena_filters(L, filt_w1[0], filt_b1[0], filt_w2[0], filt_b2[0], filt_w3[0],
                                  filt_freq[0])
    hr, hi = _hyena_spectra(h_fwd, h_bwd, cs, ss)
    z_hy = _hyena(phy_x.reshape(B, S, D_HYENA_PROJ), phy_m, conv_w[0], conv_b[0], filt_skip[0],
                  cs.astype(BF16), ss.astype(BF16), hr, hi).reshape(B * S, D_HYENA)

    p_hg = jnp.concatenate([jnp.broadcast_to(phg_m[None], (B, N_META, 5 * D_HGRN)),
                            phg_x.reshape(B, S, 5 * D_HGRN)], axis=1)
    y_hg = _hgrn_jnp(p_hg, lbf, lbb, hgrn_norm[0])
    y_hg = y_hg[:, N_META:].reshape(B * S, D_HGRN)

    n_r = 128
    w_r = jnp.concatenate([w_router_group[0], w_router_expert[0].reshape(D, N_EXPERTS),
                           jnp.zeros((D, n_r - N_GROUPS - N_EXPERTS), F32)], axis=1).astype(BF16)
    h1, a2, logits = _outproj(z_hy, y_hg, xf, hyena_norm[0], norm_ffn[0],
                              w_out[0].astype(BF16), w_r, tm=512)

    n_tok = B * S
    gate, expert_id = _route_jnp(logits)
    M = n_tok * TOP_K
    n_blocks = -(-M // MOE_BLOCK) + N_EXPERTS
    n_slots = n_blocks * MOE_BLOCK
    order = jnp.argsort(expert_id)
    eid_s = expert_id[order]
    tok_s = (jnp.arange(M, dtype=jnp.int32) // TOP_K)[order]
    gate_s = gate.reshape(-1)[order]
    counts = jnp.bincount(expert_id, length=N_EXPERTS)
    padded = (counts + MOE_BLOCK - 1) // MOE_BLOCK * MOE_BLOCK
    start = jnp.cumsum(counts) - counts
    pend = jnp.cumsum(padded)
    dest = (pend - padded)[eid_s] + jnp.arange(M, dtype=jnp.int32) - start[eid_s]
    tok_buf = jnp.zeros((n_slots,), jnp.int32).at[dest].set(tok_s)
    gate_buf = jnp.zeros((n_slots,), F32).at[dest].set(gate_s)
    block_eid = jnp.minimum(
        jnp.searchsorted(pend, jnp.arange(n_blocks, dtype=pend.dtype) * MOE_BLOCK, side='right'),
        N_EXPERTS - 1).astype(jnp.int32)
    xb = a2[tok_buf]
    yb = _experts(block_eid, xb, gate_buf, w_gate[0], w_up[0], w_down[0], tb=MOE_BLOCK)
    y = jnp.zeros((n_tok, D), F32).at[tok_buf].add(yb)
    h2 = h1 + y
    out = _rms(h2, norm_final)
    return out.reshape(B, S, D)
```

```python
import functools
import math

import jax
import jax.numpy as jnp
from jax import lax
from jax.experimental import pallas as pl
from jax.experimental.pallas import tpu as pltpu

D_MODEL = 1024
N_META = 16
D_HYENA = 512
D_HGRN = 512
HYENA_ORDER = 2
SHORT_CONV = 3
FILTER_EMB = 33
FILTER_BANDS = 16
DECAY_TARGET = 1e-2
FAST_DECAY_PCT = 0.3
SLOW_DECAY_PCT = 1.5
HGRN_HEAD_DIM = 128
HGRN_HEADS = D_HGRN // HGRN_HEAD_DIM
CHUNK = 64
N_GROUPS = 8
EXPERTS_PER_GROUP = 8
N_EXPERTS = 64
TOP_K = 2
D_EXPERT = 512
MOE_BLOCK = 128
D_HYENA_PROJ = 3 * D_HYENA
D_IN_PROJ = D_HYENA_PROJ + 5 * D_HGRN
EPS = 1e-6

F32 = jnp.float32
BF16 = jnp.bfloat16


def _rms(x, gain):
    return x * lax.rsqrt(jnp.mean(x * x, axis=-1, keepdims=True) + EPS) * gain


def _inproj_body(x_ref, g_ref, w_ref, hy_ref, hg_ref, *, tn):
    a = _rms(x_ref[...], g_ref[...]).astype(BF16)
    for j in range(D_IN_PROJ // tn):
        acc = jnp.dot(a, w_ref[:, j * tn:(j + 1) * tn], preferred_element_type=F32)
        if j * tn < D_HYENA_PROJ:
            hy_ref[:, j * tn:(j + 1) * tn] = acc.astype(hy_ref.dtype)
        else:
            c0 = j * tn - D_HYENA_PROJ
            hg_ref[:, c0:c0 + tn] = acc.astype(hg_ref.dtype)


def _inproj(x, gain, w_bf16, tm):
    m, d = x.shape
    n_hg = D_IN_PROJ - D_HYENA_PROJ
    return pl.pallas_call(
        functools.partial(_inproj_body, tn=512),
        grid=(m // tm,),
        in_specs=[pl.BlockSpec((tm, d), lambda i: (i, 0)),
                  pl.BlockSpec((1, d), lambda i: (0, 0)),
                  pl.BlockSpec((d, D_IN_PROJ), lambda i: (0, 0))],
        out_specs=[pl.BlockSpec((tm, D_HYENA_PROJ), lambda i: (i, 0)),
                   pl.BlockSpec((tm, n_hg), lambda i: (i, 0))],
        out_shape=[jax.ShapeDtypeStruct((m, D_HYENA_PROJ), BF16),
                   jax.ShapeDtypeStruct((m, n_hg), F32)],
        compiler_params=pltpu.CompilerParams(
            dimension_semantics=("arbitrary",), vmem_limit_bytes=48 * 1024 * 1024),
        name="inproj",
    )(x, gain.reshape(1, d), w_bf16)


def _outproj_body(zhy_ref, yhg_ref, h0_ref, ghy_ref, gffn_ref, wo_ref, wr_ref,
                  h1_ref, a2_ref, lg_ref):
    yhy = _rms(zhy_ref[...], ghy_ref[...]).astype(BF16)
    yhg = yhg_ref[...].astype(BF16)
    acc = jnp.dot(yhy, wo_ref[:D_HYENA, :], preferred_element_type=F32)
    acc = acc + jnp.dot(yhg, wo_ref[D_HYENA:, :], preferred_element_type=F32)
    h1 = h0_ref[...] + acc
    h1_ref[...] = h1
    a2 = _rms(h1, gffn_ref[...])
    a2b = a2.astype(BF16)
    a2_ref[...] = a2b
    lg_ref[...] = jnp.dot(a2b, wr_ref[...], preferred_element_type=F32)


def _outproj(zhy, yhg, h0, g_hy, g_ffn, wo_bf16, wr_bf16, tm):
    m = h0.shape[0]
    nr = wr_bf16.shape[1]
    return pl.pallas_call(
        _outproj_body,
        grid=(m // tm,),
        in_specs=[pl.BlockSpec((tm, D_HYENA), lambda i: (i, 0)),
                  pl.BlockSpec((tm, D_HGRN), lambda i: (i, 0)),
                  pl.BlockSpec((tm, D_MODEL), lambda i: (i, 0)),
                  pl.BlockSpec((1, D_HYENA), lambda i: (0, 0)),
                  pl.BlockSpec((1, D_MODEL), lambda i: (0, 0)),
                  pl.BlockSpec((D_MODEL, D_MODEL), lambda i: (0, 0)),
                  pl.BlockSpec((D_MODEL, nr), lambda i: (0, 0))],
        out_specs=[pl.BlockSpec((tm, D_MODEL), lambda i: (i, 0)),
                   pl.BlockSpec((tm, D_MODEL), lambda i: (i, 0)),
                   pl.BlockSpec((tm, nr), lambda i: (i, 0))],
        out_shape=[jax.ShapeDtypeStruct((m, D_MODEL), F32),
                   jax.ShapeDtypeStruct((m, D_MODEL), BF16),
                   jax.ShapeDtypeStruct((m, nr), F32)],
        compiler_params=pltpu.CompilerParams(
            dimension_semantics=("arbitrary",), vmem_limit_bytes=48 * 1024 * 1024),
        name="outproj",
    )(zhy, yhg, h0, g_hy.reshape(1, -1), g_ffn.reshape(1, -1), wo_bf16, wr_bf16)


def _expert_body(eid_ref, xb_ref, gate_ref, wg_ref, wu_ref, wd_ref, o_ref):
    xb = xb_ref[...]
    g = jnp.dot(xb, wg_ref[0].astype(BF16), preferred_element_type=F32)
    u = jnp.dot(xb, wu_ref[0].astype(BF16), preferred_element_type=F32)
    hmid = (g * jax.nn.sigmoid(g) * u).astype(BF16)
    y = jnp.dot(hmid, wd_ref[0].astype(BF16), preferred_element_type=F32)
    o_ref[...] = y * gate_ref[...]


def _experts(block_eid, xb, gate_buf, w_gate, w_up, w_down, tb):
    n_slots = xb.shape[0]
    grid_spec = pltpu.PrefetchScalarGridSpec(
        num_scalar_prefetch=1,
        grid=(n_slots // tb,),
        in_specs=[pl.BlockSpec((tb, D_MODEL), lambda i, e: (i, 0)),
                  pl.BlockSpec((tb, 1), lambda i, e: (i, 0)),
                  pl.BlockSpec((1, D_MODEL, D_EXPERT), lambda i, e: (e[i], 0, 0)),
                  pl.BlockSpec((1, D_MODEL, D_EXPERT), lambda i, e: (e[i], 0, 0)),
                  pl.BlockSpec((1, D_EXPERT, D_MODEL), lambda i, e: (e[i], 0, 0))],
        out_specs=pl.BlockSpec((tb, D_MODEL), lambda i, e: (i, 0)),
    )
    return pl.pallas_call(
        _expert_body,
        grid_spec=grid_spec,
        out_shape=jax.ShapeDtypeStruct((n_slots, D_MODEL), F32),
        compiler_params=pltpu.CompilerParams(
            dimension_semantics=("arbitrary",), vmem_limit_bytes=48 * 1024 * 1024),
        name="experts",
    )(block_eid, xb, gate_buf.reshape(n_slots, 1), w_gate, w_up, w_down)


ND = 2176
NCIRC = 2 * ND
SEQ = 2048
L_TOT = SEQ + N_META
HY_CT = 256
HY_FB = 544
HY_TB = 688
HALO = 8


def _dft_tables():
    k = jnp.arange(ND, dtype=jnp.int32)
    a = 2 * k + 1
    t_hi = jnp.arange(ND // 64, dtype=jnp.int32) * 64
    t_lo = jnp.arange(64, dtype=jnp.int32)
    m_hi = (a[:, None] * t_hi[None, :]) % (2 * NCIRC)
    m_lo = (a[:, None] * (2 * t_lo + 1)[None, :]) % (4 * NCIRC)
    ang_hi = m_hi.astype(F32) * (math.pi / NCIRC)
    ang_lo = m_lo.astype(F32) * (math.pi / (2 * NCIRC))
    c_hi, s_hi = jnp.cos(ang_hi)[:, :, None], jnp.sin(ang_hi)[:, :, None]
    c_lo, s_lo = jnp.cos(ang_lo)[:, None, :], jnp.sin(ang_lo)[:, None, :]
    cs = (c_hi * c_lo - s_hi * s_lo).reshape(ND, ND)
    ss = (s_hi * c_lo + c_hi * s_lo).reshape(ND, ND)
    return cs, ss


def _hyena_body(pv_ref, px1_ref, px2_ref, mv_ref, mx1_ref, mx2_ref,
                wv_ref, wx1_ref, wx2_ref, bv_ref, bx1_ref, bx2_ref, skip_ref,
                cs_ref, ss_ref, hr_ref, hi_ref, o_ref,
                stage_s, z_s, g1_s, g2_s, zb_s, yr_s, yi_s):
    ct = o_ref.shape[-1]

    def short_conv(p_ref, m_ref, w_ref, b_ref, dst_ref):
        stage_s[0:HALO, :] = jnp.zeros((HALO, ct), F32)
        stage_s[HALO:HALO + N_META, :] = m_ref[...].astype(F32)
        stage_s[HALO + N_META:HALO + L_TOT, :] = p_ref[0].astype(F32)
        stage_s[HALO + L_TOT:HALO + L_TOT + HALO, :] = jnp.zeros((HALO, ct), F32)
        w = w_ref[...]
        b = b_ref[...]
        rb = 344
        for r0 in range(0, L_TOT, rb):
            prev = stage_s[HALO - 1 + r0:HALO - 1 + r0 + rb, :]
            cur = stage_s[HALO + r0:HALO + r0 + rb, :]
            nxt = stage_s[HALO + 1 + r0:HALO + 1 + r0 + rb, :]
            dst_ref[r0:r0 + rb, :] = b + prev * w[0:1] + cur * w[1:2] + nxt * w[2:3]

    short_conv(pv_ref, mv_ref, wv_ref, bv_ref, z_s)
    short_conv(px1_ref, mx1_ref, wx1_ref, bx1_ref, g1_s)
    short_conv(px2_ref, mx2_ref, wx2_ref, bx2_ref, g2_s)
    z_s[L_TOT:ND, :] = jnp.zeros((ND - L_TOT, ct), F32)

    for n, g_s in enumerate((g1_s, g2_s)):
        zb_s[...] = z_s[...].astype(BF16)
        for k0 in range(0, ND, HY_FB):
            zb = zb_s[...]
            p = jnp.dot(cs_ref[k0:k0 + HY_FB, :], zb, preferred_element_type=F32)
            q = jnp.dot(ss_ref[k0:k0 + HY_FB, :], zb, preferred_element_type=F32)
            hr = hr_ref[n, k0:k0 + HY_FB, :]
            hi = hi_ref[n, k0:k0 + HY_FB, :]
            yr_s[k0:k0 + HY_FB, :] = (p * hr + q * hi).astype(BF16)
            yi_s[k0:k0 + HY_FB, :] = (q * hr - p * hi).astype(BF16)
        skip = skip_ref[n:n + 1, :]
        for t0 in range(0, L_TOT, HY_TB):
            y = jnp.dot(cs_ref[t0:t0 + HY_TB, :], yr_s[...], preferred_element_type=F32)
            y = y + jnp.dot(ss_ref[t0:t0 + HY_TB, :], yi_s[...], preferred_element_type=F32)
            z = z_s[t0:t0 + HY_TB, :]
            znew = g_s[t0:t0 + HY_TB, :] * (y + z * skip)
            if n == 0:
                z_s[t0:t0 + HY_TB, :] = znew
            elif t0 == 0:
                o_ref[0, 0:HY_TB - N_META, :] = znew[N_META:, :]
            else:
                o_ref[0, t0 - N_META:t0 - N_META + HY_TB, :] = znew


def _hyena(p_hy, pm_hy, conv_w, conv_b, skip, cs, ss, hr, hi):
    bsz = p_hy.shape[0]
    nct = D_HYENA // HY_CT
    ct = HY_CT

    def part(j):
        return [pl.BlockSpec((1, SEQ, ct), lambda c, b, j=j: (b, 0, j * nct + c))]

    def mpart(j):
        return [pl.BlockSpec((N_META, ct), lambda c, b, j=j: (0, j * nct + c))]

    def wpart(rows, j):
        return [pl.BlockSpec((rows, ct), lambda c, b, j=j: (0, j * nct + c))]

    in_specs = (part(0) + part(1) + part(2) + mpart(0) + mpart(1) + mpart(2)
                + wpart(3, 0) + wpart(3, 1) + wpart(3, 2)
                + wpart(1, 0) + wpart(1, 1) + wpart(1, 2)
                + [pl.BlockSpec((HYENA_ORDER, ct), lambda c, b: (0, c)),
                   pl.BlockSpec((ND, ND), lambda c, b: (0, 0)),
                   pl.BlockSpec((ND, ND), lambda c, b: (0, 0)),
                   pl.BlockSpec((HYENA_ORDER, ND, ct), lambda c, b: (0, 0, c),
                                pipeline_mode=pl.Buffered(1)),
                   pl.BlockSpec((HYENA_ORDER, ND, ct), lambda c, b: (0, 0, c),
                                pipeline_mode=pl.Buffered(1))])
    cb = conv_b.reshape(1, -1)
    return pl.pallas_call(
        _hyena_body,
        grid=(nct, bsz),
        in_specs=in_specs,
        out_specs=pl.BlockSpec((1, SEQ, ct), lambda c, b: (b, 0, c)),
        out_shape=jax.ShapeDtypeStruct((bsz, SEQ, D_HYENA), F32),
        scratch_shapes=[pltpu.VMEM((L_TOT + 2 * HALO, ct), F32),
                        pltpu.VMEM((ND, ct), F32),
                        pltpu.VMEM((L_TOT, ct), F32),
                        pltpu.VMEM((L_TOT, ct), F32),
                        pltpu.VMEM((ND, ct), BF16),
                        pltpu.VMEM((ND, ct), BF16),
                        pltpu.VMEM((ND, ct), BF16)],
        compiler_params=pltpu.CompilerParams(
            dimension_semantics=("arbitrary", "arbitrary"), vmem_limit_bytes=60 * 1024 * 1024),
        name="hyena",
    )(p_hy, p_hy, p_hy, pm_hy, pm_hy, pm_hy, conv_w, conv_w, conv_w, cb, cb, cb, skip,
      cs, ss, hr, hi)


def _hyena_spectra(h_fwd, h_bwd, cs, ss):
    L = h_fwd.shape[0]
    hbs = jnp.concatenate([h_bwd[1:], jnp.zeros_like(h_bwd[:1])], axis=0)
    hp = lax.Precision.HIGHEST
    a_r = jnp.einsum('kt,toc->okc', cs[:, :L], h_fwd + hbs, precision=hp)
    a_i = -jnp.einsum('kt,toc->okc', ss[:, :L], h_fwd - hbs, precision=hp)
    half = (2 * jnp.arange(ND, dtype=F32) + 1) * (math.pi / (2 * NCIRC))
    c, s = jnp.cos(half)[None, :, None], jnp.sin(half)[None, :, None]
    scale = 2.0 / NCIRC
    return (c * a_r - s * a_i) * scale, (s * a_r + c * a_i) * scale


N_CHUNKS = SEQ // CHUNK
NT_DIMS = (((1,), (1,)), ((), ()))
TN_DIMS = (((0,), (0,)), ((), ()))


def _split3(x):
    hi = x.astype(BF16)
    r1 = x - hi.astype(F32)
    mid = r1.astype(BF16)
    lo = (r1 - mid.astype(F32)).astype(BF16)
    return jnp.concatenate([hi, mid, lo], axis=1)


def _tri_cumsum(tri_bf16, x):
    w = x.shape[1]
    s = jnp.dot(tri_bf16, _split3(x), preferred_element_type=F32)
    return s[:, :w] + s[:, w:2 * w] + s[:, 2 * w:]


def _hgrn_body(q_ref, ff_ref, fb_ref, i_ref, g_ref, mff_ref, mi_ref,
               lbf_ref, lbb_ref, nw_ref, o_ref,
               o_s, qef_s, qeb_s, utf_s, utb_s, decf_s, decb_s):
    hd = HGRN_HEAD_DIM
    row = lax.broadcasted_iota(jnp.int32, (CHUNK, CHUNK), 0)
    col = lax.broadcasted_iota(jnp.int32, (CHUNK, CHUNK), 1)
    lower = row >= col
    upper = col >= row
    tri_l = lower.astype(BF16)
    tri_u = upper.astype(BF16)
    lbf = lbf_ref[...]
    lbb = lbb_ref[...]

    def forget(logit, lb):
        f = lb + (1.0 - lb) * jax.nn.sigmoid(logit)
        return 1.0 - f, jnp.log(f)

    def chunk_terms(tri, mask, lf, k, qv, vb, mid, last):
        b = _tri_cumsum(tri, lf)
        b_mid = b[mid:mid + 1, :]
        b_last = b[last:last + 1, :]
        ut = lax.dot_general(vb, (k * jnp.exp(b_last - b)).astype(BF16), TN_DIMS,
                             preferred_element_type=F32)
        dec = jnp.exp(b_last)
        if qv is None:
            return ut, dec
        qs = (qv * jnp.exp(b - b_mid)).astype(BF16)
        ks = (k * jnp.exp(b_mid - b)).astype(BF16)
        sc = lax.dot_general(qs, ks, NT_DIMS, preferred_element_type=F32)
        sc = jnp.where(mask, sc, 0.0).astype(BF16)
        o_intra = jnp.dot(sc, vb, preferred_element_type=F32)
        qe = (qv * jnp.exp(b)).astype(BF16)
        return o_intra, qe, ut, dec

    k_m, lf_m = forget(mff_ref[...], lbf)
    pad = jnp.zeros((CHUNK - N_META, hd), F32)
    ut_m, dec_m = chunk_terms(tri_l, lower, jnp.concatenate([pad, lf_m], axis=0),
                              jnp.concatenate([pad, k_m], axis=0), None,
                              jnp.concatenate([pad, mi_ref[...]], axis=0).astype(BF16),
                              CHUNK // 2, CHUNK - 1)
    del dec_m

    for n in range(N_CHUNKS):
        r0 = n * CHUNK
        qv = jax.nn.silu(q_ref[0, r0:r0 + CHUNK, :])
        vb = i_ref[0, r0:r0 + CHUNK, :].astype(BF16)
        k_f, lf_f = forget(ff_ref[0, r0:r0 + CHUNK, :], lbf)
        k_b, lf_b = forget(fb_ref[0, r0:r0 + CHUNK, :], lbb)
        oi_f, qe_f, ut_f, dec_f = chunk_terms(tri_l, lower, lf_f, k_f, qv, vb, CHUNK // 2, CHUNK - 1)
        oi_b, qe_b, ut_b, dec_b = chunk_terms(tri_u, upper, lf_b, k_b, qv, vb, CHUNK // 2 - 1, 0)
        o_s[r0:r0 + CHUNK, :] = oi_f + oi_b
        qef_s[r0:r0 + CHUNK, :] = qe_f
        qeb_s[r0:r0 + CHUNK, :] = qe_b
        utf_s[n] = ut_f
        utb_s[n] = ut_b
        decf_s[n] = dec_f
        decb_s[n] = dec_b

    st_f = ut_m
    st_b = jnp.zeros((hd, hd), F32)
    for n in range(N_CHUNKS):
        rf = n * CHUNK
        o_s[rf:rf + CHUNK, :] += lax.dot_general(qef_s[rf:rf + CHUNK, :], st_f.astype(BF16), NT_DIMS,
                                                 preferred_element_type=F32)
        st_f = decf_s[n] * st_f + utf_s[n]
        m = N_CHUNKS - 1 - n
        rb = m * CHUNK
        o_s[rb:rb + CHUNK, :] += lax.dot_general(qeb_s[rb:rb + CHUNK, :], st_b.astype(BF16), NT_DIMS,
                                                 preferred_element_type=F32)
        st_b = decb_s[m] * st_b + utb_s[m]

    nw = nw_ref[...]
    rb = 256
    for r0 in range(0, SEQ, rb):
        o = o_s[r0:r0 + rb, :]
        o = o * lax.rsqrt(jnp.mean(o * o, axis=-1, keepdims=True) + EPS)
        o_ref[0, r0:r0 + rb, :] = o * nw * jax.nn.silu(g_ref[0, r0:r0 + rb, :])


def _hgrn(phg_x, phg_m, lb_f, lb_b, norm_w):
    bsz = phg_x.shape[0]
    hd = HGRN_HEAD_DIM
    nh = HGRN_HEADS

    def part(j):
        return pl.BlockSpec((1, SEQ, hd), lambda b, h, j=j: (b, 0, j * nh + h))

    def mpart(j):
        return pl.BlockSpec((N_META, hd), lambda b, h, j=j: (0, j * nh + h))

    vec = pl.BlockSpec((1, hd), lambda b, h: (0, h))
    return pl.pallas_call(
        _hgrn_body,
        grid=(bsz, nh),
        in_specs=[part(0), part(1), part(2), part(3), part(4), mpart(1), mpart(3), vec, vec, vec],
        out_specs=pl.BlockSpec((1, SEQ, hd), lambda b, h: (b, 0, h)),
        out_shape=jax.ShapeDtypeStruct((bsz, SEQ, D_HGRN), F32),
        scratch_shapes=[pltpu.VMEM((SEQ, hd), F32),
                        pltpu.VMEM((SEQ, hd), BF16),
                        pltpu.VMEM((SEQ, hd), BF16),
                        pltpu.VMEM((N_CHUNKS, hd, hd), F32),
                        pltpu.VMEM((N_CHUNKS, hd, hd), F32),
                        pltpu.VMEM((N_CHUNKS, 1, hd), F32),
                        pltpu.VMEM((N_CHUNKS, 1, hd), F32)],
        compiler_params=pltpu.CompilerParams(
            dimension_semantics=("arbitrary", "arbitrary"), vmem_limit_bytes=40 * 1024 * 1024),
        name="hgrn",
    )(phg_x, phg_x, phg_x, phg_x, phg_x, phg_m, phg_m,
      lb_f.reshape(1, -1), lb_b.reshape(1, -1), norm_w.reshape(1, -1))


def _centred_short_conv(u, w, b):
    L = u.shape[1]
    half = SHORT_CONV // 2
    up = jnp.pad(u, ((0, 0), (half, SHORT_CONV - 1 - half), (0, 0)))
    y = b
    for j in range(SHORT_CONV):
        y = y + up[:, j:j + L] * w[j]
    return y


def _hyena_filters(L, w1, b1, w2, b2, w3, freq):
    pos = jnp.arange(L, dtype=F32)
    t = pos / max(L - 1, 1)
    bands = jnp.linspace(1e-4, FILTER_BANDS - 1, FILTER_BANDS, dtype=F32)
    ang = (2.0 * math.pi / L) * pos[:, None] * bands[None, :]
    z = jnp.concatenate([t[:, None], jnp.cos(ang), -jnp.sin(ang)], axis=-1)
    hp = lax.Precision.HIGHEST
    hid = jnp.sin(freq * (jnp.dot(z, w1, precision=hp) + b1))
    hid = jnp.sin(freq * (jnp.dot(hid, w2, precision=hp) + b2))
    filt = jnp.dot(hid, w3, precision=hp).reshape(L, 2, HYENA_ORDER, D_HYENA)
    deltas = jnp.abs(jnp.linspace(math.log(DECAY_TARGET) / SLOW_DECAY_PCT,
                                  math.log(DECAY_TARGET) / FAST_DECAY_PCT, D_HYENA, dtype=F32))
    window = jnp.exp(-t[:, None] * deltas[None, :])
    filt = filt * window[:, None, None, :]
    return filt[:, 0], filt[:, 1]


def _hyena_jnp(p, conv_w, conv_b, w1, b1, w2, b2, w3, freq, skip):
    L = p.shape[1]
    u = _centred_short_conv(p, conv_w, conv_b)
    z = u[..., :D_HYENA]
    h_fwd, h_bwd = _hyena_filters(L, w1, b1, w2, b2, w3, freq)
    two_sided = jnp.concatenate([h_fwd, jnp.zeros_like(h_fwd[:1]), h_bwd[:0:-1]], axis=0)
    spectra = jnp.fft.rfft(two_sided, axis=0)
    for n in range(HYENA_ORDER):
        gate = u[..., (n + 1) * D_HYENA:(n + 2) * D_HYENA]
        y = jnp.fft.irfft(jnp.fft.rfft(z, n=2 * L, axis=1) * spectra[None, :, n], n=2 * L, axis=1)[:, :L]
        z = gate * (y + z * skip[n])
    return z


def _scan_jnp(q, k, v, log_f):
    B, T, H, DK = q.shape
    DV = v.shape[-1]
    N = T // CHUNK
    q, k, v, log_f = [a.reshape(B, N, CHUNK, H, a.shape[-1]) for a in (q, k, v, log_f)]
    b = jnp.cumsum(log_f, axis=2)
    b_last = b[:, :, -1]
    b_mid = b[:, :, CHUNK // 2][:, :, None]
    scores = jnp.einsum('bnthd,bnshd->bnhts', q * jnp.exp(b - b_mid), k * jnp.exp(b_mid - b))
    lower_tri = jnp.tril(jnp.ones((CHUNK, CHUNK), dtype=bool))
    scores = jnp.where(lower_tri, scores, 0.0)
    o_intra = jnp.einsum('bnhts,bnshv->bnthv', scores, v)
    chunk_state = jnp.einsum('bnshd,bnshv->bnhdv', k * jnp.exp(b_last[:, :, None] - b), v)
    chunk_decay = jnp.exp(b_last)

    def step(S, inp):
        dec, U = inp
        return dec[..., None] * S + U, S

    S0 = jnp.zeros((B, H, DK, DV), q.dtype)
    _, S_in = lax.scan(step, S0, (jnp.moveaxis(chunk_decay, 1, 0), jnp.moveaxis(chunk_state, 1, 0)))
    S_in = jnp.moveaxis(S_in, 0, 1)
    o_inter = jnp.einsum('bnthd,bnhdv->bnthv', q * jnp.exp(b), S_in)
    return (o_intra + o_inter).reshape(B, T, H, DV)


def _hgrn_jnp(p, lb_f, lb_b, norm_w):
    B, L, _ = p.shape
    q, f_fwd, f_bwd, i_in, g = [p[..., j * D_HGRN:(j + 1) * D_HGRN] for j in range(5)]

    def heads(a):
        return a.reshape(B, L, HGRN_HEADS, HGRN_HEAD_DIM)

    qh = heads(jax.nn.silu(q))
    vh = heads(i_in)

    def forget(logit, lb):
        f = lb + (1.0 - lb) * jax.nn.sigmoid(logit)
        return heads(1.0 - f), heads(jnp.log(f))

    k_f, lf_f = forget(f_fwd, lb_f)
    k_b, lf_b = forget(f_bwd, lb_b)
    n_pad = (-N_META) % CHUNK

    def pad(a):
        return jnp.pad(a, ((0, 0), (n_pad, 0), (0, 0), (0, 0)))

    def flip(a):
        return a[:, ::-1]

    qp, vp = pad(qh), pad(vh)
    o_f = _scan_jnp(qp, pad(k_f), vp, pad(lf_f))
    o_b = flip(_scan_jnp(flip(qp), flip(pad(k_b)), flip(vp), flip(pad(lf_b))))
    o = (o_f + o_b)[:, n_pad:]
    o = o * lax.rsqrt(jnp.mean(o * o, axis=-1, keepdims=True) + EPS)
    return o.reshape(B, L, D_HGRN) * norm_w * jax.nn.silu(g)


def _route_jnp(logits):
    n_tok = logits.shape[0]
    g_logits = logits[:, :N_GROUPS]
    g_sel = jnp.argmax(g_logits, axis=-1)
    p_group = jnp.take_along_axis(jax.nn.softmax(g_logits, axis=-1), g_sel[:, None], axis=-1)
    e_logits = logits[:, N_GROUPS:N_GROUPS + N_EXPERTS].reshape(n_tok, N_GROUPS, EXPERTS_PER_GROUP)
    e_logits = e_logits[jnp.arange(n_tok), g_sel]
    p_top, e_top = lax.top_k(jax.nn.softmax(e_logits, axis=-1), TOP_K)
    gate = p_group * p_top / jnp.sum(p_top, axis=-1, keepdims=True)
    expert_id = (g_sel[:, None] * EXPERTS_PER_GROUP + e_top).reshape(-1).astype(jnp.int32)
    return gate, expert_id


def kernel(x, meta_tokens, w_in, conv_w, conv_b, filt_w1, filt_b1, filt_w2, filt_b2, filt_w3,
           filt_freq, filt_skip, hyena_norm, lb_fwd, lb_bwd, hgrn_norm, w_out, norm_mix, norm_ffn,
           w_router_group, w_router_expert, w_gate, w_up, w_down, norm_final):
    B, S, D = x.shape
    L = S + N_META
    lbf = jnp.cumsum(jax.nn.softmax(lb_fwd, axis=0), axis=0)[0]
    lbb = jnp.cumsum(jax.nn.softmax(lb_bwd, axis=0), axis=0)[0]

    w_in_b = w_in[0].astype(BF16)
    xf = x.reshape(B * S, D)
    phy_x, phg_x = _inproj(xf, norm_mix[0], w_in_b, tm=512)
    phy_m, phg_m = _inproj(meta_tokens, norm_mix[0], w_in_b, tm=N_META)

    cs, ss = _dft_tables()
    h_fwd, h_bwd = _hyena_filters(L, filt_w1[0], filt_b1[0], filt_w2[0], filt_b2[0], filt_w3[0],
                                  filt_freq[0])
    hr, hi = _hyena_spectra(h_fwd, h_bwd, cs, ss)
    z_hy = _hyena(phy_x.reshape(B, S, D_HYENA_PROJ), phy_m, conv_w[0], conv_b[0], filt_skip[0],
                  cs.astype(BF16), ss.astype(BF16), hr, hi).reshape(B * S, D_HYENA)

    y_hg = _hgrn(phg_x.reshape(B, S, 5 * D_HGRN), phg_m, lbf, lbb,
                 hgrn_norm[0]).reshape(B * S, D_HGRN)

    n_r = 128
    w_r = jnp.concatenate([w_router_group[0], w_router_expert[0].reshape(D, N_EXPERTS),
                           jnp.zeros((D, n_r - N_GROUPS - N_EXPERTS), F32)], axis=1).astype(BF16)
    h1, a2, logits = _outproj(z_hy, y_hg, xf, hyena_norm[0], norm_ffn[0],
                              w_out[0].astype(BF16), w_r, tm=512)

    n_tok = B * S
    gate, expert_id = _route_jnp(logits)
    M = n_tok * TOP_K
    n_blocks = -(-M // MOE_BLOCK) + N_EXPERTS
    n_slots = n_blocks * MOE_BLOCK
    order = jnp.argsort(expert_id)
    eid_s = expert_id[order]
    tok_s = (jnp.arange(M, dtype=jnp.int32) // TOP_K)[order]
    gate_s = gate.reshape(-1)[order]
    counts = jnp.bincount(expert_id, length=N_EXPERTS)
    padded = (counts + MOE_BLOCK - 1) // MOE_BLOCK * MOE_BLOCK
    start = jnp.cumsum(counts) - counts
    pend = jnp.cumsum(padded)
    dest = (pend - padded)[eid_s] + jnp.arange(M, dtype=jnp.int32) - start[eid_s]
    tok_buf = jnp.zeros((n_slots,), jnp.int32).at[dest].set(tok_s)
    gate_buf = jnp.zeros((n_slots,), F32).at[dest].set(gate_s)
    block_eid = jnp.minimum(
        jnp.searchsorted(pend, jnp.arange(n_blocks, dtype=pend.dtype) * MOE_BLOCK, side='right'),
        N_EXPERTS - 1).astype(jnp.int32)
    xb = a2[tok_buf]
    yb = _experts(block_eid, xb, gate_buf, w_gate[0], w_up[0], w_down[0], tb=MOE_BLOCK)
    y = jnp.zeros((n_tok, D), F32).at[tok_buf].add(yb)
    h2 = h1 + y
    out = _rms(h2, norm_final)
    return out.reshape(B, S, D)
```

```python
import functools
import math

import jax
import jax.numpy as jnp
from jax import lax
from jax.experimental import pallas as pl
from jax.experimental.pallas import tpu as pltpu

D_MODEL = 1024
N_META = 16
D_HYENA = 512
D_HGRN = 512
HYENA_ORDER = 2
SHORT_CONV = 3
FILTER_EMB = 33
FILTER_BANDS = 16
DECAY_TARGET = 1e-2
FAST_DECAY_PCT = 0.3
SLOW_DECAY_PCT = 1.5
HGRN_HEAD_DIM = 128
HGRN_HEADS = D_HGRN // HGRN_HEAD_DIM
CHUNK = 64
N_GROUPS = 8
EXPERTS_PER_GROUP = 8
N_EXPERTS = 64
TOP_K = 2
D_EXPERT = 512
MOE_BLOCK = 128
D_HYENA_PROJ = 3 * D_HYENA
D_IN_PROJ = D_HYENA_PROJ + 5 * D_HGRN
EPS = 1e-6

F32 = jnp.float32
BF16 = jnp.bfloat16


def _rms(x, gain):
    return x * lax.rsqrt(jnp.mean(x * x, axis=-1, keepdims=True) + EPS) * gain


def _inproj_body(x_ref, g_ref, w_ref, hy_ref, hg_ref, *, tn):
    a = _rms(x_ref[...], g_ref[...]).astype(BF16)
    for j in range(D_IN_PROJ // tn):
        acc = jnp.dot(a, w_ref[:, j * tn:(j + 1) * tn], preferred_element_type=F32)
        if j * tn < D_HYENA_PROJ:
            hy_ref[:, j * tn:(j + 1) * tn] = acc.astype(hy_ref.dtype)
        else:
            c0 = j * tn - D_HYENA_PROJ
            hg_ref[:, c0:c0 + tn] = acc.astype(hg_ref.dtype)


def _inproj(x, gain, w_bf16, tm):
    m, d = x.shape
    n_hg = D_IN_PROJ - D_HYENA_PROJ
    return pl.pallas_call(
        functools.partial(_inproj_body, tn=512),
        grid=(m // tm,),
        in_specs=[pl.BlockSpec((tm, d), lambda i: (i, 0)),
                  pl.BlockSpec((1, d), lambda i: (0, 0)),
                  pl.BlockSpec((d, D_IN_PROJ), lambda i: (0, 0))],
        out_specs=[pl.BlockSpec((tm, D_HYENA_PROJ), lambda i: (i, 0)),
                   pl.BlockSpec((tm, n_hg), lambda i: (i, 0))],
        out_shape=[jax.ShapeDtypeStruct((m, D_HYENA_PROJ), BF16),
                   jax.ShapeDtypeStruct((m, n_hg), F32)],
        compiler_params=pltpu.CompilerParams(
            dimension_semantics=("arbitrary",), vmem_limit_bytes=48 * 1024 * 1024),
        name="inproj",
    )(x, gain.reshape(1, d), w_bf16)


LANES = 128
NEG_BIG = -1e30


def _outproj_body(zhy_ref, yhg_ref, h0_ref, ghy_ref, gffn_ref, wo_ref, wr_ref,
                  h1_ref, ri_ref, rg_ref, cnt_ref, tri_s, carry_s):
    i = pl.program_id(0)
    tm = h1_ref.shape[0]

    @pl.when(i == 0)
    def _():
        r = lax.broadcasted_iota(jnp.int32, (tm, tm), 0)
        c = lax.broadcasted_iota(jnp.int32, (tm, tm), 1)
        tri_s[...] = (r > c).astype(BF16)
        carry_s[...] = jnp.zeros_like(carry_s)

    yhy = _rms(zhy_ref[...], ghy_ref[...]).astype(BF16)
    yhg = yhg_ref[...].astype(BF16)
    acc = jnp.dot(yhy, wo_ref[:D_HYENA, :], preferred_element_type=F32)
    acc = acc + jnp.dot(yhg, wo_ref[D_HYENA:, :], preferred_element_type=F32)
    h1 = h0_ref[...] + acc
    h1_ref[...] = h1
    a2b = _rms(h1, gffn_ref[...]).astype(BF16)
    lg = jnp.dot(a2b, wr_ref[...], preferred_element_type=F32)

    lane = lax.broadcasted_iota(jnp.int32, (tm, LANES), 1)
    is_g = lane < N_GROUPS
    gl = jnp.where(is_g, lg, NEG_BIG)
    gmax = jnp.max(gl, axis=1, keepdims=True)
    gsel = jnp.min(jnp.where(gl == gmax, lane, LANES), axis=1, keepdims=True)
    gden = jnp.sum(jnp.where(is_g, jnp.exp(gl - gmax), 0.0), axis=1, keepdims=True)
    p_group = 1.0 / gden
    in_grp = (lane >= N_GROUPS) & (lane < N_GROUPS + N_EXPERTS) & (
        ((lane - N_GROUPS) >> 3) == gsel)
    el = jnp.where(in_grp, lg, NEG_BIG)
    m1 = jnp.max(el, axis=1, keepdims=True)
    i1 = jnp.min(jnp.where(el == m1, lane, LANES), axis=1, keepdims=True)
    el2 = jnp.where(lane == i1, NEG_BIG, el)
    m2 = jnp.max(el2, axis=1, keepdims=True)
    i2 = jnp.min(jnp.where(el2 == m2, lane, LANES), axis=1, keepdims=True)
    r21 = jnp.exp(m2 - m1)
    gate1 = p_group / (1.0 + r21)
    gate2 = gate1 * r21

    hit1 = lane == i1
    hit2 = lane == i2
    onehot = (hit1 | hit2).astype(BF16)
    pre = jnp.dot(tri_s[...], onehot, preferred_element_type=F32) + carry_s[...]
    pos1 = jnp.sum(jnp.where(hit1, pre, 0.0), axis=1, keepdims=True).astype(jnp.int32)
    pos2 = jnp.sum(jnp.where(hit2, pre, 0.0), axis=1, keepdims=True).astype(jnp.int32)
    carry_s[...] += jnp.sum(onehot.astype(F32), axis=0, keepdims=True)
    cnt_ref[...] = carry_s[...]

    zero_i = jnp.zeros((tm, LANES), jnp.int32)
    ri_ref[...] = jnp.where(lane == 0, i1 - N_GROUPS,
                  jnp.where(lane == 1, i2 - N_GROUPS,
                  jnp.where(lane == 2, pos1, jnp.where(lane == 3, pos2, zero_i))))
    rg_ref[...] = jnp.where(lane == 0, gate1, jnp.where(lane == 1, gate2, 0.0))


def _outproj(zhy, yhg, h0, g_hy, g_ffn, wo_bf16, wr_bf16, tm):
    m = h0.shape[0]
    return pl.pallas_call(
        _outproj_body,
        grid=(m // tm,),
        in_specs=[pl.BlockSpec((tm, D_HYENA), lambda i: (i, 0)),
                  pl.BlockSpec((tm, D_HGRN), lambda i: (i, 0)),
                  pl.BlockSpec((tm, D_MODEL), lambda i: (i, 0)),
                  pl.BlockSpec((1, D_HYENA), lambda i: (0, 0)),
                  pl.BlockSpec((1, D_MODEL), lambda i: (0, 0)),
                  pl.BlockSpec((D_MODEL, D_MODEL), lambda i: (0, 0)),
                  pl.BlockSpec((D_MODEL, LANES), lambda i: (0, 0))],
        out_specs=[pl.BlockSpec((tm, D_MODEL), lambda i: (i, 0)),
                   pl.BlockSpec((tm, LANES), lambda i: (i, 0)),
                   pl.BlockSpec((tm, LANES), lambda i: (i, 0)),
                   pl.BlockSpec((1, LANES), lambda i: (0, 0))],
        out_shape=[jax.ShapeDtypeStruct((m, D_MODEL), F32),
                   jax.ShapeDtypeStruct((m, LANES), jnp.int32),
                   jax.ShapeDtypeStruct((m, LANES), F32),
                   jax.ShapeDtypeStruct((1, LANES), F32)],
        scratch_shapes=[pltpu.VMEM((tm, tm), BF16), pltpu.VMEM((1, LANES), F32)],
        compiler_params=pltpu.CompilerParams(
            dimension_semantics=("arbitrary",), vmem_limit_bytes=48 * 1024 * 1024),
        name="outproj",
    )(zhy, yhg, h0, g_hy.reshape(1, -1), g_ffn.reshape(1, -1), wo_bf16, wr_bf16)


EXP_TB = 256


def _row_copy(src, src_row, dst, dst_row, sem):
    return pltpu.make_async_copy(src.at[pl.ds(src_row, 1), :], dst.at[pl.ds(dst_row, 1), :], sem)


def _wait_rows(src, dst, n_rows, sem):
    pltpu.make_async_copy(src.at[pl.ds(0, n_rows), :], dst.at[pl.ds(0, n_rows), :], sem).wait()


def _dispatch_body(lastblk_ref, npad_ref, nused_ref, dest_ref, h1_ref, xb_ref, zero_s, sem_z, sem):
    i = pl.program_id(0)
    tm = h1_ref.shape[0]
    n_blocks = xb_ref.shape[0] // EXP_TB

    @pl.when(i == 0)
    def _():
        zero_s[...] = jnp.zeros_like(zero_s)

        def zero_copy(row0):
            row0 = pl.multiple_of(row0, EXP_TB)
            return pltpu.make_async_copy(zero_s, xb_ref.at[pl.ds(row0, EXP_TB), :], sem_z)

        for e in range(N_EXPERTS):
            @pl.when(npad_ref[e] > 0)
            def _():
                zero_copy(lastblk_ref[e]).start()
        for e in range(N_EXPERTS):
            @pl.when(npad_ref[e] > 0)
            def _():
                zero_copy(lastblk_ref[e]).wait()

        def start_tail(b, carry):
            zero_copy(b * EXP_TB).start()
            return carry

        def wait_tail(b, carry):
            zero_copy(b * EXP_TB).wait()
            return carry

        lax.fori_loop(nused_ref[0], n_blocks, start_tail, 0)
        lax.fori_loop(nused_ref[0], n_blocks, wait_tail, 0)

    def issue(t, carry):
        _row_copy(h1_ref, t, xb_ref, dest_ref[2 * t], sem).start()
        _row_copy(h1_ref, t, xb_ref, dest_ref[2 * t + 1], sem).start()
        return carry

    lax.fori_loop(0, tm, issue, 0, unroll=8)
    _wait_rows(h1_ref, xb_ref, tm, sem)
    _wait_rows(h1_ref, xb_ref, tm, sem)


def _dispatch(lastblk, npad, n_used, dest_flat, h1, n_slots, tm):
    m = h1.shape[0]
    grid_spec = pltpu.PrefetchScalarGridSpec(
        num_scalar_prefetch=3,
        grid=(m // tm,),
        in_specs=[pl.BlockSpec((TOP_K * tm,), lambda i, lb, npd, nu: (i,), memory_space=pltpu.SMEM),
                  pl.BlockSpec((tm, D_MODEL), lambda i, lb, npd, nu: (i, 0))],
        out_specs=pl.BlockSpec(memory_space=pl.ANY),
        scratch_shapes=[pltpu.VMEM((EXP_TB, D_MODEL), F32),
                        pltpu.SemaphoreType.DMA(()), pltpu.SemaphoreType.DMA(())],
    )
    return pl.pallas_call(
        _dispatch_body,
        grid_spec=grid_spec,
        out_shape=jax.ShapeDtypeStruct((n_slots, D_MODEL), F32),
        compiler_params=pltpu.CompilerParams(dimension_semantics=("arbitrary",)),
        name="dispatch",
    )(lastblk, npad, n_used, dest_flat, h1)


def _expert_body(eid_ref, nused_ref, xb_ref, gffn_ref, wg_ref, wu_ref, wd_ref, o_ref,
                 wg_s, wu_s, wd_s):
    i = pl.program_id(0)

    @pl.when(i < nused_ref[0])
    def _():
        prev = eid_ref[jnp.maximum(i - 1, 0)]

        @pl.when((i == 0) | (eid_ref[i] != prev))
        def _():
            wg_s[...] = wg_ref[0].astype(BF16)
            wu_s[...] = wu_ref[0].astype(BF16)
            wd_s[...] = wd_ref[0].astype(BF16)

        xb = _rms(xb_ref[...], gffn_ref[...]).astype(BF16)
        g = jnp.dot(xb, wg_s[...], preferred_element_type=F32)
        u = jnp.dot(xb, wu_s[...], preferred_element_type=F32)
        hmid = (g * jax.nn.sigmoid(g) * u).astype(BF16)
        o_ref[...] = jnp.dot(hmid, wd_s[...], preferred_element_type=F32)

    @pl.when(i >= nused_ref[0])
    def _():
        o_ref[...] = jnp.zeros_like(o_ref)


def _experts(block_eid, n_used, xb, g_ffn, w_gate, w_up, w_down):
    n_slots = xb.shape[0]

    def blk(i, e, nu):
        return jnp.minimum(i, nu[0] - 1)

    grid_spec = pltpu.PrefetchScalarGridSpec(
        num_scalar_prefetch=2,
        grid=(n_slots // EXP_TB,),
        in_specs=[pl.BlockSpec((EXP_TB, D_MODEL), lambda i, e, nu: (blk(i, e, nu), 0)),
                  pl.BlockSpec((1, D_MODEL), lambda i, e, nu: (0, 0)),
                  pl.BlockSpec((1, D_MODEL, D_EXPERT), lambda i, e, nu: (e[blk(i, e, nu)], 0, 0)),
                  pl.BlockSpec((1, D_MODEL, D_EXPERT), lambda i, e, nu: (e[blk(i, e, nu)], 0, 0)),
                  pl.BlockSpec((1, D_EXPERT, D_MODEL), lambda i, e, nu: (e[blk(i, e, nu)], 0, 0))],
        out_specs=pl.BlockSpec((EXP_TB, D_MODEL), lambda i, e, nu: (i, 0)),
        scratch_shapes=[pltpu.VMEM((D_MODEL, D_EXPERT), BF16),
                        pltpu.VMEM((D_MODEL, D_EXPERT), BF16),
                        pltpu.VMEM((D_EXPERT, D_MODEL), BF16)],
    )
    return pl.pallas_call(
        _expert_body,
        grid_spec=grid_spec,
        out_shape=jax.ShapeDtypeStruct((n_slots, D_MODEL), F32),
        compiler_params=pltpu.CompilerParams(
            dimension_semantics=("arbitrary",), vmem_limit_bytes=48 * 1024 * 1024),
        name="experts",
    )(block_eid, n_used, xb, g_ffn.reshape(1, -1), w_gate, w_up, w_down)


def _combine_body(dest_ref, h1_ref, rg_ref, gfin_ref, yb_ref, o_ref, y0_s, y1_s, sem):
    tm = h1_ref.shape[0]

    def issue(t, carry):
        _row_copy(yb_ref, dest_ref[2 * t], y0_s, t, sem).start()
        _row_copy(yb_ref, dest_ref[2 * t + 1], y1_s, t, sem).start()
        return carry

    lax.fori_loop(0, tm, issue, 0, unroll=8)
    _wait_rows(yb_ref, y0_s, tm, sem)
    _wait_rows(yb_ref, y1_s, tm, sem)
    rg = rg_ref[...]
    h2 = h1_ref[...] + rg[:, 0:1] * y0_s[...] + rg[:, 1:2] * y1_s[...]
    o_ref[...] = _rms(h2, gfin_ref[...])


def _combine(dest_flat, h1, rg, g_fin, yb, tm):
    m = h1.shape[0]
    return pl.pallas_call(
        _combine_body,
        grid=(m // tm,),
        in_specs=[pl.BlockSpec((TOP_K * tm,), lambda i: (i,), memory_space=pltpu.SMEM),
                  pl.BlockSpec((tm, D_MODEL), lambda i: (i, 0)),
                  pl.BlockSpec((tm, LANES), lambda i: (i, 0)),
                  pl.BlockSpec((1, D_MODEL), lambda i: (0, 0)),
                  pl.BlockSpec(memory_space=pl.ANY)],
        out_specs=pl.BlockSpec((tm, D_MODEL), lambda i: (i, 0)),
        out_shape=jax.ShapeDtypeStruct((m, D_MODEL), F32),
        scratch_shapes=[pltpu.VMEM((tm, D_MODEL), F32), pltpu.VMEM((tm, D_MODEL), F32),
                        pltpu.SemaphoreType.DMA(())],
        compiler_params=pltpu.CompilerParams(dimension_semantics=("arbitrary",)),
        name="combine",
    )(dest_flat, h1, rg, g_fin.reshape(1, -1), yb)


def _moe(h1, ri, rg, cnt, g_ffn, g_fin, w_gate, w_up, w_down):
    m = h1.shape[0]
    n_blocks = TOP_K * m // EXP_TB + N_EXPERTS
    n_slots = n_blocks * EXP_TB
    counts = cnt[0, N_GROUPS:N_GROUPS + N_EXPERTS].astype(jnp.int32)
    padded = (counts + EXP_TB - 1) // EXP_TB * EXP_TB
    pend = jnp.cumsum(padded)
    base = pend - padded
    eid = ri[:, 0:TOP_K]
    sel = eid[:, :, None] == jnp.arange(N_EXPERTS, dtype=jnp.int32)[None, None, :]
    dest = jnp.sum(jnp.where(sel, base[None, None, :], 0), axis=-1) + ri[:, TOP_K:2 * TOP_K]
    dest_flat = dest.reshape(-1).astype(jnp.int32)
    blk_start = jnp.arange(n_blocks, dtype=jnp.int32) * EXP_TB
    block_eid = jnp.minimum(jnp.sum(blk_start[:, None] >= pend[None, :], axis=1),
                            N_EXPERTS - 1).astype(jnp.int32)
    n_used = (pend[-1:] // EXP_TB).astype(jnp.int32)
    lastblk = (pend - EXP_TB).astype(jnp.int32)

    xb = _dispatch(lastblk, padded.astype(jnp.int32), n_used, dest_flat, h1, n_slots, tm=512)
    yb = _experts(block_eid, n_used, xb, g_ffn, w_gate, w_up, w_down)
    return _combine(dest_flat, h1, rg, g_fin, yb, tm=256)


ND = 2176
NCIRC = 2 * ND
SEQ = 2048
L_TOT = SEQ + N_META
HY_CT = 256
HY_FB = 544
HY_TB = 688
HALO = 8


def _dft_tables():
    k = jnp.arange(ND, dtype=jnp.int32)
    a = 2 * k + 1
    t_hi = jnp.arange(ND // 64, dtype=jnp.int32) * 64
    t_lo = jnp.arange(64, dtype=jnp.int32)
    m_hi = (a[:, None] * t_hi[None, :]) % (2 * NCIRC)
    m_lo = (a[:, None] * (2 * t_lo + 1)[None, :]) % (4 * NCIRC)
    ang_hi = m_hi.astype(F32) * (math.pi / NCIRC)
    ang_lo = m_lo.astype(F32) * (math.pi / (2 * NCIRC))
    c_hi, s_hi = jnp.cos(ang_hi)[:, :, None], jnp.sin(ang_hi)[:, :, None]
    c_lo, s_lo = jnp.cos(ang_lo)[:, None, :], jnp.sin(ang_lo)[:, None, :]
    cs = (c_hi * c_lo - s_hi * s_lo).reshape(ND, ND)
    ss = (s_hi * c_lo + c_hi * s_lo).reshape(ND, ND)
    return cs, ss


def _hyena_body(pv_ref, px1_ref, px2_ref, mv_ref, mx1_ref, mx2_ref,
                wv_ref, wx1_ref, wx2_ref, bv_ref, bx1_ref, bx2_ref, skip_ref,
                cs_ref, ss_ref, hr_ref, hi_ref, o_ref,
                stage_s, z_s, g1_s, g2_s, zb_s, yr_s, yi_s):
    ct = o_ref.shape[-1]

    def short_conv(p_ref, m_ref, w_ref, b_ref, dst_ref):
        stage_s[0:HALO, :] = jnp.zeros((HALO, ct), F32)
        stage_s[HALO:HALO + N_META, :] = m_ref[...].astype(F32)
        stage_s[HALO + N_META:HALO + L_TOT, :] = p_ref[0].astype(F32)
        stage_s[HALO + L_TOT:HALO + L_TOT + HALO, :] = jnp.zeros((HALO, ct), F32)
        w = w_ref[...]
        b = b_ref[...]
        rb = 344
        for r0 in range(0, L_TOT, rb):
            prev = stage_s[HALO - 1 + r0:HALO - 1 + r0 + rb, :]
            cur = stage_s[HALO + r0:HALO + r0 + rb, :]
            nxt = stage_s[HALO + 1 + r0:HALO + 1 + r0 + rb, :]
            dst_ref[r0:r0 + rb, :] = b + prev * w[0:1] + cur * w[1:2] + nxt * w[2:3]

    short_conv(pv_ref, mv_ref, wv_ref, bv_ref, z_s)
    short_conv(px1_ref, mx1_ref, wx1_ref, bx1_ref, g1_s)
    short_conv(px2_ref, mx2_ref, wx2_ref, bx2_ref, g2_s)
    z_s[L_TOT:ND, :] = jnp.zeros((ND - L_TOT, ct), F32)

    for n, g_s in enumerate((g1_s, g2_s)):
        zb_s[...] = z_s[...].astype(BF16)
        for k0 in range(0, ND, HY_FB):
            zb = zb_s[...]
            p = jnp.dot(cs_ref[k0:k0 + HY_FB, :], zb, preferred_element_type=F32)
            q = jnp.dot(ss_ref[k0:k0 + HY_FB, :], zb, preferred_element_type=F32)
            hr = hr_ref[n, k0:k0 + HY_FB, :]
            hi = hi_ref[n, k0:k0 + HY_FB, :]
            yr_s[k0:k0 + HY_FB, :] = (p * hr + q * hi).astype(BF16)
            yi_s[k0:k0 + HY_FB, :] = (q * hr - p * hi).astype(BF16)
        skip = skip_ref[n:n + 1, :]
        for t0 in range(0, L_TOT, HY_TB):
            y = jnp.dot(cs_ref[t0:t0 + HY_TB, :], yr_s[...], preferred_element_type=F32)
            y = y + jnp.dot(ss_ref[t0:t0 + HY_TB, :], yi_s[...], preferred_element_type=F32)
            z = z_s[t0:t0 + HY_TB, :]
            znew = g_s[t0:t0 + HY_TB, :] * (y + z * skip)
            if n == 0:
                z_s[t0:t0 + HY_TB, :] = znew
            elif t0 == 0:
                o_ref[0, 0:HY_TB - N_META, :] = znew[N_META:, :]
            else:
                o_ref[0, t0 - N_META:t0 - N_META + HY_TB, :] = znew


def _hyena(p_hy, pm_hy, conv_w, conv_b, skip, cs, ss, hr, hi):
    bsz = p_hy.shape[0]
    nct = D_HYENA // HY_CT
    ct = HY_CT

    def part(j):
        return [pl.BlockSpec((1, SEQ, ct), lambda c, b, j=j: (b, 0, j * nct + c))]

    def mpart(j):
        return [pl.BlockSpec((N_META, ct), lambda c, b, j=j: (0, j * nct + c))]

    def wpart(rows, j):
        return [pl.BlockSpec((rows, ct), lambda c, b, j=j: (0, j * nct + c))]

    in_specs = (part(0) + part(1) + part(2) + mpart(0) + mpart(1) + mpart(2)
                + wpart(3, 0) + wpart(3, 1) + wpart(3, 2)
                + wpart(1, 0) + wpart(1, 1) + wpart(1, 2)
                + [pl.BlockSpec((HYENA_ORDER, ct), lambda c, b: (0, c)),
                   pl.BlockSpec((ND, ND), lambda c, b: (0, 0)),
                   pl.BlockSpec((ND, ND), lambda c, b: (0, 0)),
                   pl.BlockSpec((HYENA_ORDER, ND, ct), lambda c, b: (0, 0, c),
                                pipeline_mode=pl.Buffered(1)),
                   pl.BlockSpec((HYENA_ORDER, ND, ct), lambda c, b: (0, 0, c),
                                pipeline_mode=pl.Buffered(1))])
    cb = conv_b.reshape(1, -1)
    return pl.pallas_call(
        _hyena_body,
        grid=(nct, bsz),
        in_specs=in_specs,
        out_specs=pl.BlockSpec((1, SEQ, ct), lambda c, b: (b, 0, c)),
        out_shape=jax.ShapeDtypeStruct((bsz, SEQ, D_HYENA), F32),
        scratch_shapes=[pltpu.VMEM((L_TOT + 2 * HALO, ct), F32),
                        pltpu.VMEM((ND, ct), F32),
                        pltpu.VMEM((L_TOT, ct), F32),
                        pltpu.VMEM((L_TOT, ct), F32),
                        pltpu.VMEM((ND, ct), BF16),
                        pltpu.VMEM((ND, ct), BF16),
                        pltpu.VMEM((ND, ct), BF16)],
        compiler_params=pltpu.CompilerParams(
            dimension_semantics=("arbitrary", "arbitrary"), vmem_limit_bytes=60 * 1024 * 1024),
        name="hyena",
    )(p_hy, p_hy, p_hy, pm_hy, pm_hy, pm_hy, conv_w, conv_w, conv_w, cb, cb, cb, skip,
      cs, ss, hr, hi)


def _hyena_spectra(h_fwd, h_bwd, cs, ss):
    L = h_fwd.shape[0]
    hbs = jnp.concatenate([h_bwd[1:], jnp.zeros_like(h_bwd[:1])], axis=0)
    hp = lax.Precision.HIGHEST
    a_r = jnp.einsum('kt,toc->okc', cs[:, :L], h_fwd + hbs, precision=hp)
    a_i = -jnp.einsum('kt,toc->okc', ss[:, :L], h_fwd - hbs, precision=hp)
    half = (2 * jnp.arange(ND, dtype=F32) + 1) * (math.pi / (2 * NCIRC))
    c, s = jnp.cos(half)[None, :, None], jnp.sin(half)[None, :, None]
    scale = 2.0 / NCIRC
    return (c * a_r - s * a_i) * scale, (s * a_r + c * a_i) * scale


N_CHUNKS = SEQ // CHUNK
NT_DIMS = (((1,), (1,)), ((), ()))
TN_DIMS = (((0,), (0,)), ((), ()))


def _split3(x):
    hi = x.astype(BF16)
    r1 = x - hi.astype(F32)
    mid = r1.astype(BF16)
    lo = (r1 - mid.astype(F32)).astype(BF16)
    return jnp.concatenate([hi, mid, lo], axis=1)


def _tri_cumsum(tri_bf16, x):
    w = x.shape[1]
    s = jnp.dot(tri_bf16, _split3(x), preferred_element_type=F32)
    return s[:, :w] + s[:, w:2 * w] + s[:, 2 * w:]


def _hgrn_body(q_ref, ff_ref, fb_ref, i_ref, g_ref, mff_ref, mi_ref,
               lbf_ref, lbb_ref, nw_ref, o_ref,
               o_s, qef_s, qeb_s, utf_s, utb_s, decf_s, decb_s):
    hd = HGRN_HEAD_DIM
    row = lax.broadcasted_iota(jnp.int32, (CHUNK, CHUNK), 0)
    col = lax.broadcasted_iota(jnp.int32, (CHUNK, CHUNK), 1)
    lower = row >= col
    upper = col >= row
    tri_l = lower.astype(BF16)
    tri_u = upper.astype(BF16)
    lbf = lbf_ref[...]
    lbb = lbb_ref[...]

    def forget(logit, lb):
        f = lb + (1.0 - lb) * jax.nn.sigmoid(logit)
        return 1.0 - f, jnp.log(f)

    def chunk_terms(tri, mask, lf, k, qv, vb, mid, last):
        b = _tri_cumsum(tri, lf)
        b_mid = b[mid:mid + 1, :]
        b_last = b[last:last + 1, :]
        ut = lax.dot_general(vb, (k * jnp.exp(b_last - b)).astype(BF16), TN_DIMS,
                             preferred_element_type=F32)
        dec = jnp.exp(b_last)
        if qv is None:
            return ut, dec
        qs = (qv * jnp.exp(b - b_mid)).astype(BF16)
        ks = (k * jnp.exp(b_mid - b)).astype(BF16)
        sc = lax.dot_general(qs, ks, NT_DIMS, preferred_element_type=F32)
        sc = jnp.where(mask, sc, 0.0).astype(BF16)
        o_intra = jnp.dot(sc, vb, preferred_element_type=F32)
        qe = (qv * jnp.exp(b)).astype(BF16)
        return o_intra, qe, ut, dec

    k_m, lf_m = forget(mff_ref[...], lbf)
    pad = jnp.zeros((CHUNK - N_META, hd), F32)
    ut_m, dec_m = chunk_terms(tri_l, lower, jnp.concatenate([pad, lf_m], axis=0),
                              jnp.concatenate([pad, k_m], axis=0), None,
                              jnp.concatenate([pad, mi_ref[...]], axis=0).astype(BF16),
                              CHUNK // 2, CHUNK - 1)
    del dec_m

    for n in range(N_CHUNKS):
        r0 = n * CHUNK
        qv = jax.nn.silu(q_ref[0, r0:r0 + CHUNK, :])
        vb = i_ref[0, r0:r0 + CHUNK, :].astype(BF16)
        k_f, lf_f = forget(ff_ref[0, r0:r0 + CHUNK, :], lbf)
        k_b, lf_b = forget(fb_ref[0, r0:r0 + CHUNK, :], lbb)
        oi_f, qe_f, ut_f, dec_f = chunk_terms(tri_l, lower, lf_f, k_f, qv, vb, CHUNK // 2, CHUNK - 1)
        oi_b, qe_b, ut_b, dec_b = chunk_terms(tri_u, upper, lf_b, k_b, qv, vb, CHUNK // 2 - 1, 0)
        o_s[r0:r0 + CHUNK, :] = oi_f + oi_b
        qef_s[r0:r0 + CHUNK, :] = qe_f
        qeb_s[r0:r0 + CHUNK, :] = qe_b
        utf_s[n] = ut_f
        utb_s[n] = ut_b
        decf_s[n] = dec_f
        decb_s[n] = dec_b

    st_f = ut_m
    st_b = jnp.zeros((hd, hd), F32)
    for n in range(N_CHUNKS):
        rf = n * CHUNK
        o_s[rf:rf + CHUNK, :] += lax.dot_general(qef_s[rf:rf + CHUNK, :], st_f.astype(BF16), NT_DIMS,
                                                 preferred_element_type=F32)
        st_f = decf_s[n] * st_f + utf_s[n]
        m = N_CHUNKS - 1 - n
        rb = m * CHUNK
        o_s[rb:rb + CHUNK, :] += lax.dot_general(qeb_s[rb:rb + CHUNK, :], st_b.astype(BF16), NT_DIMS,
                                                 preferred_element_type=F32)
        st_b = decb_s[m] * st_b + utb_s[m]

    nw = nw_ref[...]
    rb = 256
    for r0 in range(0, SEQ, rb):
        o = o_s[r0:r0 + rb, :]
        o = o * lax.rsqrt(jnp.mean(o * o, axis=-1, keepdims=True) + EPS)
        o_ref[0, r0:r0 + rb, :] = o * nw * jax.nn.silu(g_ref[0, r0:r0 + rb, :])


def _hgrn(phg_x, phg_m, lb_f, lb_b, norm_w):
    bsz = phg_x.shape[0]
    hd = HGRN_HEAD_DIM
    nh = HGRN_HEADS

    def part(j):
        return pl.BlockSpec((1, SEQ, hd), lambda b, h, j=j: (b, 0, j * nh + h))

    def mpart(j):
        return pl.BlockSpec((N_META, hd), lambda b, h, j=j: (0, j * nh + h))

    vec = pl.BlockSpec((1, hd), lambda b, h: (0, h))
    return pl.pallas_call(
        _hgrn_body,
        grid=(bsz, nh),
        in_specs=[part(0), part(1), part(2), part(3), part(4), mpart(1), mpart(3), vec, vec, vec],
        out_specs=pl.BlockSpec((1, SEQ, hd), lambda b, h: (b, 0, h)),
        out_shape=jax.ShapeDtypeStruct((bsz, SEQ, D_HGRN), F32),
        scratch_shapes=[pltpu.VMEM((SEQ, hd), F32),
                        pltpu.VMEM((SEQ, hd), BF16),
                        pltpu.VMEM((SEQ, hd), BF16),
                        pltpu.VMEM((N_CHUNKS, hd, hd), F32),
                        pltpu.VMEM((N_CHUNKS, hd, hd), F32),
                        pltpu.VMEM((N_CHUNKS, 1, hd), F32),
                        pltpu.VMEM((N_CHUNKS, 1, hd), F32)],
        compiler_params=pltpu.CompilerParams(
            dimension_semantics=("arbitrary", "arbitrary"), vmem_limit_bytes=40 * 1024 * 1024),
        name="hgrn",
    )(phg_x, phg_x, phg_x, phg_x, phg_x, phg_m, phg_m,
      lb_f.reshape(1, -1), lb_b.reshape(1, -1), norm_w.reshape(1, -1))


def _centred_short_conv(u, w, b):
    L = u.shape[1]
    half = SHORT_CONV // 2
    up = jnp.pad(u, ((0, 0), (half, SHORT_CONV - 1 - half), (0, 0)))
    y = b
    for j in range(SHORT_CONV):
        y = y + up[:, j:j + L] * w[j]
    return y


def _hyena_filters(L, w1, b1, w2, b2, w3, freq):
    pos = jnp.arange(L, dtype=F32)
    t = pos / max(L - 1, 1)
    bands = jnp.linspace(1e-4, FILTER_BANDS - 1, FILTER_BANDS, dtype=F32)
    ang = (2.0 * math.pi / L) * pos[:, None] * bands[None, :]
    z = jnp.concatenate([t[:, None], jnp.cos(ang), -jnp.sin(ang)], axis=-1)
    hp = lax.Precision.HIGHEST
    hid = jnp.sin(freq * (jnp.dot(z, w1, precision=hp) + b1))
    hid = jnp.sin(freq * (jnp.dot(hid, w2, precision=hp) + b2))
    filt = jnp.dot(hid, w3, precision=hp).reshape(L, 2, HYENA_ORDER, D_HYENA)
    deltas = jnp.abs(jnp.linspace(math.log(DECAY_TARGET) / SLOW_DECAY_PCT,
                                  math.log(DECAY_TARGET) / FAST_DECAY_PCT, D_HYENA, dtype=F32))
    window = jnp.exp(-t[:, None] * deltas[None, :])
    filt = filt * window[:, None, None, :]
    return filt[:, 0], filt[:, 1]


def _hyena_jnp(p, conv_w, conv_b, w1, b1, w2, b2, w3, freq, skip):
    L = p.shape[1]
    u = _centred_short_conv(p, conv_w, conv_b)
    z = u[..., :D_HYENA]
    h_fwd, h_bwd = _hyena_filters(L, w1, b1, w2, b2, w3, freq)
    two_sided = jnp.concatenate([h_fwd, jnp.zeros_like(h_fwd[:1]), h_bwd[:0:-1]], axis=0)
    spectra = jnp.fft.rfft(two_sided, axis=0)
    for n in range(HYENA_ORDER):
        gate = u[..., (n + 1) * D_HYENA:(n + 2) * D_HYENA]
        y = jnp.fft.irfft(jnp.fft.rfft(z, n=2 * L, axis=1) * spectra[None, :, n], n=2 * L, axis=1)[:, :L]
        z = gate * (y + z * skip[n])
    return z


def _scan_jnp(q, k, v, log_f):
    B, T, H, DK = q.shape
    DV = v.shape[-1]
    N = T // CHUNK
    q, k, v, log_f = [a.reshape(B, N, CHUNK, H, a.shape[-1]) for a in (q, k, v, log_f)]
    b = jnp.cumsum(log_f, axis=2)
    b_last = b[:, :, -1]
    b_mid = b[:, :, CHUNK // 2][:, :, None]
    scores = jnp.einsum('bnthd,bnshd->bnhts', q * jnp.exp(b - b_mid), k * jnp.exp(b_mid - b))
    lower_tri = jnp.tril(jnp.ones((CHUNK, CHUNK), dtype=bool))
    scores = jnp.where(lower_tri, scores, 0.0)
    o_intra = jnp.einsum('bnhts,bnshv->bnthv', scores, v)
    chunk_state = jnp.einsum('bnshd,bnshv->bnhdv', k * jnp.exp(b_last[:, :, None] - b), v)
    chunk_decay = jnp.exp(b_last)

    def step(S, inp):
        dec, U = inp
        return dec[..., None] * S + U, S

    S0 = jnp.zeros((B, H, DK, DV), q.dtype)
    _, S_in = lax.scan(step, S0, (jnp.moveaxis(chunk_decay, 1, 0), jnp.moveaxis(chunk_state, 1, 0)))
    S_in = jnp.moveaxis(S_in, 0, 1)
    o_inter = jnp.einsum('bnthd,bnhdv->bnthv', q * jnp.exp(b), S_in)
    return (o_intra + o_inter).reshape(B, T, H, DV)


def _hgrn_jnp(p, lb_f, lb_b, norm_w):
    B, L, _ = p.shape
    q, f_fwd, f_bwd, i_in, g = [p[..., j * D_HGRN:(j + 1) * D_HGRN] for j in range(5)]

    def heads(a):
        return a.reshape(B, L, HGRN_HEADS, HGRN_HEAD_DIM)

    qh = heads(jax.nn.silu(q))
    vh = heads(i_in)

    def forget(logit, lb):
        f = lb + (1.0 - lb) * jax.nn.sigmoid(logit)
        return heads(1.0 - f), heads(jnp.log(f))

    k_f, lf_f = forget(f_fwd, lb_f)
    k_b, lf_b = forget(f_bwd, lb_b)
    n_pad = (-N_META) % CHUNK

    def pad(a):
        return jnp.pad(a, ((0, 0), (n_pad, 0), (0, 0), (0, 0)))

    def flip(a):
        return a[:, ::-1]

    qp, vp = pad(qh), pad(vh)
    o_f = _scan_jnp(qp, pad(k_f), vp, pad(lf_f))
    o_b = flip(_scan_jnp(flip(qp), flip(pad(k_b)), flip(vp), flip(pad(lf_b))))
    o = (o_f + o_b)[:, n_pad:]
    o = o * lax.rsqrt(jnp.mean(o * o, axis=-1, keepdims=True) + EPS)
    return o.reshape(B, L, D_HGRN) * norm_w * jax.nn.silu(g)


def _route_jnp(logits):
    n_tok = logits.shape[0]
    g_logits = logits[:, :N_GROUPS]
    g_sel = jnp.argmax(g_logits, axis=-1)
    p_group = jnp.take_along_axis(jax.nn.softmax(g_logits, axis=-1), g_sel[:, None], axis=-1)
    e_logits = logits[:, N_GROUPS:N_GROUPS + N_EXPERTS].reshape(n_tok, N_GROUPS, EXPERTS_PER_GROUP)
    e_logits = e_logits[jnp.arange(n_tok), g_sel]
    p_top, e_top = lax.top_k(jax.nn.softmax(e_logits, axis=-1), TOP_K)
    gate = p_group * p_top / jnp.sum(p_top, axis=-1, keepdims=True)
    expert_id = (g_sel[:, None] * EXPERTS_PER_GROUP + e_top).reshape(-1).astype(jnp.int32)
    return gate, expert_id


def kernel(x, meta_tokens, w_in, conv_w, conv_b, filt_w1, filt_b1, filt_w2, filt_b2, filt_w3,
           filt_freq, filt_skip, hyena_norm, lb_fwd, lb_bwd, hgrn_norm, w_out, norm_mix, norm_ffn,
           w_router_group, w_router_expert, w_gate, w_up, w_down, norm_final):
    B, S, D = x.shape
    L = S + N_META
    lbf = jnp.cumsum(jax.nn.softmax(lb_fwd, axis=0), axis=0)[0]
    lbb = jnp.cumsum(jax.nn.softmax(lb_bwd, axis=0), axis=0)[0]

    w_in_b = w_in[0].astype(BF16)
    xf = x.reshape(B * S, D)
    phy_x, phg_x = _inproj(xf, norm_mix[0], w_in_b, tm=512)
    phy_m, phg_m = _inproj(meta_tokens, norm_mix[0], w_in_b, tm=N_META)

    cs, ss = _dft_tables()
    h_fwd, h_bwd = _hyena_filters(L, filt_w1[0], filt_b1[0], filt_w2[0], filt_b2[0], filt_w3[0],
                                  filt_freq[0])
    hr, hi = _hyena_spectra(h_fwd, h_bwd, cs, ss)
    z_hy = _hyena(phy_x.reshape(B, S, D_HYENA_PROJ), phy_m, conv_w[0], conv_b[0], filt_skip[0],
                  cs.astype(BF16), ss.astype(BF16), hr, hi).reshape(B * S, D_HYENA)

    y_hg = _hgrn(phg_x.reshape(B, S, 5 * D_HGRN), phg_m, lbf, lbb,
                 hgrn_norm[0]).reshape(B * S, D_HGRN)

    w_r = jnp.concatenate([w_router_group[0], w_router_expert[0].reshape(D, N_EXPERTS),
                           jnp.zeros((D, LANES - N_GROUPS - N_EXPERTS), F32)], axis=1).astype(BF16)
    h1, ri, rg, cnt = _outproj(z_hy, y_hg, xf, hyena_norm[0], norm_ffn[0],
                               w_out[0].astype(BF16), w_r, tm=512)
    out = _moe(h1, ri, rg, cnt, norm_ffn[0], norm_final, w_gate[0], w_up[0], w_down[0])
    return out.reshape(B, S, D)
```

```python
import functools
import math

import jax
import jax.numpy as jnp
from jax import lax
from jax.experimental import pallas as pl
from jax.experimental.pallas import tpu as pltpu

D_MODEL = 1024
N_META = 16
D_HYENA = 512
D_HGRN = 512
HYENA_ORDER = 2
SHORT_CONV = 3
FILTER_EMB = 33
FILTER_BANDS = 16
DECAY_TARGET = 1e-2
FAST_DECAY_PCT = 0.3
SLOW_DECAY_PCT = 1.5
HGRN_HEAD_DIM = 128
HGRN_HEADS = D_HGRN // HGRN_HEAD_DIM
CHUNK = 64
N_GROUPS = 8
EXPERTS_PER_GROUP = 8
N_EXPERTS = 64
TOP_K = 2
D_EXPERT = 512
D_HYENA_PROJ = 3 * D_HYENA
D_IN_PROJ = D_HYENA_PROJ + 5 * D_HGRN
EPS = 1e-6

F32 = jnp.float32
BF16 = jnp.bfloat16


def _rms(x, gain):
    return x * lax.rsqrt(jnp.mean(x * x, axis=-1, keepdims=True) + EPS) * gain


def _inproj_body(x_ref, g_ref, w_ref, hy_ref, hg_ref, *, tn):
    a = _rms(x_ref[...], g_ref[...]).astype(BF16)
    for j in range(D_IN_PROJ // tn):
        acc = jnp.dot(a, w_ref[:, j * tn:(j + 1) * tn], preferred_element_type=F32)
        if j * tn < D_HYENA_PROJ:
            hy_ref[:, j * tn:(j + 1) * tn] = acc.astype(hy_ref.dtype)
        else:
            c0 = j * tn - D_HYENA_PROJ
            hg_ref[:, c0:c0 + tn] = acc.astype(hg_ref.dtype)


def _inproj(x, gain, w_bf16, tm):
    m, d = x.shape
    n_hg = D_IN_PROJ - D_HYENA_PROJ
    return pl.pallas_call(
        functools.partial(_inproj_body, tn=512),
        grid=(m // tm,),
        in_specs=[pl.BlockSpec((tm, d), lambda i: (i, 0)),
                  pl.BlockSpec((1, d), lambda i: (0, 0)),
                  pl.BlockSpec((d, D_IN_PROJ), lambda i: (0, 0))],
        out_specs=[pl.BlockSpec((tm, D_HYENA_PROJ), lambda i: (i, 0)),
                   pl.BlockSpec((tm, n_hg), lambda i: (i, 0))],
        out_shape=[jax.ShapeDtypeStruct((m, D_HYENA_PROJ), BF16),
                   jax.ShapeDtypeStruct((m, n_hg), F32)],
        compiler_params=pltpu.CompilerParams(
            dimension_semantics=("arbitrary",), vmem_limit_bytes=48 * 1024 * 1024),
        name="inproj",
    )(x, gain.reshape(1, d), w_bf16)


LANES = 128
NEG_BIG = -1e30


def _outproj_body(zhy_ref, yhg_ref, h0_ref, ghy_ref, gffn_ref, wo_ref, wr_ref,
                  h1_ref, ri_ref, rg_ref, cnt_ref, tri_s, carry_s):
    i = pl.program_id(0)
    tm = h1_ref.shape[0]

    @pl.when(i == 0)
    def _():
        r = lax.broadcasted_iota(jnp.int32, (tm, tm), 0)
        c = lax.broadcasted_iota(jnp.int32, (tm, tm), 1)
        tri_s[...] = (r > c).astype(BF16)
        carry_s[...] = jnp.zeros_like(carry_s)

    yhy = _rms(zhy_ref[...], ghy_ref[...]).astype(BF16)
    yhg = yhg_ref[...].astype(BF16)
    acc = jnp.dot(yhy, wo_ref[:D_HYENA, :], preferred_element_type=F32)
    acc = acc + jnp.dot(yhg, wo_ref[D_HYENA:, :], preferred_element_type=F32)
    h1 = h0_ref[...] + acc
    h1_ref[...] = h1
    a2b = _rms(h1, gffn_ref[...]).astype(BF16)
    lg = jnp.dot(a2b, wr_ref[...], preferred_element_type=F32)

    lane = lax.broadcasted_iota(jnp.int32, (tm, LANES), 1)
    is_g = lane < N_GROUPS
    gl = jnp.where(is_g, lg, NEG_BIG)
    gmax = jnp.max(gl, axis=1, keepdims=True)
    gsel = jnp.min(jnp.where(gl == gmax, lane, LANES), axis=1, keepdims=True)
    gden = jnp.sum(jnp.where(is_g, jnp.exp(gl - gmax), 0.0), axis=1, keepdims=True)
    p_group = 1.0 / gden
    in_grp = (lane >= N_GROUPS) & (lane < N_GROUPS + N_EXPERTS) & (
        ((lane - N_GROUPS) >> 3) == gsel)
    el = jnp.where(in_grp, lg, NEG_BIG)
    m1 = jnp.max(el, axis=1, keepdims=True)
    i1 = jnp.min(jnp.where(el == m1, lane, LANES), axis=1, keepdims=True)
    el2 = jnp.where(lane == i1, NEG_BIG, el)
    m2 = jnp.max(el2, axis=1, keepdims=True)
    i2 = jnp.min(jnp.where(el2 == m2, lane, LANES), axis=1, keepdims=True)
    r21 = jnp.exp(m2 - m1)
    gate1 = p_group / (1.0 + r21)
    gate2 = gate1 * r21

    hit1 = lane == i1
    hit2 = lane == i2
    onehot = (hit1 | hit2).astype(BF16)
    pre = jnp.dot(tri_s[...], onehot, preferred_element_type=F32) + carry_s[...]
    pos1 = jnp.sum(jnp.where(hit1, pre, 0.0), axis=1, keepdims=True).astype(jnp.int32)
    pos2 = jnp.sum(jnp.where(hit2, pre, 0.0), axis=1, keepdims=True).astype(jnp.int32)
    carry_s[...] += jnp.sum(onehot.astype(F32), axis=0, keepdims=True)
    cnt_ref[...] = carry_s[...]

    zero_i = jnp.zeros((tm, LANES), jnp.int32)
    ri_ref[...] = jnp.where(lane == 0, i1 - N_GROUPS,
                  jnp.where(lane == 1, i2 - N_GROUPS,
                  jnp.where(lane == 2, pos1, jnp.where(lane == 3, pos2, zero_i))))
    rg_ref[...] = jnp.where(lane == 0, gate1, jnp.where(lane == 1, gate2, 0.0))


def _outproj(zhy, yhg, h0, g_hy, g_ffn, wo_bf16, wr_bf16, tm):
    m = h0.shape[0]
    return pl.pallas_call(
        _outproj_body,
        grid=(m // tm,),
        in_specs=[pl.BlockSpec((tm, D_HYENA), lambda i: (i, 0)),
                  pl.BlockSpec((tm, D_HGRN), lambda i: (i, 0)),
                  pl.BlockSpec((tm, D_MODEL), lambda i: (i, 0)),
                  pl.BlockSpec((1, D_HYENA), lambda i: (0, 0)),
                  pl.BlockSpec((1, D_MODEL), lambda i: (0, 0)),
                  pl.BlockSpec((D_MODEL, D_MODEL), lambda i: (0, 0)),
                  pl.BlockSpec((D_MODEL, LANES), lambda i: (0, 0))],
        out_specs=[pl.BlockSpec((tm, D_MODEL), lambda i: (i, 0)),
                   pl.BlockSpec((tm, LANES), lambda i: (i, 0)),
                   pl.BlockSpec((tm, LANES), lambda i: (i, 0)),
                   pl.BlockSpec((1, LANES), lambda i: (0, 0))],
        out_shape=[jax.ShapeDtypeStruct((m, D_MODEL), F32),
                   jax.ShapeDtypeStruct((m, LANES), jnp.int32),
                   jax.ShapeDtypeStruct((m, LANES), F32),
                   jax.ShapeDtypeStruct((1, LANES), F32)],
        scratch_shapes=[pltpu.VMEM((tm, tm), BF16), pltpu.VMEM((1, LANES), F32)],
        compiler_params=pltpu.CompilerParams(
            dimension_semantics=("arbitrary",), vmem_limit_bytes=48 * 1024 * 1024),
        name="outproj",
    )(zhy, yhg, h0, g_hy.reshape(1, -1), g_ffn.reshape(1, -1), wo_bf16, wr_bf16)


EXP_TB = 256


def _row_copy(src, src_row, dst, dst_row, sem):
    return pltpu.make_async_copy(src.at[pl.ds(src_row, 1), :], dst.at[pl.ds(dst_row, 1), :], sem)


def _wait_rows(src, dst, n_rows, sem):
    pltpu.make_async_copy(src.at[pl.ds(0, n_rows), :], dst.at[pl.ds(0, n_rows), :], sem).wait()


def _dispatch_body(lastblk_ref, npad_ref, nused_ref, dest_ref, h1_ref, xb_ref, zero_s, sem_z, sem):
    i = pl.program_id(0)
    tm = h1_ref.shape[0]
    n_blocks = xb_ref.shape[0] // EXP_TB

    @pl.when(i == 0)
    def _():
        zero_s[...] = jnp.zeros_like(zero_s)

        def zero_copy(row0):
            row0 = pl.multiple_of(row0, EXP_TB)
            return pltpu.make_async_copy(zero_s, xb_ref.at[pl.ds(row0, EXP_TB), :], sem_z)

        for e in range(N_EXPERTS):
            @pl.when(npad_ref[e] > 0)
            def _():
                zero_copy(lastblk_ref[e]).start()
        for e in range(N_EXPERTS):
            @pl.when(npad_ref[e] > 0)
            def _():
                zero_copy(lastblk_ref[e]).wait()

        def start_tail(b, carry):
            zero_copy(b * EXP_TB).start()
            return carry

        def wait_tail(b, carry):
            zero_copy(b * EXP_TB).wait()
            return carry

        lax.fori_loop(nused_ref[0], n_blocks, start_tail, 0)
        lax.fori_loop(nused_ref[0], n_blocks, wait_tail, 0)

    def issue(t, carry):
        _row_copy(h1_ref, t, xb_ref, dest_ref[2 * t], sem).start()
        _row_copy(h1_ref, t, xb_ref, dest_ref[2 * t + 1], sem).start()
        return carry

    lax.fori_loop(0, tm, issue, 0, unroll=8)
    _wait_rows(h1_ref, xb_ref, tm, sem)
    _wait_rows(h1_ref, xb_ref, tm, sem)


def _dispatch(lastblk, npad, n_used, dest_flat, h1, n_slots, tm):
    m = h1.shape[0]
    grid_spec = pltpu.PrefetchScalarGridSpec(
        num_scalar_prefetch=3,
        grid=(m // tm,),
        in_specs=[pl.BlockSpec((TOP_K * tm,), lambda i, lb, npd, nu: (i,), memory_space=pltpu.SMEM),
                  pl.BlockSpec((tm, D_MODEL), lambda i, lb, npd, nu: (i, 0))],
        out_specs=pl.BlockSpec(memory_space=pl.ANY),
        scratch_shapes=[pltpu.VMEM((EXP_TB, D_MODEL), F32),
                        pltpu.SemaphoreType.DMA(()), pltpu.SemaphoreType.DMA(())],
    )
    return pl.pallas_call(
        _dispatch_body,
        grid_spec=grid_spec,
        out_shape=jax.ShapeDtypeStruct((n_slots, D_MODEL), F32),
        compiler_params=pltpu.CompilerParams(dimension_semantics=("arbitrary",)),
        name="dispatch",
    )(lastblk, npad, n_used, dest_flat, h1)


def _expert_body(eid_ref, nused_ref, xb_ref, gffn_ref, wg_ref, wu_ref, wd_ref, o_ref,
                 wg_s, wu_s, wd_s):
    i = pl.program_id(0)

    @pl.when(i < nused_ref[0])
    def _():
        prev = eid_ref[jnp.maximum(i - 1, 0)]

        @pl.when((i == 0) | (eid_ref[i] != prev))
        def _():
            wg_s[...] = wg_ref[0].astype(BF16)
            wu_s[...] = wu_ref[0].astype(BF16)
            wd_s[...] = wd_ref[0].astype(BF16)

        xb = _rms(xb_ref[...], gffn_ref[...]).astype(BF16)
        g = jnp.dot(xb, wg_s[...], preferred_element_type=F32)
        u = jnp.dot(xb, wu_s[...], preferred_element_type=F32)
        hmid = (g * jax.nn.sigmoid(g) * u).astype(BF16)
        o_ref[...] = jnp.dot(hmid, wd_s[...], preferred_element_type=F32)

    @pl.when(i >= nused_ref[0])
    def _():
        o_ref[...] = jnp.zeros_like(o_ref)


def _experts(block_eid, n_used, xb, g_ffn, w_gate, w_up, w_down):
    n_slots = xb.shape[0]

    def blk(i, e, nu):
        return jnp.minimum(i, nu[0] - 1)

    grid_spec = pltpu.PrefetchScalarGridSpec(
        num_scalar_prefetch=2,
        grid=(n_slots // EXP_TB,),
        in_specs=[pl.BlockSpec((EXP_TB, D_MODEL), lambda i, e, nu: (blk(i, e, nu), 0)),
                  pl.BlockSpec((1, D_MODEL), lambda i, e, nu: (0, 0)),
                  pl.BlockSpec((1, D_MODEL, D_EXPERT), lambda i, e, nu: (e[blk(i, e, nu)], 0, 0)),
                  pl.BlockSpec((1, D_MODEL, D_EXPERT), lambda i, e, nu: (e[blk(i, e, nu)], 0, 0)),
                  pl.BlockSpec((1, D_EXPERT, D_MODEL), lambda i, e, nu: (e[blk(i, e, nu)], 0, 0))],
        out_specs=pl.BlockSpec((EXP_TB, D_MODEL), lambda i, e, nu: (i, 0)),
        scratch_shapes=[pltpu.VMEM((D_MODEL, D_EXPERT), BF16),
                        pltpu.VMEM((D_MODEL, D_EXPERT), BF16),
                        pltpu.VMEM((D_EXPERT, D_MODEL), BF16)],
    )
    return pl.pallas_call(
        _expert_body,
        grid_spec=grid_spec,
        out_shape=jax.ShapeDtypeStruct((n_slots, D_MODEL), F32),
        compiler_params=pltpu.CompilerParams(
            dimension_semantics=("arbitrary",), vmem_limit_bytes=48 * 1024 * 1024),
        name="experts",
    )(block_eid, n_used, xb, g_ffn.reshape(1, -1), w_gate, w_up, w_down)


def _combine_body(dest_ref, h1_ref, rg_ref, gfin_ref, yb_ref, o_ref, y0_s, y1_s, sem):
    tm = h1_ref.shape[0]

    def issue(t, carry):
        _row_copy(yb_ref, dest_ref[2 * t], y0_s, t, sem).start()
        _row_copy(yb_ref, dest_ref[2 * t + 1], y1_s, t, sem).start()
        return carry

    lax.fori_loop(0, tm, issue, 0, unroll=8)
    _wait_rows(yb_ref, y0_s, tm, sem)
    _wait_rows(yb_ref, y1_s, tm, sem)
    rg = rg_ref[...]
    h2 = h1_ref[...] + rg[:, 0:1] * y0_s[...] + rg[:, 1:2] * y1_s[...]
    o_ref[...] = _rms(h2, gfin_ref[...])


def _combine(dest_flat, h1, rg, g_fin, yb, tm):
    m = h1.shape[0]
    return pl.pallas_call(
        _combine_body,
        grid=(m // tm,),
        in_specs=[pl.BlockSpec((TOP_K * tm,), lambda i: (i,), memory_space=pltpu.SMEM),
                  pl.BlockSpec((tm, D_MODEL), lambda i: (i, 0)),
                  pl.BlockSpec((tm, LANES), lambda i: (i, 0)),
                  pl.BlockSpec((1, D_MODEL), lambda i: (0, 0)),
                  pl.BlockSpec(memory_space=pl.ANY)],
        out_specs=pl.BlockSpec((tm, D_MODEL), lambda i: (i, 0)),
        out_shape=jax.ShapeDtypeStruct((m, D_MODEL), F32),
        scratch_shapes=[pltpu.VMEM((tm, D_MODEL), F32), pltpu.VMEM((tm, D_MODEL), F32),
                        pltpu.SemaphoreType.DMA(())],
        compiler_params=pltpu.CompilerParams(dimension_semantics=("arbitrary",)),
        name="combine",
    )(dest_flat, h1, rg, g_fin.reshape(1, -1), yb)


def _moe(h1, ri, rg, cnt, g_ffn, g_fin, w_gate, w_up, w_down):
    m = h1.shape[0]
    n_blocks = TOP_K * m // EXP_TB + N_EXPERTS
    n_slots = n_blocks * EXP_TB
    counts = cnt[0, N_GROUPS:N_GROUPS + N_EXPERTS].astype(jnp.int32)
    padded = (counts + EXP_TB - 1) // EXP_TB * EXP_TB
    pend = jnp.cumsum(padded)
    base = pend - padded
    eid = ri[:, 0:TOP_K]
    sel = eid[:, :, None] == jnp.arange(N_EXPERTS, dtype=jnp.int32)[None, None, :]
    dest = jnp.sum(jnp.where(sel, base[None, None, :], 0), axis=-1) + ri[:, TOP_K:2 * TOP_K]
    dest_flat = dest.reshape(-1).astype(jnp.int32)
    blk_start = jnp.arange(n_blocks, dtype=jnp.int32) * EXP_TB
    block_eid = jnp.minimum(jnp.sum(blk_start[:, None] >= pend[None, :], axis=1),
                            N_EXPERTS - 1).astype(jnp.int32)
    n_used = (pend[-1:] // EXP_TB).astype(jnp.int32)
    lastblk = (pend - EXP_TB).astype(jnp.int32)

    xb = _dispatch(lastblk, padded.astype(jnp.int32), n_used, dest_flat, h1, n_slots, tm=512)
    yb = _experts(block_eid, n_used, xb, g_ffn, w_gate, w_up, w_down)
    return _combine(dest_flat, h1, rg, g_fin, yb, tm=256)


ND = 2176
NCIRC = 2 * ND
SEQ = 2048
L_TOT = SEQ + N_META
HY_CT = 256
HY_FB = 544
HY_TB = 688
HALO = 8


def _dft_tables():
    k = jnp.arange(ND, dtype=jnp.int32)
    a = 2 * k + 1
    t_hi = jnp.arange(ND // 64, dtype=jnp.int32) * 64
    t_lo = jnp.arange(64, dtype=jnp.int32)
    m_hi = (a[:, None] * t_hi[None, :]) % (2 * NCIRC)
    m_lo = (a[:, None] * (2 * t_lo + 1)[None, :]) % (4 * NCIRC)
    ang_hi = m_hi.astype(F32) * (math.pi / NCIRC)
    ang_lo = m_lo.astype(F32) * (math.pi / (2 * NCIRC))
    c_hi, s_hi = jnp.cos(ang_hi)[:, :, None], jnp.sin(ang_hi)[:, :, None]
    c_lo, s_lo = jnp.cos(ang_lo)[:, None, :], jnp.sin(ang_lo)[:, None, :]
    cs = (c_hi * c_lo - s_hi * s_lo).reshape(ND, ND)
    ss = (s_hi * c_lo + c_hi * s_lo).reshape(ND, ND)
    return cs, ss


def _hyena_body(pv_ref, px1_ref, px2_ref, mv_ref, mx1_ref, mx2_ref,
                wv_ref, wx1_ref, wx2_ref, bv_ref, bx1_ref, bx2_ref, skip_ref,
                cs_ref, ss_ref, hr_ref, hi_ref, o_ref,
                stage_s, z_s, g1_s, g2_s, zb_s, yr_s, yi_s):
    ct = o_ref.shape[-1]

    def short_conv(p_ref, m_ref, w_ref, b_ref, dst_ref):
        stage_s[0:HALO, :] = jnp.zeros((HALO, ct), F32)
        stage_s[HALO:HALO + N_META, :] = m_ref[...].astype(F32)
        stage_s[HALO + N_META:HALO + L_TOT, :] = p_ref[0].astype(F32)
        stage_s[HALO + L_TOT:HALO + L_TOT + HALO, :] = jnp.zeros((HALO, ct), F32)
        w = w_ref[...]
        b = b_ref[...]
        rb = 344
        for r0 in range(0, L_TOT, rb):
            prev = stage_s[HALO - 1 + r0:HALO - 1 + r0 + rb, :]
            cur = stage_s[HALO + r0:HALO + r0 + rb, :]
            nxt = stage_s[HALO + 1 + r0:HALO + 1 + r0 + rb, :]
            dst_ref[r0:r0 + rb, :] = b + prev * w[0:1] + cur * w[1:2] + nxt * w[2:3]

    short_conv(pv_ref, mv_ref, wv_ref, bv_ref, z_s)
    short_conv(px1_ref, mx1_ref, wx1_ref, bx1_ref, g1_s)
    short_conv(px2_ref, mx2_ref, wx2_ref, bx2_ref, g2_s)
    z_s[L_TOT:ND, :] = jnp.zeros((ND - L_TOT, ct), F32)

    for n, g_s in enumerate((g1_s, g2_s)):
        zb_s[...] = z_s[...].astype(BF16)
        for k0 in range(0, ND, HY_FB):
            zb = zb_s[...]
            p = jnp.dot(cs_ref[k0:k0 + HY_FB, :], zb, preferred_element_type=F32)
            q = jnp.dot(ss_ref[k0:k0 + HY_FB, :], zb, preferred_element_type=F32)
            hr = hr_ref[n, k0:k0 + HY_FB, :]
            hi = hi_ref[n, k0:k0 + HY_FB, :]
            yr_s[k0:k0 + HY_FB, :] = (p * hr + q * hi).astype(BF16)
            yi_s[k0:k0 + HY_FB, :] = (q * hr - p * hi).astype(BF16)
        skip = skip_ref[n:n + 1, :]
        for t0 in range(0, L_TOT, HY_TB):
            y = jnp.dot(cs_ref[t0:t0 + HY_TB, :], yr_s[...], preferred_element_type=F32)
            y = y + jnp.dot(ss_ref[t0:t0 + HY_TB, :], yi_s[...], preferred_element_type=F32)
            z = z_s[t0:t0 + HY_TB, :]
            znew = g_s[t0:t0 + HY_TB, :] * (y + z * skip)
            if n == 0:
                z_s[t0:t0 + HY_TB, :] = znew
            elif t0 == 0:
                o_ref[0, 0:HY_TB - N_META, :] = znew[N_META:, :]
            else:
                o_ref[0, t0 - N_META:t0 - N_META + HY_TB, :] = znew


def _hyena(p_hy, pm_hy, conv_w, conv_b, skip, cs, ss, hr, hi):
    bsz = p_hy.shape[0]
    nct = D_HYENA // HY_CT
    ct = HY_CT

    def part(j):
        return [pl.BlockSpec((1, SEQ, ct), lambda c, b, j=j: (b, 0, j * nct + c))]

    def mpart(j):
        return [pl.BlockSpec((N_META, ct), lambda c, b, j=j: (0, j * nct + c))]

    def wpart(rows, j):
        return [pl.BlockSpec((rows, ct), lambda c, b, j=j: (0, j * nct + c))]

    in_specs = (part(0) + part(1) + part(2) + mpart(0) + mpart(1) + mpart(2)
                + wpart(3, 0) + wpart(3, 1) + wpart(3, 2)
                + wpart(1, 0) + wpart(1, 1) + wpart(1, 2)
                + [pl.BlockSpec((HYENA_ORDER, ct), lambda c, b: (0, c)),
                   pl.BlockSpec((ND, ND), lambda c, b: (0, 0)),
                   pl.BlockSpec((ND, ND), lambda c, b: (0, 0)),
                   pl.BlockSpec((HYENA_ORDER, ND, ct), lambda c, b: (0, 0, c),
                                pipeline_mode=pl.Buffered(1)),
                   pl.BlockSpec((HYENA_ORDER, ND, ct), lambda c, b: (0, 0, c),
                                pipeline_mode=pl.Buffered(1))])
    cb = conv_b.reshape(1, -1)
    return pl.pallas_call(
        _hyena_body,
        grid=(nct, bsz),
        in_specs=in_specs,
        out_specs=pl.BlockSpec((1, SEQ, ct), lambda c, b: (b, 0, c)),
        out_shape=jax.ShapeDtypeStruct((bsz, SEQ, D_HYENA), F32),
        scratch_shapes=[pltpu.VMEM((L_TOT + 2 * HALO, ct), F32),
                        pltpu.VMEM((ND, ct), F32),
                        pltpu.VMEM((L_TOT, ct), F32),
                        pltpu.VMEM((L_TOT, ct), F32),
                        pltpu.VMEM((ND, ct), BF16),
                        pltpu.VMEM((ND, ct), BF16),
                        pltpu.VMEM((ND, ct), BF16)],
        compiler_params=pltpu.CompilerParams(
            dimension_semantics=("arbitrary", "arbitrary"), vmem_limit_bytes=60 * 1024 * 1024),
        name="hyena",
    )(p_hy, p_hy, p_hy, pm_hy, pm_hy, pm_hy, conv_w, conv_w, conv_w, cb, cb, cb, skip,
      cs, ss, hr, hi)


def _hyena_spectra(h_fwd, h_bwd, cs, ss):
    L = h_fwd.shape[0]
    hbs = jnp.concatenate([h_bwd[1:], jnp.zeros_like(h_bwd[:1])], axis=0)
    hp = lax.Precision.HIGHEST
    a_r = jnp.einsum('kt,toc->okc', cs[:, :L], h_fwd + hbs, precision=hp)
    a_i = -jnp.einsum('kt,toc->okc', ss[:, :L], h_fwd - hbs, precision=hp)
    half = (2 * jnp.arange(ND, dtype=F32) + 1) * (math.pi / (2 * NCIRC))
    c, s = jnp.cos(half)[None, :, None], jnp.sin(half)[None, :, None]
    scale = 2.0 / NCIRC
    return (c * a_r - s * a_i) * scale, (s * a_r + c * a_i) * scale


N_CHUNKS = SEQ // CHUNK
NT_DIMS = (((1,), (1,)), ((), ()))
TN_DIMS = (((0,), (0,)), ((), ()))
MID_F = CHUNK // 2
MID_B = CHUNK // 2 - 1


def _split2(x):
    hi = x.astype(BF16)
    lo = (x - hi.astype(F32)).astype(BF16)
    return hi, lo


def _hgrn_body(q_ref, ff_ref, fb_ref, i_ref, g_ref, mff_ref, mi_ref,
               lbf_ref, lbb_ref, nw_ref, o_ref,
               qe_s, sc_s, ut_s, dec_s, st_s):
    hd = HGRN_HEAD_DIM
    row = lax.broadcasted_iota(jnp.int32, (CHUNK, CHUNK), 0)
    col = lax.broadcasted_iota(jnp.int32, (CHUNK, CHUNK), 1)
    lower = row >= col
    upper = col >= row
    tri_l = lower.astype(BF16)
    lbf = lbf_ref[...]
    lbb = lbb_ref[...]

    def forget(logit, lb):
        f = lb + (1.0 - lb) * jax.nn.sigmoid(logit)
        return 1.0 - f, jnp.log(f)

    def prefix_sums(lf_f, lf_b):
        parts = _split2(lf_f) + _split2(lf_b)
        s = jnp.dot(tri_l, jnp.concatenate(parts, axis=1), preferred_element_type=F32)
        return s[:, :hd] + s[:, hd:2 * hd], s[:, 2 * hd:3 * hd] + s[:, 3 * hd:]

    def scaled(b, k, qv, mid, last):
        b_mid = b[mid:mid + 1, :]
        b_last = b[last:last + 1, :]
        ks = k * jnp.exp(b_mid - b)
        kl = ks * jnp.exp(b_last - b_mid)
        dec = jnp.exp(b_last)
        if qv is None:
            return None, None, kl, None, dec
        qs = qv * jnp.exp(b - b_mid)
        return qs, ks, kl, qs * jnp.exp(b_mid), dec

    k_m, lf_m = forget(mff_ref[...], lbf)
    pad = jnp.zeros((CHUNK - N_META, hd), F32)
    lf_m = jnp.concatenate([pad, lf_m], axis=0)
    b_m, _ = prefix_sums(lf_m, lf_m)
    _, _, kl_m, _, _ = scaled(b_m, jnp.concatenate([pad, k_m], axis=0), None, MID_F, CHUNK - 1)
    v_m = jnp.concatenate([pad, mi_ref[...]], axis=0).astype(BF16)
    st_meta = lax.dot_general(v_m, kl_m.astype(BF16), TN_DIMS, preferred_element_type=F32)

    def phase_a(n, carry):
        r0 = pl.multiple_of(n * CHUNK, CHUNK)
        rows = pl.ds(r0, CHUNK)
        qv = jax.nn.silu(q_ref[0, rows, :])
        vb = i_ref[0, rows, :].astype(BF16)
        k_f, lf_f = forget(ff_ref[0, rows, :], lbf)
        k_b, lf_b = forget(fb_ref[0, rows, :], lbb)
        b_f, p_b = prefix_sums(lf_f, lf_b)
        c_b = p_b[CHUNK - 1:CHUNK, :] - p_b + lf_b
        qs_f, ks_f, kl_f, qe_f, dec_f = scaled(b_f, k_f, qv, MID_F, CHUNK - 1)
        qs_b, ks_b, kl_b, qe_b, dec_b = scaled(c_b, k_b, qv, MID_B, 0)
        sc_f = lax.dot_general(qs_f.astype(BF16), ks_f.astype(BF16), NT_DIMS,
                               preferred_element_type=F32)
        sc_b = lax.dot_general(qs_b.astype(BF16), ks_b.astype(BF16), NT_DIMS,
                               preferred_element_type=F32)
        sc_s[n] = (jnp.where(lower, sc_f, 0.0) + jnp.where(upper, sc_b, 0.0)).astype(BF16)
        qe_s[rows, :] = jnp.concatenate([qe_f, qe_b], axis=1).astype(BF16)
        kl = jnp.concatenate([kl_f, kl_b], axis=1).astype(BF16)
        ut_s[n] = lax.dot_general(vb, kl, TN_DIMS, preferred_element_type=F32)
        dec_s[n] = jnp.concatenate([dec_f, dec_b], axis=1)
        return carry

    lax.fori_loop(0, N_CHUNKS, phase_a, 0, unroll=2)

    st_f = st_meta
    st_b = jnp.zeros((hd, hd), F32)
    for n in range(N_CHUNKS):
        st_s[n, :, 0:hd] = st_f.astype(BF16)
        st_f = dec_s[n, :, 0:hd] * st_f + ut_s[n, :, 0:hd]
        m = N_CHUNKS - 1 - n
        st_s[m, :, hd:2 * hd] = st_b.astype(BF16)
        st_b = dec_s[m, :, hd:2 * hd] * st_b + ut_s[m, :, hd:2 * hd]

    nw = nw_ref[...]

    def phase_c(n, carry):
        r0 = pl.multiple_of(n * CHUNK, CHUNK)
        rows = pl.ds(r0, CHUNK)
        vb = i_ref[0, rows, :].astype(BF16)
        o = jnp.dot(sc_s[n], vb, preferred_element_type=F32)
        o = o + lax.dot_general(qe_s[rows, :], st_s[n], NT_DIMS, preferred_element_type=F32)
        o = o * lax.rsqrt(jnp.mean(o * o, axis=-1, keepdims=True) + EPS)
        o_ref[0, rows, :] = o * nw * jax.nn.silu(g_ref[0, rows, :])
        return carry

    lax.fori_loop(0, N_CHUNKS, phase_c, 0, unroll=2)


def _hgrn(phg_x, phg_m, lb_f, lb_b, norm_w):
    bsz = phg_x.shape[0]
    hd = HGRN_HEAD_DIM
    nh = HGRN_HEADS

    def part(j):
        return pl.BlockSpec((1, SEQ, hd), lambda b, h, j=j: (b, 0, j * nh + h))

    def mpart(j):
        return pl.BlockSpec((N_META, hd), lambda b, h, j=j: (0, j * nh + h))

    vec = pl.BlockSpec((1, hd), lambda b, h: (0, h))
    return pl.pallas_call(
        _hgrn_body,
        grid=(bsz, nh),
        in_specs=[part(0), part(1), part(2), part(3), part(4), mpart(1), mpart(3), vec, vec, vec],
        out_specs=pl.BlockSpec((1, SEQ, hd), lambda b, h: (b, 0, h)),
        out_shape=jax.ShapeDtypeStruct((bsz, SEQ, D_HGRN), F32),
        scratch_shapes=[pltpu.VMEM((SEQ, 2 * hd), BF16),
                        pltpu.VMEM((N_CHUNKS, CHUNK, CHUNK), BF16),
                        pltpu.VMEM((N_CHUNKS, hd, 2 * hd), F32),
                        pltpu.VMEM((N_CHUNKS, 1, 2 * hd), F32),
                        pltpu.VMEM((N_CHUNKS, hd, 2 * hd), BF16)],
        compiler_params=pltpu.CompilerParams(
            dimension_semantics=("arbitrary", "arbitrary"), vmem_limit_bytes=40 * 1024 * 1024),
        name="hgrn",
    )(phg_x, phg_x, phg_x, phg_x, phg_x, phg_m, phg_m,
      lb_f.reshape(1, -1), lb_b.reshape(1, -1), norm_w.reshape(1, -1))


def _hyena_filters(L, w1, b1, w2, b2, w3, freq):
    pos = jnp.arange(L, dtype=F32)
    t = pos / max(L - 1, 1)
    bands = jnp.linspace(1e-4, FILTER_BANDS - 1, FILTER_BANDS, dtype=F32)
    ang = (2.0 * math.pi / L) * pos[:, None] * bands[None, :]
    z = jnp.concatenate([t[:, None], jnp.cos(ang), -jnp.sin(ang)], axis=-1)
    hp = lax.Precision.HIGHEST
    hid = jnp.sin(freq * (jnp.dot(z, w1, precision=hp) + b1))
    hid = jnp.sin(freq * (jnp.dot(hid, w2, precision=hp) + b2))
    filt = jnp.dot(hid, w3, precision=hp).reshape(L, 2, HYENA_ORDER, D_HYENA)
    deltas = jnp.abs(jnp.linspace(math.log(DECAY_TARGET) / SLOW_DECAY_PCT,
                                  math.log(DECAY_TARGET) / FAST_DECAY_PCT, D_HYENA, dtype=F32))
    window = jnp.exp(-t[:, None] * deltas[None, :])
    filt = filt * window[:, None, None, :]
    return filt[:, 0], filt[:, 1]


def kernel(x, meta_tokens, w_in, conv_w, conv_b, filt_w1, filt_b1, filt_w2, filt_b2, filt_w3,
           filt_freq, filt_skip, hyena_norm, lb_fwd, lb_bwd, hgrn_norm, w_out, norm_mix, norm_ffn,
           w_router_group, w_router_expert, w_gate, w_up, w_down, norm_final):
    B, S, D = x.shape
    L = S + N_META
    lbf = jnp.cumsum(jax.nn.softmax(lb_fwd, axis=0), axis=0)[0]
    lbb = jnp.cumsum(jax.nn.softmax(lb_bwd, axis=0), axis=0)[0]

    w_in_b = w_in[0].astype(BF16)
    xf = x.reshape(B * S, D)
    phy_x, phg_x = _inproj(xf, norm_mix[0], w_in_b, tm=512)
    phy_m, phg_m = _inproj(meta_tokens, norm_mix[0], w_in_b, tm=N_META)

    cs, ss = _dft_tables()
    h_fwd, h_bwd = _hyena_filters(L, filt_w1[0], filt_b1[0], filt_w2[0], filt_b2[0], filt_w3[0],
                                  filt_freq[0])
    hr, hi = _hyena_spectra(h_fwd, h_bwd, cs, ss)
    z_hy = _hyena(phy_x.reshape(B, S, D_HYENA_PROJ), phy_m, conv_w[0], conv_b[0], filt_skip[0],
                  cs.astype(BF16), ss.astype(BF16), hr, hi).reshape(B * S, D_HYENA)

    y_hg = _hgrn(phg_x.reshape(B, S, 5 * D_HGRN), phg_m, lbf, lbb,
                 hgrn_norm[0]).reshape(B * S, D_HGRN)

    w_r = jnp.concatenate([w_router_group[0], w_router_expert[0].reshape(D, N_EXPERTS),
                           jnp.zeros((D, LANES - N_GROUPS - N_EXPERTS), F32)], axis=1).astype(BF16)
    h1, ri, rg, cnt = _outproj(z_hy, y_hg, xf, hyena_norm[0], norm_ffn[0],
                               w_out[0].astype(BF16), w_r, tm=512)
    out = _moe(h1, ri, rg, cnt, norm_ffn[0], norm_final, w_gate[0], w_up[0], w_down[0])
    return out.reshape(B, S, D)
```

```python
import functools
import math

import jax
import jax.numpy as jnp
from jax import lax
from jax.experimental import pallas as pl
from jax.experimental.pallas import tpu as pltpu

D_MODEL = 1024
N_META = 16
D_HYENA = 512
D_HGRN = 512
HYENA_ORDER = 2
SHORT_CONV = 3
FILTER_EMB = 33
FILTER_BANDS = 16
DECAY_TARGET = 1e-2
FAST_DECAY_PCT = 0.3
SLOW_DECAY_PCT = 1.5
HGRN_HEAD_DIM = 128
HGRN_HEADS = D_HGRN // HGRN_HEAD_DIM
CHUNK = 64
N_GROUPS = 8
EXPERTS_PER_GROUP = 8
N_EXPERTS = 64
TOP_K = 2
D_EXPERT = 512
D_HYENA_PROJ = 3 * D_HYENA
D_IN_PROJ = D_HYENA_PROJ + 5 * D_HGRN
EPS = 1e-6

F32 = jnp.float32
BF16 = jnp.bfloat16


def _rms(x, gain):
    return x * lax.rsqrt(jnp.mean(x * x, axis=-1, keepdims=True) + EPS) * gain


def _inproj_body(x_ref, g_ref, w_ref, hy_ref, hg_ref, *, tn):
    a = _rms(x_ref[...], g_ref[...]).astype(BF16)
    for j in range(D_IN_PROJ // tn):
        acc = jnp.dot(a, w_ref[:, j * tn:(j + 1) * tn], preferred_element_type=F32)
        if j * tn < D_HYENA_PROJ:
            hy_ref[:, j * tn:(j + 1) * tn] = acc.astype(hy_ref.dtype)
        else:
            c0 = j * tn - D_HYENA_PROJ
            hg_ref[:, c0:c0 + tn] = acc.astype(hg_ref.dtype)


def _inproj(x, gain, w_bf16, tm):
    m, d = x.shape
    n_hg = D_IN_PROJ - D_HYENA_PROJ
    return pl.pallas_call(
        functools.partial(_inproj_body, tn=512),
        grid=(m // tm,),
        in_specs=[pl.BlockSpec((tm, d), lambda i: (i, 0)),
                  pl.BlockSpec((1, d), lambda i: (0, 0)),
                  pl.BlockSpec((d, D_IN_PROJ), lambda i: (0, 0))],
        out_specs=[pl.BlockSpec((tm, D_HYENA_PROJ), lambda i: (i, 0)),
                   pl.BlockSpec((tm, n_hg), lambda i: (i, 0))],
        out_shape=[jax.ShapeDtypeStruct((m, D_HYENA_PROJ), BF16),
                   jax.ShapeDtypeStruct((m, n_hg), F32)],
        compiler_params=pltpu.CompilerParams(
            dimension_semantics=("arbitrary",), vmem_limit_bytes=48 * 1024 * 1024),
        name="inproj",
    )(x, gain.reshape(1, d), w_bf16)


LANES = 128
NEG_BIG = -1e30


def _outproj_body(zhy_ref, yhg_ref, h0_ref, ghy_ref, gffn_ref, wo_ref, wr_ref,
                  h1_ref, ri_ref, rg_ref, cnt_ref, tri_s, carry_s):
    i = pl.program_id(0)
    tm = h1_ref.shape[0]

    @pl.when(i == 0)
    def _():
        r = lax.broadcasted_iota(jnp.int32, (tm, tm), 0)
        c = lax.broadcasted_iota(jnp.int32, (tm, tm), 1)
        tri_s[...] = (r > c).astype(BF16)
        carry_s[...] = jnp.zeros_like(carry_s)

    yhy = _rms(zhy_ref[...], ghy_ref[...]).astype(BF16)
    yhg = yhg_ref[...].astype(BF16)
    acc = jnp.dot(yhy, wo_ref[:D_HYENA, :], preferred_element_type=F32)
    acc = acc + jnp.dot(yhg, wo_ref[D_HYENA:, :], preferred_element_type=F32)
    h1 = h0_ref[...] + acc
    h1_ref[...] = h1
    a2b = _rms(h1, gffn_ref[...]).astype(BF16)
    lg = jnp.dot(a2b, wr_ref[...], preferred_element_type=F32)

    lane = lax.broadcasted_iota(jnp.int32, (tm, LANES), 1)
    is_g = lane < N_GROUPS
    gl = jnp.where(is_g, lg, NEG_BIG)
    gmax = jnp.max(gl, axis=1, keepdims=True)
    gsel = jnp.min(jnp.where(gl == gmax, lane, LANES), axis=1, keepdims=True)
    gden = jnp.sum(jnp.where(is_g, jnp.exp(gl - gmax), 0.0), axis=1, keepdims=True)
    p_group = 1.0 / gden
    in_grp = (lane >= N_GROUPS) & (lane < N_GROUPS + N_EXPERTS) & (
        ((lane - N_GROUPS) >> 3) == gsel)
    el = jnp.where(in_grp, lg, NEG_BIG)
    m1 = jnp.max(el, axis=1, keepdims=True)
    i1 = jnp.min(jnp.where(el == m1, lane, LANES), axis=1, keepdims=True)
    el2 = jnp.where(lane == i1, NEG_BIG, el)
    m2 = jnp.max(el2, axis=1, keepdims=True)
    i2 = jnp.min(jnp.where(el2 == m2, lane, LANES), axis=1, keepdims=True)
    r21 = jnp.exp(m2 - m1)
    gate1 = p_group / (1.0 + r21)
    gate2 = gate1 * r21

    hit1 = lane == i1
    hit2 = lane == i2
    onehot = (hit1 | hit2).astype(BF16)
    pre = jnp.dot(tri_s[...], onehot, preferred_element_type=F32) + carry_s[...]
    pos1 = jnp.sum(jnp.where(hit1, pre, 0.0), axis=1, keepdims=True).astype(jnp.int32)
    pos2 = jnp.sum(jnp.where(hit2, pre, 0.0), axis=1, keepdims=True).astype(jnp.int32)
    carry_s[...] += jnp.sum(onehot.astype(F32), axis=0, keepdims=True)
    cnt_ref[...] = carry_s[...]

    zero_i = jnp.zeros((tm, LANES), jnp.int32)
    ri_ref[...] = jnp.where(lane == 0, i1 - N_GROUPS,
                  jnp.where(lane == 1, i2 - N_GROUPS,
                  jnp.where(lane == 2, pos1, jnp.where(lane == 3, pos2, zero_i))))
    rg_ref[...] = jnp.where(lane == 0, gate1, jnp.where(lane == 1, gate2, 0.0))


def _outproj(zhy, yhg, h0, g_hy, g_ffn, wo_bf16, wr_bf16, tm):
    m = h0.shape[0]
    return pl.pallas_call(
        _outproj_body,
        grid=(m // tm,),
        in_specs=[pl.BlockSpec((tm, D_HYENA), lambda i: (i, 0)),
                  pl.BlockSpec((tm, D_HGRN), lambda i: (i, 0)),
                  pl.BlockSpec((tm, D_MODEL), lambda i: (i, 0)),
                  pl.BlockSpec((1, D_HYENA), lambda i: (0, 0)),
                  pl.BlockSpec((1, D_MODEL), lambda i: (0, 0)),
                  pl.BlockSpec((D_MODEL, D_MODEL), lambda i: (0, 0)),
                  pl.BlockSpec((D_MODEL, LANES), lambda i: (0, 0))],
        out_specs=[pl.BlockSpec((tm, D_MODEL), lambda i: (i, 0)),
                   pl.BlockSpec((tm, LANES), lambda i: (i, 0)),
                   pl.BlockSpec((tm, LANES), lambda i: (i, 0)),
                   pl.BlockSpec((1, LANES), lambda i: (0, 0))],
        out_shape=[jax.ShapeDtypeStruct((m, D_MODEL), F32),
                   jax.ShapeDtypeStruct((m, LANES), jnp.int32),
                   jax.ShapeDtypeStruct((m, LANES), F32),
                   jax.ShapeDtypeStruct((1, LANES), F32)],
        scratch_shapes=[pltpu.VMEM((tm, tm), BF16), pltpu.VMEM((1, LANES), F32)],
        compiler_params=pltpu.CompilerParams(
            dimension_semantics=("arbitrary",), vmem_limit_bytes=48 * 1024 * 1024),
        name="outproj",
    )(zhy, yhg, h0, g_hy.reshape(1, -1), g_ffn.reshape(1, -1), wo_bf16, wr_bf16)


EXP_TB = 256


def _row_copy(src, src_row, dst, dst_row, sem):
    return pltpu.make_async_copy(src.at[pl.ds(src_row, 1), :], dst.at[pl.ds(dst_row, 1), :], sem)


def _wait_rows(src, dst, n_rows, sem):
    pltpu.make_async_copy(src.at[pl.ds(0, n_rows), :], dst.at[pl.ds(0, n_rows), :], sem).wait()


def _dispatch_body(lastblk_ref, npad_ref, nused_ref, dest_ref, h1_ref, xb_ref, zero_s, sem_z, sem):
    i = pl.program_id(0)
    tm = h1_ref.shape[0]
    n_blocks = xb_ref.shape[0] // EXP_TB

    @pl.when(i == 0)
    def _():
        zero_s[...] = jnp.zeros_like(zero_s)

        def zero_copy(row0):
            row0 = pl.multiple_of(row0, EXP_TB)
            return pltpu.make_async_copy(zero_s, xb_ref.at[pl.ds(row0, EXP_TB), :], sem_z)

        for e in range(N_EXPERTS):
            @pl.when(npad_ref[e] > 0)
            def _():
                zero_copy(lastblk_ref[e]).start()
        for e in range(N_EXPERTS):
            @pl.when(npad_ref[e] > 0)
            def _():
                zero_copy(lastblk_ref[e]).wait()

        def start_tail(b, carry):
            zero_copy(b * EXP_TB).start()
            return carry

        def wait_tail(b, carry):
            zero_copy(b * EXP_TB).wait()
            return carry

        lax.fori_loop(nused_ref[0], n_blocks, start_tail, 0)
        lax.fori_loop(nused_ref[0], n_blocks, wait_tail, 0)

    def issue(t, carry):
        _row_copy(h1_ref, t, xb_ref, dest_ref[2 * t], sem).start()
        _row_copy(h1_ref, t, xb_ref, dest_ref[2 * t + 1], sem).start()
        return carry

    lax.fori_loop(0, tm, issue, 0, unroll=8)
    _wait_rows(h1_ref, xb_ref, tm, sem)
    _wait_rows(h1_ref, xb_ref, tm, sem)


def _dispatch(lastblk, npad, n_used, dest_flat, h1, n_slots, tm):
    m = h1.shape[0]
    grid_spec = pltpu.PrefetchScalarGridSpec(
        num_scalar_prefetch=3,
        grid=(m // tm,),
        in_specs=[pl.BlockSpec((TOP_K * tm,), lambda i, lb, npd, nu: (i,), memory_space=pltpu.SMEM),
                  pl.BlockSpec((tm, D_MODEL), lambda i, lb, npd, nu: (i, 0))],
        out_specs=pl.BlockSpec(memory_space=pl.ANY),
        scratch_shapes=[pltpu.VMEM((EXP_TB, D_MODEL), F32),
                        pltpu.SemaphoreType.DMA(()), pltpu.SemaphoreType.DMA(())],
    )
    return pl.pallas_call(
        _dispatch_body,
        grid_spec=grid_spec,
        out_shape=jax.ShapeDtypeStruct((n_slots, D_MODEL), F32),
        compiler_params=pltpu.CompilerParams(dimension_semantics=("arbitrary",)),
        name="dispatch",
    )(lastblk, npad, n_used, dest_flat, h1)


def _expert_body(eid_ref, nused_ref, xb_ref, gffn_ref, wg_ref, wu_ref, wd_ref, o_ref,
                 wg_s, wu_s, wd_s):
    i = pl.program_id(0)

    @pl.when(i < nused_ref[0])
    def _():
        prev = eid_ref[jnp.maximum(i - 1, 0)]

        @pl.when((i == 0) | (eid_ref[i] != prev))
        def _():
            wg_s[...] = wg_ref[0].astype(BF16)
            wu_s[...] = wu_ref[0].astype(BF16)
            wd_s[...] = wd_ref[0].astype(BF16)

        xb = _rms(xb_ref[...], gffn_ref[...]).astype(BF16)
        g = jnp.dot(xb, wg_s[...], preferred_element_type=F32)
        u = jnp.dot(xb, wu_s[...], preferred_element_type=F32)
        hmid = (g * jax.nn.sigmoid(g) * u).astype(BF16)
        o_ref[...] = jnp.dot(hmid, wd_s[...], preferred_element_type=F32)

    @pl.when(i >= nused_ref[0])
    def _():
        o_ref[...] = jnp.zeros_like(o_ref)


def _experts(block_eid, n_used, xb, g_ffn, w_gate, w_up, w_down):
    n_slots = xb.shape[0]

    def blk(i, e, nu):
        return jnp.minimum(i, nu[0] - 1)

    grid_spec = pltpu.PrefetchScalarGridSpec(
        num_scalar_prefetch=2,
        grid=(n_slots // EXP_TB,),
        in_specs=[pl.BlockSpec((EXP_TB, D_MODEL), lambda i, e, nu: (blk(i, e, nu), 0)),
                  pl.BlockSpec((1, D_MODEL), lambda i, e, nu: (0, 0)),
                  pl.BlockSpec((1, D_MODEL, D_EXPERT), lambda i, e, nu: (e[blk(i, e, nu)], 0, 0)),
                  pl.BlockSpec((1, D_MODEL, D_EXPERT), lambda i, e, nu: (e[blk(i, e, nu)], 0, 0)),
                  pl.BlockSpec((1, D_EXPERT, D_MODEL), lambda i, e, nu: (e[blk(i, e, nu)], 0, 0))],
        out_specs=pl.BlockSpec((EXP_TB, D_MODEL), lambda i, e, nu: (i, 0)),
        scratch_shapes=[pltpu.VMEM((D_MODEL, D_EXPERT), BF16),
                        pltpu.VMEM((D_MODEL, D_EXPERT), BF16),
                        pltpu.VMEM((D_EXPERT, D_MODEL), BF16)],
    )
    return pl.pallas_call(
        _expert_body,
        grid_spec=grid_spec,
        out_shape=jax.ShapeDtypeStruct((n_slots, D_MODEL), F32),
        compiler_params=pltpu.CompilerParams(
            dimension_semantics=("arbitrary",), vmem_limit_bytes=48 * 1024 * 1024),
        name="experts",
    )(block_eid, n_used, xb, g_ffn.reshape(1, -1), w_gate, w_up, w_down)


def _combine_body(dest_ref, h1_ref, rg_ref, gfin_ref, yb_ref, o_ref, y0_s, y1_s, sem):
    tm = h1_ref.shape[0]

    def issue(t, carry):
        _row_copy(yb_ref, dest_ref[2 * t], y0_s, t, sem).start()
        _row_copy(yb_ref, dest_ref[2 * t + 1], y1_s, t, sem).start()
        return carry

    lax.fori_loop(0, tm, issue, 0, unroll=8)
    _wait_rows(yb_ref, y0_s, tm, sem)
    _wait_rows(yb_ref, y1_s, tm, sem)
    rg = rg_ref[...]
    h2 = h1_ref[...] + rg[:, 0:1] * y0_s[...] + rg[:, 1:2] * y1_s[...]
    o_ref[...] = _rms(h2, gfin_ref[...])


def _combine(dest_flat, h1, rg, g_fin, yb, tm):
    m = h1.shape[0]
    return pl.pallas_call(
        _combine_body,
        grid=(m // tm,),
        in_specs=[pl.BlockSpec((TOP_K * tm,), lambda i: (i,), memory_space=pltpu.SMEM),
                  pl.BlockSpec((tm, D_MODEL), lambda i: (i, 0)),
                  pl.BlockSpec((tm, LANES), lambda i: (i, 0)),
                  pl.BlockSpec((1, D_MODEL), lambda i: (0, 0)),
                  pl.BlockSpec(memory_space=pl.ANY)],
        out_specs=pl.BlockSpec((tm, D_MODEL), lambda i: (i, 0)),
        out_shape=jax.ShapeDtypeStruct((m, D_MODEL), F32),
        scratch_shapes=[pltpu.VMEM((tm, D_MODEL), F32), pltpu.VMEM((tm, D_MODEL), F32),
                        pltpu.SemaphoreType.DMA(())],
        compiler_params=pltpu.CompilerParams(dimension_semantics=("arbitrary",)),
        name="combine",
    )(dest_flat, h1, rg, g_fin.reshape(1, -1), yb)


def _moe(h1, ri, rg, cnt, g_ffn, g_fin, w_gate, w_up, w_down):
    m = h1.shape[0]
    n_blocks = TOP_K * m // EXP_TB + N_EXPERTS
    n_slots = n_blocks * EXP_TB
    counts = cnt[0, N_GROUPS:N_GROUPS + N_EXPERTS].astype(jnp.int32)
    padded = (counts + EXP_TB - 1) // EXP_TB * EXP_TB
    pend = jnp.cumsum(padded)
    base = pend - padded
    eid = ri[:, 0:TOP_K]
    sel = eid[:, :, None] == jnp.arange(N_EXPERTS, dtype=jnp.int32)[None, None, :]
    dest = jnp.sum(jnp.where(sel, base[None, None, :], 0), axis=-1) + ri[:, TOP_K:2 * TOP_K]
    dest_flat = dest.reshape(-1).astype(jnp.int32)
    blk_start = jnp.arange(n_blocks, dtype=jnp.int32) * EXP_TB
    block_eid = jnp.minimum(jnp.sum(blk_start[:, None] >= pend[None, :], axis=1),
                            N_EXPERTS - 1).astype(jnp.int32)
    n_used = (pend[-1:] // EXP_TB).astype(jnp.int32)
    lastblk = (pend - EXP_TB).astype(jnp.int32)

    xb = _dispatch(lastblk, padded.astype(jnp.int32), n_used, dest_flat, h1, n_slots, tm=512)
    yb = _experts(block_eid, n_used, xb, g_ffn, w_gate, w_up, w_down)
    return _combine(dest_flat, h1, rg, g_fin, yb, tm=256)


ND = 2176
NCIRC = 2 * ND
SEQ = 2048
L_TOT = SEQ + N_META
HY_CT = 256
HY_FB = 544
HY_TB = 688
HALO = 8


def _dft_tables():
    k = jnp.arange(ND, dtype=jnp.int32)
    a = 2 * k + 1
    t_hi = jnp.arange(ND // 64, dtype=jnp.int32) * 64
    t_lo = jnp.arange(64, dtype=jnp.int32)
    m_hi = (a[:, None] * t_hi[None, :]) % (2 * NCIRC)
    m_lo = (a[:, None] * (2 * t_lo + 1)[None, :]) % (4 * NCIRC)
    ang_hi = m_hi.astype(F32) * (math.pi / NCIRC)
    ang_lo = m_lo.astype(F32) * (math.pi / (2 * NCIRC))
    c_hi, s_hi = jnp.cos(ang_hi)[:, :, None], jnp.sin(ang_hi)[:, :, None]
    c_lo, s_lo = jnp.cos(ang_lo)[:, None, :], jnp.sin(ang_lo)[:, None, :]
    cs = (c_hi * c_lo - s_hi * s_lo).reshape(ND, ND)
    ss = (s_hi * c_lo + c_hi * s_lo).reshape(ND, ND)
    return cs, ss


def _hyena_body(pv_ref, px1_ref, px2_ref, mv_ref, mx1_ref, mx2_ref,
                wv_ref, wx1_ref, wx2_ref, bv_ref, bx1_ref, bx2_ref, skip_ref,
                cs_ref, ss_ref, hr_ref, hi_ref, o_ref,
                stage_s, z_s, g1_s, g2_s, zb_s, yr_s, yi_s):
    ct = o_ref.shape[-1]

    def short_conv(p_ref, m_ref, w_ref, b_ref, dst_ref):
        stage_s[0:HALO, :] = jnp.zeros((HALO, ct), F32)
        stage_s[HALO:HALO + N_META, :] = m_ref[...].astype(F32)
        stage_s[HALO + N_META:HALO + L_TOT, :] = p_ref[0].astype(F32)
        stage_s[HALO + L_TOT:HALO + L_TOT + HALO, :] = jnp.zeros((HALO, ct), F32)
        w = w_ref[...]
        b = b_ref[...]
        rb = 344
        for r0 in range(0, L_TOT, rb):
            prev = stage_s[HALO - 1 + r0:HALO - 1 + r0 + rb, :]
            cur = stage_s[HALO + r0:HALO + r0 + rb, :]
            nxt = stage_s[HALO + 1 + r0:HALO + 1 + r0 + rb, :]
            dst_ref[r0:r0 + rb, :] = b + prev * w[0:1] + cur * w[1:2] + nxt * w[2:3]

    short_conv(pv_ref, mv_ref, wv_ref, bv_ref, z_s)
    short_conv(px1_ref, mx1_ref, wx1_ref, bx1_ref, g1_s)
    short_conv(px2_ref, mx2_ref, wx2_ref, bx2_ref, g2_s)
    z_s[L_TOT:ND, :] = jnp.zeros((ND - L_TOT, ct), F32)

    for n, g_s in enumerate((g1_s, g2_s)):
        zb_s[...] = z_s[...].astype(BF16)
        for k0 in range(0, ND, HY_FB):
            zb = zb_s[...]
            p = jnp.dot(cs_ref[k0:k0 + HY_FB, :], zb, preferred_element_type=F32)
            q = jnp.dot(ss_ref[k0:k0 + HY_FB, :], zb, preferred_element_type=F32)
            hr = hr_ref[n, k0:k0 + HY_FB, :]
            hi = hi_ref[n, k0:k0 + HY_FB, :]
            yr_s[k0:k0 + HY_FB, :] = (p * hr + q * hi).astype(BF16)
            yi_s[k0:k0 + HY_FB, :] = (q * hr - p * hi).astype(BF16)
        skip = skip_ref[n:n + 1, :]
        for t0 in range(0, L_TOT, HY_TB):
            y = jnp.dot(cs_ref[t0:t0 + HY_TB, :], yr_s[...], preferred_element_type=F32)
            y = y + jnp.dot(ss_ref[t0:t0 + HY_TB, :], yi_s[...], preferred_element_type=F32)
            z = z_s[t0:t0 + HY_TB, :]
            znew = g_s[t0:t0 + HY_TB, :] * (y + z * skip)
            if n == 0:
                z_s[t0:t0 + HY_TB, :] = znew
            elif t0 == 0:
                o_ref[0, 0:HY_TB - N_META, :] = znew[N_META:, :]
            else:
                o_ref[0, t0 - N_META:t0 - N_META + HY_TB, :] = znew


def _hyena(p_hy, pm_hy, conv_w, conv_b, skip, cs, ss, hr, hi):
    bsz = p_hy.shape[0]
    nct = D_HYENA // HY_CT
    ct = HY_CT

    def part(j):
        return [pl.BlockSpec((1, SEQ, ct), lambda c, b, j=j: (b, 0, j * nct + c))]

    def mpart(j):
        return [pl.BlockSpec((N_META, ct), lambda c, b, j=j: (0, j * nct + c))]

    def wpart(rows, j):
        return [pl.BlockSpec((rows, ct), lambda c, b, j=j: (0, j * nct + c))]

    in_specs = (part(0) + part(1) + part(2) + mpart(0) + mpart(1) + mpart(2)
                + wpart(3, 0) + wpart(3, 1) + wpart(3, 2)
                + wpart(1, 0) + wpart(1, 1) + wpart(1, 2)
                + [pl.BlockSpec((HYENA_ORDER, ct), lambda c, b: (0, c)),
                   pl.BlockSpec((ND, ND), lambda c, b: (0, 0)),
                   pl.BlockSpec((ND, ND), lambda c, b: (0, 0)),
                   pl.BlockSpec((HYENA_ORDER, ND, ct), lambda c, b: (0, 0, c),
                                pipeline_mode=pl.Buffered(1)),
                   pl.BlockSpec((HYENA_ORDER, ND, ct), lambda c, b: (0, 0, c),
                                pipeline_mode=pl.Buffered(1))])
    cb = conv_b.reshape(1, -1)
    return pl.pallas_call(
        _hyena_body,
        grid=(nct, bsz),
        in_specs=in_specs,
        out_specs=pl.BlockSpec((1, SEQ, ct), lambda c, b: (b, 0, c)),
        out_shape=jax.ShapeDtypeStruct((bsz, SEQ, D_HYENA), F32),
        scratch_shapes=[pltpu.VMEM((L_TOT + 2 * HALO, ct), F32),
                        pltpu.VMEM((ND, ct), F32),
                        pltpu.VMEM((L_TOT, ct), F32),
                        pltpu.VMEM((L_TOT, ct), F32),
                        pltpu.VMEM((ND, ct), BF16),
                        pltpu.VMEM((ND, ct), BF16),
                        pltpu.VMEM((ND, ct), BF16)],
        compiler_params=pltpu.CompilerParams(
            dimension_semantics=("arbitrary", "arbitrary"), vmem_limit_bytes=60 * 1024 * 1024),
        name="hyena",
    )(p_hy, p_hy, p_hy, pm_hy, pm_hy, pm_hy, conv_w, conv_w, conv_w, cb, cb, cb, skip,
      cs, ss, hr, hi)


def _hyena_spectra(h_fwd, h_bwd, cs, ss):
    L = h_fwd.shape[0]
    hbs = jnp.concatenate([h_bwd[1:], jnp.zeros_like(h_bwd[:1])], axis=0)
    hp = lax.Precision.HIGHEST
    a_r = jnp.einsum('kt,toc->okc', cs[:, :L], h_fwd + hbs, precision=hp)
    a_i = -jnp.einsum('kt,toc->okc', ss[:, :L], h_fwd - hbs, precision=hp)
    half = (2 * jnp.arange(ND, dtype=F32) + 1) * (math.pi / (2 * NCIRC))
    c, s = jnp.cos(half)[None, :, None], jnp.sin(half)[None, :, None]
    scale = 2.0 / NCIRC
    return (c * a_r - s * a_i) * scale, (s * a_r + c * a_i) * scale


N_CHUNKS = SEQ // CHUNK
NT_DIMS = (((1,), (1,)), ((), ()))
TN_DIMS = (((0,), (0,)), ((), ()))
MID_F = CHUNK // 2
MID_B = CHUNK // 2 - 1


HG_G = 4
HG_ROWS = HG_G * CHUNK


def _split2(x):
    hi = x.astype(BF16)
    lo = (x - hi.astype(F32)).astype(BF16)
    return hi, lo


def _chunk_selectors(mid):
    r = lax.broadcasted_iota(jnp.int32, (3 * HG_ROWS, HG_ROWS), 0)
    c = lax.broadcasted_iota(jnp.int32, (3 * HG_ROWS, HG_ROWS), 1)
    blk = r // HG_ROWS
    rr = r - blk * HG_ROWS
    same = (rr // CHUNK) == (c // CHUNK)
    limit = jnp.where(blk == 0, rr % CHUNK, jnp.where(blk == 1, mid, CHUNK - 1))
    return (same & ((c % CHUNK) <= limit)).astype(BF16)


def _hgrn_body(q_ref, ff_ref, fb_ref, i_ref, g_ref, mff_ref, mi_ref,
               lbf_ref, lbb_ref, nw_ref, o_ref,
               self_s, selb_s, qe_s, sc_s, ut_s, dec_s, st_s):
    hd = HGRN_HEAD_DIM
    row = lax.broadcasted_iota(jnp.int32, (CHUNK, CHUNK), 0)
    col = lax.broadcasted_iota(jnp.int32, (CHUNK, CHUNK), 1)
    lower = row >= col
    upper = col >= row
    lbf = lbf_ref[...]
    lbb = lbb_ref[...]

    @pl.when((pl.program_id(0) == 0) & (pl.program_id(1) == 0))
    def _():
        self_s[...] = _chunk_selectors(MID_F)
        selb_s[...] = _chunk_selectors(MID_B - 1)

    def forget(logit, lb):
        f = lb + (1.0 - lb) * jax.nn.sigmoid(logit)
        return 1.0 - f, jnp.log(f)

    def selected_sums(sel_ref, lf):
        n = lf.shape[0]
        s = jnp.dot(sel_ref[...], jnp.concatenate(_split2(lf), axis=1), preferred_element_type=F32)
        s = s[:, :hd] + s[:, hd:]
        return s[0:n], s[n:2 * n], s[2 * n:3 * n]

    k_m, lf_m = forget(mff_ref[...], lbf)
    pad = jnp.zeros((CHUNK - N_META, hd), F32)
    lf_m = jnp.concatenate([pad, lf_m] * HG_G, axis=0)
    b_m, _, bl_m = selected_sums(self_s, lf_m)
    kl_m = jnp.concatenate([pad, k_m], axis=0) * jnp.exp(bl_m[0:CHUNK] - b_m[0:CHUNK])
    v_m = jnp.concatenate([pad, mi_ref[...]], axis=0).astype(BF16)
    st_meta = lax.dot_general(v_m, kl_m.astype(BF16), TN_DIMS, preferred_element_type=F32)

    def phase_a(j, carry):
        r0 = pl.multiple_of(j * HG_ROWS, HG_ROWS)
        rows = pl.ds(r0, HG_ROWS)
        qv = jax.nn.silu(q_ref[0, rows, :])
        vb = i_ref[0, rows, :].astype(BF16)
        k_f, lf_f = forget(ff_ref[0, rows, :], lbf)
        k_b, lf_b = forget(fb_ref[0, rows, :], lbb)
        b_f, bmid_f, blast_f = selected_sums(self_s, lf_f)
        p_b, pmid_b, tot_b = selected_sums(selb_s, lf_b)
        c_b = tot_b - p_b + lf_b
        cmid_b = tot_b - pmid_b
        d_f = b_f - bmid_f
        d_b = c_b - cmid_b
        qs_f = qv * jnp.exp(d_f)
        ks_f = k_f * jnp.exp(-d_f)
        qs_b = qv * jnp.exp(d_b)
        ks_b = k_b * jnp.exp(-d_b)
        qs_fb, ks_fb = qs_f.astype(BF16), ks_f.astype(BF16)
        qs_bb, ks_bb = qs_b.astype(BF16), ks_b.astype(BF16)
        for g in range(HG_G):
            sl = slice(g * CHUNK, (g + 1) * CHUNK)
            r1 = slice(g * CHUNK, g * CHUNK + 1)
            n = j * HG_G + g
            rows_g = pl.ds(pl.multiple_of(r0 + g * CHUNK, CHUNK), CHUNK)
            sc_f = lax.dot_general(qs_fb[sl], ks_fb[sl], NT_DIMS, preferred_element_type=F32)
            sc_b = lax.dot_general(qs_bb[sl], ks_bb[sl], NT_DIMS, preferred_element_type=F32)
            sc_s[n] = (jnp.where(lower, sc_f, 0.0) + jnp.where(upper, sc_b, 0.0)).astype(BF16)
            em_f = jnp.exp(bmid_f[r1])
            el_f = jnp.exp(blast_f[r1] - bmid_f[r1])
            em_b = jnp.exp(cmid_b[r1])
            el_b = jnp.exp(tot_b[r1] - cmid_b[r1])
            qe_s[rows_g, :] = jnp.concatenate(
                [qs_f[sl] * em_f, qs_b[sl] * em_b], axis=1).astype(BF16)
            kl = jnp.concatenate([ks_f[sl] * el_f, ks_b[sl] * el_b], axis=1).astype(BF16)
            ut_s[n] = lax.dot_general(vb[sl], kl, TN_DIMS, preferred_element_type=F32)
            dec_s[n] = jnp.concatenate([jnp.exp(blast_f[r1]), jnp.exp(tot_b[r1])], axis=1)
        return carry

    lax.fori_loop(0, N_CHUNKS // HG_G, phase_a, 0)

    st_f = st_meta
    st_b = jnp.zeros((hd, hd), F32)
    for n in range(N_CHUNKS):
        st_s[n, :, 0:hd] = st_f.astype(BF16)
        st_f = dec_s[n, :, 0:hd] * st_f + ut_s[n, :, 0:hd]
        m = N_CHUNKS - 1 - n
        st_s[m, :, hd:2 * hd] = st_b.astype(BF16)
        st_b = dec_s[m, :, hd:2 * hd] * st_b + ut_s[m, :, hd:2 * hd]

    nw = nw_ref[...]

    def phase_c(j, carry):
        r0 = pl.multiple_of(j * HG_ROWS, HG_ROWS)
        rows = pl.ds(r0, HG_ROWS)
        vb = i_ref[0, rows, :].astype(BF16)
        outs = []
        for g in range(HG_G):
            sl = slice(g * CHUNK, (g + 1) * CHUNK)
            n = j * HG_G + g
            rows_g = pl.ds(pl.multiple_of(r0 + g * CHUNK, CHUNK), CHUNK)
            o = jnp.dot(sc_s[n], vb[sl], preferred_element_type=F32)
            outs.append(o + lax.dot_general(qe_s[rows_g, :], st_s[n], NT_DIMS,
                                            preferred_element_type=F32))
        o = jnp.concatenate(outs, axis=0)
        o = o * lax.rsqrt(jnp.mean(o * o, axis=-1, keepdims=True) + EPS)
        o_ref[0, rows, :] = o * nw * jax.nn.silu(g_ref[0, rows, :])
        return carry

    lax.fori_loop(0, N_CHUNKS // HG_G, phase_c, 0)


def _hgrn(phg_x, phg_m, lb_f, lb_b, norm_w):
    bsz = phg_x.shape[0]
    hd = HGRN_HEAD_DIM
    nh = HGRN_HEADS

    def part(j):
        return pl.BlockSpec((1, SEQ, hd), lambda b, h, j=j: (b, 0, j * nh + h))

    def mpart(j):
        return pl.BlockSpec((N_META, hd), lambda b, h, j=j: (0, j * nh + h))

    vec = pl.BlockSpec((1, hd), lambda b, h: (0, h))
    return pl.pallas_call(
        _hgrn_body,
        grid=(bsz, nh),
        in_specs=[part(0), part(1), part(2), part(3), part(4), mpart(1), mpart(3), vec, vec, vec],
        out_specs=pl.BlockSpec((1, SEQ, hd), lambda b, h: (b, 0, h)),
        out_shape=jax.ShapeDtypeStruct((bsz, SEQ, D_HGRN), F32),
        scratch_shapes=[pltpu.VMEM((3 * HG_ROWS, HG_ROWS), BF16),
                        pltpu.VMEM((3 * HG_ROWS, HG_ROWS), BF16),
                        pltpu.VMEM((SEQ, 2 * hd), BF16),
                        pltpu.VMEM((N_CHUNKS, CHUNK, CHUNK), BF16),
                        pltpu.VMEM((N_CHUNKS, hd, 2 * hd), F32),
                        pltpu.VMEM((N_CHUNKS, 1, 2 * hd), F32),
                        pltpu.VMEM((N_CHUNKS, hd, 2 * hd), BF16)],
        compiler_params=pltpu.CompilerParams(
            dimension_semantics=("arbitrary", "arbitrary"), vmem_limit_bytes=40 * 1024 * 1024),
        name="hgrn",
    )(phg_x, phg_x, phg_x, phg_x, phg_x, phg_m, phg_m,
      lb_f.reshape(1, -1), lb_b.reshape(1, -1), norm_w.reshape(1, -1))


def _hyena_filters(L, w1, b1, w2, b2, w3, freq):
    pos = jnp.arange(L, dtype=F32)
    t = pos / max(L - 1, 1)
    bands = jnp.linspace(1e-4, FILTER_BANDS - 1, FILTER_BANDS, dtype=F32)
    ang = (2.0 * math.pi / L) * pos[:, None] * bands[None, :]
    z = jnp.concatenate([t[:, None], jnp.cos(ang), -jnp.sin(ang)], axis=-1)
    hp = lax.Precision.HIGHEST
    hid = jnp.sin(freq * (jnp.dot(z, w1, precision=hp) + b1))
    hid = jnp.sin(freq * (jnp.dot(hid, w2, precision=hp) + b2))
    filt = jnp.dot(hid, w3, precision=hp).reshape(L, 2, HYENA_ORDER, D_HYENA)
    deltas = jnp.abs(jnp.linspace(math.log(DECAY_TARGET) / SLOW_DECAY_PCT,
                                  math.log(DECAY_TARGET) / FAST_DECAY_PCT, D_HYENA, dtype=F32))
    window = jnp.exp(-t[:, None] * deltas[None, :])
    filt = filt * window[:, None, None, :]
    return filt[:, 0], filt[:, 1]


def kernel(x, meta_tokens, w_in, conv_w, conv_b, filt_w1, filt_b1, filt_w2, filt_b2, filt_w3,
           filt_freq, filt_skip, hyena_norm, lb_fwd, lb_bwd, hgrn_norm, w_out, norm_mix, norm_ffn,
           w_router_group, w_router_expert, w_gate, w_up, w_down, norm_final):
    B, S, D = x.shape
    L = S + N_META
    lbf = jnp.cumsum(jax.nn.softmax(lb_fwd, axis=0), axis=0)[0]
    lbb = jnp.cumsum(jax.nn.softmax(lb_bwd, axis=0), axis=0)[0]

    w_in_b = w_in[0].astype(BF16)
    xf = x.reshape(B * S, D)
    phy_x, phg_x = _inproj(xf, norm_mix[0], w_in_b, tm=512)
    phy_m, phg_m = _inproj(meta_tokens, norm_mix[0], w_in_b, tm=N_META)

    cs, ss = _dft_tables()
    h_fwd, h_bwd = _hyena_filters(L, filt_w1[0], filt_b1[0], filt_w2[0], filt_b2[0], filt_w3[0],
                                  filt_freq[0])
    hr, hi = _hyena_spectra(h_fwd, h_bwd, cs, ss)
    z_hy = _hyena(phy_x.reshape(B, S, D_HYENA_PROJ), phy_m, conv_w[0], conv_b[0], filt_skip[0],
                  cs.astype(BF16), ss.astype(BF16), hr, hi).reshape(B * S, D_HYENA)

    y_hg = _hgrn(phg_x.reshape(B, S, 5 * D_HGRN), phg_m, lbf, lbb,
                 hgrn_norm[0]).reshape(B * S, D_HGRN)

    w_r = jnp.concatenate([w_router_group[0], w_router_expert[0].reshape(D, N_EXPERTS),
                           jnp.zeros((D, LANES - N_GROUPS - N_EXPERTS), F32)], axis=1).astype(BF16)
    h1, ri, rg, cnt = _outproj(z_hy, y_hg, xf, hyena_norm[0], norm_ffn[0],
                               w_out[0].astype(BF16), w_r, tm=512)
    out = _moe(h1, ri, rg, cnt, norm_ffn[0], norm_final, w_gate[0], w_up[0], w_down[0])
    return out.reshape(B, S, D)
```

```python
import functools
import math

import jax
import jax.numpy as jnp
from jax import lax
from jax.experimental import pallas as pl
from jax.experimental.pallas import tpu as pltpu

D_MODEL = 1024
N_META = 16
D_HYENA = 512
D_HGRN = 512
HYENA_ORDER = 2
SHORT_CONV = 3
FILTER_EMB = 33
FILTER_BANDS = 16
DECAY_TARGET = 1e-2
FAST_DECAY_PCT = 0.3
SLOW_DECAY_PCT = 1.5
HGRN_HEAD_DIM = 128
HGRN_HEADS = D_HGRN // HGRN_HEAD_DIM
CHUNK = 64
N_GROUPS = 8
EXPERTS_PER_GROUP = 8
N_EXPERTS = 64
TOP_K = 2
D_EXPERT = 512
D_HYENA_PROJ = 3 * D_HYENA
D_IN_PROJ = D_HYENA_PROJ + 5 * D_HGRN
EPS = 1e-6

F32 = jnp.float32
BF16 = jnp.bfloat16


def _rms(x, gain):
    return x * lax.rsqrt(jnp.mean(x * x, axis=-1, keepdims=True) + EPS) * gain


def _inproj_body(x_ref, g_ref, w_ref, hy_ref, hg_ref, *, tn):
    a = _rms(x_ref[...], g_ref[...]).astype(BF16)
    for j in range(D_IN_PROJ // tn):
        acc = jnp.dot(a, w_ref[:, j * tn:(j + 1) * tn], preferred_element_type=F32)
        if j * tn < D_HYENA_PROJ:
            hy_ref[:, j * tn:(j + 1) * tn] = acc.astype(hy_ref.dtype)
        else:
            c0 = j * tn - D_HYENA_PROJ
            hg_ref[:, c0:c0 + tn] = acc.astype(hg_ref.dtype)


def _inproj(x, gain, w_bf16, tm):
    m, d = x.shape
    n_hg = D_IN_PROJ - D_HYENA_PROJ
    return pl.pallas_call(
        functools.partial(_inproj_body, tn=512),
        grid=(m // tm,),
        in_specs=[pl.BlockSpec((tm, d), lambda i: (i, 0)),
                  pl.BlockSpec((1, d), lambda i: (0, 0)),
                  pl.BlockSpec((d, D_IN_PROJ), lambda i: (0, 0))],
        out_specs=[pl.BlockSpec((tm, D_HYENA_PROJ), lambda i: (i, 0)),
                   pl.BlockSpec((tm, n_hg), lambda i: (i, 0))],
        out_shape=[jax.ShapeDtypeStruct((m, D_HYENA_PROJ), BF16),
                   jax.ShapeDtypeStruct((m, n_hg), F32)],
        compiler_params=pltpu.CompilerParams(
            dimension_semantics=("arbitrary",), vmem_limit_bytes=48 * 1024 * 1024),
        name="inproj",
    )(x, gain.reshape(1, d), w_bf16)


LANES = 128
NEG_BIG = -1e30


def _outproj_body(zhy_ref, yhg_ref, h0_ref, ghy_ref, gffn_ref, wo_ref, wr_ref,
                  h1_ref, ri_ref, rg_ref, cnt_ref, tri_s, carry_s):
    i = pl.program_id(0)
    tm = h1_ref.shape[0]

    @pl.when(i == 0)
    def _():
        r = lax.broadcasted_iota(jnp.int32, (tm, tm), 0)
        c = lax.broadcasted_iota(jnp.int32, (tm, tm), 1)
        tri_s[...] = (r > c).astype(BF16)
        carry_s[...] = jnp.zeros_like(carry_s)

    yhy = _rms(zhy_ref[...], ghy_ref[...]).astype(BF16)
    yhg = yhg_ref[...].astype(BF16)
    acc = jnp.dot(yhy, wo_ref[:D_HYENA, :], preferred_element_type=F32)
    acc = acc + jnp.dot(yhg, wo_ref[D_HYENA:, :], preferred_element_type=F32)
    h1 = h0_ref[...] + acc
    h1_ref[...] = h1
    a2b = _rms(h1, gffn_ref[...]).astype(BF16)
    lg = jnp.dot(a2b, wr_ref[...], preferred_element_type=F32)

    lane = lax.broadcasted_iota(jnp.int32, (tm, LANES), 1)
    is_g = lane < N_GROUPS
    gl = jnp.where(is_g, lg, NEG_BIG)
    gmax = jnp.max(gl, axis=1, keepdims=True)
    gsel = jnp.min(jnp.where(gl == gmax, lane, LANES), axis=1, keepdims=True)
    gden = jnp.sum(jnp.where(is_g, jnp.exp(gl - gmax), 0.0), axis=1, keepdims=True)
    p_group = 1.0 / gden
    in_grp = (lane >= N_GROUPS) & (lane < N_GROUPS + N_EXPERTS) & (
        ((lane - N_GROUPS) >> 3) == gsel)
    el = jnp.where(in_grp, lg, NEG_BIG)
    m1 = jnp.max(el, axis=1, keepdims=True)
    i1 = jnp.min(jnp.where(el == m1, lane, LANES), axis=1, keepdims=True)
    el2 = jnp.where(lane == i1, NEG_BIG, el)
    m2 = jnp.max(el2, axis=1, keepdims=True)
    i2 = jnp.min(jnp.where(el2 == m2, lane, LANES), axis=1, keepdims=True)
    r21 = jnp.exp(m2 - m1)
    gate1 = p_group / (1.0 + r21)
    gate2 = gate1 * r21

    hit1 = lane == i1
    hit2 = lane == i2
    onehot = (hit1 | hit2).astype(BF16)
    pre = jnp.dot(tri_s[...], onehot, preferred_element_type=F32) + carry_s[...]
    pos1 = jnp.sum(jnp.where(hit1, pre, 0.0), axis=1, keepdims=True).astype(jnp.int32)
    pos2 = jnp.sum(jnp.where(hit2, pre, 0.0), axis=1, keepdims=True).astype(jnp.int32)
    carry_s[...] += jnp.sum(onehot.astype(F32), axis=0, keepdims=True)
    cnt_ref[...] = carry_s[...]

    zero_i = jnp.zeros((tm, LANES), jnp.int32)
    ri_ref[...] = jnp.where(lane == 0, i1 - N_GROUPS,
                  jnp.where(lane == 1, i2 - N_GROUPS,
                  jnp.where(lane == 2, pos1, jnp.where(lane == 3, pos2, zero_i))))
    rg_ref[...] = jnp.where(lane == 0, gate1, jnp.where(lane == 1, gate2, 0.0))


def _outproj(zhy, yhg, h0, g_hy, g_ffn, wo_bf16, wr_bf16, tm):
    m = h0.shape[0]
    return pl.pallas_call(
        _outproj_body,
        grid=(m // tm,),
        in_specs=[pl.BlockSpec((tm, D_HYENA), lambda i: (i, 0)),
                  pl.BlockSpec((tm, D_HGRN), lambda i: (i, 0)),
                  pl.BlockSpec((tm, D_MODEL), lambda i: (i, 0)),
                  pl.BlockSpec((1, D_HYENA), lambda i: (0, 0)),
                  pl.BlockSpec((1, D_MODEL), lambda i: (0, 0)),
                  pl.BlockSpec((D_MODEL, D_MODEL), lambda i: (0, 0)),
                  pl.BlockSpec((D_MODEL, LANES), lambda i: (0, 0))],
        out_specs=[pl.BlockSpec((tm, D_MODEL), lambda i: (i, 0)),
                   pl.BlockSpec((tm, LANES), lambda i: (i, 0)),
                   pl.BlockSpec((tm, LANES), lambda i: (i, 0)),
                   pl.BlockSpec((1, LANES), lambda i: (0, 0))],
        out_shape=[jax.ShapeDtypeStruct((m, D_MODEL), F32),
                   jax.ShapeDtypeStruct((m, LANES), jnp.int32),
                   jax.ShapeDtypeStruct((m, LANES), F32),
                   jax.ShapeDtypeStruct((1, LANES), F32)],
        scratch_shapes=[pltpu.VMEM((tm, tm), BF16), pltpu.VMEM((1, LANES), F32)],
        compiler_params=pltpu.CompilerParams(
            dimension_semantics=("arbitrary",), vmem_limit_bytes=48 * 1024 * 1024),
        name="outproj",
    )(zhy, yhg, h0, g_hy.reshape(1, -1), g_ffn.reshape(1, -1), wo_bf16, wr_bf16)


EXP_TB = 256


def _row_copy(src, src_row, dst, dst_row, sem):
    return pltpu.make_async_copy(src.at[pl.ds(src_row, 1), :], dst.at[pl.ds(dst_row, 1), :], sem)


def _wait_rows(src, dst, n_rows, sem):
    pltpu.make_async_copy(src.at[pl.ds(0, n_rows), :], dst.at[pl.ds(0, n_rows), :], sem).wait()


def _dispatch_body(lastblk_ref, npad_ref, nused_ref, dest_ref, h1_ref, xb_ref, zero_s, sem_z, sem):
    i = pl.program_id(0)
    tm = dest_ref.shape[0] // TOP_K
    n_blocks = xb_ref.shape[0] // EXP_TB

    @pl.when(i == 0)
    def _():
        zero_s[...] = jnp.zeros_like(zero_s)

        def zero_copy(row0):
            row0 = pl.multiple_of(row0, EXP_TB)
            return pltpu.make_async_copy(zero_s, xb_ref.at[pl.ds(row0, EXP_TB), :], sem_z)

        for e in range(N_EXPERTS):
            @pl.when(npad_ref[e] > 0)
            def _():
                zero_copy(lastblk_ref[e]).start()
        for e in range(N_EXPERTS):
            @pl.when(npad_ref[e] > 0)
            def _():
                zero_copy(lastblk_ref[e]).wait()

        def start_tail(b, carry):
            zero_copy(b * EXP_TB).start()
            return carry

        def wait_tail(b, carry):
            zero_copy(b * EXP_TB).wait()
            return carry

        lax.fori_loop(nused_ref[0], n_blocks, start_tail, 0)
        lax.fori_loop(nused_ref[0], n_blocks, wait_tail, 0)

    row0 = i * tm

    def issue(t, carry):
        _row_copy(h1_ref, row0 + t, xb_ref, dest_ref[2 * t], sem).start()
        _row_copy(h1_ref, row0 + t, xb_ref, dest_ref[2 * t + 1], sem).start()
        return carry

    lax.fori_loop(0, tm, issue, 0, unroll=8)

    @pl.when(i > 0)
    def _():
        _wait_rows(h1_ref, xb_ref, TOP_K * tm, sem)

    @pl.when(i == pl.num_programs(0) - 1)
    def _():
        _wait_rows(h1_ref, xb_ref, TOP_K * tm, sem)


def _dispatch(lastblk, npad, n_used, dest_flat, h1, n_slots, tm):
    m = h1.shape[0]
    grid_spec = pltpu.PrefetchScalarGridSpec(
        num_scalar_prefetch=3,
        grid=(m // tm,),
        in_specs=[pl.BlockSpec((TOP_K * tm,), lambda i, lb, npd, nu: (i,), memory_space=pltpu.SMEM),
                  pl.BlockSpec(memory_space=pl.ANY)],
        out_specs=pl.BlockSpec(memory_space=pl.ANY),
        scratch_shapes=[pltpu.VMEM((EXP_TB, D_MODEL), F32),
                        pltpu.SemaphoreType.DMA(()), pltpu.SemaphoreType.DMA(())],
    )
    return pl.pallas_call(
        _dispatch_body,
        grid_spec=grid_spec,
        out_shape=jax.ShapeDtypeStruct((n_slots, D_MODEL), F32),
        compiler_params=pltpu.CompilerParams(dimension_semantics=("arbitrary",)),
        name="dispatch",
    )(lastblk, npad, n_used, dest_flat, h1)


def _expert_body(eid_ref, nused_ref, xb_ref, gffn_ref, wg_ref, wu_ref, wd_ref, o_ref,
                 wg_s, wu_s, wd_s):
    i = pl.program_id(0)

    @pl.when(i < nused_ref[0])
    def _():
        prev = eid_ref[jnp.maximum(i - 1, 0)]

        @pl.when((i == 0) | (eid_ref[i] != prev))
        def _():
            wg_s[...] = wg_ref[0].astype(BF16)
            wu_s[...] = wu_ref[0].astype(BF16)
            wd_s[...] = wd_ref[0].astype(BF16)

        xb = _rms(xb_ref[...], gffn_ref[...]).astype(BF16)
        g = jnp.dot(xb, wg_s[...], preferred_element_type=F32)
        u = jnp.dot(xb, wu_s[...], preferred_element_type=F32)
        hmid = (g * jax.nn.sigmoid(g) * u).astype(BF16)
        o_ref[...] = jnp.dot(hmid, wd_s[...], preferred_element_type=F32)

    @pl.when(i >= nused_ref[0])
    def _():
        o_ref[...] = jnp.zeros_like(o_ref)


def _experts(block_eid, n_used, xb, g_ffn, w_gate, w_up, w_down):
    n_slots = xb.shape[0]

    def blk(i, e, nu):
        return jnp.minimum(i, nu[0] - 1)

    grid_spec = pltpu.PrefetchScalarGridSpec(
        num_scalar_prefetch=2,
        grid=(n_slots // EXP_TB,),
        in_specs=[pl.BlockSpec((EXP_TB, D_MODEL), lambda i, e, nu: (blk(i, e, nu), 0)),
                  pl.BlockSpec((1, D_MODEL), lambda i, e, nu: (0, 0)),
                  pl.BlockSpec((1, D_MODEL, D_EXPERT), lambda i, e, nu: (e[blk(i, e, nu)], 0, 0)),
                  pl.BlockSpec((1, D_MODEL, D_EXPERT), lambda i, e, nu: (e[blk(i, e, nu)], 0, 0)),
                  pl.BlockSpec((1, D_EXPERT, D_MODEL), lambda i, e, nu: (e[blk(i, e, nu)], 0, 0))],
        out_specs=pl.BlockSpec((EXP_TB, D_MODEL), lambda i, e, nu: (i, 0)),
        scratch_shapes=[pltpu.VMEM((D_MODEL, D_EXPERT), BF16),
                        pltpu.VMEM((D_MODEL, D_EXPERT), BF16),
                        pltpu.VMEM((D_EXPERT, D_MODEL), BF16)],
    )
    return pl.pallas_call(
        _expert_body,
        grid_spec=grid_spec,
        out_shape=jax.ShapeDtypeStruct((n_slots, D_MODEL), F32),
        compiler_params=pltpu.CompilerParams(
            dimension_semantics=("arbitrary",), vmem_limit_bytes=48 * 1024 * 1024),
        name="experts",
    )(block_eid, n_used, xb, g_ffn.reshape(1, -1), w_gate, w_up, w_down)


def _combine_body(dest_ref, dnext_ref, h1_ref, rg_ref, gfin_ref, yb_ref, o_ref, y_s, sem):
    i = pl.program_id(0)
    tm = h1_ref.shape[0]
    slot = i % 2

    def issue(idx_ref, sl):
        def body(t, carry):
            _row_copy(yb_ref, idx_ref[2 * t], y_s.at[sl, 0], t, sem.at[sl]).start()
            _row_copy(yb_ref, idx_ref[2 * t + 1], y_s.at[sl, 1], t, sem.at[sl]).start()
            return carry

        lax.fori_loop(0, tm, body, 0, unroll=8)

    @pl.when(i == 0)
    def _():
        issue(dest_ref, 0)

    @pl.when(i + 1 < pl.num_programs(0))
    def _():
        issue(dnext_ref, 1 - slot)

    _wait_rows(yb_ref, y_s.at[slot, 0], tm, sem.at[slot])
    _wait_rows(yb_ref, y_s.at[slot, 1], tm, sem.at[slot])
    rg = rg_ref[...]
    h2 = h1_ref[...] + rg[:, 0:1] * y_s[slot, 0] + rg[:, 1:2] * y_s[slot, 1]
    o_ref[...] = _rms(h2, gfin_ref[...])


def _combine(dest_flat, h1, rg, g_fin, yb, tm):
    m = h1.shape[0]
    last = m // tm - 1
    return pl.pallas_call(
        _combine_body,
        grid=(m // tm,),
        in_specs=[pl.BlockSpec((TOP_K * tm,), lambda i: (i,), memory_space=pltpu.SMEM),
                  pl.BlockSpec((TOP_K * tm,), lambda i: (jnp.minimum(i + 1, last),),
                               memory_space=pltpu.SMEM),
                  pl.BlockSpec((tm, D_MODEL), lambda i: (i, 0)),
                  pl.BlockSpec((tm, LANES), lambda i: (i, 0)),
                  pl.BlockSpec((1, D_MODEL), lambda i: (0, 0)),
                  pl.BlockSpec(memory_space=pl.ANY)],
        out_specs=pl.BlockSpec((tm, D_MODEL), lambda i: (i, 0)),
        out_shape=jax.ShapeDtypeStruct((m, D_MODEL), F32),
        scratch_shapes=[pltpu.VMEM((2, TOP_K, tm, D_MODEL), F32),
                        pltpu.SemaphoreType.DMA((2,))],
        compiler_params=pltpu.CompilerParams(dimension_semantics=("arbitrary",)),
        name="combine",
    )(dest_flat, dest_flat, h1, rg, g_fin.reshape(1, -1), yb)


def _moe(h1, ri, rg, cnt, g_ffn, g_fin, w_gate, w_up, w_down):
    m = h1.shape[0]
    n_blocks = TOP_K * m // EXP_TB + N_EXPERTS
    n_slots = n_blocks * EXP_TB
    counts = cnt[0, N_GROUPS:N_GROUPS + N_EXPERTS].astype(jnp.int32)
    padded = (counts + EXP_TB - 1) // EXP_TB * EXP_TB
    pend = jnp.cumsum(padded)
    base = pend - padded
    eid = ri[:, 0:TOP_K]
    sel = eid[:, :, None] == jnp.arange(N_EXPERTS, dtype=jnp.int32)[None, None, :]
    dest = jnp.sum(jnp.where(sel, base[None, None, :], 0), axis=-1) + ri[:, TOP_K:2 * TOP_K]
    dest_flat = dest.reshape(-1).astype(jnp.int32)
    blk_start = jnp.arange(n_blocks, dtype=jnp.int32) * EXP_TB
    block_eid = jnp.minimum(jnp.sum(blk_start[:, None] >= pend[None, :], axis=1),
                            N_EXPERTS - 1).astype(jnp.int32)
    n_used = (pend[-1:] // EXP_TB).astype(jnp.int32)
    lastblk = (pend - EXP_TB).astype(jnp.int32)

    xb = _dispatch(lastblk, padded.astype(jnp.int32), n_used, dest_flat, h1, n_slots, tm=512)
    yb = _experts(block_eid, n_used, xb, g_ffn, w_gate, w_up, w_down)
    return _combine(dest_flat, h1, rg, g_fin, yb, tm=256)


ND = 2176
NCIRC = 2 * ND
SEQ = 2048
L_TOT = SEQ + N_META
HY_CT = 256
HY_FB = 544
HY_TB = 688
HALO = 8


def _dft_tables():
    k = jnp.arange(ND, dtype=jnp.int32)
    a = 2 * k + 1
    t_hi = jnp.arange(ND // LANES, dtype=jnp.int32) * LANES
    t_lo = jnp.arange(LANES, dtype=jnp.int32)
    m_hi = (a[:, None] * t_hi[None, :]) % (2 * NCIRC)
    m_lo = (a[:, None] * (2 * t_lo + 1)[None, :]) % (4 * NCIRC)
    ang_hi = m_hi.astype(F32) * (math.pi / NCIRC)
    ang_lo = m_lo.astype(F32) * (math.pi / (2 * NCIRC))
    c_hi, s_hi = jnp.cos(ang_hi)[:, :, None], jnp.sin(ang_hi)[:, :, None]
    c_lo, s_lo = jnp.cos(ang_lo)[:, None, :], jnp.sin(ang_lo)[:, None, :]
    cs = (c_hi * c_lo - s_hi * s_lo).reshape(ND, ND)
    ss = (s_hi * c_lo + c_hi * s_lo).reshape(ND, ND)
    return cs, ss


def _hyena_body(pv_ref, px1_ref, px2_ref, mv_ref, mx1_ref, mx2_ref,
                wv_ref, wx1_ref, wx2_ref, bv_ref, bx1_ref, bx2_ref, skip_ref,
                cs_ref, ss_ref, hr_ref, hi_ref, o_ref,
                stage_s, z_s, g1_s, g2_s, zb_s, yr_s, yi_s):
    ct = o_ref.shape[-1]

    def short_conv(p_ref, m_ref, w_ref, b_ref, dst_ref):
        stage_s[0:HALO, :] = jnp.zeros((HALO, ct), F32)
        stage_s[HALO:HALO + N_META, :] = m_ref[...].astype(F32)
        stage_s[HALO + N_META:HALO + L_TOT, :] = p_ref[0].astype(F32)
        stage_s[HALO + L_TOT:HALO + L_TOT + HALO, :] = jnp.zeros((HALO, ct), F32)
        w = w_ref[...]
        b = b_ref[...]
        rb = 344
        for r0 in range(0, L_TOT, rb):
            prev = stage_s[HALO - 1 + r0:HALO - 1 + r0 + rb, :]
            cur = stage_s[HALO + r0:HALO + r0 + rb, :]
            nxt = stage_s[HALO + 1 + r0:HALO + 1 + r0 + rb, :]
            dst_ref[r0:r0 + rb, :] = b + prev * w[0:1] + cur * w[1:2] + nxt * w[2:3]

    short_conv(pv_ref, mv_ref, wv_ref, bv_ref, z_s)
    short_conv(px1_ref, mx1_ref, wx1_ref, bx1_ref, g1_s)
    short_conv(px2_ref, mx2_ref, wx2_ref, bx2_ref, g2_s)
    z_s[L_TOT:ND, :] = jnp.zeros((ND - L_TOT, ct), F32)

    for n, g_s in enumerate((g1_s, g2_s)):
        zb_s[...] = z_s[...].astype(BF16)
        for k0 in range(0, ND, HY_FB):
            zb = zb_s[...]
            p = jnp.dot(cs_ref[k0:k0 + HY_FB, :], zb, preferred_element_type=F32)
            q = jnp.dot(ss_ref[k0:k0 + HY_FB, :], zb, preferred_element_type=F32)
            hr = hr_ref[n, k0:k0 + HY_FB, :]
            hi = hi_ref[n, k0:k0 + HY_FB, :]
            yr_s[k0:k0 + HY_FB, :] = (p * hr + q * hi).astype(BF16)
            yi_s[k0:k0 + HY_FB, :] = (q * hr - p * hi).astype(BF16)
        skip = skip_ref[n:n + 1, :]
        for t0 in range(0, L_TOT, HY_TB):
            y = jnp.dot(cs_ref[t0:t0 + HY_TB, :], yr_s[...], preferred_element_type=F32)
            y = y + jnp.dot(ss_ref[t0:t0 + HY_TB, :], yi_s[...], preferred_element_type=F32)
            z = z_s[t0:t0 + HY_TB, :]
            znew = g_s[t0:t0 + HY_TB, :] * (y + z * skip)
            if n == 0:
                z_s[t0:t0 + HY_TB, :] = znew
            elif t0 == 0:
                o_ref[0, 0:HY_TB - N_META, :] = znew[N_META:, :]
            else:
                o_ref[0, t0 - N_META:t0 - N_META + HY_TB, :] = znew


def _hyena(p_hy, pm_hy, conv_w, conv_b, skip, cs, ss, hr, hi):
    bsz = p_hy.shape[0]
    nct = D_HYENA // HY_CT
    ct = HY_CT

    def part(j):
        return [pl.BlockSpec((1, SEQ, ct), lambda c, b, j=j: (b, 0, j * nct + c))]

    def mpart(j):
        return [pl.BlockSpec((N_META, ct), lambda c, b, j=j: (0, j * nct + c))]

    def wpart(rows, j):
        return [pl.BlockSpec((rows, ct), lambda c, b, j=j: (0, j * nct + c))]

    in_specs = (part(0) + part(1) + part(2) + mpart(0) + mpart(1) + mpart(2)
                + wpart(3, 0) + wpart(3, 1) + wpart(3, 2)
                + wpart(1, 0) + wpart(1, 1) + wpart(1, 2)
                + [pl.BlockSpec((HYENA_ORDER, ct), lambda c, b: (0, c)),
                   pl.BlockSpec((ND, ND), lambda c, b: (0, 0)),
                   pl.BlockSpec((ND, ND), lambda c, b: (0, 0)),
                   pl.BlockSpec((HYENA_ORDER, ND, ct), lambda c, b: (0, 0, c),
                                pipeline_mode=pl.Buffered(1)),
                   pl.BlockSpec((HYENA_ORDER, ND, ct), lambda c, b: (0, 0, c),
                                pipeline_mode=pl.Buffered(1))])
    cb = conv_b.reshape(1, -1)
    return pl.pallas_call(
        _hyena_body,
        grid=(nct, bsz),
        in_specs=in_specs,
        out_specs=pl.BlockSpec((1, SEQ, ct), lambda c, b: (b, 0, c)),
        out_shape=jax.ShapeDtypeStruct((bsz, SEQ, D_HYENA), F32),
        scratch_shapes=[pltpu.VMEM((L_TOT + 2 * HALO, ct), F32),
                        pltpu.VMEM((ND, ct), F32),
                        pltpu.VMEM((L_TOT, ct), F32),
                        pltpu.VMEM((L_TOT, ct), F32),
                        pltpu.VMEM((ND, ct), BF16),
                        pltpu.VMEM((ND, ct), BF16),
                        pltpu.VMEM((ND, ct), BF16)],
        compiler_params=pltpu.CompilerParams(
            dimension_semantics=("arbitrary", "arbitrary"), vmem_limit_bytes=60 * 1024 * 1024),
        name="hyena",
    )(p_hy, p_hy, p_hy, pm_hy, pm_hy, pm_hy, conv_w, conv_w, conv_w, cb, cb, cb, skip,
      cs, ss, hr, hi)


def _hyena_spectra(h_fwd, h_bwd, cs, ss):
    L = h_fwd.shape[0]
    hbs = jnp.concatenate([h_bwd[1:], jnp.zeros_like(h_bwd[:1])], axis=0)
    hp = lax.Precision.HIGHEST
    a_r = jnp.einsum('kt,toc->okc', cs[:, :L], h_fwd + hbs, precision=hp)
    a_i = -jnp.einsum('kt,toc->okc', ss[:, :L], h_fwd - hbs, precision=hp)
    half = (2 * jnp.arange(ND, dtype=F32) + 1) * (math.pi / (2 * NCIRC))
    c, s = jnp.cos(half)[None, :, None], jnp.sin(half)[None, :, None]
    scale = 2.0 / NCIRC
    return (c * a_r - s * a_i) * scale, (s * a_r + c * a_i) * scale


N_CHUNKS = SEQ // CHUNK
NT_DIMS = (((1,), (1,)), ((), ()))
TN_DIMS = (((0,), (0,)), ((), ()))
MID_F = CHUNK // 2
MID_B = CHUNK // 2 - 1


HG_G = 8
HG_ROWS = HG_G * CHUNK


def _split2(x):
    hi = x.astype(BF16)
    lo = (x - hi.astype(F32)).astype(BF16)
    return hi, lo


def _chunk_prefix_matrix():
    r = lax.broadcasted_iota(jnp.int32, (HG_ROWS, HG_ROWS), 0)
    c = lax.broadcasted_iota(jnp.int32, (HG_ROWS, HG_ROWS), 1)
    return (((r // CHUNK) == (c // CHUNK)) & (c <= r)).astype(BF16)


def _chunk_rows(x, row):
    return jnp.concatenate(
        [jnp.broadcast_to(x[g * CHUNK + row:g * CHUNK + row + 1, :], (CHUNK, x.shape[1]))
         for g in range(x.shape[0] // CHUNK)], axis=0)


def _hgrn_body(q_ref, ff_ref, fb_ref, i_ref, g_ref, mff_ref, mi_ref,
               lbf_ref, lbb_ref, nw_ref, o_ref,
               tri_s, qe_s, sc_s, ut_s, dec_s, st_s):
    hd = HGRN_HEAD_DIM
    row = lax.broadcasted_iota(jnp.int32, (CHUNK, CHUNK), 0)
    col = lax.broadcasted_iota(jnp.int32, (CHUNK, CHUNK), 1)
    lower = row >= col
    upper = col >= row
    lbf = lbf_ref[...]
    lbb = lbb_ref[...]

    @pl.when((pl.program_id(0) == 0) & (pl.program_id(1) == 0))
    def _():
        tri_s[...] = _chunk_prefix_matrix()

    def forget(logit, lb):
        f = lb + (1.0 - lb) * jax.nn.sigmoid(logit)
        return 1.0 - f, jnp.log(f)

    def prefix_sums(lf):
        s = jnp.dot(tri_s[...], jnp.concatenate(_split2(lf), axis=1), preferred_element_type=F32)
        return s[:, :hd] + s[:, hd:]

    k_m, lf_m = forget(mff_ref[...], lbf)
    pad = jnp.zeros((CHUNK - N_META, hd), F32)
    lf_m = jnp.concatenate([pad, lf_m] * HG_G, axis=0)
    b_m = prefix_sums(lf_m)[0:CHUNK]
    kl_m = jnp.concatenate([pad, k_m], axis=0) * jnp.exp(b_m[CHUNK - 1:CHUNK] - b_m)
    v_m = jnp.concatenate([pad, mi_ref[...]], axis=0).astype(BF16)
    st_meta = lax.dot_general(v_m, kl_m.astype(BF16), TN_DIMS, preferred_element_type=F32)

    def phase_a(j, carry):
        r0 = pl.multiple_of(j * HG_ROWS, HG_ROWS)
        rows = pl.ds(r0, HG_ROWS)
        qv = jax.nn.silu(q_ref[0, rows, :])
        vb = i_ref[0, rows, :].astype(BF16)
        k_f, lf_f = forget(ff_ref[0, rows, :], lbf)
        k_b, lf_b = forget(fb_ref[0, rows, :], lbb)
        b_f = prefix_sums(lf_f)
        p_b = prefix_sums(lf_b)
        bmid_f = _chunk_rows(b_f, MID_F)
        blast_f = _chunk_rows(b_f, CHUNK - 1)
        tot_b = _chunk_rows(p_b, CHUNK - 1)
        c_b = tot_b - p_b + lf_b
        cmid_b = _chunk_rows(c_b, MID_B)
        d_f = b_f - bmid_f
        d_b = c_b - cmid_b
        e_f = jnp.exp(d_f)
        e_b = jnp.exp(d_b)
        qs_f = qv * e_f
        ks_f = k_f / e_f
        qs_b = qv * e_b
        ks_b = k_b / e_b
        qs_fb, ks_fb = qs_f.astype(BF16), ks_f.astype(BF16)
        qs_bb, ks_bb = qs_b.astype(BF16), ks_b.astype(BF16)
        for g in range(HG_G):
            sl = slice(g * CHUNK, (g + 1) * CHUNK)
            r1 = slice(g * CHUNK, g * CHUNK + 1)
            n = j * HG_G + g
            rows_g = pl.ds(pl.multiple_of(r0 + g * CHUNK, CHUNK), CHUNK)
            sc_f = lax.dot_general(qs_fb[sl], ks_fb[sl], NT_DIMS, preferred_element_type=F32)
            sc_b = lax.dot_general(qs_bb[sl], ks_bb[sl], NT_DIMS, preferred_element_type=F32)
            sc_s[n] = (jnp.where(lower, sc_f, 0.0) + jnp.where(upper, sc_b, 0.0)).astype(BF16)
            em_f = jnp.exp(bmid_f[r1])
            el_f = jnp.exp(blast_f[r1] - bmid_f[r1])
            em_b = jnp.exp(cmid_b[r1])
            el_b = jnp.exp(tot_b[r1] - cmid_b[r1])
            qe_s[rows_g, :] = jnp.concatenate(
                [qs_f[sl] * em_f, qs_b[sl] * em_b], axis=1).astype(BF16)
            kl = jnp.concatenate([ks_f[sl] * el_f, ks_b[sl] * el_b], axis=1).astype(BF16)
            ut_s[n] = lax.dot_general(vb[sl], kl, TN_DIMS, preferred_element_type=F32)
            dec_s[n] = jnp.concatenate([jnp.exp(blast_f[r1]), jnp.exp(tot_b[r1])], axis=1)
        return carry

    lax.fori_loop(0, N_CHUNKS // HG_G, phase_a, 0, unroll=True)

    st_f = st_meta
    st_b = jnp.zeros((hd, hd), F32)
    for n in range(N_CHUNKS):
        st_s[n, :, 0:hd] = st_f.astype(BF16)
        st_f = dec_s[n, :, 0:hd] * st_f + ut_s[n, :, 0:hd]
        m = N_CHUNKS - 1 - n
        st_s[m, :, hd:2 * hd] = st_b.astype(BF16)
        st_b = dec_s[m, :, hd:2 * hd] * st_b + ut_s[m, :, hd:2 * hd]

    nw = nw_ref[...]

    def phase_c(j, carry):
        r0 = pl.multiple_of(j * HG_ROWS, HG_ROWS)
        rows = pl.ds(r0, HG_ROWS)
        vb = i_ref[0, rows, :].astype(BF16)
        outs = []
        for g in range(HG_G):
            sl = slice(g * CHUNK, (g + 1) * CHUNK)
            n = j * HG_G + g
            rows_g = pl.ds(pl.multiple_of(r0 + g * CHUNK, CHUNK), CHUNK)
            o = jnp.dot(sc_s[n], vb[sl], preferred_element_type=F32)
            outs.append(o + lax.dot_general(qe_s[rows_g, :], st_s[n], NT_DIMS,
                                            preferred_element_type=F32))
        o = jnp.concatenate(outs, axis=0)
        o = o * lax.rsqrt(jnp.mean(o * o, axis=-1, keepdims=True) + EPS)
        o_ref[0, rows, :] = o * nw * jax.nn.silu(g_ref[0, rows, :])
        return carry

    lax.fori_loop(0, N_CHUNKS // HG_G, phase_c, 0, unroll=True)


def _hgrn(phg_x, phg_m, lb_f, lb_b, norm_w):
    bsz = phg_x.shape[0]
    hd = HGRN_HEAD_DIM
    nh = HGRN_HEADS

    def part(j):
        return pl.BlockSpec((1, SEQ, hd), lambda b, h, j=j: (b, 0, j * nh + h))

    def mpart(j):
        return pl.BlockSpec((N_META, hd), lambda b, h, j=j: (0, j * nh + h))

    vec = pl.BlockSpec((1, hd), lambda b, h: (0, h))
    return pl.pallas_call(
        _hgrn_body,
        grid=(bsz, nh),
        in_specs=[part(0), part(1), part(2), part(3), part(4), mpart(1), mpart(3), vec, vec, vec],
        out_specs=pl.BlockSpec((1, SEQ, hd), lambda b, h: (b, 0, h)),
        out_shape=jax.ShapeDtypeStruct((bsz, SEQ, D_HGRN), F32),
        scratch_shapes=[pltpu.VMEM((HG_ROWS, HG_ROWS), BF16),
                        pltpu.VMEM((SEQ, 2 * hd), BF16),
                        pltpu.VMEM((N_CHUNKS, CHUNK, CHUNK), BF16),
                        pltpu.VMEM((N_CHUNKS, hd, 2 * hd), F32),
                        pltpu.VMEM((N_CHUNKS, 1, 2 * hd), F32),
                        pltpu.VMEM((N_CHUNKS, hd, 2 * hd), BF16)],
        compiler_params=pltpu.CompilerParams(
            dimension_semantics=("arbitrary", "arbitrary"), vmem_limit_bytes=40 * 1024 * 1024),
        name="hgrn",
    )(phg_x, phg_x, phg_x, phg_x, phg_x, phg_m, phg_m,
      lb_f.reshape(1, -1), lb_b.reshape(1, -1), norm_w.reshape(1, -1))


def _hyena_filters(L, w1, b1, w2, b2, w3, freq):
    pos = jnp.arange(L, dtype=F32)
    t = pos / max(L - 1, 1)
    bands = jnp.linspace(1e-4, FILTER_BANDS - 1, FILTER_BANDS, dtype=F32)
    ang = (2.0 * math.pi / L) * pos[:, None] * bands[None, :]
    z = jnp.concatenate([t[:, None], jnp.cos(ang), -jnp.sin(ang)], axis=-1)
    hp = lax.Precision.HIGHEST
    hid = jnp.sin(freq * (jnp.dot(z, w1, precision=hp) + b1))
    hid = jnp.sin(freq * (jnp.dot(hid, w2, precision=hp) + b2))
    filt = jnp.dot(hid, w3, precision=hp).reshape(L, 2, HYENA_ORDER, D_HYENA)
    deltas = jnp.abs(jnp.linspace(math.log(DECAY_TARGET) / SLOW_DECAY_PCT,
                                  math.log(DECAY_TARGET) / FAST_DECAY_PCT, D_HYENA, dtype=F32))
    window = jnp.exp(-t[:, None] * deltas[None, :])
    filt = filt * window[:, None, None, :]
    return filt[:, 0], filt[:, 1]


def kernel(x, meta_tokens, w_in, conv_w, conv_b, filt_w1, filt_b1, filt_w2, filt_b2, filt_w3,
           filt_freq, filt_skip, hyena_norm, lb_fwd, lb_bwd, hgrn_norm, w_out, norm_mix, norm_ffn,
           w_router_group, w_router_expert, w_gate, w_up, w_down, norm_final):
    B, S, D = x.shape
    L = S + N_META
    lbf = jnp.cumsum(jax.nn.softmax(lb_fwd, axis=0), axis=0)[0]
    lbb = jnp.cumsum(jax.nn.softmax(lb_bwd, axis=0), axis=0)[0]

    w_in_b = w_in[0].astype(BF16)
    xf = x.reshape(B * S, D)
    phy_x, phg_x = _inproj(xf, norm_mix[0], w_in_b, tm=512)
    phy_m, phg_m = _inproj(meta_tokens, norm_mix[0], w_in_b, tm=N_META)

    cs, ss = _dft_tables()
    h_fwd, h_bwd = _hyena_filters(L, filt_w1[0], filt_b1[0], filt_w2[0], filt_b2[0], filt_w3[0],
                                  filt_freq[0])
    hr, hi = _hyena_spectra(h_fwd, h_bwd, cs, ss)
    z_hy = _hyena(phy_x.reshape(B, S, D_HYENA_PROJ), phy_m, conv_w[0], conv_b[0], filt_skip[0],
                  cs.astype(BF16), ss.astype(BF16), hr, hi).reshape(B * S, D_HYENA)

    y_hg = _hgrn(phg_x.reshape(B, S, 5 * D_HGRN), phg_m, lbf, lbb,
                 hgrn_norm[0]).reshape(B * S, D_HGRN)

    w_r = jnp.concatenate([w_router_group[0], w_router_expert[0].reshape(D, N_EXPERTS),
                           jnp.zeros((D, LANES - N_GROUPS - N_EXPERTS), F32)], axis=1).astype(BF16)
    h1, ri, rg, cnt = _outproj(z_hy, y_hg, xf, hyena_norm[0], norm_ffn[0],
                               w_out[0].astype(BF16), w_r, tm=512)
    out = _moe(h1, ri, rg, cnt, norm_ffn[0], norm_final, w_gate[0], w_up[0], w_down[0])
    return out.reshape(B, S, D)
```

```python
import functools
import math

import jax
import jax.numpy as jnp
from jax import lax
from jax.experimental import pallas as pl
from jax.experimental.pallas import tpu as pltpu

D_MODEL = 1024
N_META = 16
D_HYENA = 512
D_HGRN = 512
HYENA_ORDER = 2
SHORT_CONV = 3
FILTER_EMB = 33
FILTER_BANDS = 16
DECAY_TARGET = 1e-2
FAST_DECAY_PCT = 0.3
SLOW_DECAY_PCT = 1.5
HGRN_HEAD_DIM = 128
HGRN_HEADS = D_HGRN // HGRN_HEAD_DIM
CHUNK = 64
N_GROUPS = 8
EXPERTS_PER_GROUP = 8
N_EXPERTS = 64
TOP_K = 2
D_EXPERT = 512
D_HYENA_PROJ = 3 * D_HYENA
D_IN_PROJ = D_HYENA_PROJ + 5 * D_HGRN
EPS = 1e-6

F32 = jnp.float32
BF16 = jnp.bfloat16


def _rms(x, gain):
    return x * lax.rsqrt(jnp.mean(x * x, axis=-1, keepdims=True) + EPS) * gain


HI16 = 0xFFFF0000


def _pack_bf16_pairs(x):
    c = x.shape[1] // 2
    bits = lax.bitcast_convert_type(x.astype(BF16).astype(F32), jnp.uint32)
    return (bits[:, :c] >> 16) | (bits[:, c:] & jnp.uint32(HI16))


def _unpack_bf16_pairs(w):
    lo = lax.bitcast_convert_type(w << 16, F32)
    hi = lax.bitcast_convert_type(w & jnp.uint32(HI16), F32)
    return lo, hi


def _inproj_body(x_ref, g_ref, w_ref, hy_ref, hg_ref, *, tn):
    a = _rms(x_ref[...], g_ref[...]).astype(BF16)
    for j in range(D_IN_PROJ // tn):
        acc = jnp.dot(a, w_ref[:, j * tn:(j + 1) * tn], preferred_element_type=F32)
        if j * tn < D_HYENA_PROJ:
            hy_ref[:, j * tn:(j + 1) * tn] = acc.astype(hy_ref.dtype)
        else:
            c0 = j * tn - D_HYENA_PROJ
            hg_ref[:, c0:c0 + tn] = acc.astype(hg_ref.dtype)


def _inproj(x, gain, w_bf16, tm):
    m, d = x.shape
    n_hg = D_IN_PROJ - D_HYENA_PROJ
    return pl.pallas_call(
        functools.partial(_inproj_body, tn=512),
        grid=(m // tm,),
        in_specs=[pl.BlockSpec((tm, d), lambda i: (i, 0)),
                  pl.BlockSpec((1, d), lambda i: (0, 0)),
                  pl.BlockSpec((d, D_IN_PROJ), lambda i: (0, 0))],
        out_specs=[pl.BlockSpec((tm, D_HYENA_PROJ), lambda i: (i, 0)),
                   pl.BlockSpec((tm, n_hg), lambda i: (i, 0))],
        out_shape=[jax.ShapeDtypeStruct((m, D_HYENA_PROJ), BF16),
                   jax.ShapeDtypeStruct((m, n_hg), F32)],
        compiler_params=pltpu.CompilerParams(
            dimension_semantics=("arbitrary",), vmem_limit_bytes=48 * 1024 * 1024),
        name="inproj",
    )(x, gain.reshape(1, d), w_bf16)


LANES = 128
NEG_BIG = -1e30


def _outproj_body(zhy_ref, yhg_ref, h0_ref, ghy_ref, gffn_ref, wo_ref, wr_ref,
                  h1_ref, a2p_ref, ri_ref, rg_ref, cnt_ref, tri_s, carry_s):
    i = pl.program_id(0)
    tm = h1_ref.shape[0]

    @pl.when(i == 0)
    def _():
        r = lax.broadcasted_iota(jnp.int32, (tm, tm), 0)
        c = lax.broadcasted_iota(jnp.int32, (tm, tm), 1)
        tri_s[...] = (r > c).astype(BF16)
        carry_s[...] = jnp.zeros_like(carry_s)

    yhy = _rms(zhy_ref[...], ghy_ref[...]).astype(BF16)
    yhg = yhg_ref[...].astype(BF16)
    acc = jnp.dot(yhy, wo_ref[:D_HYENA, :], preferred_element_type=F32)
    acc = acc + jnp.dot(yhg, wo_ref[D_HYENA:, :], preferred_element_type=F32)
    h1 = h0_ref[...] + acc
    h1_ref[...] = h1
    a2 = _rms(h1, gffn_ref[...])
    a2p_ref[...] = _pack_bf16_pairs(a2)
    a2b = a2.astype(BF16)
    lg = jnp.dot(a2b, wr_ref[...], preferred_element_type=F32)

    lane = lax.broadcasted_iota(jnp.int32, (tm, LANES), 1)
    is_g = lane < N_GROUPS
    gl = jnp.where(is_g, lg, NEG_BIG)
    gmax = jnp.max(gl, axis=1, keepdims=True)
    gsel = jnp.min(jnp.where(gl == gmax, lane, LANES), axis=1, keepdims=True)
    gden = jnp.sum(jnp.where(is_g, jnp.exp(gl - gmax), 0.0), axis=1, keepdims=True)
    p_group = 1.0 / gden
    in_grp = (lane >= N_GROUPS) & (lane < N_GROUPS + N_EXPERTS) & (
        ((lane - N_GROUPS) >> 3) == gsel)
    el = jnp.where(in_grp, lg, NEG_BIG)
    m1 = jnp.max(el, axis=1, keepdims=True)
    i1 = jnp.min(jnp.where(el == m1, lane, LANES), axis=1, keepdims=True)
    el2 = jnp.where(lane == i1, NEG_BIG, el)
    m2 = jnp.max(el2, axis=1, keepdims=True)
    i2 = jnp.min(jnp.where(el2 == m2, lane, LANES), axis=1, keepdims=True)
    r21 = jnp.exp(m2 - m1)
    gate1 = p_group / (1.0 + r21)
    gate2 = gate1 * r21

    hit1 = lane == i1
    hit2 = lane == i2
    onehot = (hit1 | hit2).astype(BF16)
    pre = jnp.dot(tri_s[...], onehot, preferred_element_type=F32) + carry_s[...]
    pos1 = jnp.sum(jnp.where(hit1, pre, 0.0), axis=1, keepdims=True).astype(jnp.int32)
    pos2 = jnp.sum(jnp.where(hit2, pre, 0.0), axis=1, keepdims=True).astype(jnp.int32)
    carry_s[...] += jnp.sum(onehot.astype(F32), axis=0, keepdims=True)
    cnt_ref[...] = carry_s[...]

    zero_i = jnp.zeros((tm, LANES), jnp.int32)
    ri_ref[...] = jnp.where(lane == 0, i1 - N_GROUPS,
                  jnp.where(lane == 1, i2 - N_GROUPS,
                  jnp.where(lane == 2, pos1, jnp.where(lane == 3, pos2, zero_i))))
    rg_ref[...] = jnp.where(lane == 0, gate1, jnp.where(lane == 1, gate2, 0.0))


def _outproj(zhy, yhg, h0, g_hy, g_ffn, wo_bf16, wr_bf16, tm):
    m = h0.shape[0]
    return pl.pallas_call(
        _outproj_body,
        grid=(m // tm,),
        in_specs=[pl.BlockSpec((tm, D_HYENA), lambda i: (i, 0)),
                  pl.BlockSpec((tm, D_HGRN), lambda i: (i, 0)),
                  pl.BlockSpec((tm, D_MODEL), lambda i: (i, 0)),
                  pl.BlockSpec((1, D_HYENA), lambda i: (0, 0)),
                  pl.BlockSpec((1, D_MODEL), lambda i: (0, 0)),
                  pl.BlockSpec((D_MODEL, D_MODEL), lambda i: (0, 0)),
                  pl.BlockSpec((D_MODEL, LANES), lambda i: (0, 0))],
        out_specs=[pl.BlockSpec((tm, D_MODEL), lambda i: (i, 0)),
                   pl.BlockSpec((tm, D_MODEL // 2), lambda i: (i, 0)),
                   pl.BlockSpec((tm, LANES), lambda i: (i, 0)),
                   pl.BlockSpec((tm, LANES), lambda i: (i, 0)),
                   pl.BlockSpec((1, LANES), lambda i: (0, 0))],
        out_shape=[jax.ShapeDtypeStruct((m, D_MODEL), F32),
                   jax.ShapeDtypeStruct((m, D_MODEL // 2), jnp.uint32),
                   jax.ShapeDtypeStruct((m, LANES), jnp.int32),
                   jax.ShapeDtypeStruct((m, LANES), F32),
                   jax.ShapeDtypeStruct((1, LANES), F32)],
        scratch_shapes=[pltpu.VMEM((tm, tm), BF16), pltpu.VMEM((1, LANES), F32)],
        compiler_params=pltpu.CompilerParams(
            dimension_semantics=("arbitrary",), vmem_limit_bytes=48 * 1024 * 1024),
        name="outproj",
    )(zhy, yhg, h0, g_hy.reshape(1, -1), g_ffn.reshape(1, -1), wo_bf16, wr_bf16)


EXP_TB = 256


def _row_copy(src, src_row, dst, dst_row, sem):
    return pltpu.make_async_copy(src.at[pl.ds(src_row, 1), :], dst.at[pl.ds(dst_row, 1), :], sem)


def _wait_rows(src, dst, n_rows, sem):
    pltpu.make_async_copy(src.at[pl.ds(0, n_rows), :], dst.at[pl.ds(0, n_rows), :], sem).wait()


def _dispatch_body(lastblk_ref, npad_ref, nused_ref, dest_ref, h1_ref, xb_ref, zero_s, sem_z, sem):
    i = pl.program_id(0)
    tm = dest_ref.shape[0] // TOP_K
    n_blocks = xb_ref.shape[0] // EXP_TB

    @pl.when(i == 0)
    def _():
        zero_s[...] = jnp.zeros_like(zero_s)

        def zero_copy(row0):
            row0 = pl.multiple_of(row0, EXP_TB)
            return pltpu.make_async_copy(zero_s, xb_ref.at[pl.ds(row0, EXP_TB), :], sem_z)

        for e in range(N_EXPERTS):
            @pl.when(npad_ref[e] > 0)
            def _():
                zero_copy(lastblk_ref[e]).start()
        for e in range(N_EXPERTS):
            @pl.when(npad_ref[e] > 0)
            def _():
                zero_copy(lastblk_ref[e]).wait()

        def start_tail(b, carry):
            zero_copy(b * EXP_TB).start()
            return carry

        def wait_tail(b, carry):
            zero_copy(b * EXP_TB).wait()
            return carry

        lax.fori_loop(nused_ref[0], n_blocks, start_tail, 0)
        lax.fori_loop(nused_ref[0], n_blocks, wait_tail, 0)

    def issue(t, carry):
        _row_copy(h1_ref, t, xb_ref, dest_ref[2 * t], sem).start()
        _row_copy(h1_ref, t, xb_ref, dest_ref[2 * t + 1], sem).start()
        return carry

    lax.fori_loop(0, tm, issue, 0, unroll=8)
    _wait_rows(h1_ref, xb_ref, tm, sem)
    _wait_rows(h1_ref, xb_ref, tm, sem)


def _dispatch(lastblk, npad, n_used, dest_flat, rows, n_slots, tm):
    m, w = rows.shape
    grid_spec = pltpu.PrefetchScalarGridSpec(
        num_scalar_prefetch=3,
        grid=(m // tm,),
        in_specs=[pl.BlockSpec((TOP_K * tm,), lambda i, lb, npd, nu: (i,), memory_space=pltpu.SMEM),
                  pl.BlockSpec((tm, w), lambda i, lb, npd, nu: (i, 0))],
        out_specs=pl.BlockSpec(memory_space=pl.ANY),
        scratch_shapes=[pltpu.VMEM((EXP_TB, w), rows.dtype),
                        pltpu.SemaphoreType.DMA(()), pltpu.SemaphoreType.DMA(())],
    )
    return pl.pallas_call(
        _dispatch_body,
        grid_spec=grid_spec,
        out_shape=jax.ShapeDtypeStruct((n_slots, w), rows.dtype),
        compiler_params=pltpu.CompilerParams(dimension_semantics=("arbitrary",)),
        name="dispatch",
    )(lastblk, npad, n_used, dest_flat, rows)


def _expert_body(eid_ref, nused_ref, xb_ref, wg_ref, wu_ref, wd_ref, o_ref, wg_s, wu_s, wd_s):
    i = pl.program_id(0)
    half = D_MODEL // 2

    @pl.when(i < nused_ref[0])
    def _():
        prev = eid_ref[jnp.maximum(i - 1, 0)]

        @pl.when((i == 0) | (eid_ref[i] != prev))
        def _():
            wg_s[...] = wg_ref[0].astype(BF16)
            wu_s[...] = wu_ref[0].astype(BF16)
            wd_s[...] = wd_ref[0].astype(BF16)

        lo, hi = _unpack_bf16_pairs(xb_ref[...])
        lo, hi = lo.astype(BF16), hi.astype(BF16)

        def proj(w_s):
            return (jnp.dot(lo, w_s[:half, :], preferred_element_type=F32)
                    + jnp.dot(hi, w_s[half:, :], preferred_element_type=F32))

        g = proj(wg_s)
        u = proj(wu_s)
        hmid = (g * jax.nn.sigmoid(g) * u).astype(BF16)
        o_ref[...] = _pack_bf16_pairs(jnp.dot(hmid, wd_s[...], preferred_element_type=F32))

    @pl.when(i >= nused_ref[0])
    def _():
        o_ref[...] = jnp.zeros_like(o_ref)


def _experts(block_eid, n_used, xb, w_gate, w_up, w_down):
    n_slots, w = xb.shape

    def blk(i, e, nu):
        return jnp.minimum(i, nu[0] - 1)

    grid_spec = pltpu.PrefetchScalarGridSpec(
        num_scalar_prefetch=2,
        grid=(n_slots // EXP_TB,),
        in_specs=[pl.BlockSpec((EXP_TB, w), lambda i, e, nu: (blk(i, e, nu), 0)),
                  pl.BlockSpec((1, D_MODEL, D_EXPERT), lambda i, e, nu: (e[blk(i, e, nu)], 0, 0)),
                  pl.BlockSpec((1, D_MODEL, D_EXPERT), lambda i, e, nu: (e[blk(i, e, nu)], 0, 0)),
                  pl.BlockSpec((1, D_EXPERT, D_MODEL), lambda i, e, nu: (e[blk(i, e, nu)], 0, 0))],
        out_specs=pl.BlockSpec((EXP_TB, w), lambda i, e, nu: (i, 0)),
        scratch_shapes=[pltpu.VMEM((D_MODEL, D_EXPERT), BF16),
                        pltpu.VMEM((D_MODEL, D_EXPERT), BF16),
                        pltpu.VMEM((D_EXPERT, D_MODEL), BF16)],
    )
    return pl.pallas_call(
        _expert_body,
        grid_spec=grid_spec,
        out_shape=jax.ShapeDtypeStruct((n_slots, w), jnp.uint32),
        compiler_params=pltpu.CompilerParams(
            dimension_semantics=("arbitrary",), vmem_limit_bytes=48 * 1024 * 1024),
        name="experts",
    )(block_eid, n_used, xb, w_gate, w_up, w_down)


def _combine_body(dest_ref, dnext_ref, h1_ref, rg_ref, gfin_ref, yb_ref, o_ref, y_s, sem):
    i = pl.program_id(0)
    tm = h1_ref.shape[0]
    slot = i % 2

    def issue(idx_ref, sl):
        def body(t, carry):
            _row_copy(yb_ref, idx_ref[2 * t], y_s.at[sl, 0], t, sem.at[sl]).start()
            _row_copy(yb_ref, idx_ref[2 * t + 1], y_s.at[sl, 1], t, sem.at[sl]).start()
            return carry

        lax.fori_loop(0, tm, body, 0, unroll=8)

    @pl.when(i == 0)
    def _():
        issue(dest_ref, 0)

    @pl.when(i + 1 < pl.num_programs(0))
    def _():
        issue(dnext_ref, 1 - slot)

    _wait_rows(yb_ref, y_s.at[slot, 0], tm, sem.at[slot])
    _wait_rows(yb_ref, y_s.at[slot, 1], tm, sem.at[slot])
    rg = rg_ref[...]
    g1, g2 = rg[:, 0:1], rg[:, 1:2]
    lo1, hi1 = _unpack_bf16_pairs(y_s[slot, 0])
    lo2, hi2 = _unpack_bf16_pairs(y_s[slot, 1])
    half = D_MODEL // 2
    h2_lo = h1_ref[:, :half] + g1 * lo1 + g2 * lo2
    h2_hi = h1_ref[:, half:] + g1 * hi1 + g2 * hi2
    ms = (jnp.sum(h2_lo * h2_lo, axis=-1, keepdims=True)
          + jnp.sum(h2_hi * h2_hi, axis=-1, keepdims=True)) * (1.0 / D_MODEL)
    inv = lax.rsqrt(ms + EPS)
    o_ref[:, :half] = h2_lo * inv * gfin_ref[:, :half]
    o_ref[:, half:] = h2_hi * inv * gfin_ref[:, half:]


def _combine(dest_flat, h1, rg, g_fin, yb, tm):
    m = h1.shape[0]
    last = m // tm - 1
    return pl.pallas_call(
        _combine_body,
        grid=(m // tm,),
        in_specs=[pl.BlockSpec((TOP_K * tm,), lambda i: (i,), memory_space=pltpu.SMEM),
                  pl.BlockSpec((TOP_K * tm,), lambda i: (jnp.minimum(i + 1, last),),
                               memory_space=pltpu.SMEM),
                  pl.BlockSpec((tm, D_MODEL), lambda i: (i, 0)),
                  pl.BlockSpec((tm, LANES), lambda i: (i, 0)),
                  pl.BlockSpec((1, D_MODEL), lambda i: (0, 0)),
                  pl.BlockSpec(memory_space=pl.ANY)],
        out_specs=pl.BlockSpec((tm, D_MODEL), lambda i: (i, 0)),
        out_shape=jax.ShapeDtypeStruct((m, D_MODEL), F32),
        scratch_shapes=[pltpu.VMEM((2, TOP_K, tm, yb.shape[1]), yb.dtype),
                        pltpu.SemaphoreType.DMA((2,))],
        compiler_params=pltpu.CompilerParams(dimension_semantics=("arbitrary",)),
        name="combine",
    )(dest_flat, dest_flat, h1, rg, g_fin.reshape(1, -1), yb)


def _moe(h1, a2p, ri, rg, cnt, g_fin, w_gate, w_up, w_down):
    m = h1.shape[0]
    n_blocks = TOP_K * m // EXP_TB + N_EXPERTS
    n_slots = n_blocks * EXP_TB
    counts = cnt[0, N_GROUPS:N_GROUPS + N_EXPERTS].astype(jnp.int32)
    padded = (counts + EXP_TB - 1) // EXP_TB * EXP_TB
    pend = jnp.cumsum(padded)
    base = pend - padded
    eid = ri[:, 0:TOP_K]
    sel = eid[:, :, None] == jnp.arange(N_EXPERTS, dtype=jnp.int32)[None, None, :]
    dest = jnp.sum(jnp.where(sel, base[None, None, :], 0), axis=-1) + ri[:, TOP_K:2 * TOP_K]
    dest_flat = dest.reshape(-1).astype(jnp.int32)
    blk_start = jnp.arange(n_blocks, dtype=jnp.int32) * EXP_TB
    block_eid = jnp.minimum(jnp.sum(blk_start[:, None] >= pend[None, :], axis=1),
                            N_EXPERTS - 1).astype(jnp.int32)
    n_used = (pend[-1:] // EXP_TB).astype(jnp.int32)
    lastblk = (pend - EXP_TB).astype(jnp.int32)

    xb = _dispatch(lastblk, padded.astype(jnp.int32), n_used, dest_flat, a2p, n_slots, tm=512)
    yb = _experts(block_eid, n_used, xb, w_gate, w_up, w_down)
    return _combine(dest_flat, h1, rg, g_fin, yb, tm=256)


ND = 2176
NCIRC = 2 * ND
SEQ = 2048
L_TOT = SEQ + N_META
HY_CT = 256
HY_FB = 544
HY_TB = 688
HALO = 8


def _dft_tables():
    k = jnp.arange(ND, dtype=jnp.int32)
    a = 2 * k + 1
    t_hi = jnp.arange(ND // LANES, dtype=jnp.int32) * LANES
    t_lo = jnp.arange(LANES, dtype=jnp.int32)
    m_hi = (a[:, None] * t_hi[None, :]) % (2 * NCIRC)
    m_lo = (a[:, None] * (2 * t_lo + 1)[None, :]) % (4 * NCIRC)
    ang_hi = m_hi.astype(F32) * (math.pi / NCIRC)
    ang_lo = m_lo.astype(F32) * (math.pi / (2 * NCIRC))
    c_hi, s_hi = jnp.cos(ang_hi)[:, :, None], jnp.sin(ang_hi)[:, :, None]
    c_lo, s_lo = jnp.cos(ang_lo)[:, None, :], jnp.sin(ang_lo)[:, None, :]
    cs = (c_hi * c_lo - s_hi * s_lo).reshape(ND, ND)
    ss = (s_hi * c_lo + c_hi * s_lo).reshape(ND, ND)
    return cs, ss


def _hyena_body(pv_ref, px1_ref, px2_ref, mv_ref, mx1_ref, mx2_ref,
                wv_ref, wx1_ref, wx2_ref, bv_ref, bx1_ref, bx2_ref, skip_ref,
                cs_ref, ss_ref, hr_ref, hi_ref, o_ref,
                stage_s, z_s, g1_s, g2_s, zb_s, yr_s, yi_s):
    ct = o_ref.shape[-1]

    def short_conv(p_ref, m_ref, w_ref, b_ref, dst_ref):
        stage_s[0:HALO, :] = jnp.zeros((HALO, ct), F32)
        stage_s[HALO:HALO + N_META, :] = m_ref[...].astype(F32)
        stage_s[HALO + N_META:HALO + L_TOT, :] = p_ref[0].astype(F32)
        stage_s[HALO + L_TOT:HALO + L_TOT + HALO, :] = jnp.zeros((HALO, ct), F32)
        w = w_ref[...]
        b = b_ref[...]
        rb = 344
        for r0 in range(0, L_TOT, rb):
            prev = stage_s[HALO - 1 + r0:HALO - 1 + r0 + rb, :]
            cur = stage_s[HALO + r0:HALO + r0 + rb, :]
            nxt = stage_s[HALO + 1 + r0:HALO + 1 + r0 + rb, :]
            dst_ref[r0:r0 + rb, :] = b + prev * w[0:1] + cur * w[1:2] + nxt * w[2:3]

    short_conv(pv_ref, mv_ref, wv_ref, bv_ref, z_s)
    short_conv(px1_ref, mx1_ref, wx1_ref, bx1_ref, g1_s)
    short_conv(px2_ref, mx2_ref, wx2_ref, bx2_ref, g2_s)
    z_s[L_TOT:ND, :] = jnp.zeros((ND - L_TOT, ct), F32)

    for n, g_s in enumerate((g1_s, g2_s)):
        zb_s[...] = z_s[...].astype(BF16)
        for k0 in range(0, ND, HY_FB):
            zb = zb_s[...]
            p = jnp.dot(cs_ref[k0:k0 + HY_FB, :], zb, preferred_element_type=F32)
            q = jnp.dot(ss_ref[k0:k0 + HY_FB, :], zb, preferred_element_type=F32)
            hr = hr_ref[n, k0:k0 + HY_FB, :]
            hi = hi_ref[n, k0:k0 + HY_FB, :]
            yr_s[k0:k0 + HY_FB, :] = (p * hr + q * hi).astype(BF16)
            yi_s[k0:k0 + HY_FB, :] = (q * hr - p * hi).astype(BF16)
        skip = skip_ref[n:n + 1, :]
        for t0 in range(0, L_TOT, HY_TB):
            y = jnp.dot(cs_ref[t0:t0 + HY_TB, :], yr_s[...], preferred_element_type=F32)
            y = y + jnp.dot(ss_ref[t0:t0 + HY_TB, :], yi_s[...], preferred_element_type=F32)
            z = z_s[t0:t0 + HY_TB, :]
            znew = g_s[t0:t0 + HY_TB, :] * (y + z * skip)
            if n == 0:
                z_s[t0:t0 + HY_TB, :] = znew
            elif t0 == 0:
                o_ref[0, 0:HY_TB - N_META, :] = znew[N_META:, :]
            else:
                o_ref[0, t0 - N_META:t0 - N_META + HY_TB, :] = znew


def _hyena(p_hy, pm_hy, conv_w, conv_b, skip, cs, ss, hr, hi):
    bsz = p_hy.shape[0]
    nct = D_HYENA // HY_CT
    ct = HY_CT

    def part(j):
        return [pl.BlockSpec((1, SEQ, ct), lambda c, b, j=j: (b, 0, j * nct + c))]

    def mpart(j):
        return [pl.BlockSpec((N_META, ct), lambda c, b, j=j: (0, j * nct + c))]

    def wpart(rows, j):
        return [pl.BlockSpec((rows, ct), lambda c, b, j=j: (0, j * nct + c))]

    in_specs = (part(0) + part(1) + part(2) + mpart(0) + mpart(1) + mpart(2)
                + wpart(3, 0) + wpart(3, 1) + wpart(3, 2)
                + wpart(1, 0) + wpart(1, 1) + wpart(1, 2)
                + [pl.BlockSpec((HYENA_ORDER, ct), lambda c, b: (0, c)),
                   pl.BlockSpec((ND, ND), lambda c, b: (0, 0)),
                   pl.BlockSpec((ND, ND), lambda c, b: (0, 0)),
                   pl.BlockSpec((HYENA_ORDER, ND, ct), lambda c, b: (0, 0, c),
                                pipeline_mode=pl.Buffered(1)),
                   pl.BlockSpec((HYENA_ORDER, ND, ct), lambda c, b: (0, 0, c),
                                pipeline_mode=pl.Buffered(1))])
    cb = conv_b.reshape(1, -1)
    return pl.pallas_call(
        _hyena_body,
        grid=(nct, bsz),
        in_specs=in_specs,
        out_specs=pl.BlockSpec((1, SEQ, ct), lambda c, b: (b, 0, c)),
        out_shape=jax.ShapeDtypeStruct((bsz, SEQ, D_HYENA), F32),
        scratch_shapes=[pltpu.VMEM((L_TOT + 2 * HALO, ct), F32),
                        pltpu.VMEM((ND, ct), F32),
                        pltpu.VMEM((L_TOT, ct), F32),
                        pltpu.VMEM((L_TOT, ct), F32),
                        pltpu.VMEM((ND, ct), BF16),
                        pltpu.VMEM((ND, ct), BF16),
                        pltpu.VMEM((ND, ct), BF16)],
        compiler_params=pltpu.CompilerParams(
            dimension_semantics=("arbitrary", "arbitrary"), vmem_limit_bytes=60 * 1024 * 1024),
        name="hyena",
    )(p_hy, p_hy, p_hy, pm_hy, pm_hy, pm_hy, conv_w, conv_w, conv_w, cb, cb, cb, skip,
      cs, ss, hr, hi)


def _hyena_spectra(h_fwd, h_bwd, cs, ss):
    L = h_fwd.shape[0]
    hbs = jnp.concatenate([h_bwd[1:], jnp.zeros_like(h_bwd[:1])], axis=0)
    hp = lax.Precision.HIGHEST
    a_r = jnp.einsum('kt,toc->okc', cs[:, :L], h_fwd + hbs, precision=hp)
    a_i = -jnp.einsum('kt,toc->okc', ss[:, :L], h_fwd - hbs, precision=hp)
    half = (2 * jnp.arange(ND, dtype=F32) + 1) * (math.pi / (2 * NCIRC))
    c, s = jnp.cos(half)[None, :, None], jnp.sin(half)[None, :, None]
    scale = 2.0 / NCIRC
    return (c * a_r - s * a_i) * scale, (s * a_r + c * a_i) * scale


N_CHUNKS = SEQ // CHUNK
NT_DIMS = (((1,), (1,)), ((), ()))
TN_DIMS = (((0,), (0,)), ((), ()))
MID_F = CHUNK // 2
MID_B = CHUNK // 2 - 1


HG_G = 8
HG_ROWS = HG_G * CHUNK


def _split2(x):
    hi = x.astype(BF16)
    lo = (x - hi.astype(F32)).astype(BF16)
    return hi, lo


def _chunk_prefix_matrix():
    r = lax.broadcasted_iota(jnp.int32, (HG_ROWS, HG_ROWS), 0)
    c = lax.broadcasted_iota(jnp.int32, (HG_ROWS, HG_ROWS), 1)
    return (((r // CHUNK) == (c // CHUNK)) & (c <= r)).astype(BF16)


def _chunk_rows(x, row):
    return jnp.concatenate(
        [jnp.broadcast_to(x[g * CHUNK + row:g * CHUNK + row + 1, :], (CHUNK, x.shape[1]))
         for g in range(x.shape[0] // CHUNK)], axis=0)


def _hgrn_body(q_ref, ff_ref, fb_ref, i_ref, g_ref, mff_ref, mi_ref,
               lbf_ref, lbb_ref, nw_ref, o_ref,
               tri_s, qe_s, sc_s, ut_s, dec_s, st_s):
    hd = HGRN_HEAD_DIM
    row = lax.broadcasted_iota(jnp.int32, (CHUNK, CHUNK), 0)
    col = lax.broadcasted_iota(jnp.int32, (CHUNK, CHUNK), 1)
    lower = row >= col
    upper = col >= row
    lbf = lbf_ref[...]
    lbb = lbb_ref[...]

    @pl.when((pl.program_id(0) == 0) & (pl.program_id(1) == 0))
    def _():
        tri_s[...] = _chunk_prefix_matrix()

    def forget(logit, lb):
        f = lb + (1.0 - lb) * jax.nn.sigmoid(logit)
        return 1.0 - f, jnp.log(f)

    def prefix_sums(lf):
        s = jnp.dot(tri_s[...], jnp.concatenate(_split2(lf), axis=1), preferred_element_type=F32)
        return s[:, :hd] + s[:, hd:]

    k_m, lf_m = forget(mff_ref[...], lbf)
    pad = jnp.zeros((CHUNK - N_META, hd), F32)
    lf_m = jnp.concatenate([pad, lf_m] * HG_G, axis=0)
    b_m = prefix_sums(lf_m)[0:CHUNK]
    kl_m = jnp.concatenate([pad, k_m], axis=0) * jnp.exp(b_m[CHUNK - 1:CHUNK] - b_m)
    v_m = jnp.concatenate([pad, mi_ref[...]], axis=0).astype(BF16)
    st_meta = lax.dot_general(v_m, kl_m.astype(BF16), TN_DIMS, preferred_element_type=F32)

    def phase_a(j, carry):
        r0 = pl.multiple_of(j * HG_ROWS, HG_ROWS)
        rows = pl.ds(r0, HG_ROWS)
        qv = jax.nn.silu(q_ref[0, rows, :])
        vb = i_ref[0, rows, :].astype(BF16)
        k_f, lf_f = forget(ff_ref[0, rows, :], lbf)
        k_b, lf_b = forget(fb_ref[0, rows, :], lbb)
        b_f = prefix_sums(lf_f)
        p_b = prefix_sums(lf_b)
        bmid_f = _chunk_rows(b_f, MID_F)
        blast_f = _chunk_rows(b_f, CHUNK - 1)
        tot_b = _chunk_rows(p_b, CHUNK - 1)
        c_b = tot_b - p_b + lf_b
        cmid_b = _chunk_rows(c_b, MID_B)
        d_f = b_f - bmid_f
        d_b = c_b - cmid_b
        e_f = jnp.exp(d_f)
        e_b = jnp.exp(d_b)
        qs_f = qv * e_f
        ks_f = k_f / e_f
        qs_b = qv * e_b
        ks_b = k_b / e_b
        qs_fb, ks_fb = qs_f.astype(BF16), ks_f.astype(BF16)
        qs_bb, ks_bb = qs_b.astype(BF16), ks_b.astype(BF16)
        for g in range(HG_G):
            sl = slice(g * CHUNK, (g + 1) * CHUNK)
            r1 = slice(g * CHUNK, g * CHUNK + 1)
            n = j * HG_G + g
            rows_g = pl.ds(pl.multiple_of(r0 + g * CHUNK, CHUNK), CHUNK)
            sc_f = lax.dot_general(qs_fb[sl], ks_fb[sl], NT_DIMS, preferred_element_type=F32)
            sc_b = lax.dot_general(qs_bb[sl], ks_bb[sl], NT_DIMS, preferred_element_type=F32)
            sc_s[n] = (jnp.where(lower, sc_f, 0.0) + jnp.where(upper, sc_b, 0.0)).astype(BF16)
            em_f = jnp.exp(bmid_f[r1])
            el_f = jnp.exp(blast_f[r1] - bmid_f[r1])
            em_b = jnp.exp(cmid_b[r1])
            el_b = jnp.exp(tot_b[r1] - cmid_b[r1])
            qe_s[rows_g, :] = jnp.concatenate(
                [qs_f[sl] * em_f, qs_b[sl] * em_b], axis=1).astype(BF16)
            kl = jnp.concatenate([ks_f[sl] * el_f, ks_b[sl] * el_b], axis=1).astype(BF16)
            ut_s[n] = lax.dot_general(vb[sl], kl, TN_DIMS, preferred_element_type=F32)
            dec_s[n] = jnp.concatenate([jnp.exp(blast_f[r1]), jnp.exp(tot_b[r1])], axis=1)
        return carry

    lax.fori_loop(0, N_CHUNKS // HG_G, phase_a, 0, unroll=True)

    st_f = st_meta
    st_b = jnp.zeros((hd, hd), F32)
    for n in range(N_CHUNKS):
        st_s[n, :, 0:hd] = st_f.astype(BF16)
        st_f = dec_s[n, :, 0:hd] * st_f + ut_s[n, :, 0:hd]
        m = N_CHUNKS - 1 - n
        st_s[m, :, hd:2 * hd] = st_b.astype(BF16)
        st_b = dec_s[m, :, hd:2 * hd] * st_b + ut_s[m, :, hd:2 * hd]

    nw = nw_ref[...]

    def phase_c(j, carry):
        r0 = pl.multiple_of(j * HG_ROWS, HG_ROWS)
        rows = pl.ds(r0, HG_ROWS)
        vb = i_ref[0, rows, :].astype(BF16)
        outs = []
        for g in range(HG_G):
            sl = slice(g * CHUNK, (g + 1) * CHUNK)
            n = j * HG_G + g
            rows_g = pl.ds(pl.multiple_of(r0 + g * CHUNK, CHUNK), CHUNK)
            o = jnp.dot(sc_s[n], vb[sl], preferred_element_type=F32)
            outs.append(o + lax.dot_general(qe_s[rows_g, :], st_s[n], NT_DIMS,
                                            preferred_element_type=F32))
        o = jnp.concatenate(outs, axis=0)
        o = o * lax.rsqrt(jnp.mean(o * o, axis=-1, keepdims=True) + EPS)
        o_ref[0, rows, :] = o * nw * jax.nn.silu(g_ref[0, rows, :])
        return carry

    lax.fori_loop(0, N_CHUNKS // HG_G, phase_c, 0, unroll=True)


def _hgrn(phg_x, phg_m, lb_f, lb_b, norm_w):
    bsz = phg_x.shape[0]
    hd = HGRN_HEAD_DIM
    nh = HGRN_HEADS

    def part(j):
        return pl.BlockSpec((1, SEQ, hd), lambda b, h, j=j: (b, 0, j * nh + h))

    def mpart(j):
        return pl.BlockSpec((N_META, hd), lambda b, h, j=j: (0, j * nh + h))

    vec = pl.BlockSpec((1, hd), lambda b, h: (0, h))
    return pl.pallas_call(
        _hgrn_body,
        grid=(bsz, nh),
        in_specs=[part(0), part(1), part(2), part(3), part(4), mpart(1), mpart(3), vec, vec, vec],
        out_specs=pl.BlockSpec((1, SEQ, hd), lambda b, h: (b, 0, h)),
        out_shape=jax.ShapeDtypeStruct((bsz, SEQ, D_HGRN), F32),
        scratch_shapes=[pltpu.VMEM((HG_ROWS, HG_ROWS), BF16),
                        pltpu.VMEM((SEQ, 2 * hd), BF16),
                        pltpu.VMEM((N_CHUNKS, CHUNK, CHUNK), BF16),
                        pltpu.VMEM((N_CHUNKS, hd, 2 * hd), F32),
                        pltpu.VMEM((N_CHUNKS, 1, 2 * hd), F32),
                        pltpu.VMEM((N_CHUNKS, hd, 2 * hd), BF16)],
        compiler_params=pltpu.CompilerParams(
            dimension_semantics=("arbitrary", "arbitrary"), vmem_limit_bytes=40 * 1024 * 1024),
        name="hgrn",
    )(phg_x, phg_x, phg_x, phg_x, phg_x, phg_m, phg_m,
      lb_f.reshape(1, -1), lb_b.reshape(1, -1), norm_w.reshape(1, -1))


def _hyena_filters(L, w1, b1, w2, b2, w3, freq):
    pos = jnp.arange(L, dtype=F32)
    t = pos / max(L - 1, 1)
    bands = jnp.linspace(1e-4, FILTER_BANDS - 1, FILTER_BANDS, dtype=F32)
    ang = (2.0 * math.pi / L) * pos[:, None] * bands[None, :]
    z = jnp.concatenate([t[:, None], jnp.cos(ang), -jnp.sin(ang)], axis=-1)
    hp = lax.Precision.HIGHEST
    hid = jnp.sin(freq * (jnp.dot(z, w1, precision=hp) + b1))
    hid = jnp.sin(freq * (jnp.dot(hid, w2, precision=hp) + b2))
    filt = jnp.dot(hid, w3, precision=hp).reshape(L, 2, HYENA_ORDER, D_HYENA)
    deltas = jnp.abs(jnp.linspace(math.log(DECAY_TARGET) / SLOW_DECAY_PCT,
                                  math.log(DECAY_TARGET) / FAST_DECAY_PCT, D_HYENA, dtype=F32))
    window = jnp.exp(-t[:, None] * deltas[None, :])
    filt = filt * window[:, None, None, :]
    return filt[:, 0], filt[:, 1]


def kernel(x, meta_tokens, w_in, conv_w, conv_b, filt_w1, filt_b1, filt_w2, filt_b2, filt_w3,
           filt_freq, filt_skip, hyena_norm, lb_fwd, lb_bwd, hgrn_norm, w_out, norm_mix, norm_ffn,
           w_router_group, w_router_expert, w_gate, w_up, w_down, norm_final):
    B, S, D = x.shape
    L = S + N_META
    lbf = jnp.cumsum(jax.nn.softmax(lb_fwd, axis=0), axis=0)[0]
    lbb = jnp.cumsum(jax.nn.softmax(lb_bwd, axis=0), axis=0)[0]

    w_in_b = w_in[0].astype(BF16)
    xf = x.reshape(B * S, D)
    phy_x, phg_x = _inproj(xf, norm_mix[0], w_in_b, tm=512)
    phy_m, phg_m = _inproj(meta_tokens, norm_mix[0], w_in_b, tm=N_META)

    cs, ss = _dft_tables()
    h_fwd, h_bwd = _hyena_filters(L, filt_w1[0], filt_b1[0], filt_w2[0], filt_b2[0], filt_w3[0],
                                  filt_freq[0])
    hr, hi = _hyena_spectra(h_fwd, h_bwd, cs, ss)
    z_hy = _hyena(phy_x.reshape(B, S, D_HYENA_PROJ), phy_m, conv_w[0], conv_b[0], filt_skip[0],
                  cs.astype(BF16), ss.astype(BF16), hr, hi).reshape(B * S, D_HYENA)

    y_hg = _hgrn(phg_x.reshape(B, S, 5 * D_HGRN), phg_m, lbf, lbb,
                 hgrn_norm[0]).reshape(B * S, D_HGRN)

    w_r = jnp.concatenate([w_router_group[0], w_router_expert[0].reshape(D, N_EXPERTS),
                           jnp.zeros((D, LANES - N_GROUPS - N_EXPERTS), F32)], axis=1).astype(BF16)
    h1, a2p, ri, rg, cnt = _outproj(z_hy, y_hg, xf, hyena_norm[0], norm_ffn[0],
                                    w_out[0].astype(BF16), w_r, tm=512)
    out = _moe(h1, a2p, ri, rg, cnt, norm_final, w_gate[0], w_up[0], w_down[0])
    return out.reshape(B, S, D)
```

```python
import functools
import math

import jax
import jax.numpy as jnp
from jax import lax
from jax.experimental import pallas as pl
from jax.experimental.pallas import tpu as pltpu

D_MODEL = 1024
N_META = 16
D_HYENA = 512
D_HGRN = 512
HYENA_ORDER = 2
SHORT_CONV = 3
FILTER_EMB = 33
FILTER_BANDS = 16
DECAY_TARGET = 1e-2
FAST_DECAY_PCT = 0.3
SLOW_DECAY_PCT = 1.5
HGRN_HEAD_DIM = 128
HGRN_HEADS = D_HGRN // HGRN_HEAD_DIM
CHUNK = 64
N_GROUPS = 8
EXPERTS_PER_GROUP = 8
N_EXPERTS = 64
TOP_K = 2
D_EXPERT = 512
D_HYENA_PROJ = 3 * D_HYENA
D_IN_PROJ = D_HYENA_PROJ + 5 * D_HGRN
EPS = 1e-6

F32 = jnp.float32
BF16 = jnp.bfloat16


def _rms(x, gain):
    return x * lax.rsqrt(jnp.mean(x * x, axis=-1, keepdims=True) + EPS) * gain


HI16 = 0xFFFF0000


def _pack_bf16_pairs(x):
    c = x.shape[1] // 2
    bits = lax.bitcast_convert_type(x.astype(BF16).astype(F32), jnp.uint32)
    return (bits[:, :c] >> 16) | (bits[:, c:] & jnp.uint32(HI16))


def _unpack_bf16_pairs(w):
    lo = lax.bitcast_convert_type(w << 16, F32)
    hi = lax.bitcast_convert_type(w & jnp.uint32(HI16), F32)
    return lo, hi


def _inproj_body(x_ref, g_ref, w_ref, hy_ref, hg_ref, *, tn):
    a = _rms(x_ref[...], g_ref[...]).astype(BF16)
    for j in range(D_IN_PROJ // tn):
        acc = jnp.dot(a, w_ref[:, j * tn:(j + 1) * tn], preferred_element_type=F32)
        if j * tn < D_HYENA_PROJ:
            hy_ref[:, j * tn:(j + 1) * tn] = acc.astype(hy_ref.dtype)
        else:
            c0 = j * tn - D_HYENA_PROJ
            hg_ref[:, c0:c0 + tn] = acc.astype(hg_ref.dtype)


def _inproj(x, gain, w_bf16, tm):
    m, d = x.shape
    n_hg = D_IN_PROJ - D_HYENA_PROJ
    return pl.pallas_call(
        functools.partial(_inproj_body, tn=512),
        grid=(m // tm,),
        in_specs=[pl.BlockSpec((tm, d), lambda i: (i, 0)),
                  pl.BlockSpec((1, d), lambda i: (0, 0)),
                  pl.BlockSpec((d, D_IN_PROJ), lambda i: (0, 0))],
        out_specs=[pl.BlockSpec((tm, D_HYENA_PROJ), lambda i: (i, 0)),
                   pl.BlockSpec((tm, n_hg), lambda i: (i, 0))],
        out_shape=[jax.ShapeDtypeStruct((m, D_HYENA_PROJ), BF16),
                   jax.ShapeDtypeStruct((m, n_hg), F32)],
        compiler_params=pltpu.CompilerParams(
            dimension_semantics=("arbitrary",), vmem_limit_bytes=48 * 1024 * 1024),
        name="inproj",
    )(x, gain.reshape(1, d), w_bf16)


LANES = 128
NEG_BIG = -1e30


def _outproj_body(zhy_ref, yhg_ref, h0_ref, ghy_ref, gffn_ref, wo_ref, wr_ref,
                  h1_ref, a2p_ref, ri_ref, rg_ref, cnt_ref, tri_s, carry_s):
    i = pl.program_id(0)
    tm = h1_ref.shape[0]

    @pl.when(i == 0)
    def _():
        r = lax.broadcasted_iota(jnp.int32, (tm, tm), 0)
        c = lax.broadcasted_iota(jnp.int32, (tm, tm), 1)
        tri_s[...] = (r > c).astype(BF16)
        carry_s[...] = jnp.zeros_like(carry_s)

    yhy = _rms(zhy_ref[...], ghy_ref[...]).astype(BF16)
    yhg = yhg_ref[...].astype(BF16)
    acc = jnp.dot(yhy, wo_ref[:D_HYENA, :], preferred_element_type=F32)
    acc = acc + jnp.dot(yhg, wo_ref[D_HYENA:, :], preferred_element_type=F32)
    h1 = h0_ref[...] + acc
    h1_ref[...] = h1
    a2 = _rms(h1, gffn_ref[...])
    a2p_ref[...] = _pack_bf16_pairs(a2)
    a2b = a2.astype(BF16)
    lg = jnp.dot(a2b, wr_ref[...], preferred_element_type=F32)

    lane = lax.broadcasted_iota(jnp.int32, (tm, LANES), 1)
    is_g = lane < N_GROUPS
    gl = jnp.where(is_g, lg, NEG_BIG)
    gmax = jnp.max(gl, axis=1, keepdims=True)
    gsel = jnp.min(jnp.where(gl == gmax, lane, LANES), axis=1, keepdims=True)
    gden = jnp.sum(jnp.where(is_g, jnp.exp(gl - gmax), 0.0), axis=1, keepdims=True)
    p_group = 1.0 / gden
    in_grp = (lane >= N_GROUPS) & (lane < N_GROUPS + N_EXPERTS) & (
        ((lane - N_GROUPS) >> 3) == gsel)
    el = jnp.where(in_grp, lg, NEG_BIG)
    m1 = jnp.max(el, axis=1, keepdims=True)
    i1 = jnp.min(jnp.where(el == m1, lane, LANES), axis=1, keepdims=True)
    el2 = jnp.where(lane == i1, NEG_BIG, el)
    m2 = jnp.max(el2, axis=1, keepdims=True)
    i2 = jnp.min(jnp.where(el2 == m2, lane, LANES), axis=1, keepdims=True)
    r21 = jnp.exp(m2 - m1)
    gate1 = p_group / (1.0 + r21)
    gate2 = gate1 * r21

    hit1 = lane == i1
    hit2 = lane == i2
    onehot = (hit1 | hit2).astype(BF16)
    pre = jnp.dot(tri_s[...], onehot, preferred_element_type=F32) + carry_s[...]
    pos1 = jnp.sum(jnp.where(hit1, pre, 0.0), axis=1, keepdims=True).astype(jnp.int32)
    pos2 = jnp.sum(jnp.where(hit2, pre, 0.0), axis=1, keepdims=True).astype(jnp.int32)
    carry_s[...] += jnp.sum(onehot.astype(F32), axis=0, keepdims=True)
    cnt_ref[...] = carry_s[...]

    zero_i = jnp.zeros((tm, LANES), jnp.int32)
    ri_ref[...] = jnp.where(lane == 0, i1 - N_GROUPS,
                  jnp.where(lane == 1, i2 - N_GROUPS,
                  jnp.where(lane == 2, pos1, jnp.where(lane == 3, pos2, zero_i))))
    rg_ref[...] = jnp.where(lane == 0, gate1, jnp.where(lane == 1, gate2, 0.0))


def _outproj(zhy, yhg, h0, g_hy, g_ffn, wo_bf16, wr_bf16, tm):
    m = h0.shape[0]
    return pl.pallas_call(
        _outproj_body,
        grid=(m // tm,),
        in_specs=[pl.BlockSpec((tm, D_HYENA), lambda i: (i, 0)),
                  pl.BlockSpec((tm, D_HGRN), lambda i: (i, 0)),
                  pl.BlockSpec((tm, D_MODEL), lambda i: (i, 0)),
                  pl.BlockSpec((1, D_HYENA), lambda i: (0, 0)),
                  pl.BlockSpec((1, D_MODEL), lambda i: (0, 0)),
                  pl.BlockSpec((D_MODEL, D_MODEL), lambda i: (0, 0)),
                  pl.BlockSpec((D_MODEL, LANES), lambda i: (0, 0))],
        out_specs=[pl.BlockSpec((tm, D_MODEL), lambda i: (i, 0)),
                   pl.BlockSpec((tm, D_MODEL // 2), lambda i: (i, 0)),
                   pl.BlockSpec((tm, LANES), lambda i: (i, 0)),
                   pl.BlockSpec((tm, LANES), lambda i: (i, 0)),
                   pl.BlockSpec((1, LANES), lambda i: (0, 0))],
        out_shape=[jax.ShapeDtypeStruct((m, D_MODEL), F32),
                   jax.ShapeDtypeStruct((m, D_MODEL // 2), jnp.uint32),
                   jax.ShapeDtypeStruct((m, LANES), jnp.int32),
                   jax.ShapeDtypeStruct((m, LANES), F32),
                   jax.ShapeDtypeStruct((1, LANES), F32)],
        scratch_shapes=[pltpu.VMEM((tm, tm), BF16), pltpu.VMEM((1, LANES), F32)],
        compiler_params=pltpu.CompilerParams(
            dimension_semantics=("arbitrary",), vmem_limit_bytes=48 * 1024 * 1024),
        name="outproj",
    )(zhy, yhg, h0, g_hy.reshape(1, -1), g_ffn.reshape(1, -1), wo_bf16, wr_bf16)


EXP_TB = 256


def _row_copy(src, src_row, dst, dst_row, sem):
    return pltpu.make_async_copy(src.at[pl.ds(src_row, 1), :], dst.at[pl.ds(dst_row, 1), :], sem)


def _wait_rows(src, dst, n_rows, sem):
    pltpu.make_async_copy(src.at[pl.ds(0, n_rows), :], dst.at[pl.ds(0, n_rows), :], sem).wait()


def _dispatch_body(lastblk_ref, npad_ref, nused_ref, dest_ref, h1_ref, xb_ref, zero_s, sem_z, sem):
    i = pl.program_id(0)
    tm = dest_ref.shape[0] // TOP_K
    n_blocks = xb_ref.shape[0] // EXP_TB

    @pl.when(i == 0)
    def _():
        zero_s[...] = jnp.zeros_like(zero_s)

        def zero_copy(row0):
            row0 = pl.multiple_of(row0, EXP_TB)
            return pltpu.make_async_copy(zero_s, xb_ref.at[pl.ds(row0, EXP_TB), :], sem_z)

        for e in range(N_EXPERTS):
            @pl.when(npad_ref[e] > 0)
            def _():
                zero_copy(lastblk_ref[e]).start()
        for e in range(N_EXPERTS):
            @pl.when(npad_ref[e] > 0)
            def _():
                zero_copy(lastblk_ref[e]).wait()

        def start_tail(b, carry):
            zero_copy(b * EXP_TB).start()
            return carry

        def wait_tail(b, carry):
            zero_copy(b * EXP_TB).wait()
            return carry

        lax.fori_loop(nused_ref[0], n_blocks, start_tail, 0)
        lax.fori_loop(nused_ref[0], n_blocks, wait_tail, 0)

    def issue(t, carry):
        _row_copy(h1_ref, t, xb_ref, dest_ref[2 * t], sem).start()
        _row_copy(h1_ref, t, xb_ref, dest_ref[2 * t + 1], sem).start()
        return carry

    lax.fori_loop(0, tm, issue, 0, unroll=8)
    _wait_rows(h1_ref, xb_ref, tm, sem)
    _wait_rows(h1_ref, xb_ref, tm, sem)


def _dispatch(lastblk, npad, n_used, dest_flat, rows, n_slots, tm):
    m, w = rows.shape
    grid_spec = pltpu.PrefetchScalarGridSpec(
        num_scalar_prefetch=3,
        grid=(m // tm,),
        in_specs=[pl.BlockSpec((TOP_K * tm,), lambda i, lb, npd, nu: (i,), memory_space=pltpu.SMEM),
                  pl.BlockSpec((tm, w), lambda i, lb, npd, nu: (i, 0))],
        out_specs=pl.BlockSpec(memory_space=pl.ANY),
        scratch_shapes=[pltpu.VMEM((EXP_TB, w), rows.dtype),
                        pltpu.SemaphoreType.DMA(()), pltpu.SemaphoreType.DMA(())],
    )
    return pl.pallas_call(
        _dispatch_body,
        grid_spec=grid_spec,
        out_shape=jax.ShapeDtypeStruct((n_slots, w), rows.dtype),
        compiler_params=pltpu.CompilerParams(dimension_semantics=("arbitrary",)),
        name="dispatch",
    )(lastblk, npad, n_used, dest_flat, rows)


def _expert_body(eid_ref, nused_ref, xb_ref, wg_ref, wu_ref, wd_ref, o_ref, wg_s, wu_s, wd_s):
    i = pl.program_id(0)
    half = D_MODEL // 2

    @pl.when(i < nused_ref[0])
    def _():
        prev = eid_ref[jnp.maximum(i - 1, 0)]

        @pl.when((i == 0) | (eid_ref[i] != prev))
        def _():
            wg_s[...] = wg_ref[0].astype(BF16)
            wu_s[...] = wu_ref[0].astype(BF16)
            wd_s[...] = wd_ref[0].astype(BF16)

        lo, hi = _unpack_bf16_pairs(xb_ref[...])
        lo, hi = lo.astype(BF16), hi.astype(BF16)

        def proj(w_s):
            return (jnp.dot(lo, w_s[:half, :], preferred_element_type=F32)
                    + jnp.dot(hi, w_s[half:, :], preferred_element_type=F32))

        g = proj(wg_s)
        u = proj(wu_s)
        hmid = (g * jax.nn.sigmoid(g) * u).astype(BF16)
        o_ref[...] = _pack_bf16_pairs(jnp.dot(hmid, wd_s[...], preferred_element_type=F32))

    @pl.when(i >= nused_ref[0])
    def _():
        o_ref[...] = jnp.zeros_like(o_ref)


def _experts(block_eid, n_used, xb, w_gate, w_up, w_down):
    n_slots, w = xb.shape

    def blk(i, e, nu):
        return jnp.minimum(i, nu[0] - 1)

    grid_spec = pltpu.PrefetchScalarGridSpec(
        num_scalar_prefetch=2,
        grid=(n_slots // EXP_TB,),
        in_specs=[pl.BlockSpec((EXP_TB, w), lambda i, e, nu: (blk(i, e, nu), 0)),
                  pl.BlockSpec((1, D_MODEL, D_EXPERT), lambda i, e, nu: (e[blk(i, e, nu)], 0, 0)),
                  pl.BlockSpec((1, D_MODEL, D_EXPERT), lambda i, e, nu: (e[blk(i, e, nu)], 0, 0)),
                  pl.BlockSpec((1, D_EXPERT, D_MODEL), lambda i, e, nu: (e[blk(i, e, nu)], 0, 0))],
        out_specs=pl.BlockSpec((EXP_TB, w), lambda i, e, nu: (i, 0)),
        scratch_shapes=[pltpu.VMEM((D_MODEL, D_EXPERT), BF16),
                        pltpu.VMEM((D_MODEL, D_EXPERT), BF16),
                        pltpu.VMEM((D_EXPERT, D_MODEL), BF16)],
    )
    return pl.pallas_call(
        _expert_body,
        grid_spec=grid_spec,
        out_shape=jax.ShapeDtypeStruct((n_slots, w), jnp.uint32),
        compiler_params=pltpu.CompilerParams(
            dimension_semantics=("arbitrary",), vmem_limit_bytes=48 * 1024 * 1024),
        name="experts",
    )(block_eid, n_used, xb, w_gate, w_up, w_down)


def _combine_body(dest_ref, dnext_ref, h1_ref, rg_ref, gfin_ref, yb_ref, o_ref, y_s, sem):
    i = pl.program_id(0)
    tm = h1_ref.shape[0]
    slot = i % 2

    def issue(idx_ref, sl):
        def body(t, carry):
            _row_copy(yb_ref, idx_ref[2 * t], y_s.at[sl, 0], t, sem.at[sl]).start()
            _row_copy(yb_ref, idx_ref[2 * t + 1], y_s.at[sl, 1], t, sem.at[sl]).start()
            return carry

        lax.fori_loop(0, tm, body, 0, unroll=8)

    @pl.when(i == 0)
    def _():
        issue(dest_ref, 0)

    @pl.when(i + 1 < pl.num_programs(0))
    def _():
        issue(dnext_ref, 1 - slot)

    _wait_rows(yb_ref, y_s.at[slot, 0], tm, sem.at[slot])
    _wait_rows(yb_ref, y_s.at[slot, 1], tm, sem.at[slot])
    rg = rg_ref[...]
    g1, g2 = rg[:, 0:1], rg[:, 1:2]
    lo1, hi1 = _unpack_bf16_pairs(y_s[slot, 0])
    lo2, hi2 = _unpack_bf16_pairs(y_s[slot, 1])
    half = D_MODEL // 2
    h2_lo = h1_ref[:, :half] + g1 * lo1 + g2 * lo2
    h2_hi = h1_ref[:, half:] + g1 * hi1 + g2 * hi2
    ms = (jnp.sum(h2_lo * h2_lo, axis=-1, keepdims=True)
          + jnp.sum(h2_hi * h2_hi, axis=-1, keepdims=True)) * (1.0 / D_MODEL)
    inv = lax.rsqrt(ms + EPS)
    o_ref[:, :half] = h2_lo * inv * gfin_ref[:, :half]
    o_ref[:, half:] = h2_hi * inv * gfin_ref[:, half:]


def _combine(dest_flat, h1, rg, g_fin, yb, tm):
    m = h1.shape[0]
    last = m // tm - 1
    return pl.pallas_call(
        _combine_body,
        grid=(m // tm,),
        in_specs=[pl.BlockSpec((TOP_K * tm,), lambda i: (i,), memory_space=pltpu.SMEM),
                  pl.BlockSpec((TOP_K * tm,), lambda i: (jnp.minimum(i + 1, last),),
                               memory_space=pltpu.SMEM),
                  pl.BlockSpec((tm, D_MODEL), lambda i: (i, 0)),
                  pl.BlockSpec((tm, LANES), lambda i: (i, 0)),
                  pl.BlockSpec((1, D_MODEL), lambda i: (0, 0)),
                  pl.BlockSpec(memory_space=pl.ANY)],
        out_specs=pl.BlockSpec((tm, D_MODEL), lambda i: (i, 0)),
        out_shape=jax.ShapeDtypeStruct((m, D_MODEL), F32),
        scratch_shapes=[pltpu.VMEM((2, TOP_K, tm, yb.shape[1]), yb.dtype),
                        pltpu.SemaphoreType.DMA((2,))],
        compiler_params=pltpu.CompilerParams(dimension_semantics=("arbitrary",)),
        name="combine",
    )(dest_flat, dest_flat, h1, rg, g_fin.reshape(1, -1), yb)


def _moe(h1, a2p, ri, rg, cnt, g_fin, w_gate, w_up, w_down):
    m = h1.shape[0]
    n_blocks = TOP_K * m // EXP_TB + N_EXPERTS
    n_slots = n_blocks * EXP_TB
    counts = cnt[0, N_GROUPS:N_GROUPS + N_EXPERTS].astype(jnp.int32)
    padded = (counts + EXP_TB - 1) // EXP_TB * EXP_TB
    pend = jnp.cumsum(padded)
    base = pend - padded
    eid = ri[:, 0:TOP_K]
    sel = eid[:, :, None] == jnp.arange(N_EXPERTS, dtype=jnp.int32)[None, None, :]
    dest = jnp.sum(jnp.where(sel, base[None, None, :], 0), axis=-1) + ri[:, TOP_K:2 * TOP_K]
    dest_flat = dest.reshape(-1).astype(jnp.int32)
    blk_start = jnp.arange(n_blocks, dtype=jnp.int32) * EXP_TB
    block_eid = jnp.minimum(jnp.sum(blk_start[:, None] >= pend[None, :], axis=1),
                            N_EXPERTS - 1).astype(jnp.int32)
    n_used = (pend[-1:] // EXP_TB).astype(jnp.int32)
    lastblk = (pend - EXP_TB).astype(jnp.int32)

    xb = _dispatch(lastblk, padded.astype(jnp.int32), n_used, dest_flat, a2p, n_slots, tm=512)
    yb = _experts(block_eid, n_used, xb, w_gate, w_up, w_down)
    return _combine(dest_flat, h1, rg, g_fin, yb, tm=256)


ND = 2176
NCIRC = 2 * ND
NH = ND // 2
SEQ = 2048
L_TOT = SEQ + N_META
LH = L_TOT // 2
LHP = 1152
LHR = 1040
HY_CT = 256
HY_FB = NH // 2
HY_IB = ((0, 528), (528, 512))
HY_CB = LH // 3
HALO = 8


def _dft_matrices():
    a = 2 * jnp.arange(NH, dtype=jnp.int32) + 1
    t_hi = jnp.arange(LHP // LANES, dtype=jnp.int32) * LANES
    t_lo = jnp.arange(LANES, dtype=jnp.int32)
    valid = ((t_hi[:, None] + t_lo[None, :]) < LH).reshape(1, LHP)

    def cos_sin(m, denom):
        ang = m.astype(F32) * (math.pi / denom)
        return jnp.cos(ang), jnp.sin(ang)

    c_hi, s_hi = cos_sin((a[:, None] * ((4 * t_hi) % (4 * NCIRC))[None, :]) % (4 * NCIRC), 2 * NCIRC)
    c_hi, s_hi = c_hi[:, :, None], s_hi[:, :, None]
    out = []
    for c in (1, 3):
        c_lo, s_lo = cos_sin((a[:, None] * (4 * t_lo + c)[None, :]) % (4 * NCIRC), 2 * NCIRC)
        c_lo, s_lo = c_lo[:, None, :], s_lo[:, None, :]
        cos_phi = jnp.where(valid, (c_hi * c_lo - s_hi * s_lo).reshape(NH, LHP), 0.0)
        sin_phi = jnp.where(valid, (s_hi * c_lo + c_hi * s_lo).reshape(NH, LHP), 0.0)
        out.append(jnp.concatenate([cos_phi, sin_phi if c == 1 else -sin_phi], axis=0).astype(BF16))
    ce, co = out
    return ce, co, ce.T, co.T


def _forward_half_spectrum(ce_ref, co_ref, xe, xo, j0):
    a1 = jnp.dot(ce_ref[j0:j0 + HY_FB, :], xe, preferred_element_type=F32)
    a2 = jnp.dot(ce_ref[NH + j0:NH + j0 + HY_FB, :], xe, preferred_element_type=F32)
    b1 = jnp.dot(co_ref[j0:j0 + HY_FB, :], xo, preferred_element_type=F32)
    b2 = jnp.dot(co_ref[NH + j0:NH + j0 + HY_FB, :], xo, preferred_element_type=F32)
    return a1 + b1, a2 - b2, a2 + b2, a1 - b1


def _spectra_body(se_ref, so_ref, de_ref, do_ref, ce_ref, co_ref, rot_ref, hr_ref, hi_ref):
    for j0 in range(0, NH, HY_FB):
        ar1, _, ar2, _ = _forward_half_spectrum(ce_ref, co_ref, se_ref[0], so_ref[0], j0)
        _, q1, _, q2 = _forward_half_spectrum(ce_ref, co_ref, de_ref[0], do_ref[0], j0)
        for r0, ar, q in ((j0, ar1, q1), (NH + j0, ar2, q2)):
            c = rot_ref[r0:r0 + HY_FB, 0:1]
            s = rot_ref[r0:r0 + HY_FB, 1:2]
            hr_ref[0, r0:r0 + HY_FB, :] = c * ar + s * q
            hi_ref[0, r0:r0 + HY_FB, :] = s * ar - c * q


def _spectra(h_fwd, h_bwd, ce, co):
    hbs = jnp.concatenate([h_bwd[1:], jnp.zeros_like(h_bwd[:1])], axis=0)

    def split(x):
        x = jnp.transpose(x, (1, 0, 2))
        pad = ((0, 0), (0, LHP - LH), (0, 0))
        return (jnp.pad(x[:, 0::2], pad).astype(BF16), jnp.pad(x[:, 1::2], pad).astype(BF16))

    se, so = split(h_fwd + hbs)
    de, do = split(h_fwd - hbs)
    j = jnp.arange(NH, dtype=F32)
    k = jnp.concatenate([j, ND - 1 - j])
    half = (2 * k + 1) * (math.pi / (2 * NCIRC))
    scale = 2.0 / NCIRC
    rot = jnp.stack([jnp.cos(half) * scale, jnp.sin(half) * scale], axis=1)
    rot = jnp.pad(rot, ((0, 0), (0, LANES - 2)))
    n_ord, ct = HYENA_ORDER, HY_CT
    col = pl.BlockSpec((1, LHP, ct), lambda o, c: (o, 0, c))
    mat = pl.BlockSpec((ND, LHP), lambda o, c: (0, 0))
    out = pl.BlockSpec((1, ND, ct), lambda o, c: (o, 0, c))
    return pl.pallas_call(
        _spectra_body,
        grid=(n_ord, D_HYENA // ct),
        in_specs=[col, col, col, col, mat, mat, pl.BlockSpec((ND, LANES), lambda o, c: (0, 0))],
        out_specs=[out, out],
        out_shape=[jax.ShapeDtypeStruct((n_ord, ND, D_HYENA), F32)] * 2,
        compiler_params=pltpu.CompilerParams(
            dimension_semantics=("arbitrary", "arbitrary"), vmem_limit_bytes=48 * 1024 * 1024),
        name="spectra",
    )(se, so, de, do, ce, co, rot)


def _hyena_body(pv_ref, px1_ref, px2_ref, mv_ref, mx1_ref, mx2_ref,
                wv_ref, wx1_ref, wx2_ref, bv_ref, bx1_ref, bx2_ref, skip_ref,
                ce_ref, co_ref, cet_ref, cot_ref, hr_ref, hi_ref, o_ref,
                stage_s, ze_s, zo_s, g1e_s, g1o_s, g2e_s, g2o_s, zbe_s, zbo_s, we_s, wo_s):
    ct = o_ref.shape[-1]
    nl = ct // LANES

    def strided(off, r0, rows):
        return jnp.concatenate(
            [stage_s.at[j][pl.ds(HALO + off + 2 * r0, rows, stride=2), :] for j in range(nl)], axis=1)

    def short_conv(p_ref, m_ref, w_ref, b_ref, dst_e, dst_o):
        for j in range(nl):
            ln = slice(j * LANES, (j + 1) * LANES)
            stage_s[j, 0:HALO, :] = jnp.zeros((HALO, LANES), F32)
            stage_s[j, HALO:HALO + N_META, :] = m_ref[:, ln].astype(F32)
            stage_s[j, HALO + N_META:HALO + L_TOT, :] = p_ref[0, :, ln].astype(F32)
            stage_s[j, HALO + L_TOT:HALO + L_TOT + HALO, :] = jnp.zeros((HALO, LANES), F32)
        w = w_ref[...]
        b = b_ref[...]
        for r0 in range(0, LH, HY_CB):
            sm1, s0, s1, s2 = (strided(off, r0, HY_CB) for off in (-1, 0, 1, 2))
            dst_e[r0:r0 + HY_CB, :] = b + sm1 * w[0:1] + s0 * w[1:2] + s1 * w[2:3]
            dst_o[r0:r0 + HY_CB, :] = b + s0 * w[0:1] + s1 * w[1:2] + s2 * w[2:3]
        dst_e[LH:LHR, :] = jnp.zeros((LHR - LH, ct), F32)
        dst_o[LH:LHR, :] = jnp.zeros((LHR - LH, ct), F32)

    short_conv(pv_ref, mv_ref, wv_ref, bv_ref, ze_s, zo_s)
    short_conv(px1_ref, mx1_ref, wx1_ref, bx1_ref, g1e_s, g1o_s)
    short_conv(px2_ref, mx2_ref, wx2_ref, bx2_ref, g2e_s, g2o_s)
    zbe_s[LH:LHP, :] = jnp.zeros((LHP - LH, ct), BF16)
    zbo_s[LH:LHP, :] = jnp.zeros((LHP - LH, ct), BF16)

    for n, (ge_s, go_s) in enumerate(((g1e_s, g1o_s), (g2e_s, g2o_s))):
        zbe_s[0:LH, :] = ze_s[0:LH, :].astype(BF16)
        zbo_s[0:LH, :] = zo_s[0:LH, :].astype(BF16)
        for j0 in range(0, NH, HY_FB):
            p1, q1, p2, q2 = _forward_half_spectrum(ce_ref, co_ref, zbe_s[...], zbo_s[...], j0)
            h1r = hr_ref[n, j0:j0 + HY_FB, :]
            h1i = hi_ref[n, j0:j0 + HY_FB, :]
            h2r = hr_ref[n, NH + j0:NH + j0 + HY_FB, :]
            h2i = hi_ref[n, NH + j0:NH + j0 + HY_FB, :]
            yr1 = p1 * h1r + q1 * h1i
            ny1 = q1 * h1r - p1 * h1i
            yr2 = p2 * h2r + q2 * h2i
            ny2 = q2 * h2r - p2 * h2i
            we_s[j0:j0 + HY_FB, :] = (yr1 + ny2).astype(BF16)
            we_s[NH + j0:NH + j0 + HY_FB, :] = (yr2 + ny1).astype(BF16)
            wo_s[j0:j0 + HY_FB, :] = (yr1 - ny2).astype(BF16)
            wo_s[NH + j0:NH + j0 + HY_FB, :] = (yr2 - ny1).astype(BF16)
        skip = skip_ref[n:n + 1, :]
        for parity, (ct_ref, w_s, z_s, g_s) in enumerate(((cet_ref, we_s, ze_s, ge_s),
                                                          (cot_ref, wo_s, zo_s, go_s))):
            for r0, rb in HY_IB:
                y = jnp.dot(ct_ref[r0:r0 + rb, :], w_s[...], preferred_element_type=F32)
                znew = g_s[r0:r0 + rb, :] * (y + z_s[r0:r0 + rb, :] * skip)
                if n == 0:
                    z_s[r0:r0 + rb, :] = znew
                else:
                    n_valid = min(r0 + rb, LH) - r0
                    for j in range(nl):
                        stage_s.at[j][pl.ds(HALO + parity + 2 * r0, n_valid, stride=2), :] = (
                            znew[0:n_valid, j * LANES:(j + 1) * LANES])
    for j in range(nl):
        o_ref[0, :, j * LANES:(j + 1) * LANES] = stage_s[j, HALO + N_META:HALO + L_TOT, :]


def _hyena(p_hy, pm_hy, conv_w, conv_b, skip, mats, hr, hi):
    bsz = p_hy.shape[0]
    nct = D_HYENA // HY_CT
    ct = HY_CT
    ce, co, cet, cot = mats

    def part(j):
        return [pl.BlockSpec((1, SEQ, ct), lambda c, b, j=j: (b, 0, j * nct + c))]

    def mpart(j):
        return [pl.BlockSpec((N_META, ct), lambda c, b, j=j: (0, j * nct + c))]

    def wpart(rows, j):
        return [pl.BlockSpec((rows, ct), lambda c, b, j=j: (0, j * nct + c))]

    spec = pl.BlockSpec((HYENA_ORDER, ND, ct), lambda c, b: (0, 0, c), pipeline_mode=pl.Buffered(1))
    in_specs = (part(0) + part(1) + part(2) + mpart(0) + mpart(1) + mpart(2)
                + wpart(3, 0) + wpart(3, 1) + wpart(3, 2)
                + wpart(1, 0) + wpart(1, 1) + wpart(1, 2)
                + [pl.BlockSpec((HYENA_ORDER, ct), lambda c, b: (0, c)),
                   pl.BlockSpec((ND, LHP), lambda c, b: (0, 0)),
                   pl.BlockSpec((ND, LHP), lambda c, b: (0, 0)),
                   pl.BlockSpec((LHP, ND), lambda c, b: (0, 0)),
                   pl.BlockSpec((LHP, ND), lambda c, b: (0, 0)),
                   spec, spec])
    cb = conv_b.reshape(1, -1)
    f32_half = pltpu.VMEM((LHR, ct), F32)
    return pl.pallas_call(
        _hyena_body,
        grid=(nct, bsz),
        in_specs=in_specs,
        out_specs=pl.BlockSpec((1, SEQ, ct), lambda c, b: (b, 0, c)),
        out_shape=jax.ShapeDtypeStruct((bsz, SEQ, D_HYENA), F32),
        scratch_shapes=[pltpu.VMEM((ct // LANES, L_TOT + 2 * HALO, LANES), F32),
                        f32_half, f32_half, f32_half, f32_half, f32_half, f32_half,
                        pltpu.VMEM((LHP, ct), BF16), pltpu.VMEM((LHP, ct), BF16),
                        pltpu.VMEM((ND, ct), BF16), pltpu.VMEM((ND, ct), BF16)],
        compiler_params=pltpu.CompilerParams(
            dimension_semantics=("arbitrary", "arbitrary"), vmem_limit_bytes=60 * 1024 * 1024),
        name="hyena",
    )(p_hy, p_hy, p_hy, pm_hy, pm_hy, pm_hy, conv_w, conv_w, conv_w, cb, cb, cb, skip,
      ce, co, cet, cot, hr, hi)


N_CHUNKS = SEQ // CHUNK
NT_DIMS = (((1,), (1,)), ((), ()))
TN_DIMS = (((0,), (0,)), ((), ()))
MID_F = CHUNK // 2
MID_B = CHUNK // 2 - 1


HG_G = 8
HG_ROWS = HG_G * CHUNK


def _split2(x):
    hi = x.astype(BF16)
    lo = (x - hi.astype(F32)).astype(BF16)
    return hi, lo


def _chunk_prefix_matrix():
    r = lax.broadcasted_iota(jnp.int32, (HG_ROWS, HG_ROWS), 0)
    c = lax.broadcasted_iota(jnp.int32, (HG_ROWS, HG_ROWS), 1)
    return (((r // CHUNK) == (c // CHUNK)) & (c <= r)).astype(BF16)


def _chunk_rows(x, row):
    return jnp.concatenate(
        [jnp.broadcast_to(x[g * CHUNK + row:g * CHUNK + row + 1, :], (CHUNK, x.shape[1]))
         for g in range(x.shape[0] // CHUNK)], axis=0)


def _hgrn_body(q_ref, ff_ref, fb_ref, i_ref, g_ref, mff_ref, mi_ref,
               lbf_ref, lbb_ref, nw_ref, o_ref,
               tri_s, qe_s, sc_s, ut_s, dec_s, st_s):
    hd = HGRN_HEAD_DIM
    row = lax.broadcasted_iota(jnp.int32, (CHUNK, CHUNK), 0)
    col = lax.broadcasted_iota(jnp.int32, (CHUNK, CHUNK), 1)
    lower = row >= col
    upper = col >= row
    lbf = lbf_ref[...]
    lbb = lbb_ref[...]

    @pl.when((pl.program_id(0) == 0) & (pl.program_id(1) == 0))
    def _():
        tri_s[...] = _chunk_prefix_matrix()

    def forget(logit, lb):
        f = lb + (1.0 - lb) * jax.nn.sigmoid(logit)
        return 1.0 - f, jnp.log(f)

    def prefix_sums(lf):
        s = jnp.dot(tri_s[...], jnp.concatenate(_split2(lf), axis=1), preferred_element_type=F32)
        return s[:, :hd] + s[:, hd:]

    k_m, lf_m = forget(mff_ref[...], lbf)
    pad = jnp.zeros((CHUNK - N_META, hd), F32)
    lf_m = jnp.concatenate([pad, lf_m] * HG_G, axis=0)
    b_m = prefix_sums(lf_m)[0:CHUNK]
    kl_m = jnp.concatenate([pad, k_m], axis=0) * jnp.exp(b_m[CHUNK - 1:CHUNK] - b_m)
    v_m = jnp.concatenate([pad, mi_ref[...]], axis=0).astype(BF16)
    st_meta = lax.dot_general(v_m, kl_m.astype(BF16), TN_DIMS, preferred_element_type=F32)

    def phase_a(j, carry):
        r0 = pl.multiple_of(j * HG_ROWS, HG_ROWS)
        rows = pl.ds(r0, HG_ROWS)
        qv = jax.nn.silu(q_ref[0, rows, :])
        vb = i_ref[0, rows, :].astype(BF16)
        k_f, lf_f = forget(ff_ref[0, rows, :], lbf)
        k_b, lf_b = forget(fb_ref[0, rows, :], lbb)
        b_f = prefix_sums(lf_f)
        p_b = prefix_sums(lf_b)
        bmid_f = _chunk_rows(b_f, MID_F)
        blast_f = _chunk_rows(b_f, CHUNK - 1)
        tot_b = _chunk_rows(p_b, CHUNK - 1)
        c_b = tot_b - p_b + lf_b
        cmid_b = _chunk_rows(c_b, MID_B)
        d_f = b_f - bmid_f
        d_b = c_b - cmid_b
        e_f = jnp.exp(d_f)
        e_b = jnp.exp(d_b)
        qs_f = qv * e_f
        ks_f = k_f / e_f
        qs_b = qv * e_b
        ks_b = k_b / e_b
        qs_fb, ks_fb = qs_f.astype(BF16), ks_f.astype(BF16)
        qs_bb, ks_bb = qs_b.astype(BF16), ks_b.astype(BF16)
        for g in range(HG_G):
            sl = slice(g * CHUNK, (g + 1) * CHUNK)
            r1 = slice(g * CHUNK, g * CHUNK + 1)
            n = j * HG_G + g
            rows_g = pl.ds(pl.multiple_of(r0 + g * CHUNK, CHUNK), CHUNK)
            sc_f = lax.dot_general(qs_fb[sl], ks_fb[sl], NT_DIMS, preferred_element_type=F32)
            sc_b = lax.dot_general(qs_bb[sl], ks_bb[sl], NT_DIMS, preferred_element_type=F32)
            sc_s[n] = (jnp.where(lower, sc_f, 0.0) + jnp.where(upper, sc_b, 0.0)).astype(BF16)
            em_f = jnp.exp(bmid_f[r1])
            el_f = jnp.exp(blast_f[r1] - bmid_f[r1])
            em_b = jnp.exp(cmid_b[r1])
            el_b = jnp.exp(tot_b[r1] - cmid_b[r1])
            qe_s[rows_g, :] = jnp.concatenate(
                [qs_f[sl] * em_f, qs_b[sl] * em_b], axis=1).astype(BF16)
            kl = jnp.concatenate([ks_f[sl] * el_f, ks_b[sl] * el_b], axis=1).astype(BF16)
            ut_s[n] = lax.dot_general(vb[sl], kl, TN_DIMS, preferred_element_type=F32)
            dec_s[n] = jnp.concatenate([jnp.exp(blast_f[r1]), jnp.exp(tot_b[r1])], axis=1)
        return carry

    lax.fori_loop(0, N_CHUNKS // HG_G, phase_a, 0, unroll=True)

    st_f = st_meta
    st_b = jnp.zeros((hd, hd), F32)
    for n in range(N_CHUNKS):
        st_s[n, :, 0:hd] = st_f.astype(BF16)
        st_f = dec_s[n, :, 0:hd] * st_f + ut_s[n, :, 0:hd]
        m = N_CHUNKS - 1 - n
        st_s[m, :, hd:2 * hd] = st_b.astype(BF16)
        st_b = dec_s[m, :, hd:2 * hd] * st_b + ut_s[m, :, hd:2 * hd]

    nw = nw_ref[...]

    def phase_c(j, carry):
        r0 = pl.multiple_of(j * HG_ROWS, HG_ROWS)
        rows = pl.ds(r0, HG_ROWS)
        vb = i_ref[0, rows, :].astype(BF16)
        outs = []
        for g in range(HG_G):
            sl = slice(g * CHUNK, (g + 1) * CHUNK)
            n = j * HG_G + g
            rows_g = pl.ds(pl.multiple_of(r0 + g * CHUNK, CHUNK), CHUNK)
            o = jnp.dot(sc_s[n], vb[sl], preferred_element_type=F32)
            outs.append(o + lax.dot_general(qe_s[rows_g, :], st_s[n], NT_DIMS,
                                            preferred_element_type=F32))
        o = jnp.concatenate(outs, axis=0)
        o = o * lax.rsqrt(jnp.mean(o * o, axis=-1, keepdims=True) + EPS)
        o_ref[0, rows, :] = o * nw * jax.nn.silu(g_ref[0, rows, :])
        return carry

    lax.fori_loop(0, N_CHUNKS // HG_G, phase_c, 0, unroll=True)


def _hgrn(phg_x, phg_m, lb_f, lb_b, norm_w):
    bsz = phg_x.shape[0]
    hd = HGRN_HEAD_DIM
    nh = HGRN_HEADS

    def part(j):
        return pl.BlockSpec((1, SEQ, hd), lambda b, h, j=j: (b, 0, j * nh + h))

    def mpart(j):
        return pl.BlockSpec((N_META, hd), lambda b, h, j=j: (0, j * nh + h))

    vec = pl.BlockSpec((1, hd), lambda b, h: (0, h))
    return pl.pallas_call(
        _hgrn_body,
        grid=(bsz, nh),
        in_specs=[part(0), part(1), part(2), part(3), part(4), mpart(1), mpart(3), vec, vec, vec],
        out_specs=pl.BlockSpec((1, SEQ, hd), lambda b, h: (b, 0, h)),
        out_shape=jax.ShapeDtypeStruct((bsz, SEQ, D_HGRN), F32),
        scratch_shapes=[pltpu.VMEM((HG_ROWS, HG_ROWS), BF16),
                        pltpu.VMEM((SEQ, 2 * hd), BF16),
                        pltpu.VMEM((N_CHUNKS, CHUNK, CHUNK), BF16),
                        pltpu.VMEM((N_CHUNKS, hd, 2 * hd), F32),
                        pltpu.VMEM((N_CHUNKS, 1, 2 * hd), F32),
                        pltpu.VMEM((N_CHUNKS, hd, 2 * hd), BF16)],
        compiler_params=pltpu.CompilerParams(
            dimension_semantics=("arbitrary", "arbitrary"), vmem_limit_bytes=40 * 1024 * 1024),
        name="hgrn",
    )(phg_x, phg_x, phg_x, phg_x, phg_x, phg_m, phg_m,
      lb_f.reshape(1, -1), lb_b.reshape(1, -1), norm_w.reshape(1, -1))


def _hyena_filters(L, w1, b1, w2, b2, w3, freq):
    pos = jnp.arange(L, dtype=F32)
    t = pos / max(L - 1, 1)
    bands = jnp.linspace(1e-4, FILTER_BANDS - 1, FILTER_BANDS, dtype=F32)
    ang = (2.0 * math.pi / L) * pos[:, None] * bands[None, :]
    z = jnp.concatenate([t[:, None], jnp.cos(ang), -jnp.sin(ang)], axis=-1)
    hp = lax.Precision.HIGHEST
    hid = jnp.sin(freq * (jnp.dot(z, w1, precision=hp) + b1))
    hid = jnp.sin(freq * (jnp.dot(hid, w2, precision=hp) + b2))
    filt = jnp.dot(hid, w3, precision=hp).reshape(L, 2, HYENA_ORDER, D_HYENA)
    deltas = jnp.abs(jnp.linspace(math.log(DECAY_TARGET) / SLOW_DECAY_PCT,
                                  math.log(DECAY_TARGET) / FAST_DECAY_PCT, D_HYENA, dtype=F32))
    window = jnp.exp(-t[:, None] * deltas[None, :])
    filt = filt * window[:, None, None, :]
    return filt[:, 0], filt[:, 1]


def kernel(x, meta_tokens, w_in, conv_w, conv_b, filt_w1, filt_b1, filt_w2, filt_b2, filt_w3,
           filt_freq, filt_skip, hyena_norm, lb_fwd, lb_bwd, hgrn_norm, w_out, norm_mix, norm_ffn,
           w_router_group, w_router_expert, w_gate, w_up, w_down, norm_final):
    B, S, D = x.shape
    L = S + N_META
    lbf = jnp.cumsum(jax.nn.softmax(lb_fwd, axis=0), axis=0)[0]
    lbb = jnp.cumsum(jax.nn.softmax(lb_bwd, axis=0), axis=0)[0]

    w_in_b = w_in[0].astype(BF16)
    xf = x.reshape(B * S, D)
    phy_x, phg_x = _inproj(xf, norm_mix[0], w_in_b, tm=512)
    phy_m, phg_m = _inproj(meta_tokens, norm_mix[0], w_in_b, tm=N_META)

    mats = _dft_matrices()
    h_fwd, h_bwd = _hyena_filters(L, filt_w1[0], filt_b1[0], filt_w2[0], filt_b2[0], filt_w3[0],
                                  filt_freq[0])
    hr, hi = _spectra(h_fwd, h_bwd, mats[0], mats[1])
    z_hy = _hyena(phy_x.reshape(B, S, D_HYENA_PROJ), phy_m, conv_w[0], conv_b[0], filt_skip[0],
                  mats, hr, hi).reshape(B * S, D_HYENA)

    y_hg = _hgrn(phg_x.reshape(B, S, 5 * D_HGRN), phg_m, lbf, lbb,
                 hgrn_norm[0]).reshape(B * S, D_HGRN)

    w_r = jnp.concatenate([w_router_group[0], w_router_expert[0].reshape(D, N_EXPERTS),
                           jnp.zeros((D, LANES - N_GROUPS - N_EXPERTS), F32)], axis=1).astype(BF16)
    h1, a2p, ri, rg, cnt = _outproj(z_hy, y_hg, xf, hyena_norm[0], norm_ffn[0],
                                    w_out[0].astype(BF16), w_r, tm=512)
    out = _moe(h1, a2p, ri, rg, cnt, norm_final, w_gate[0], w_up[0], w_down[0])
    return out.reshape(B, S, D)
```

```python
import functools
import math

import jax
import jax.numpy as jnp
from jax import lax
from jax.experimental import pallas as pl
from jax.experimental.pallas import tpu as pltpu

D_MODEL = 1024
N_META = 16
D_HYENA = 512
D_HGRN = 512
HYENA_ORDER = 2
SHORT_CONV = 3
FILTER_EMB = 33
FILTER_BANDS = 16
DECAY_TARGET = 1e-2
FAST_DECAY_PCT = 0.3
SLOW_DECAY_PCT = 1.5
HGRN_HEAD_DIM = 128
HGRN_HEADS = D_HGRN // HGRN_HEAD_DIM
CHUNK = 64
N_GROUPS = 8
EXPERTS_PER_GROUP = 8
N_EXPERTS = 64
TOP_K = 2
D_EXPERT = 512
D_HYENA_PROJ = 3 * D_HYENA
D_IN_PROJ = D_HYENA_PROJ + 5 * D_HGRN
EPS = 1e-6

F32 = jnp.float32
BF16 = jnp.bfloat16


def _rms(x, gain):
    return x * lax.rsqrt(jnp.mean(x * x, axis=-1, keepdims=True) + EPS) * gain


HI16 = 0xFFFF0000


def _pack_bf16_pairs(x):
    c = x.shape[1] // 2
    bits = lax.bitcast_convert_type(x.astype(BF16).astype(F32), jnp.uint32)
    return (bits[:, :c] >> 16) | (bits[:, c:] & jnp.uint32(HI16))


def _unpack_bf16_pairs(w):
    lo = lax.bitcast_convert_type(w << 16, F32)
    hi = lax.bitcast_convert_type(w & jnp.uint32(HI16), F32)
    return lo, hi


def _inproj_body(x_ref, g_ref, w_ref, hy_ref, hg_ref, *, tn):
    a = _rms(x_ref[...], g_ref[...]).astype(BF16)
    for j in range(D_IN_PROJ // tn):
        acc = jnp.dot(a, w_ref[:, j * tn:(j + 1) * tn], preferred_element_type=F32)
        if j * tn < D_HYENA_PROJ:
            hy_ref[:, j * tn:(j + 1) * tn] = acc.astype(hy_ref.dtype)
        else:
            c0 = j * tn - D_HYENA_PROJ
            hg_ref[:, c0:c0 + tn] = acc.astype(hg_ref.dtype)


def _inproj(x, gain, w_bf16, tm):
    m, d = x.shape
    n_hg = D_IN_PROJ - D_HYENA_PROJ
    return pl.pallas_call(
        functools.partial(_inproj_body, tn=512),
        grid=(m // tm,),
        in_specs=[pl.BlockSpec((tm, d), lambda i: (i, 0)),
                  pl.BlockSpec((1, d), lambda i: (0, 0)),
                  pl.BlockSpec((d, D_IN_PROJ), lambda i: (0, 0))],
        out_specs=[pl.BlockSpec((tm, D_HYENA_PROJ), lambda i: (i, 0)),
                   pl.BlockSpec((tm, n_hg), lambda i: (i, 0))],
        out_shape=[jax.ShapeDtypeStruct((m, D_HYENA_PROJ), BF16),
                   jax.ShapeDtypeStruct((m, n_hg), F32)],
        compiler_params=pltpu.CompilerParams(
            dimension_semantics=("arbitrary",), vmem_limit_bytes=48 * 1024 * 1024),
        name="inproj",
    )(x, gain.reshape(1, d), w_bf16)


LANES = 128
NEG_BIG = -1e30
OP_SUB = 512
ROW_W = D_MODEL // 2 + LANES


def _outproj_body(zhy_ref, yhg_ref, h0_ref, ghy_ref, gffn_ref, wo_ref, wr_ref,
                  h1_ref, a2p_ref, ri_ref, rg_ref, cnt_ref, tri_s, carry_s):
    i = pl.program_id(0)
    tm = h1_ref.shape[0]
    ts = tri_s.shape[0]

    @pl.when(i == 0)
    def _():
        r = lax.broadcasted_iota(jnp.int32, (ts, ts), 0)
        c = lax.broadcasted_iota(jnp.int32, (ts, ts), 1)
        tri_s[...] = (r > c).astype(BF16)
        carry_s[...] = jnp.zeros_like(carry_s)

    lane = lax.broadcasted_iota(jnp.int32, (ts, LANES), 1)
    is_g = lane < N_GROUPS
    carry = carry_s[...]
    for r0 in range(0, tm, ts):
        rows = slice(r0, r0 + ts)
        yhy = _rms(zhy_ref[rows, :], ghy_ref[...]).astype(BF16)
        yhg = yhg_ref[rows, :].astype(BF16)
        acc = jnp.dot(yhy, wo_ref[:D_HYENA, :], preferred_element_type=F32)
        acc = acc + jnp.dot(yhg, wo_ref[D_HYENA:, :], preferred_element_type=F32)
        h1 = h0_ref[rows, :] + acc
        h1_ref[rows, :] = h1
        a2 = _rms(h1, gffn_ref[...])
        a2p_ref[rows, :D_MODEL // 2] = _pack_bf16_pairs(a2)
        lg = jnp.dot(a2.astype(BF16), wr_ref[...], preferred_element_type=F32)

        gl = jnp.where(is_g, lg, NEG_BIG)
        gmax = jnp.max(gl, axis=1, keepdims=True)
        gsel = jnp.min(jnp.where(gl == gmax, lane, LANES), axis=1, keepdims=True)
        gden = jnp.sum(jnp.where(is_g, jnp.exp(gl - gmax), 0.0), axis=1, keepdims=True)
        p_group = 1.0 / gden
        in_grp = (lane >= N_GROUPS) & (lane < N_GROUPS + N_EXPERTS) & (
            ((lane - N_GROUPS) >> 3) == gsel)
        el = jnp.where(in_grp, lg, NEG_BIG)
        m1 = jnp.max(el, axis=1, keepdims=True)
        i1 = jnp.min(jnp.where(el == m1, lane, LANES), axis=1, keepdims=True)
        el2 = jnp.where(lane == i1, NEG_BIG, el)
        m2 = jnp.max(el2, axis=1, keepdims=True)
        i2 = jnp.min(jnp.where(el2 == m2, lane, LANES), axis=1, keepdims=True)
        r21 = jnp.exp(m2 - m1)
        gate1 = p_group / (1.0 + r21)
        gate2 = gate1 * r21

        hit1 = lane == i1
        hit2 = lane == i2
        onehot = (hit1 | hit2).astype(BF16)
        pre = jnp.dot(tri_s[...], onehot, preferred_element_type=F32) + carry
        pos1 = jnp.sum(jnp.where(hit1, pre, 0.0), axis=1, keepdims=True).astype(jnp.int32)
        pos2 = jnp.sum(jnp.where(hit2, pre, 0.0), axis=1, keepdims=True).astype(jnp.int32)
        carry = carry + jnp.sum(onehot.astype(F32), axis=0, keepdims=True)

        zero_i = jnp.zeros((ts, LANES), jnp.int32)
        ri_ref[rows, :] = jnp.where(lane == 0, i1 - N_GROUPS,
                          jnp.where(lane == 1, i2 - N_GROUPS,
                          jnp.where(lane == 2, pos1, jnp.where(lane == 3, pos2, zero_i))))
        rg_ref[rows, :] = jnp.where(lane == 0, gate1, jnp.where(lane == 1, gate2, 0.0))
        tok = lax.broadcasted_iota(jnp.int32, (ts, LANES), 0) + (i * tm + r0)
        tag = jnp.where(lane == 0, tok, jnp.where(lane == 1, i1 - N_GROUPS,
                                                  jnp.where(lane == 2, 1, zero_i)))
        a2p_ref[rows, D_MODEL // 2:] = tag.astype(jnp.uint32)
    carry_s[...] = carry
    cnt_ref[...] = carry


def _outproj(zhy, yhg, h0, g_hy, g_ffn, wo_bf16, wr_bf16, tm):
    m = h0.shape[0]
    return pl.pallas_call(
        _outproj_body,
        grid=(m // tm,),
        in_specs=[pl.BlockSpec((tm, D_HYENA), lambda i: (i, 0)),
                  pl.BlockSpec((tm, D_HGRN), lambda i: (i, 0)),
                  pl.BlockSpec((tm, D_MODEL), lambda i: (i, 0)),
                  pl.BlockSpec((1, D_HYENA), lambda i: (0, 0)),
                  pl.BlockSpec((1, D_MODEL), lambda i: (0, 0)),
                  pl.BlockSpec((D_MODEL, D_MODEL), lambda i: (0, 0)),
                  pl.BlockSpec((D_MODEL, LANES), lambda i: (0, 0))],
        out_specs=[pl.BlockSpec((tm, D_MODEL), lambda i: (i, 0)),
                   pl.BlockSpec((tm, ROW_W), lambda i: (i, 0)),
                   pl.BlockSpec((tm, LANES), lambda i: (i, 0)),
                   pl.BlockSpec((tm, LANES), lambda i: (i, 0)),
                   pl.BlockSpec((1, LANES), lambda i: (0, 0))],
        out_shape=[jax.ShapeDtypeStruct((m, D_MODEL), F32),
                   jax.ShapeDtypeStruct((m, ROW_W), jnp.uint32),
                   jax.ShapeDtypeStruct((m, LANES), jnp.int32),
                   jax.ShapeDtypeStruct((m, LANES), F32),
                   jax.ShapeDtypeStruct((1, LANES), F32)],
        scratch_shapes=[pltpu.VMEM((OP_SUB, OP_SUB), BF16), pltpu.VMEM((1, LANES), F32)],
        compiler_params=pltpu.CompilerParams(
            dimension_semantics=("arbitrary",), vmem_limit_bytes=48 * 1024 * 1024),
        name="outproj",
    )(zhy, yhg, h0, g_hy.reshape(1, -1), g_ffn.reshape(1, -1), wo_bf16, wr_bf16)


EXP_TB = 256


def _wait_rows(ref, n_rows, sem):
    pltpu.make_async_copy(ref.at[pl.ds(0, n_rows), :], ref.at[pl.ds(n_rows, n_rows), :], sem).wait()


def _dispatch_body(lastblk_ref, npad_ref, nused_ref, dest_ref, h1_ref, xb_ref, zero_s, sem_z, sem):
    i = pl.program_id(0)
    tm = dest_ref.shape[0] // TOP_K
    n_blocks = xb_ref.shape[0] // EXP_TB

    @pl.when(i == 0)
    def _():
        zero_s[...] = jnp.zeros_like(zero_s)

        def zero_copy(row0):
            row0 = pl.multiple_of(row0, EXP_TB)
            return pltpu.make_async_copy(zero_s, xb_ref.at[pl.ds(row0, EXP_TB), :], sem_z)

        for e in range(N_EXPERTS):
            @pl.when(npad_ref[e] > 0)
            def _():
                zero_copy(lastblk_ref[e]).start()
        for e in range(N_EXPERTS):
            @pl.when(npad_ref[e] > 0)
            def _():
                zero_copy(lastblk_ref[e]).wait()

        def start_tail(b, carry):
            zero_copy(b * EXP_TB).start()
            return carry

        def wait_tail(b, carry):
            zero_copy(b * EXP_TB).wait()
            return carry

        lax.fori_loop(nused_ref[0], n_blocks, start_tail, 0)
        lax.fori_loop(nused_ref[0], n_blocks, wait_tail, 0)

    def issue(g, carry):
        for u in range(8):
            t = g * 8 + u
            src = h1_ref.at[g, pl.ds(u, 1), :]
            pltpu.make_async_copy(src, xb_ref.at[pl.ds(dest_ref[2 * t], 1), :], sem).start()
            pltpu.make_async_copy(src, xb_ref.at[pl.ds(dest_ref[2 * t + 1], 1), :], sem).start()
        return carry

    lax.fori_loop(0, tm // 8, issue, 0, unroll=2)
    _wait_rows(xb_ref, TOP_K * tm, sem)


def _dispatch(lastblk, npad, n_used, dest_flat, rows, n_slots, tm):
    m, w = rows.shape
    grid_spec = pltpu.PrefetchScalarGridSpec(
        num_scalar_prefetch=3,
        grid=(m // tm,),
        in_specs=[pl.BlockSpec((TOP_K * tm,), lambda i, lb, npd, nu: (i,), memory_space=pltpu.SMEM),
                  pl.BlockSpec((tm // 8, 8, w), lambda i, lb, npd, nu: (i, 0, 0))],
        out_specs=pl.BlockSpec(memory_space=pl.ANY),
        scratch_shapes=[pltpu.VMEM((EXP_TB, w), rows.dtype),
                        pltpu.SemaphoreType.DMA(()), pltpu.SemaphoreType.DMA(())],
    )
    return pl.pallas_call(
        _dispatch_body,
        grid_spec=grid_spec,
        out_shape=jax.ShapeDtypeStruct((n_slots, w), rows.dtype),
        compiler_params=pltpu.CompilerParams(dimension_semantics=("arbitrary",)),
        name="dispatch",
    )(lastblk, npad, n_used, dest_flat, rows.reshape(m // 8, 8, w))


def _expert_body(eid_ref, nused_ref, pinfo_ref, xb_ref, wg_ref, wu_ref, wd_ref, y2_ref,
                 wg_s, wu_s, wd_s, res_s, sem):
    i = pl.program_id(0)
    half = D_MODEL // 2
    landing = y2_ref.shape[0] - EXP_TB
    cur = i % 2

    @pl.when(i == 0)
    def _():
        res_s[...] = jnp.zeros_like(res_s)

    @pl.when(i <= nused_ref[0])
    def _():
        prev_e = eid_ref[jnp.maximum(i - 1, 0)]

        @pl.when((i == 0) | (eid_ref[i] != prev_e))
        def _():
            wg_s[...] = wg_ref[0].astype(BF16)
            wu_s[...] = wu_ref[0].astype(BF16)
            wd_s[...] = wd_ref[0].astype(BF16)

        live = i > 0
        for g in range(EXP_TB // 8):
            for u in range(8):
                r = g * 8 + u
                p = jnp.where(live, pinfo_ref[r], landing + r)
                pltpu.make_async_copy(res_s.at[1 - cur, g, pl.ds(u, 1), :],
                                      y2_ref.at[pl.ds(p, 1), :], sem).start()

        lo, hi = _unpack_bf16_pairs(xb_ref[:, :half])
        lo, hi = lo.astype(BF16), hi.astype(BF16)

        def proj(w_s):
            return (jnp.dot(lo, w_s[:half, :], preferred_element_type=F32)
                    + jnp.dot(hi, w_s[half:, :], preferred_element_type=F32))

        gate = proj(wg_s)
        up = proj(wu_s)
        hmid = (gate * jax.nn.sigmoid(gate) * up).astype(BF16)
        y = _pack_bf16_pairs(jnp.dot(hmid, wd_s[...], preferred_element_type=F32))
        res_s[cur] = y.reshape(EXP_TB // 8, 8, half)
        _wait_rows(y2_ref, EXP_TB, sem)


def _experts(block_eid, n_used, pinfo, xb, w_gate, w_up, w_down, n_tok):
    n_slots, w = xb.shape
    half = D_MODEL // 2
    n_blocks = n_slots // EXP_TB

    def blk(i, e, nu):
        return jnp.minimum(i, nu[0] - 1)

    grid_spec = pltpu.PrefetchScalarGridSpec(
        num_scalar_prefetch=2,
        grid=(n_blocks + 1,),
        in_specs=[pl.BlockSpec((EXP_TB,), lambda i, e, nu: (jnp.clip(i - 1, 0, n_blocks - 1),),
                               memory_space=pltpu.SMEM),
                  pl.BlockSpec((EXP_TB, w), lambda i, e, nu: (blk(i, e, nu), 0)),
                  pl.BlockSpec((1, D_MODEL, D_EXPERT), lambda i, e, nu: (e[blk(i, e, nu)], 0, 0)),
                  pl.BlockSpec((1, D_MODEL, D_EXPERT), lambda i, e, nu: (e[blk(i, e, nu)], 0, 0)),
                  pl.BlockSpec((1, D_EXPERT, D_MODEL), lambda i, e, nu: (e[blk(i, e, nu)], 0, 0))],
        out_specs=pl.BlockSpec(memory_space=pl.ANY),
        scratch_shapes=[pltpu.VMEM((D_MODEL, D_EXPERT), BF16),
                        pltpu.VMEM((D_MODEL, D_EXPERT), BF16),
                        pltpu.VMEM((D_EXPERT, D_MODEL), BF16),
                        pltpu.VMEM((2, EXP_TB // 8, 8, half), jnp.uint32),
                        pltpu.SemaphoreType.DMA(())],
    )
    return pl.pallas_call(
        _expert_body,
        grid_spec=grid_spec,
        out_shape=jax.ShapeDtypeStruct((TOP_K * n_tok + EXP_TB, half), jnp.uint32),
        compiler_params=pltpu.CompilerParams(
            dimension_semantics=("arbitrary",), vmem_limit_bytes=48 * 1024 * 1024),
        name="experts",
    )(block_eid, n_used, pinfo, xb, w_gate, w_up, w_down)


def _combine_body(h1_ref, rg_ref, gfin_ref, y1_ref, y2_ref, o_ref):
    rg = rg_ref[...]
    g1, g2 = rg[:, 0:1], rg[:, 1:2]
    lo1, hi1 = _unpack_bf16_pairs(y1_ref[...])
    lo2, hi2 = _unpack_bf16_pairs(y2_ref[...])
    half = D_MODEL // 2
    h2_lo = h1_ref[:, :half] + g1 * lo1 + g2 * lo2
    h2_hi = h1_ref[:, half:] + g1 * hi1 + g2 * hi2
    ms = (jnp.sum(h2_lo * h2_lo, axis=-1, keepdims=True)
          + jnp.sum(h2_hi * h2_hi, axis=-1, keepdims=True)) * (1.0 / D_MODEL)
    inv = lax.rsqrt(ms + EPS)
    o_ref[:, :half] = h2_lo * inv * gfin_ref[:, :half]
    o_ref[:, half:] = h2_hi * inv * gfin_ref[:, half:]


def _combine(h1, rg, g_fin, y2, tm):
    m = h1.shape[0]
    half = D_MODEL // 2
    return pl.pallas_call(
        _combine_body,
        grid=(m // tm,),
        in_specs=[pl.BlockSpec((tm, D_MODEL), lambda i: (i, 0)),
                  pl.BlockSpec((tm, LANES), lambda i: (i, 0)),
                  pl.BlockSpec((1, D_MODEL), lambda i: (0, 0)),
                  pl.BlockSpec((tm, half), lambda i: (i, 0)),
                  pl.BlockSpec((tm, half), lambda i: (m // tm + i, 0))],
        out_specs=pl.BlockSpec((tm, D_MODEL), lambda i: (i, 0)),
        out_shape=jax.ShapeDtypeStruct((m, D_MODEL), F32),
        compiler_params=pltpu.CompilerParams(dimension_semantics=("arbitrary",)),
        name="combine",
    )(h1, rg, g_fin.reshape(1, -1), y2, y2)


def _moe(h1, a2p, ri, rg, cnt, g_fin, w_gate, w_up, w_down):
    m = h1.shape[0]
    half = D_MODEL // 2
    n_blocks = TOP_K * m // EXP_TB + N_EXPERTS
    n_slots = n_blocks * EXP_TB
    counts = cnt[0, N_GROUPS:N_GROUPS + N_EXPERTS].astype(jnp.int32)
    padded = (counts + EXP_TB - 1) // EXP_TB * EXP_TB
    pend = jnp.cumsum(padded)
    base = pend - padded
    eid = ri[:, 0:TOP_K]
    sel = eid[:, :, None] == jnp.arange(N_EXPERTS, dtype=jnp.int32)[None, None, :]
    dest = jnp.sum(jnp.where(sel, base[None, None, :], 0), axis=-1) + ri[:, TOP_K:2 * TOP_K]
    dest_flat = dest.reshape(-1).astype(jnp.int32)
    blk_start = jnp.arange(n_blocks, dtype=jnp.int32) * EXP_TB
    block_eid = jnp.minimum(jnp.sum(blk_start[:, None] >= pend[None, :], axis=1),
                            N_EXPERTS - 1).astype(jnp.int32)
    n_used = (pend[-1:] // EXP_TB).astype(jnp.int32)
    lastblk = (pend - EXP_TB).astype(jnp.int32)

    xb = _dispatch(lastblk, padded.astype(jnp.int32), n_used, dest_flat, a2p, n_slots, tm=512)
    tags = xb[:, half:half + 3].astype(jnp.int32).reshape(n_blocks, EXP_TB, 3)
    k_of = (tags[:, :, 1] != block_eid[:, None]).astype(jnp.int32)
    landing = TOP_K * m + jnp.arange(EXP_TB, dtype=jnp.int32)[None, :]
    pinfo = jnp.where(tags[:, :, 2] > 0, k_of * m + tags[:, :, 0], landing).reshape(-1)
    y2 = _experts(block_eid, n_used, pinfo, xb, w_gate, w_up, w_down, m)
    return _combine(h1, rg, g_fin, y2, tm=512)


ND = 2176
NCIRC = 2 * ND
NH = ND // 2
SEQ = 2048
L_TOT = SEQ + N_META
LH = L_TOT // 2
LHP = 1152
LHR = 1040
HY_CT = 256
HY_FB = NH // 2
HY_IB = ((0, 528), (528, 512))
HY_CB = LH // 3
HALO = 8


def _dft_matrices():
    a = 2 * jnp.arange(NH, dtype=jnp.int32) + 1
    t_hi = jnp.arange(LHP // LANES, dtype=jnp.int32) * LANES
    t_lo = jnp.arange(LANES, dtype=jnp.int32)
    valid = ((t_hi[:, None] + t_lo[None, :]) < LH).reshape(1, LHP)

    def cos_sin(m, denom):
        ang = m.astype(F32) * (math.pi / denom)
        return jnp.cos(ang), jnp.sin(ang)

    c_hi, s_hi = cos_sin((a[:, None] * ((4 * t_hi) % (4 * NCIRC))[None, :]) % (4 * NCIRC), 2 * NCIRC)
    c_hi, s_hi = c_hi[:, :, None], s_hi[:, :, None]
    out = []
    for c in (1, 3):
        c_lo, s_lo = cos_sin((a[:, None] * (4 * t_lo + c)[None, :]) % (4 * NCIRC), 2 * NCIRC)
        c_lo, s_lo = c_lo[:, None, :], s_lo[:, None, :]
        cos_phi = jnp.where(valid, (c_hi * c_lo - s_hi * s_lo).reshape(NH, LHP), 0.0)
        sin_phi = jnp.where(valid, (s_hi * c_lo + c_hi * s_lo).reshape(NH, LHP), 0.0)
        out.append(jnp.concatenate([cos_phi, sin_phi if c == 1 else -sin_phi], axis=0).astype(BF16))
    ce, co = out
    return ce, co, ce.T, co.T


def _forward_half_spectrum(ce_ref, co_ref, xe, xo, j0):
    a1 = jnp.dot(ce_ref[j0:j0 + HY_FB, :], xe, preferred_element_type=F32)
    a2 = jnp.dot(ce_ref[NH + j0:NH + j0 + HY_FB, :], xe, preferred_element_type=F32)
    b1 = jnp.dot(co_ref[j0:j0 + HY_FB, :], xo, preferred_element_type=F32)
    b2 = jnp.dot(co_ref[NH + j0:NH + j0 + HY_FB, :], xo, preferred_element_type=F32)
    return a1 + b1, a2 - b2, a2 + b2, a1 - b1


def _spectra_body(se_ref, so_ref, de_ref, do_ref, ce_ref, co_ref, rot_ref, hr_ref, hi_ref):
    for j0 in range(0, NH, HY_FB):
        ar1, _, ar2, _ = _forward_half_spectrum(ce_ref, co_ref, se_ref[0], so_ref[0], j0)
        _, q1, _, q2 = _forward_half_spectrum(ce_ref, co_ref, de_ref[0], do_ref[0], j0)
        for r0, ar, q in ((j0, ar1, q1), (NH + j0, ar2, q2)):
            c = rot_ref[r0:r0 + HY_FB, 0:1]
            s = rot_ref[r0:r0 + HY_FB, 1:2]
            hr_ref[0, r0:r0 + HY_FB, :] = c * ar + s * q
            hi_ref[0, r0:r0 + HY_FB, :] = s * ar - c * q


def _spectra(h_fwd, h_bwd, ce, co):
    hbs = jnp.concatenate([h_bwd[1:], jnp.zeros_like(h_bwd[:1])], axis=0)

    def split(x):
        x = jnp.transpose(x, (1, 0, 2))
        pad = ((0, 0), (0, LHP - LH), (0, 0))
        return (jnp.pad(x[:, 0::2], pad).astype(BF16), jnp.pad(x[:, 1::2], pad).astype(BF16))

    se, so = split(h_fwd + hbs)
    de, do = split(h_fwd - hbs)
    j = jnp.arange(NH, dtype=F32)
    k = jnp.concatenate([j, ND - 1 - j])
    half = (2 * k + 1) * (math.pi / (2 * NCIRC))
    scale = 2.0 / NCIRC
    rot = jnp.stack([jnp.cos(half) * scale, jnp.sin(half) * scale], axis=1)
    rot = jnp.pad(rot, ((0, 0), (0, LANES - 2)))
    n_ord, ct = HYENA_ORDER, HY_CT
    col = pl.BlockSpec((1, LHP, ct), lambda o, c: (o, 0, c))
    mat = pl.BlockSpec((ND, LHP), lambda o, c: (0, 0))
    out = pl.BlockSpec((1, ND, ct), lambda o, c: (o, 0, c))
    return pl.pallas_call(
        _spectra_body,
        grid=(n_ord, D_HYENA // ct),
        in_specs=[col, col, col, col, mat, mat, pl.BlockSpec((ND, LANES), lambda o, c: (0, 0))],
        out_specs=[out, out],
        out_shape=[jax.ShapeDtypeStruct((n_ord, ND, D_HYENA), F32)] * 2,
        compiler_params=pltpu.CompilerParams(
            dimension_semantics=("arbitrary", "arbitrary"), vmem_limit_bytes=48 * 1024 * 1024),
        name="spectra",
    )(se, so, de, do, ce, co, rot)


def _hyena_body(pv_ref, px1_ref, px2_ref, mv_ref, mx1_ref, mx2_ref,
                wv_ref, wx1_ref, wx2_ref, bv_ref, bx1_ref, bx2_ref, skip_ref,
                ce_ref, co_ref, cet_ref, cot_ref, hr_ref, hi_ref, o_ref,
                stage_s, ze_s, zo_s, g1e_s, g1o_s, g2e_s, g2o_s, zbe_s, zbo_s, we_s, wo_s):
    ct = o_ref.shape[-1]
    nl = ct // LANES

    def strided(off, r0, rows):
        return jnp.concatenate(
            [stage_s.at[j][pl.ds(HALO + off + 2 * r0, rows, stride=2), :] for j in range(nl)], axis=1)

    def short_conv(p_ref, m_ref, w_ref, b_ref, dst_e, dst_o):
        for j in range(nl):
            ln = slice(j * LANES, (j + 1) * LANES)
            stage_s[j, 0:HALO, :] = jnp.zeros((HALO, LANES), F32)
            stage_s[j, HALO:HALO + N_META, :] = m_ref[:, ln].astype(F32)
            stage_s[j, HALO + N_META:HALO + L_TOT, :] = p_ref[0, :, ln].astype(F32)
            stage_s[j, HALO + L_TOT:HALO + L_TOT + HALO, :] = jnp.zeros((HALO, LANES), F32)
        w = w_ref[...]
        b = b_ref[...]
        for r0 in range(0, LH, HY_CB):
            sm1, s0, s1, s2 = (strided(off, r0, HY_CB) for off in (-1, 0, 1, 2))
            dst_e[r0:r0 + HY_CB, :] = b + sm1 * w[0:1] + s0 * w[1:2] + s1 * w[2:3]
            dst_o[r0:r0 + HY_CB, :] = b + s0 * w[0:1] + s1 * w[1:2] + s2 * w[2:3]
        dst_e[LH:LHR, :] = jnp.zeros((LHR - LH, ct), F32)
        dst_o[LH:LHR, :] = jnp.zeros((LHR - LH, ct), F32)

    short_conv(pv_ref, mv_ref, wv_ref, bv_ref, ze_s, zo_s)
    short_conv(px1_ref, mx1_ref, wx1_ref, bx1_ref, g1e_s, g1o_s)
    short_conv(px2_ref, mx2_ref, wx2_ref, bx2_ref, g2e_s, g2o_s)
    zbe_s[LH:LHP, :] = jnp.zeros((LHP - LH, ct), BF16)
    zbo_s[LH:LHP, :] = jnp.zeros((LHP - LH, ct), BF16)

    for n, (ge_s, go_s) in enumerate(((g1e_s, g1o_s), (g2e_s, g2o_s))):
        zbe_s[0:LH, :] = ze_s[0:LH, :].astype(BF16)
        zbo_s[0:LH, :] = zo_s[0:LH, :].astype(BF16)
        for j0 in range(0, NH, HY_FB):
            p1, q1, p2, q2 = _forward_half_spectrum(ce_ref, co_ref, zbe_s[...], zbo_s[...], j0)
            h1r = hr_ref[n, j0:j0 + HY_FB, :]
            h1i = hi_ref[n, j0:j0 + HY_FB, :]
            h2r = hr_ref[n, NH + j0:NH + j0 + HY_FB, :]
            h2i = hi_ref[n, NH + j0:NH + j0 + HY_FB, :]
            yr1 = p1 * h1r + q1 * h1i
            ny1 = q1 * h1r - p1 * h1i
            yr2 = p2 * h2r + q2 * h2i
            ny2 = q2 * h2r - p2 * h2i
            we_s[j0:j0 + HY_FB, :] = (yr1 + ny2).astype(BF16)
            we_s[NH + j0:NH + j0 + HY_FB, :] = (yr2 + ny1).astype(BF16)
            wo_s[j0:j0 + HY_FB, :] = (yr1 - ny2).astype(BF16)
            wo_s[NH + j0:NH + j0 + HY_FB, :] = (yr2 - ny1).astype(BF16)
        skip = skip_ref[n:n + 1, :]
        for parity, (ct_ref, w_s, z_s, g_s) in enumerate(((cet_ref, we_s, ze_s, ge_s),
                                                          (cot_ref, wo_s, zo_s, go_s))):
            for r0, rb in HY_IB:
                y = jnp.dot(ct_ref[r0:r0 + rb, :], w_s[...], preferred_element_type=F32)
                znew = g_s[r0:r0 + rb, :] * (y + z_s[r0:r0 + rb, :] * skip)
                if n == 0:
                    z_s[r0:r0 + rb, :] = znew
                else:
                    n_valid = min(r0 + rb, LH) - r0
                    for j in range(nl):
                        stage_s.at[j][pl.ds(HALO + parity + 2 * r0, n_valid, stride=2), :] = (
                            znew[0:n_valid, j * LANES:(j + 1) * LANES])
    for j in range(nl):
        o_ref[0, :, j * LANES:(j + 1) * LANES] = stage_s[j, HALO + N_META:HALO + L_TOT, :]


def _hyena(p_hy, pm_hy, conv_w, conv_b, skip, mats, hr, hi):
    bsz = p_hy.shape[0]
    nct = D_HYENA // HY_CT
    ct = HY_CT
    ce, co, cet, cot = mats

    def part(j):
        return [pl.BlockSpec((1, SEQ, ct), lambda c, b, j=j: (b, 0, j * nct + c))]

    def mpart(j):
        return [pl.BlockSpec((N_META, ct), lambda c, b, j=j: (0, j * nct + c))]

    def wpart(rows, j):
        return [pl.BlockSpec((rows, ct), lambda c, b, j=j: (0, j * nct + c))]

    spec = pl.BlockSpec((HYENA_ORDER, ND, ct), lambda c, b: (0, 0, c), pipeline_mode=pl.Buffered(1))
    in_specs = (part(0) + part(1) + part(2) + mpart(0) + mpart(1) + mpart(2)
                + wpart(3, 0) + wpart(3, 1) + wpart(3, 2)
                + wpart(1, 0) + wpart(1, 1) + wpart(1, 2)
                + [pl.BlockSpec((HYENA_ORDER, ct), lambda c, b: (0, c)),
                   pl.BlockSpec((ND, LHP), lambda c, b: (0, 0)),
                   pl.BlockSpec((ND, LHP), lambda c, b: (0, 0)),
                   pl.BlockSpec((LHP, ND), lambda c, b: (0, 0)),
                   pl.BlockSpec((LHP, ND), lambda c, b: (0, 0)),
                   spec, spec])
    cb = conv_b.reshape(1, -1)
    f32_half = pltpu.VMEM((LHR, ct), F32)
    return pl.pallas_call(
        _hyena_body,
        grid=(nct, bsz),
        in_specs=in_specs,
        out_specs=pl.BlockSpec((1, SEQ, ct), lambda c, b: (b, 0, c)),
        out_shape=jax.ShapeDtypeStruct((bsz, SEQ, D_HYENA), F32),
        scratch_shapes=[pltpu.VMEM((ct // LANES, L_TOT + 2 * HALO, LANES), F32),
                        f32_half, f32_half, f32_half, f32_half, f32_half, f32_half,
                        pltpu.VMEM((LHP, ct), BF16), pltpu.VMEM((LHP, ct), BF16),
                        pltpu.VMEM((ND, ct), BF16), pltpu.VMEM((ND, ct), BF16)],
        compiler_params=pltpu.CompilerParams(
            dimension_semantics=("arbitrary", "arbitrary"), vmem_limit_bytes=60 * 1024 * 1024),
        name="hyena",
    )(p_hy, p_hy, p_hy, pm_hy, pm_hy, pm_hy, conv_w, conv_w, conv_w, cb, cb, cb, skip,
      ce, co, cet, cot, hr, hi)


N_CHUNKS = SEQ // CHUNK
NT_DIMS = (((1,), (1,)), ((), ()))
TN_DIMS = (((0,), (0,)), ((), ()))
MID_F = CHUNK // 2
MID_B = CHUNK // 2 - 1


HG_G = 8
HG_ROWS = HG_G * CHUNK


def _split2(x):
    hi = x.astype(BF16)
    lo = (x - hi.astype(F32)).astype(BF16)
    return hi, lo


def _chunk_prefix_matrix():
    r = lax.broadcasted_iota(jnp.int32, (HG_ROWS, HG_ROWS), 0)
    c = lax.broadcasted_iota(jnp.int32, (HG_ROWS, HG_ROWS), 1)
    return (((r // CHUNK) == (c // CHUNK)) & (c <= r)).astype(BF16)


def _chunk_rows(x, row):
    return jnp.concatenate(
        [jnp.broadcast_to(x[g * CHUNK + row:g * CHUNK + row + 1, :], (CHUNK, x.shape[1]))
         for g in range(x.shape[0] // CHUNK)], axis=0)


def _hgrn_body(q_ref, ff_ref, fb_ref, i_ref, g_ref, mff_ref, mi_ref,
               lbf_ref, lbb_ref, nw_ref, o_ref,
               tri_s, qe_s, sc_s, ut_s, dec_s, st_s):
    hd = HGRN_HEAD_DIM
    row = lax.broadcasted_iota(jnp.int32, (CHUNK, CHUNK), 0)
    col = lax.broadcasted_iota(jnp.int32, (CHUNK, CHUNK), 1)
    lower = row >= col
    upper = col >= row
    lbf = lbf_ref[...]
    lbb = lbb_ref[...]

    @pl.when((pl.program_id(0) == 0) & (pl.program_id(1) == 0))
    def _():
        tri_s[...] = _chunk_prefix_matrix()

    def forget(logit, lb):
        f = lb + (1.0 - lb) * jax.nn.sigmoid(logit)
        return 1.0 - f, jnp.log(f)

    def prefix_sums(lf):
        s = jnp.dot(tri_s[...], jnp.concatenate(_split2(lf), axis=1), preferred_element_type=F32)
        return s[:, :hd] + s[:, hd:]

    k_m, lf_m = forget(mff_ref[...], lbf)
    pad = jnp.zeros((CHUNK - N_META, hd), F32)
    lf_m = jnp.concatenate([pad, lf_m] * HG_G, axis=0)
    b_m = prefix_sums(lf_m)[0:CHUNK]
    kl_m = jnp.concatenate([pad, k_m], axis=0) * jnp.exp(b_m[CHUNK - 1:CHUNK] - b_m)
    v_m = jnp.concatenate([pad, mi_ref[...]], axis=0).astype(BF16)
    st_meta = lax.dot_general(v_m, kl_m.astype(BF16), TN_DIMS, preferred_element_type=F32)

    def phase_a(j, carry):
        r0 = pl.multiple_of(j * HG_ROWS, HG_ROWS)
        rows = pl.ds(r0, HG_ROWS)
        qv = jax.nn.silu(q_ref[0, rows, :])
        vb = i_ref[0, rows, :].astype(BF16)
        k_f, lf_f = forget(ff_ref[0, rows, :], lbf)
        k_b, lf_b = forget(fb_ref[0, rows, :], lbb)
        b_f = prefix_sums(lf_f)
        p_b = prefix_sums(lf_b)
        bmid_f = _chunk_rows(b_f, MID_F)
        blast_f = _chunk_rows(b_f, CHUNK - 1)
        tot_b = _chunk_rows(p_b, CHUNK - 1)
        c_b = tot_b - p_b + lf_b
        cmid_b = _chunk_rows(c_b, MID_B)
        d_f = b_f - bmid_f
        d_b = c_b - cmid_b
        e_f = jnp.exp(d_f)
        e_b = jnp.exp(d_b)
        qs_f = qv * e_f
        ks_f = k_f / e_f
        qs_b = qv * e_b
        ks_b = k_b / e_b
        qs_fb, ks_fb = qs_f.astype(BF16), ks_f.astype(BF16)
        qs_bb, ks_bb = qs_b.astype(BF16), ks_b.astype(BF16)
        for g in range(HG_G):
            sl = slice(g * CHUNK, (g + 1) * CHUNK)
            r1 = slice(g * CHUNK, g * CHUNK + 1)
            n = j * HG_G + g
            rows_g = pl.ds(pl.multiple_of(r0 + g * CHUNK, CHUNK), CHUNK)
            sc_f = lax.dot_general(qs_fb[sl], ks_fb[sl], NT_DIMS, preferred_element_type=F32)
            sc_b = lax.dot_general(qs_bb[sl], ks_bb[sl], NT_DIMS, preferred_element_type=F32)
            sc_s[n] = (jnp.where(lower, sc_f, 0.0) + jnp.where(upper, sc_b, 0.0)).astype(BF16)
            em_f = jnp.exp(bmid_f[r1])
            el_f = jnp.exp(blast_f[r1] - bmid_f[r1])
            em_b = jnp.exp(cmid_b[r1])
            el_b = jnp.exp(tot_b[r1] - cmid_b[r1])
            qe_s[rows_g, :] = jnp.concatenate(
                [qs_f[sl] * em_f, qs_b[sl] * em_b], axis=1).astype(BF16)
            kl = jnp.concatenate([ks_f[sl] * el_f, ks_b[sl] * el_b], axis=1).astype(BF16)
            ut_s[n] = lax.dot_general(vb[sl], kl, TN_DIMS, preferred_element_type=F32)
            dec_s[n] = jnp.concatenate([jnp.exp(blast_f[r1]), jnp.exp(tot_b[r1])], axis=1)
        return carry

    lax.fori_loop(0, N_CHUNKS // HG_G, phase_a, 0, unroll=True)

    st_f = st_meta
    st_b = jnp.zeros((hd, hd), F32)
    for n in range(N_CHUNKS):
        st_s[n, :, 0:hd] = st_f.astype(BF16)
        st_f = dec_s[n, :, 0:hd] * st_f + ut_s[n, :, 0:hd]
        m = N_CHUNKS - 1 - n
        st_s[m, :, hd:2 * hd] = st_b.astype(BF16)
        st_b = dec_s[m, :, hd:2 * hd] * st_b + ut_s[m, :, hd:2 * hd]

    nw = nw_ref[...]

    def phase_c(j, carry):
        r0 = pl.multiple_of(j * HG_ROWS, HG_ROWS)
        rows = pl.ds(r0, HG_ROWS)
        vb = i_ref[0, rows, :].astype(BF16)
        outs = []
        for g in range(HG_G):
            sl = slice(g * CHUNK, (g + 1) * CHUNK)
            n = j * HG_G + g
            rows_g = pl.ds(pl.multiple_of(r0 + g * CHUNK, CHUNK), CHUNK)
            o = jnp.dot(sc_s[n], vb[sl], preferred_element_type=F32)
            outs.append(o + lax.dot_general(qe_s[rows_g, :], st_s[n], NT_DIMS,
                                            preferred_element_type=F32))
        o = jnp.concatenate(outs, axis=0)
        o = o * lax.rsqrt(jnp.mean(o * o, axis=-1, keepdims=True) + EPS)
        o_ref[0, rows, :] = o * nw * jax.nn.silu(g_ref[0, rows, :])
        return carry

    lax.fori_loop(0, N_CHUNKS // HG_G, phase_c, 0, unroll=True)


def _hgrn(phg_x, phg_m, lb_f, lb_b, norm_w):
    bsz = phg_x.shape[0]
    hd = HGRN_HEAD_DIM
    nh = HGRN_HEADS

    def part(j):
        return pl.BlockSpec((1, SEQ, hd), lambda b, h, j=j: (b, 0, j * nh + h))

    def mpart(j):
        return pl.BlockSpec((N_META, hd), lambda b, h, j=j: (0, j * nh + h))

    vec = pl.BlockSpec((1, hd), lambda b, h: (0, h))
    return pl.pallas_call(
        _hgrn_body,
        grid=(bsz, nh),
        in_specs=[part(0), part(1), part(2), part(3), part(4), mpart(1), mpart(3), vec, vec, vec],
        out_specs=pl.BlockSpec((1, SEQ, hd), lambda b, h: (b, 0, h)),
        out_shape=jax.ShapeDtypeStruct((bsz, SEQ, D_HGRN), F32),
        scratch_shapes=[pltpu.VMEM((HG_ROWS, HG_ROWS), BF16),
                        pltpu.VMEM((SEQ, 2 * hd), BF16),
                        pltpu.VMEM((N_CHUNKS, CHUNK, CHUNK), BF16),
                        pltpu.VMEM((N_CHUNKS, hd, 2 * hd), F32),
                        pltpu.VMEM((N_CHUNKS, 1, 2 * hd), F32),
                        pltpu.VMEM((N_CHUNKS, hd, 2 * hd), BF16)],
        compiler_params=pltpu.CompilerParams(
            dimension_semantics=("arbitrary", "arbitrary"), vmem_limit_bytes=40 * 1024 * 1024),
        name="hgrn",
    )(phg_x, phg_x, phg_x, phg_x, phg_x, phg_m, phg_m,
      lb_f.reshape(1, -1), lb_b.reshape(1, -1), norm_w.reshape(1, -1))


def _hyena_filters(L, w1, b1, w2, b2, w3, freq):
    pos = jnp.arange(L, dtype=F32)
    t = pos / max(L - 1, 1)
    bands = jnp.linspace(1e-4, FILTER_BANDS - 1, FILTER_BANDS, dtype=F32)
    ang = (2.0 * math.pi / L) * pos[:, None] * bands[None, :]
    z = jnp.concatenate([t[:, None], jnp.cos(ang), -jnp.sin(ang)], axis=-1)
    hp = lax.Precision.HIGHEST
    hid = jnp.sin(freq * (jnp.dot(z, w1, precision=hp) + b1))
    hid = jnp.sin(freq * (jnp.dot(hid, w2, precision=hp) + b2))
    filt = jnp.dot(hid, w3, precision=hp).reshape(L, 2, HYENA_ORDER, D_HYENA)
    deltas = jnp.abs(jnp.linspace(math.log(DECAY_TARGET) / SLOW_DECAY_PCT,
                                  math.log(DECAY_TARGET) / FAST_DECAY_PCT, D_HYENA, dtype=F32))
    window = jnp.exp(-t[:, None] * deltas[None, :])
    filt = filt * window[:, None, None, :]
    return filt[:, 0], filt[:, 1]


def kernel(x, meta_tokens, w_in, conv_w, conv_b, filt_w1, filt_b1, filt_w2, filt_b2, filt_w3,
           filt_freq, filt_skip, hyena_norm, lb_fwd, lb_bwd, hgrn_norm, w_out, norm_mix, norm_ffn,
           w_router_group, w_router_expert, w_gate, w_up, w_down, norm_final):
    B, S, D = x.shape
    L = S + N_META
    lbf = jnp.cumsum(jax.nn.softmax(lb_fwd, axis=0), axis=0)[0]
    lbb = jnp.cumsum(jax.nn.softmax(lb_bwd, axis=0), axis=0)[0]

    w_in_b = w_in[0].astype(BF16)
    xf = x.reshape(B * S, D)
    phy_x, phg_x = _inproj(xf, norm_mix[0], w_in_b, tm=512)
    phy_m, phg_m = _inproj(meta_tokens, norm_mix[0], w_in_b, tm=N_META)

    mats = _dft_matrices()
    h_fwd, h_bwd = _hyena_filters(L, filt_w1[0], filt_b1[0], filt_w2[0], filt_b2[0], filt_w3[0],
                                  filt_freq[0])
    hr, hi = _spectra(h_fwd, h_bwd, mats[0], mats[1])
    z_hy = _hyena(phy_x.reshape(B, S, D_HYENA_PROJ), phy_m, conv_w[0], conv_b[0], filt_skip[0],
                  mats, hr, hi).reshape(B * S, D_HYENA)

    y_hg = _hgrn(phg_x.reshape(B, S, 5 * D_HGRN), phg_m, lbf, lbb,
                 hgrn_norm[0]).reshape(B * S, D_HGRN)

    w_r = jnp.concatenate([w_router_group[0], w_router_expert[0].reshape(D, N_EXPERTS),
                           jnp.zeros((D, LANES - N_GROUPS - N_EXPERTS), F32)], axis=1).astype(BF16)
    h1, a2p, ri, rg, cnt = _outproj(z_hy, y_hg, xf, hyena_norm[0], norm_ffn[0],
                                    w_out[0].astype(BF16), w_r, tm=1024)
    out = _moe(h1, a2p, ri, rg, cnt, norm_final, w_gate[0], w_up[0], w_down[0])
    return out.reshape(B, S, D)
```

```python
import functools
import math

import jax
import jax.numpy as jnp
from jax import lax
from jax.experimental import pallas as pl
from jax.experimental.pallas import tpu as pltpu

D_MODEL = 1024
N_META = 16
D_HYENA = 512
D_HGRN = 512
HYENA_ORDER = 2
SHORT_CONV = 3
FILTER_EMB = 33
FILTER_BANDS = 16
DECAY_TARGET = 1e-2
FAST_DECAY_PCT = 0.3
SLOW_DECAY_PCT = 1.5
HGRN_HEAD_DIM = 128
HGRN_HEADS = D_HGRN // HGRN_HEAD_DIM
CHUNK = 64
N_GROUPS = 8
EXPERTS_PER_GROUP = 8
N_EXPERTS = 64
TOP_K = 2
D_EXPERT = 512
D_HYENA_PROJ = 3 * D_HYENA
D_IN_PROJ = D_HYENA_PROJ + 5 * D_HGRN
EPS = 1e-6

F32 = jnp.float32
BF16 = jnp.bfloat16


def _rms(x, gain):
    return x * lax.rsqrt(jnp.mean(x * x, axis=-1, keepdims=True) + EPS) * gain


HI16 = 0xFFFF0000


def _pack_bf16_pairs(x):
    c = x.shape[1] // 2
    bits = lax.bitcast_convert_type(x.astype(BF16).astype(F32), jnp.uint32)
    return (bits[:, :c] >> 16) | (bits[:, c:] & jnp.uint32(HI16))


def _unpack_bf16_pairs(w):
    lo = lax.bitcast_convert_type(w << 16, F32)
    hi = lax.bitcast_convert_type(w & jnp.uint32(HI16), F32)
    return lo, hi


def _inproj_body(x_ref, g_ref, w_ref, hy_ref, hg_ref, *, tn):
    a = _rms(x_ref[...], g_ref[...]).astype(BF16)
    for j in range(D_IN_PROJ // tn):
        acc = jnp.dot(a, w_ref[:, j * tn:(j + 1) * tn], preferred_element_type=F32)
        if j * tn < D_HYENA_PROJ:
            hy_ref[:, j * tn:(j + 1) * tn] = acc.astype(hy_ref.dtype)
        else:
            c0 = j * tn - D_HYENA_PROJ
            hg_ref[:, c0:c0 + tn] = acc.astype(hg_ref.dtype)


def _inproj(x, gain, w_bf16, tm):
    m, d = x.shape
    n_hg = D_IN_PROJ - D_HYENA_PROJ
    return pl.pallas_call(
        functools.partial(_inproj_body, tn=512),
        grid=(m // tm,),
        in_specs=[pl.BlockSpec((tm, d), lambda i: (i, 0)),
                  pl.BlockSpec((1, d), lambda i: (0, 0)),
                  pl.BlockSpec((d, D_IN_PROJ), lambda i: (0, 0))],
        out_specs=[pl.BlockSpec((tm, D_HYENA_PROJ), lambda i: (i, 0)),
                   pl.BlockSpec((tm, n_hg), lambda i: (i, 0))],
        out_shape=[jax.ShapeDtypeStruct((m, D_HYENA_PROJ), BF16),
                   jax.ShapeDtypeStruct((m, n_hg), F32)],
        compiler_params=pltpu.CompilerParams(
            dimension_semantics=("arbitrary",), vmem_limit_bytes=48 * 1024 * 1024),
        name="inproj",
    )(x, gain.reshape(1, d), w_bf16)


LANES = 128
NEG_BIG = -1e30
OP_SUB = 512


def _outproj_body(zhy_ref, yhg_ref, h0_ref, ghy_ref, gffn_ref, wo_ref, wr_ref,
                  h1_ref, a2p_ref, ri_ref, rg_ref, cnt_ref, tri_s, carry_s):
    i = pl.program_id(0)
    tm = h1_ref.shape[0]
    ts = tri_s.shape[0]

    @pl.when(i == 0)
    def _():
        r = lax.broadcasted_iota(jnp.int32, (ts, ts), 0)
        c = lax.broadcasted_iota(jnp.int32, (ts, ts), 1)
        tri_s[...] = (r > c).astype(BF16)
        carry_s[...] = jnp.zeros_like(carry_s)

    lane = lax.broadcasted_iota(jnp.int32, (ts, LANES), 1)
    is_g = lane < N_GROUPS
    carry = carry_s[...]
    for r0 in range(0, tm, ts):
        rows = slice(r0, r0 + ts)
        yhy = _rms(zhy_ref[rows, :], ghy_ref[...]).astype(BF16)
        yhg = yhg_ref[rows, :].astype(BF16)
        acc = jnp.dot(yhy, wo_ref[:D_HYENA, :], preferred_element_type=F32)
        acc = acc + jnp.dot(yhg, wo_ref[D_HYENA:, :], preferred_element_type=F32)
        h1 = h0_ref[rows, :] + acc
        h1_ref[rows, :] = h1
        a2 = _rms(h1, gffn_ref[...])
        a2p_ref[rows, :] = _pack_bf16_pairs(a2)
        lg = jnp.dot(a2.astype(BF16), wr_ref[...], preferred_element_type=F32)

        gl = jnp.where(is_g, lg, NEG_BIG)
        gmax = jnp.max(gl, axis=1, keepdims=True)
        gsel = jnp.min(jnp.where(gl == gmax, lane, LANES), axis=1, keepdims=True)
        gden = jnp.sum(jnp.where(is_g, jnp.exp(gl - gmax), 0.0), axis=1, keepdims=True)
        p_group = 1.0 / gden
        in_grp = (lane >= N_GROUPS) & (lane < N_GROUPS + N_EXPERTS) & (
            ((lane - N_GROUPS) >> 3) == gsel)
        el = jnp.where(in_grp, lg, NEG_BIG)
        m1 = jnp.max(el, axis=1, keepdims=True)
        i1 = jnp.min(jnp.where(el == m1, lane, LANES), axis=1, keepdims=True)
        el2 = jnp.where(lane == i1, NEG_BIG, el)
        m2 = jnp.max(el2, axis=1, keepdims=True)
        i2 = jnp.min(jnp.where(el2 == m2, lane, LANES), axis=1, keepdims=True)
        r21 = jnp.exp(m2 - m1)
        gate1 = p_group / (1.0 + r21)
        gate2 = gate1 * r21

        hit1 = lane == i1
        hit2 = lane == i2
        onehot = (hit1 | hit2).astype(BF16)
        pre = jnp.dot(tri_s[...], onehot, preferred_element_type=F32) + carry
        pos1 = jnp.sum(jnp.where(hit1, pre, 0.0), axis=1, keepdims=True).astype(jnp.int32)
        pos2 = jnp.sum(jnp.where(hit2, pre, 0.0), axis=1, keepdims=True).astype(jnp.int32)
        carry = carry + jnp.sum(onehot.astype(F32), axis=0, keepdims=True)

        zero_i = jnp.zeros((ts, LANES), jnp.int32)
        ri_ref[rows, :] = jnp.where(lane == 0, i1 - N_GROUPS,
                          jnp.where(lane == 1, i2 - N_GROUPS,
                          jnp.where(lane == 2, pos1, jnp.where(lane == 3, pos2, zero_i))))
        rg_ref[rows, :] = jnp.where(lane == 0, gate1, jnp.where(lane == 1, gate2, 0.0))
    carry_s[...] = carry
    cnt_ref[...] = carry


def _outproj(zhy, yhg, h0, g_hy, g_ffn, wo_bf16, wr_bf16, tm):
    m = h0.shape[0]
    return pl.pallas_call(
        _outproj_body,
        grid=(m // tm,),
        in_specs=[pl.BlockSpec((tm, D_HYENA), lambda i: (i, 0)),
                  pl.BlockSpec((tm, D_HGRN), lambda i: (i, 0)),
                  pl.BlockSpec((tm, D_MODEL), lambda i: (i, 0)),
                  pl.BlockSpec((1, D_HYENA), lambda i: (0, 0)),
                  pl.BlockSpec((1, D_MODEL), lambda i: (0, 0)),
                  pl.BlockSpec((D_MODEL, D_MODEL), lambda i: (0, 0)),
                  pl.BlockSpec((D_MODEL, LANES), lambda i: (0, 0))],
        out_specs=[pl.BlockSpec((tm, D_MODEL), lambda i: (i, 0)),
                   pl.BlockSpec((tm, D_MODEL // 2), lambda i: (i, 0)),
                   pl.BlockSpec((tm, LANES), lambda i: (i, 0)),
                   pl.BlockSpec((tm, LANES), lambda i: (i, 0)),
                   pl.BlockSpec((1, LANES), lambda i: (0, 0))],
        out_shape=[jax.ShapeDtypeStruct((m, D_MODEL), F32),
                   jax.ShapeDtypeStruct((m, D_MODEL // 2), jnp.uint32),
                   jax.ShapeDtypeStruct((m, LANES), jnp.int32),
                   jax.ShapeDtypeStruct((m, LANES), F32),
                   jax.ShapeDtypeStruct((1, LANES), F32)],
        scratch_shapes=[pltpu.VMEM((OP_SUB, OP_SUB), BF16), pltpu.VMEM((1, LANES), F32)],
        compiler_params=pltpu.CompilerParams(
            dimension_semantics=("arbitrary",), vmem_limit_bytes=48 * 1024 * 1024),
        name="outproj",
    )(zhy, yhg, h0, g_hy.reshape(1, -1), g_ffn.reshape(1, -1), wo_bf16, wr_bf16)


EXP_TB = 256


def _wait_rows(ref, n_rows, sem):
    pltpu.make_async_copy(ref.at[pl.ds(0, n_rows), :], ref.at[pl.ds(n_rows, n_rows), :], sem).wait()


def _dispatch_body(lastblk_ref, npad_ref, nused_ref, dest_ref, h1_ref, xb_ref, zero_s, sem_z, sem):
    i = pl.program_id(0)
    tm = dest_ref.shape[0] // TOP_K
    n_blocks = xb_ref.shape[0] // EXP_TB

    @pl.when(i == 0)
    def _():
        zero_s[...] = jnp.zeros_like(zero_s)

        def zero_copy(row0):
            row0 = pl.multiple_of(row0, EXP_TB)
            return pltpu.make_async_copy(zero_s, xb_ref.at[pl.ds(row0, EXP_TB), :], sem_z)

        for e in range(N_EXPERTS):
            @pl.when(npad_ref[e] > 0)
            def _():
                zero_copy(lastblk_ref[e]).start()
        for e in range(N_EXPERTS):
            @pl.when(npad_ref[e] > 0)
            def _():
                zero_copy(lastblk_ref[e]).wait()

        def start_tail(b, carry):
            zero_copy(b * EXP_TB).start()
            return carry

        def wait_tail(b, carry):
            zero_copy(b * EXP_TB).wait()
            return carry

        lax.fori_loop(nused_ref[0], n_blocks, start_tail, 0)
        lax.fori_loop(nused_ref[0], n_blocks, wait_tail, 0)

    def issue(g, carry):
        for u in range(8):
            t = g * 8 + u
            src = h1_ref.at[g, pl.ds(u, 1), :]
            pltpu.make_async_copy(src, xb_ref.at[pl.ds(dest_ref[2 * t], 1), :], sem).start()
            pltpu.make_async_copy(src, xb_ref.at[pl.ds(dest_ref[2 * t + 1], 1), :], sem).start()
        return carry

    lax.fori_loop(0, tm // 8, issue, 0, unroll=2)
    _wait_rows(xb_ref, TOP_K * tm, sem)


def _dispatch(lastblk, npad, n_used, dest_flat, rows, n_slots, tm):
    m, w = rows.shape
    grid_spec = pltpu.PrefetchScalarGridSpec(
        num_scalar_prefetch=3,
        grid=(m // tm,),
        in_specs=[pl.BlockSpec((TOP_K * tm,), lambda i, lb, npd, nu: (i,), memory_space=pltpu.SMEM),
                  pl.BlockSpec((tm // 8, 8, w), lambda i, lb, npd, nu: (i, 0, 0))],
        out_specs=pl.BlockSpec(memory_space=pl.ANY),
        scratch_shapes=[pltpu.VMEM((EXP_TB, w), rows.dtype),
                        pltpu.SemaphoreType.DMA(()), pltpu.SemaphoreType.DMA(())],
    )
    return pl.pallas_call(
        _dispatch_body,
        grid_spec=grid_spec,
        out_shape=jax.ShapeDtypeStruct((n_slots, w), rows.dtype),
        compiler_params=pltpu.CompilerParams(dimension_semantics=("arbitrary",)),
        name="dispatch",
    )(lastblk, npad, n_used, dest_flat, rows.reshape(m // 8, 8, w))


def _expert_body(eid_ref, nused_ref, xb_ref, wg_ref, wu_ref, wd_ref, o_ref, wg_s, wu_s, wd_s):
    i = pl.program_id(0)
    half = D_MODEL // 2

    @pl.when(i < nused_ref[0])
    def _():
        prev = eid_ref[jnp.maximum(i - 1, 0)]

        @pl.when((i == 0) | (eid_ref[i] != prev))
        def _():
            wg_s[...] = wg_ref[0].astype(BF16)
            wu_s[...] = wu_ref[0].astype(BF16)
            wd_s[...] = wd_ref[0].astype(BF16)

        lo, hi = _unpack_bf16_pairs(xb_ref[...])
        lo, hi = lo.astype(BF16), hi.astype(BF16)

        def proj(w_s):
            return (jnp.dot(lo, w_s[:half, :], preferred_element_type=F32)
                    + jnp.dot(hi, w_s[half:, :], preferred_element_type=F32))

        g = proj(wg_s)
        u = proj(wu_s)
        hmid = (g * jax.nn.sigmoid(g) * u).astype(BF16)
        o_ref[...] = _pack_bf16_pairs(jnp.dot(hmid, wd_s[...], preferred_element_type=F32))

    @pl.when(i >= nused_ref[0])
    def _():
        o_ref[...] = jnp.zeros_like(o_ref)


def _experts(block_eid, n_used, xb, w_gate, w_up, w_down):
    n_slots, w = xb.shape

    def blk(i, e, nu):
        return jnp.minimum(i, nu[0] - 1)

    grid_spec = pltpu.PrefetchScalarGridSpec(
        num_scalar_prefetch=2,
        grid=(n_slots // EXP_TB,),
        in_specs=[pl.BlockSpec((EXP_TB, w), lambda i, e, nu: (blk(i, e, nu), 0)),
                  pl.BlockSpec((1, D_MODEL, D_EXPERT), lambda i, e, nu: (e[blk(i, e, nu)], 0, 0)),
                  pl.BlockSpec((1, D_MODEL, D_EXPERT), lambda i, e, nu: (e[blk(i, e, nu)], 0, 0)),
                  pl.BlockSpec((1, D_EXPERT, D_MODEL), lambda i, e, nu: (e[blk(i, e, nu)], 0, 0))],
        out_specs=pl.BlockSpec((EXP_TB, w), lambda i, e, nu: (i, 0)),
        scratch_shapes=[pltpu.VMEM((D_MODEL, D_EXPERT), BF16),
                        pltpu.VMEM((D_MODEL, D_EXPERT), BF16),
                        pltpu.VMEM((D_EXPERT, D_MODEL), BF16)],
    )
    return pl.pallas_call(
        _expert_body,
        grid_spec=grid_spec,
        out_shape=jax.ShapeDtypeStruct((n_slots, w), jnp.uint32),
        compiler_params=pltpu.CompilerParams(
            dimension_semantics=("arbitrary",), vmem_limit_bytes=48 * 1024 * 1024),
        name="experts",
    )(block_eid, n_used, xb, w_gate, w_up, w_down)


def _combine_body(dest_ref, dnext_ref, h1_ref, rg_ref, gfin_ref, yb_ref, o_ref, y_s, sem):
    i = pl.program_id(0)
    tm = h1_ref.shape[0]
    slot = i % 2

    def issue(idx_ref, sl):
        def body(g, carry):
            for u in range(8):
                t = g * 8 + u
                for k in range(TOP_K):
                    pltpu.make_async_copy(yb_ref.at[pl.ds(idx_ref[TOP_K * t + k], 1), :],
                                          y_s.at[sl, k, g, pl.ds(u, 1), :], sem.at[sl]).start()
            return carry

        lax.fori_loop(0, tm // 8, body, 0, unroll=2)

    @pl.when(i == 0)
    def _():
        issue(dest_ref, 0)

    @pl.when(i + 1 < pl.num_programs(0))
    def _():
        issue(dnext_ref, 1 - slot)

    _wait_rows(yb_ref, TOP_K * tm, sem.at[slot])
    rg = rg_ref[...]
    g1, g2 = rg[:, 0:1], rg[:, 1:2]
    wy = yb_ref.shape[1]
    lo1, hi1 = _unpack_bf16_pairs(y_s[slot, 0].reshape(tm, wy))
    lo2, hi2 = _unpack_bf16_pairs(y_s[slot, 1].reshape(tm, wy))
    half = D_MODEL // 2
    h2_lo = h1_ref[:, :half] + g1 * lo1 + g2 * lo2
    h2_hi = h1_ref[:, half:] + g1 * hi1 + g2 * hi2
    ms = (jnp.sum(h2_lo * h2_lo, axis=-1, keepdims=True)
          + jnp.sum(h2_hi * h2_hi, axis=-1, keepdims=True)) * (1.0 / D_MODEL)
    inv = lax.rsqrt(ms + EPS)
    o_ref[:, :half] = h2_lo * inv * gfin_ref[:, :half]
    o_ref[:, half:] = h2_hi * inv * gfin_ref[:, half:]


def _combine(dest_flat, h1, rg, g_fin, yb, tm):
    m = h1.shape[0]
    last = m // tm - 1
    return pl.pallas_call(
        _combine_body,
        grid=(m // tm,),
        in_specs=[pl.BlockSpec((TOP_K * tm,), lambda i: (i,), memory_space=pltpu.SMEM),
                  pl.BlockSpec((TOP_K * tm,), lambda i: (jnp.minimum(i + 1, last),),
                               memory_space=pltpu.SMEM),
                  pl.BlockSpec((tm, D_MODEL), lambda i: (i, 0)),
                  pl.BlockSpec((tm, LANES), lambda i: (i, 0)),
                  pl.BlockSpec((1, D_MODEL), lambda i: (0, 0)),
                  pl.BlockSpec(memory_space=pl.ANY)],
        out_specs=pl.BlockSpec((tm, D_MODEL), lambda i: (i, 0)),
        out_shape=jax.ShapeDtypeStruct((m, D_MODEL), F32),
        scratch_shapes=[pltpu.VMEM((2, TOP_K, tm // 8, 8, yb.shape[1]), yb.dtype),
                        pltpu.SemaphoreType.DMA((2,))],
        compiler_params=pltpu.CompilerParams(dimension_semantics=("arbitrary",)),
        name="combine",
    )(dest_flat, dest_flat, h1, rg, g_fin.reshape(1, -1), yb)


def _moe(h1, a2p, ri, rg, cnt, g_fin, w_gate, w_up, w_down):
    m = h1.shape[0]
    n_blocks = TOP_K * m // EXP_TB + N_EXPERTS
    n_slots = n_blocks * EXP_TB
    counts = cnt[0, N_GROUPS:N_GROUPS + N_EXPERTS].astype(jnp.int32)
    padded = (counts + EXP_TB - 1) // EXP_TB * EXP_TB
    pend = jnp.cumsum(padded)
    base = pend - padded
    eid = ri[:, 0:TOP_K]
    sel = eid[:, :, None] == jnp.arange(N_EXPERTS, dtype=jnp.int32)[None, None, :]
    dest = jnp.sum(jnp.where(sel, base[None, None, :], 0), axis=-1) + ri[:, TOP_K:2 * TOP_K]
    dest_flat = dest.reshape(-1).astype(jnp.int32)
    blk_start = jnp.arange(n_blocks, dtype=jnp.int32) * EXP_TB
    block_eid = jnp.minimum(jnp.sum(blk_start[:, None] >= pend[None, :], axis=1),
                            N_EXPERTS - 1).astype(jnp.int32)
    n_used = (pend[-1:] // EXP_TB).astype(jnp.int32)
    lastblk = (pend - EXP_TB).astype(jnp.int32)

    xb = _dispatch(lastblk, padded.astype(jnp.int32), n_used, dest_flat, a2p, n_slots, tm=512)
    yb = _experts(block_eid, n_used, xb, w_gate, w_up, w_down)
    return _combine(dest_flat, h1, rg, g_fin, yb, tm=256)


ND = 2176
NCIRC = 2 * ND
NH = ND // 2
SEQ = 2048
L_TOT = SEQ + N_META
LH = L_TOT // 2
LHP = 1152
LHR = 1040
HY_CT = 256
HY_FB = NH // 2
HY_IB = ((0, 528), (528, 512))
HY_CB = LH // 3
HALO = 8


def _dft_matrices():
    a = 2 * jnp.arange(NH, dtype=jnp.int32) + 1
    t_hi = jnp.arange(LHP // LANES, dtype=jnp.int32) * LANES
    t_lo = jnp.arange(LANES, dtype=jnp.int32)
    valid = ((t_hi[:, None] + t_lo[None, :]) < LH).reshape(1, LHP)

    def cos_sin(m, denom):
        ang = m.astype(F32) * (math.pi / denom)
        return jnp.cos(ang), jnp.sin(ang)

    c_hi, s_hi = cos_sin((a[:, None] * ((4 * t_hi) % (4 * NCIRC))[None, :]) % (4 * NCIRC), 2 * NCIRC)
    c_hi, s_hi = c_hi[:, :, None], s_hi[:, :, None]
    out = []
    for c in (1, 3):
        c_lo, s_lo = cos_sin((a[:, None] * (4 * t_lo + c)[None, :]) % (4 * NCIRC), 2 * NCIRC)
        c_lo, s_lo = c_lo[:, None, :], s_lo[:, None, :]
        cos_phi = jnp.where(valid, (c_hi * c_lo - s_hi * s_lo).reshape(NH, LHP), 0.0)
        sin_phi = jnp.where(valid, (s_hi * c_lo + c_hi * s_lo).reshape(NH, LHP), 0.0)
        out.append(jnp.concatenate([cos_phi, sin_phi if c == 1 else -sin_phi], axis=0).astype(BF16))
    ce, co = out
    return ce, co, ce.T, co.T


def _forward_half_spectrum(ce_ref, co_ref, xe, xo, j0):
    a1 = jnp.dot(ce_ref[j0:j0 + HY_FB, :], xe, preferred_element_type=F32)
    a2 = jnp.dot(ce_ref[NH + j0:NH + j0 + HY_FB, :], xe, preferred_element_type=F32)
    b1 = jnp.dot(co_ref[j0:j0 + HY_FB, :], xo, preferred_element_type=F32)
    b2 = jnp.dot(co_ref[NH + j0:NH + j0 + HY_FB, :], xo, preferred_element_type=F32)
    return a1 + b1, a2 - b2, a2 + b2, a1 - b1


def _spectra_body(se_ref, so_ref, de_ref, do_ref, ce_ref, co_ref, rot_ref, hr_ref, hi_ref):
    for j0 in range(0, NH, HY_FB):
        ar1, _, ar2, _ = _forward_half_spectrum(ce_ref, co_ref, se_ref[0], so_ref[0], j0)
        _, q1, _, q2 = _forward_half_spectrum(ce_ref, co_ref, de_ref[0], do_ref[0], j0)
        for r0, ar, q in ((j0, ar1, q1), (NH + j0, ar2, q2)):
            c = rot_ref[r0:r0 + HY_FB, 0:1]
            s = rot_ref[r0:r0 + HY_FB, 1:2]
            hr_ref[0, r0:r0 + HY_FB, :] = c * ar + s * q
            hi_ref[0, r0:r0 + HY_FB, :] = s * ar - c * q


def _spectra(h_fwd, h_bwd, ce, co):
    hbs = jnp.concatenate([h_bwd[1:], jnp.zeros_like(h_bwd[:1])], axis=0)

    def split(x):
        x = x.reshape(LH, 2, HYENA_ORDER, D_HYENA)
        pad = ((0, 0), (0, LHP - LH), (0, 0))
        return tuple(jnp.pad(jnp.transpose(x[:, par], (1, 0, 2)), pad).astype(BF16)
                     for par in range(2))

    se, so = split(h_fwd + hbs)
    de, do = split(h_fwd - hbs)
    j = jnp.arange(NH, dtype=F32)
    k = jnp.concatenate([j, ND - 1 - j])
    half = (2 * k + 1) * (math.pi / (2 * NCIRC))
    scale = 2.0 / NCIRC
    rot = jnp.stack([jnp.cos(half) * scale, jnp.sin(half) * scale], axis=1)
    rot = jnp.pad(rot, ((0, 0), (0, LANES - 2)))
    n_ord, ct = HYENA_ORDER, HY_CT
    col = pl.BlockSpec((1, LHP, ct), lambda o, c: (o, 0, c))
    mat = pl.BlockSpec((ND, LHP), lambda o, c: (0, 0))
    out = pl.BlockSpec((1, ND, ct), lambda o, c: (o, 0, c))
    return pl.pallas_call(
        _spectra_body,
        grid=(n_ord, D_HYENA // ct),
        in_specs=[col, col, col, col, mat, mat, pl.BlockSpec((ND, LANES), lambda o, c: (0, 0))],
        out_specs=[out, out],
        out_shape=[jax.ShapeDtypeStruct((n_ord, ND, D_HYENA), F32)] * 2,
        compiler_params=pltpu.CompilerParams(
            dimension_semantics=("arbitrary", "arbitrary"), vmem_limit_bytes=48 * 1024 * 1024),
        name="spectra",
    )(se, so, de, do, ce, co, rot)


def _hyena_body(pv_ref, px1_ref, px2_ref, mv_ref, mx1_ref, mx2_ref,
                wv_ref, wx1_ref, wx2_ref, bv_ref, bx1_ref, bx2_ref, skip_ref,
                ce_ref, co_ref, cet_ref, cot_ref, hr_ref, hi_ref, o_ref,
                stage_s, ze_s, zo_s, g1e_s, g1o_s, g2e_s, g2o_s, zbe_s, zbo_s, we_s, wo_s):
    ct = o_ref.shape[-1]
    nl = ct // LANES

    def strided(off, r0, rows):
        return jnp.concatenate(
            [stage_s.at[j][pl.ds(HALO + off + 2 * r0, rows, stride=2), :] for j in range(nl)], axis=1)

    def short_conv(p_ref, m_ref, w_ref, b_ref, dst_e, dst_o):
        for j in range(nl):
            ln = slice(j * LANES, (j + 1) * LANES)
            stage_s[j, 0:HALO, :] = jnp.zeros((HALO, LANES), F32)
            stage_s[j, HALO:HALO + N_META, :] = m_ref[:, ln].astype(F32)
            stage_s[j, HALO + N_META:HALO + L_TOT, :] = p_ref[0, :, ln].astype(F32)
            stage_s[j, HALO + L_TOT:HALO + L_TOT + HALO, :] = jnp.zeros((HALO, LANES), F32)
        w = w_ref[...]
        b = b_ref[...]
        for r0 in range(0, LH, HY_CB):
            sm1, s0, s1, s2 = (strided(off, r0, HY_CB) for off in (-1, 0, 1, 2))
            dst_e[r0:r0 + HY_CB, :] = b + sm1 * w[0:1] + s0 * w[1:2] + s1 * w[2:3]
            dst_o[r0:r0 + HY_CB, :] = b + s0 * w[0:1] + s1 * w[1:2] + s2 * w[2:3]
        dst_e[LH:LHR, :] = jnp.zeros((LHR - LH, ct), F32)
        dst_o[LH:LHR, :] = jnp.zeros((LHR - LH, ct), F32)

    short_conv(pv_ref, mv_ref, wv_ref, bv_ref, ze_s, zo_s)
    short_conv(px1_ref, mx1_ref, wx1_ref, bx1_ref, g1e_s, g1o_s)
    short_conv(px2_ref, mx2_ref, wx2_ref, bx2_ref, g2e_s, g2o_s)
    zbe_s[LH:LHP, :] = jnp.zeros((LHP - LH, ct), BF16)
    zbo_s[LH:LHP, :] = jnp.zeros((LHP - LH, ct), BF16)

    for n, (ge_s, go_s) in enumerate(((g1e_s, g1o_s), (g2e_s, g2o_s))):
        zbe_s[0:LH, :] = ze_s[0:LH, :].astype(BF16)
        zbo_s[0:LH, :] = zo_s[0:LH, :].astype(BF16)
        for j0 in range(0, NH, HY_FB):
            p1, q1, p2, q2 = _forward_half_spectrum(ce_ref, co_ref, zbe_s[...], zbo_s[...], j0)
            h1r = hr_ref[n, j0:j0 + HY_FB, :]
            h1i = hi_ref[n, j0:j0 + HY_FB, :]
            h2r = hr_ref[n, NH + j0:NH + j0 + HY_FB, :]
            h2i = hi_ref[n, NH + j0:NH + j0 + HY_FB, :]
            yr1 = p1 * h1r + q1 * h1i
            ny1 = q1 * h1r - p1 * h1i
            yr2 = p2 * h2r + q2 * h2i
            ny2 = q2 * h2r - p2 * h2i
            we_s[j0:j0 + HY_FB, :] = (yr1 + ny2).astype(BF16)
            we_s[NH + j0:NH + j0 + HY_FB, :] = (yr2 + ny1).astype(BF16)
            wo_s[j0:j0 + HY_FB, :] = (yr1 - ny2).astype(BF16)
            wo_s[NH + j0:NH + j0 + HY_FB, :] = (yr2 - ny1).astype(BF16)
        skip = skip_ref[n:n + 1, :]
        for parity, (ct_ref, w_s, z_s, g_s) in enumerate(((cet_ref, we_s, ze_s, ge_s),
                                                          (cot_ref, wo_s, zo_s, go_s))):
            for r0, rb in HY_IB:
                y = jnp.dot(ct_ref[r0:r0 + rb, :], w_s[...], preferred_element_type=F32)
                znew = g_s[r0:r0 + rb, :] * (y + z_s[r0:r0 + rb, :] * skip)
                if n == 0:
                    z_s[r0:r0 + rb, :] = znew
                else:
                    n_valid = min(r0 + rb, LH) - r0
                    for j in range(nl):
                        stage_s.at[j][pl.ds(HALO + parity + 2 * r0, n_valid, stride=2), :] = (
                            znew[0:n_valid, j * LANES:(j + 1) * LANES])
    for j in range(nl):
        o_ref[0, :, j * LANES:(j + 1) * LANES] = stage_s[j, HALO + N_META:HALO + L_TOT, :]


def _hyena(p_hy, pm_hy, conv_w, conv_b, skip, mats, hr, hi):
    bsz = p_hy.shape[0]
    nct = D_HYENA // HY_CT
    ct = HY_CT
    ce, co, cet, cot = mats

    def part(j):
        return [pl.BlockSpec((1, SEQ, ct), lambda c, b, j=j: (b, 0, j * nct + c))]

    def mpart(j):
        return [pl.BlockSpec((N_META, ct), lambda c, b, j=j: (0, j * nct + c))]

    def wpart(rows, j):
        return [pl.BlockSpec((rows, ct), lambda c, b, j=j: (0, j * nct + c))]

    spec = pl.BlockSpec((HYENA_ORDER, ND, ct), lambda c, b: (0, 0, c), pipeline_mode=pl.Buffered(1))
    in_specs = (part(0) + part(1) + part(2) + mpart(0) + mpart(1) + mpart(2)
                + wpart(3, 0) + wpart(3, 1) + wpart(3, 2)
                + wpart(1, 0) + wpart(1, 1) + wpart(1, 2)
                + [pl.BlockSpec((HYENA_ORDER, ct), lambda c, b: (0, c)),
                   pl.BlockSpec((ND, LHP), lambda c, b: (0, 0)),
                   pl.BlockSpec((ND, LHP), lambda c, b: (0, 0)),
                   pl.BlockSpec((LHP, ND), lambda c, b: (0, 0)),
                   pl.BlockSpec((LHP, ND), lambda c, b: (0, 0)),
                   spec, spec])
    cb = conv_b.reshape(1, -1)
    f32_half = pltpu.VMEM((LHR, ct), F32)
    return pl.pallas_call(
        _hyena_body,
        grid=(nct, bsz),
        in_specs=in_specs,
        out_specs=pl.BlockSpec((1, SEQ, ct), lambda c, b: (b, 0, c)),
        out_shape=jax.ShapeDtypeStruct((bsz, SEQ, D_HYENA), F32),
        scratch_shapes=[pltpu.VMEM((ct // LANES, L_TOT + 2 * HALO, LANES), F32),
                        f32_half, f32_half, f32_half, f32_half, f32_half, f32_half,
                        pltpu.VMEM((LHP, ct), BF16), pltpu.VMEM((LHP, ct), BF16),
                        pltpu.VMEM((ND, ct), BF16), pltpu.VMEM((ND, ct), BF16)],
        compiler_params=pltpu.CompilerParams(
            dimension_semantics=("arbitrary", "arbitrary"), vmem_limit_bytes=60 * 1024 * 1024),
        name="hyena",
    )(p_hy, p_hy, p_hy, pm_hy, pm_hy, pm_hy, conv_w, conv_w, conv_w, cb, cb, cb, skip,
      ce, co, cet, cot, hr, hi)


N_CHUNKS = SEQ // CHUNK
NT_DIMS = (((1,), (1,)), ((), ()))
TN_DIMS = (((0,), (0,)), ((), ()))
MID_F = CHUNK // 2
MID_B = CHUNK // 2 - 1


HG_G = 8
HG_ROWS = HG_G * CHUNK


def _split2(x):
    hi = x.astype(BF16)
    lo = (x - hi.astype(F32)).astype(BF16)
    return hi, lo


def _chunk_prefix_matrix():
    r = lax.broadcasted_iota(jnp.int32, (HG_ROWS, HG_ROWS), 0)
    c = lax.broadcasted_iota(jnp.int32, (HG_ROWS, HG_ROWS), 1)
    return (((r // CHUNK) == (c // CHUNK)) & (c <= r)).astype(BF16)


def _chunk_rows(x, row):
    return jnp.concatenate(
        [jnp.broadcast_to(x[g * CHUNK + row:g * CHUNK + row + 1, :], (CHUNK, x.shape[1]))
         for g in range(x.shape[0] // CHUNK)], axis=0)


def _hgrn_body(q_ref, ff_ref, fb_ref, i_ref, g_ref, mff_ref, mi_ref,
               lbf_ref, lbb_ref, nw_ref, o_ref,
               tri_s, qe_s, sc_s, ut_s, dec_s, st_s):
    hd = HGRN_HEAD_DIM
    row = lax.broadcasted_iota(jnp.int32, (CHUNK, CHUNK), 0)
    col = lax.broadcasted_iota(jnp.int32, (CHUNK, CHUNK), 1)
    lower = row >= col
    upper = col >= row
    lbf = lbf_ref[...]
    lbb = lbb_ref[...]

    @pl.when((pl.program_id(0) == 0) & (pl.program_id(1) == 0))
    def _():
        tri_s[...] = _chunk_prefix_matrix()

    def forget(logit, lb):
        f = lb + (1.0 - lb) * jax.nn.sigmoid(logit)
        return 1.0 - f, jnp.log(f)

    def prefix_sums(lf):
        s = jnp.dot(tri_s[...], jnp.concatenate(_split2(lf), axis=1), preferred_element_type=F32)
        return s[:, :hd] + s[:, hd:]

    k_m, lf_m = forget(mff_ref[...], lbf)
    pad = jnp.zeros((CHUNK - N_META, hd), F32)
    lf_m = jnp.concatenate([pad, lf_m] * HG_G, axis=0)
    b_m = prefix_sums(lf_m)[0:CHUNK]
    kl_m = jnp.concatenate([pad, k_m], axis=0) * jnp.exp(b_m[CHUNK - 1:CHUNK] - b_m)
    v_m = jnp.concatenate([pad, mi_ref[...]], axis=0).astype(BF16)
    st_meta = lax.dot_general(v_m, kl_m.astype(BF16), TN_DIMS, preferred_element_type=F32)

    def phase_a(j, carry):
        r0 = pl.multiple_of(j * HG_ROWS, HG_ROWS)
        rows = pl.ds(r0, HG_ROWS)
        qv = jax.nn.silu(q_ref[0, rows, :])
        vb = i_ref[0, rows, :].astype(BF16)
        k_f, lf_f = forget(ff_ref[0, rows, :], lbf)
        k_b, lf_b = forget(fb_ref[0, rows, :], lbb)
        b_f = prefix_sums(lf_f)
        p_b = prefix_sums(lf_b)
        bmid_f = _chunk_rows(b_f, MID_F)
        blast_f = _chunk_rows(b_f, CHUNK - 1)
        tot_b = _chunk_rows(p_b, CHUNK - 1)
        c_b = tot_b - p_b + lf_b
        cmid_b = _chunk_rows(c_b, MID_B)
        d_f = b_f - bmid_f
        d_b = c_b - cmid_b
        e_f = jnp.exp(d_f)
        e_b = jnp.exp(d_b)
        qs_f = qv * e_f
        ks_f = k_f / e_f
        qs_b = qv * e_b
        ks_b = k_b / e_b
        qs_fb, ks_fb = qs_f.astype(BF16), ks_f.astype(BF16)
        qs_bb, ks_bb = qs_b.astype(BF16), ks_b.astype(BF16)
        for g in range(HG_G):
            sl = slice(g * CHUNK, (g + 1) * CHUNK)
            r1 = slice(g * CHUNK, g * CHUNK + 1)
            n = j * HG_G + g
            rows_g = pl.ds(pl.multiple_of(r0 + g * CHUNK, CHUNK), CHUNK)
            sc_f = lax.dot_general(qs_fb[sl], ks_fb[sl], NT_DIMS, preferred_element_type=F32)
            sc_b = lax.dot_general(qs_bb[sl], ks_bb[sl], NT_DIMS, preferred_element_type=F32)
            sc_s[n] = (jnp.where(lower, sc_f, 0.0) + jnp.where(upper, sc_b, 0.0)).astype(BF16)
            em_f = jnp.exp(bmid_f[r1])
            el_f = jnp.exp(blast_f[r1] - bmid_f[r1])
            em_b = jnp.exp(cmid_b[r1])
            el_b = jnp.exp(tot_b[r1] - cmid_b[r1])
            qe_s[rows_g, :] = jnp.concatenate(
                [qs_f[sl] * em_f, qs_b[sl] * em_b], axis=1).astype(BF16)
            kl = jnp.concatenate([ks_f[sl] * el_f, ks_b[sl] * el_b], axis=1).astype(BF16)
            ut_s[n] = lax.dot_general(vb[sl], kl, TN_DIMS, preferred_element_type=F32)
            dec_s[n] = jnp.concatenate([jnp.exp(blast_f[r1]), jnp.exp(tot_b[r1])], axis=1)
        return carry

    lax.fori_loop(0, N_CHUNKS // HG_G, phase_a, 0, unroll=True)

    st_f = st_meta
    st_b = jnp.zeros((hd, hd), F32)
    for n in range(N_CHUNKS):
        st_s[n, :, 0:hd] = st_f.astype(BF16)
        st_f = dec_s[n, :, 0:hd] * st_f + ut_s[n, :, 0:hd]
        m = N_CHUNKS - 1 - n
        st_s[m, :, hd:2 * hd] = st_b.astype(BF16)
        st_b = dec_s[m, :, hd:2 * hd] * st_b + ut_s[m, :, hd:2 * hd]

    nw = nw_ref[...]

    def phase_c(j, carry):
        r0 = pl.multiple_of(j * HG_ROWS, HG_ROWS)
        rows = pl.ds(r0, HG_ROWS)
        vb = i_ref[0, rows, :].astype(BF16)
        outs = []
        for g in range(HG_G):
            sl = slice(g * CHUNK, (g + 1) * CHUNK)
            n = j * HG_G + g
            rows_g = pl.ds(pl.multiple_of(r0 + g * CHUNK, CHUNK), CHUNK)
            o = jnp.dot(sc_s[n], vb[sl], preferred_element_type=F32)
            outs.append(o + lax.dot_general(qe_s[rows_g, :], st_s[n], NT_DIMS,
                                            preferred_element_type=F32))
        o = jnp.concatenate(outs, axis=0)
        o = o * lax.rsqrt(jnp.mean(o * o, axis=-1, keepdims=True) + EPS)
        o_ref[0, rows, :] = o * nw * jax.nn.silu(g_ref[0, rows, :])
        return carry

    lax.fori_loop(0, N_CHUNKS // HG_G, phase_c, 0, unroll=True)


def _hgrn(phg_x, phg_m, lb_f, lb_b, norm_w):
    bsz = phg_x.shape[0]
    hd = HGRN_HEAD_DIM
    nh = HGRN_HEADS

    def part(j):
        return pl.BlockSpec((1, SEQ, hd), lambda b, h, j=j: (b, 0, j * nh + h))

    def mpart(j):
        return pl.BlockSpec((N_META, hd), lambda b, h, j=j: (0, j * nh + h))

    vec = pl.BlockSpec((1, hd), lambda b, h: (0, h))
    return pl.pallas_call(
        _hgrn_body,
        grid=(bsz, nh),
        in_specs=[part(0), part(1), part(2), part(3), part(4), mpart(1), mpart(3), vec, vec, vec],
        out_specs=pl.BlockSpec((1, SEQ, hd), lambda b, h: (b, 0, h)),
        out_shape=jax.ShapeDtypeStruct((bsz, SEQ, D_HGRN), F32),
        scratch_shapes=[pltpu.VMEM((HG_ROWS, HG_ROWS), BF16),
                        pltpu.VMEM((SEQ, 2 * hd), BF16),
                        pltpu.VMEM((N_CHUNKS, CHUNK, CHUNK), BF16),
                        pltpu.VMEM((N_CHUNKS, hd, 2 * hd), F32),
                        pltpu.VMEM((N_CHUNKS, 1, 2 * hd), F32),
                        pltpu.VMEM((N_CHUNKS, hd, 2 * hd), BF16)],
        compiler_params=pltpu.CompilerParams(
            dimension_semantics=("arbitrary", "arbitrary"), vmem_limit_bytes=40 * 1024 * 1024),
        name="hgrn",
    )(phg_x, phg_x, phg_x, phg_x, phg_x, phg_m, phg_m,
      lb_f.reshape(1, -1), lb_b.reshape(1, -1), norm_w.reshape(1, -1))


def _hyena_filters(L, w1, b1, w2, b2, w3, freq):
    pos = jnp.arange(L, dtype=F32)
    t = pos / max(L - 1, 1)
    bands = jnp.linspace(1e-4, FILTER_BANDS - 1, FILTER_BANDS, dtype=F32)
    ang = (2.0 * math.pi / L) * pos[:, None] * bands[None, :]
    z = jnp.concatenate([t[:, None], jnp.cos(ang), -jnp.sin(ang)], axis=-1)
    hp = lax.Precision.HIGHEST
    hid = jnp.sin(freq * (jnp.dot(z, w1, precision=hp) + b1))
    hid = jnp.sin(freq * (jnp.dot(hid, w2, precision=hp) + b2))
    filt = jnp.dot(hid, w3, precision=hp).reshape(L, 2, HYENA_ORDER, D_HYENA)
    deltas = jnp.abs(jnp.linspace(math.log(DECAY_TARGET) / SLOW_DECAY_PCT,
                                  math.log(DECAY_TARGET) / FAST_DECAY_PCT, D_HYENA, dtype=F32))
    window = jnp.exp(-t[:, None] * deltas[None, :])
    filt = filt * window[:, None, None, :]
    return filt[:, 0], filt[:, 1]


def kernel(x, meta_tokens, w_in, conv_w, conv_b, filt_w1, filt_b1, filt_w2, filt_b2, filt_w3,
           filt_freq, filt_skip, hyena_norm, lb_fwd, lb_bwd, hgrn_norm, w_out, norm_mix, norm_ffn,
           w_router_group, w_router_expert, w_gate, w_up, w_down, norm_final):
    B, S, D = x.shape
    L = S + N_META
    lbf = jnp.cumsum(jax.nn.softmax(lb_fwd, axis=0), axis=0)[0]
    lbb = jnp.cumsum(jax.nn.softmax(lb_bwd, axis=0), axis=0)[0]

    w_in_b = w_in[0].astype(BF16)
    xf = x.reshape(B * S, D)
    phy_x, phg_x = _inproj(xf, norm_mix[0], w_in_b, tm=512)
    phy_m, phg_m = _inproj(meta_tokens, norm_mix[0], w_in_b, tm=N_META)

    mats = _dft_matrices()
    h_fwd, h_bwd = _hyena_filters(L, filt_w1[0], filt_b1[0], filt_w2[0], filt_b2[0], filt_w3[0],
                                  filt_freq[0])
    hr, hi = _spectra(h_fwd, h_bwd, mats[0], mats[1])
    z_hy = _hyena(phy_x.reshape(B, S, D_HYENA_PROJ), phy_m, conv_w[0], conv_b[0], filt_skip[0],
                  mats, hr, hi).reshape(B * S, D_HYENA)

    y_hg = _hgrn(phg_x.reshape(B, S, 5 * D_HGRN), phg_m, lbf, lbb,
                 hgrn_norm[0]).reshape(B * S, D_HGRN)

    w_r = jnp.concatenate([w_router_group[0], w_router_expert[0].reshape(D, N_EXPERTS),
                           jnp.zeros((D, LANES - N_GROUPS - N_EXPERTS), F32)], axis=1).astype(BF16)
    h1, a2p, ri, rg, cnt = _outproj(z_hy, y_hg, xf, hyena_norm[0], norm_ffn[0],
                                    w_out[0].astype(BF16), w_r, tm=1024)
    out = _moe(h1, a2p, ri, rg, cnt, norm_final, w_gate[0], w_up[0], w_down[0])
    return out.reshape(B, S, D)
```

```python
import functools
import math

import jax
import jax.numpy as jnp
from jax import lax
from jax.experimental import pallas as pl
from jax.experimental.pallas import tpu as pltpu

D_MODEL = 1024
N_META = 16
D_HYENA = 512
D_HGRN = 512
HYENA_ORDER = 2
SHORT_CONV = 3
FILTER_EMB = 33
FILTER_BANDS = 16
DECAY_TARGET = 1e-2
FAST_DECAY_PCT = 0.3
SLOW_DECAY_PCT = 1.5
HGRN_HEAD_DIM = 128
HGRN_HEADS = D_HGRN // HGRN_HEAD_DIM
CHUNK = 64
N_GROUPS = 8
EXPERTS_PER_GROUP = 8
N_EXPERTS = 64
TOP_K = 2
D_EXPERT = 512
D_HYENA_PROJ = 3 * D_HYENA
D_IN_PROJ = D_HYENA_PROJ + 5 * D_HGRN
EPS = 1e-6

F32 = jnp.float32
BF16 = jnp.bfloat16


def _rms(x, gain):
    return x * lax.rsqrt(jnp.mean(x * x, axis=-1, keepdims=True) + EPS) * gain


HI16 = 0xFFFF0000


def _pack_bf16_pairs(x):
    c = x.shape[1] // 2
    bits = lax.bitcast_convert_type(x.astype(BF16).astype(F32), jnp.uint32)
    return (bits[:, :c] >> 16) | (bits[:, c:] & jnp.uint32(HI16))


def _unpack_bf16_pairs(w):
    lo = lax.bitcast_convert_type(w << 16, F32)
    hi = lax.bitcast_convert_type(w & jnp.uint32(HI16), F32)
    return lo, hi


def _inproj_body(x_ref, g_ref, w_ref, hy_ref, hg_ref, *, tn):
    a = _rms(x_ref[...], g_ref[...]).astype(BF16)
    for j in range(D_IN_PROJ // tn):
        acc = jnp.dot(a, w_ref[:, j * tn:(j + 1) * tn], preferred_element_type=F32)
        if j * tn < D_HYENA_PROJ:
            hy_ref[:, j * tn:(j + 1) * tn] = acc.astype(hy_ref.dtype)
        else:
            c0 = j * tn - D_HYENA_PROJ
            hg_ref[:, c0:c0 + tn] = acc.astype(hg_ref.dtype)


def _inproj(x, gain, w_bf16, tm):
    m, d = x.shape
    n_hg = D_IN_PROJ - D_HYENA_PROJ
    return pl.pallas_call(
        functools.partial(_inproj_body, tn=512),
        grid=(m // tm,),
        in_specs=[pl.BlockSpec((tm, d), lambda i: (i, 0)),
                  pl.BlockSpec((1, d), lambda i: (0, 0)),
                  pl.BlockSpec((d, D_IN_PROJ), lambda i: (0, 0))],
        out_specs=[pl.BlockSpec((tm, D_HYENA_PROJ), lambda i: (i, 0)),
                   pl.BlockSpec((tm, n_hg), lambda i: (i, 0))],
        out_shape=[jax.ShapeDtypeStruct((m, D_HYENA_PROJ), BF16),
                   jax.ShapeDtypeStruct((m, n_hg), F32)],
        compiler_params=pltpu.CompilerParams(
            dimension_semantics=("arbitrary",), vmem_limit_bytes=48 * 1024 * 1024),
        name="inproj",
    )(x, gain.reshape(1, d), w_bf16)


LANES = 128
NEG_BIG = -1e30
OP_SUB = 512


def _outproj_body(zhy_ref, yhg_ref, h0_ref, ghy_ref, gffn_ref, wo_ref, wr_ref,
                  h1_ref, a2p_ref, ri_ref, rg_ref, cnt_ref, tri_s, carry_s):
    i = pl.program_id(0)
    tm = h1_ref.shape[0]
    ts = tri_s.shape[0]

    @pl.when(i == 0)
    def _():
        r = lax.broadcasted_iota(jnp.int32, (ts, ts), 0)
        c = lax.broadcasted_iota(jnp.int32, (ts, ts), 1)
        tri_s[...] = (r > c).astype(BF16)
        carry_s[...] = jnp.zeros_like(carry_s)

    lane = lax.broadcasted_iota(jnp.int32, (ts, LANES), 1)
    is_g = lane < N_GROUPS
    carry = carry_s[...]
    for r0 in range(0, tm, ts):
        rows = slice(r0, r0 + ts)
        yhy = _rms(zhy_ref[rows, :], ghy_ref[...]).astype(BF16)
        yhg = yhg_ref[rows, :].astype(BF16)
        acc = jnp.dot(yhy, wo_ref[:D_HYENA, :], preferred_element_type=F32)
        acc = acc + jnp.dot(yhg, wo_ref[D_HYENA:, :], preferred_element_type=F32)
        h1 = h0_ref[rows, :] + acc
        h1_ref[rows, :] = h1
        a2 = _rms(h1, gffn_ref[...])
        a2p_ref[rows, :] = _pack_bf16_pairs(a2)
        lg = jnp.dot(a2.astype(BF16), wr_ref[...], preferred_element_type=F32)

        gl = jnp.where(is_g, lg, NEG_BIG)
        gmax = jnp.max(gl, axis=1, keepdims=True)
        gsel = jnp.min(jnp.where(gl == gmax, lane, LANES), axis=1, keepdims=True)
        gden = jnp.sum(jnp.where(is_g, jnp.exp(gl - gmax), 0.0), axis=1, keepdims=True)
        p_group = 1.0 / gden
        in_grp = (lane >= N_GROUPS) & (lane < N_GROUPS + N_EXPERTS) & (
            ((lane - N_GROUPS) >> 3) == gsel)
        el = jnp.where(in_grp, lg, NEG_BIG)
        m1 = jnp.max(el, axis=1, keepdims=True)
        i1 = jnp.min(jnp.where(el == m1, lane, LANES), axis=1, keepdims=True)
        el2 = jnp.where(lane == i1, NEG_BIG, el)
        m2 = jnp.max(el2, axis=1, keepdims=True)
        i2 = jnp.min(jnp.where(el2 == m2, lane, LANES), axis=1, keepdims=True)
        r21 = jnp.exp(m2 - m1)
        gate1 = p_group / (1.0 + r21)
        gate2 = gate1 * r21

        hit1 = lane == i1
        hit2 = lane == i2
        onehot = (hit1 | hit2).astype(BF16)
        pre = jnp.dot(tri_s[...], onehot, preferred_element_type=F32) + carry
        pos1 = jnp.sum(jnp.where(hit1, pre, 0.0), axis=1, keepdims=True).astype(jnp.int32)
        pos2 = jnp.sum(jnp.where(hit2, pre, 0.0), axis=1, keepdims=True).astype(jnp.int32)
        carry = carry + jnp.sum(onehot.astype(F32), axis=0, keepdims=True)

        zero_i = jnp.zeros((ts, LANES), jnp.int32)
        ri_ref[rows, :] = jnp.where(lane == 0, i1 - N_GROUPS,
                          jnp.where(lane == 1, i2 - N_GROUPS,
                          jnp.where(lane == 2, pos1, jnp.where(lane == 3, pos2, zero_i))))
        rg_ref[rows, :] = jnp.where(lane == 0, gate1, jnp.where(lane == 1, gate2, 0.0))
    carry_s[...] = carry
    cnt_ref[...] = carry


def _outproj(zhy, yhg, h0, g_hy, g_ffn, wo_bf16, wr_bf16, tm):
    m = h0.shape[0]
    return pl.pallas_call(
        _outproj_body,
        grid=(m // tm,),
        in_specs=[pl.BlockSpec((tm, D_HYENA), lambda i: (i, 0)),
                  pl.BlockSpec((tm, D_HGRN), lambda i: (i, 0)),
                  pl.BlockSpec((tm, D_MODEL), lambda i: (i, 0)),
                  pl.BlockSpec((1, D_HYENA), lambda i: (0, 0)),
                  pl.BlockSpec((1, D_MODEL), lambda i: (0, 0)),
                  pl.BlockSpec((D_MODEL, D_MODEL), lambda i: (0, 0)),
                  pl.BlockSpec((D_MODEL, LANES), lambda i: (0, 0))],
        out_specs=[pl.BlockSpec((tm, D_MODEL), lambda i: (i, 0)),
                   pl.BlockSpec((tm, D_MODEL // 2), lambda i: (i, 0)),
                   pl.BlockSpec((tm, LANES), lambda i: (i, 0)),
                   pl.BlockSpec((tm, LANES), lambda i: (i, 0)),
                   pl.BlockSpec((1, LANES), lambda i: (0, 0))],
        out_shape=[jax.ShapeDtypeStruct((m, D_MODEL), F32),
                   jax.ShapeDtypeStruct((m, D_MODEL // 2), jnp.uint32),
                   jax.ShapeDtypeStruct((m, LANES), jnp.int32),
                   jax.ShapeDtypeStruct((m, LANES), F32),
                   jax.ShapeDtypeStruct((1, LANES), F32)],
        scratch_shapes=[pltpu.VMEM((OP_SUB, OP_SUB), BF16), pltpu.VMEM((1, LANES), F32)],
        compiler_params=pltpu.CompilerParams(
            dimension_semantics=("arbitrary",), vmem_limit_bytes=48 * 1024 * 1024),
        name="outproj",
    )(zhy, yhg, h0, g_hy.reshape(1, -1), g_ffn.reshape(1, -1), wo_bf16, wr_bf16)


EXP_TB = 256


def _wait_rows(ref, n_rows, sem):
    pltpu.make_async_copy(ref.at[pl.ds(0, n_rows), :], ref.at[pl.ds(n_rows, n_rows), :], sem).wait()


def _dispatch_body(lastblk_ref, npad_ref, nused_ref, d0_ref, d1_ref, h1_ref, xb_ref, zero_s, sem_z,
                   sem):
    i = pl.program_id(0)
    tm = d0_ref.shape[0]
    n_blocks = xb_ref.shape[0] // EXP_TB

    @pl.when(i == 0)
    def _():
        zero_s[...] = jnp.zeros_like(zero_s)

        def zero_copy(row0):
            row0 = pl.multiple_of(row0, EXP_TB)
            return pltpu.make_async_copy(zero_s, xb_ref.at[pl.ds(row0, EXP_TB), :], sem_z)

        for e in range(N_EXPERTS):
            @pl.when(npad_ref[e] > 0)
            def _():
                zero_copy(lastblk_ref[e]).start()
        for e in range(N_EXPERTS):
            @pl.when(npad_ref[e] > 0)
            def _():
                zero_copy(lastblk_ref[e]).wait()

        def start_tail(b, carry):
            zero_copy(b * EXP_TB).start()
            return carry

        def wait_tail(b, carry):
            zero_copy(b * EXP_TB).wait()
            return carry

        lax.fori_loop(nused_ref[0], n_blocks, start_tail, 0)
        lax.fori_loop(nused_ref[0], n_blocks, wait_tail, 0)

    def issue(g, carry):
        for u in range(8):
            t = g * 8 + u
            src = h1_ref.at[g, pl.ds(u, 1), :]
            pltpu.make_async_copy(src, xb_ref.at[pl.ds(d0_ref[t], 1), :], sem).start()
            pltpu.make_async_copy(src, xb_ref.at[pl.ds(d1_ref[t], 1), :], sem).start()
        return carry

    lax.fori_loop(0, tm // 8, issue, 0, unroll=2)
    _wait_rows(xb_ref, TOP_K * tm, sem)


def _dispatch(lastblk, npad, n_used, dest, rows, n_slots, tm):
    m, w = rows.shape
    grid_spec = pltpu.PrefetchScalarGridSpec(
        num_scalar_prefetch=3,
        grid=(m // tm,),
        in_specs=[pl.BlockSpec((tm,), lambda i, lb, npd, nu: (i,), memory_space=pltpu.SMEM),
                  pl.BlockSpec((tm,), lambda i, lb, npd, nu: (i,), memory_space=pltpu.SMEM),
                  pl.BlockSpec((tm // 8, 8, w), lambda i, lb, npd, nu: (i, 0, 0))],
        out_specs=pl.BlockSpec(memory_space=pl.ANY),
        scratch_shapes=[pltpu.VMEM((EXP_TB, w), rows.dtype),
                        pltpu.SemaphoreType.DMA(()), pltpu.SemaphoreType.DMA(())],
    )
    return pl.pallas_call(
        _dispatch_body,
        grid_spec=grid_spec,
        out_shape=jax.ShapeDtypeStruct((n_slots, w), rows.dtype),
        compiler_params=pltpu.CompilerParams(dimension_semantics=("arbitrary",)),
        name="dispatch",
    )(lastblk, npad, n_used, dest[0], dest[1], rows.reshape(m // 8, 8, w))


def _expert_body(eid_ref, nused_ref, xb_ref, wg_ref, wu_ref, wd_ref, o_ref, wg_s, wu_s, wd_s):
    i = pl.program_id(0)
    half = D_MODEL // 2

    @pl.when(i < nused_ref[0])
    def _():
        prev = eid_ref[jnp.maximum(i - 1, 0)]

        @pl.when((i == 0) | (eid_ref[i] != prev))
        def _():
            wg_s[...] = wg_ref[0].astype(BF16)
            wu_s[...] = wu_ref[0].astype(BF16)
            wd_s[...] = wd_ref[0].astype(BF16)

        lo, hi = _unpack_bf16_pairs(xb_ref[...])
        lo, hi = lo.astype(BF16), hi.astype(BF16)

        def proj(w_s):
            return (jnp.dot(lo, w_s[:half, :], preferred_element_type=F32)
                    + jnp.dot(hi, w_s[half:, :], preferred_element_type=F32))

        g = proj(wg_s)
        u = proj(wu_s)
        hmid = (g * jax.nn.sigmoid(g) * u).astype(BF16)
        o_ref[...] = _pack_bf16_pairs(jnp.dot(hmid, wd_s[...], preferred_element_type=F32))

    @pl.when(i >= nused_ref[0])
    def _():
        o_ref[...] = jnp.zeros_like(o_ref)


def _experts(block_eid, n_used, xb, w_gate, w_up, w_down):
    n_slots, w = xb.shape

    def blk(i, e, nu):
        return jnp.minimum(i, nu[0] - 1)

    grid_spec = pltpu.PrefetchScalarGridSpec(
        num_scalar_prefetch=2,
        grid=(n_slots // EXP_TB,),
        in_specs=[pl.BlockSpec((EXP_TB, w), lambda i, e, nu: (blk(i, e, nu), 0)),
                  pl.BlockSpec((1, D_MODEL, D_EXPERT), lambda i, e, nu: (e[blk(i, e, nu)], 0, 0)),
                  pl.BlockSpec((1, D_MODEL, D_EXPERT), lambda i, e, nu: (e[blk(i, e, nu)], 0, 0)),
                  pl.BlockSpec((1, D_EXPERT, D_MODEL), lambda i, e, nu: (e[blk(i, e, nu)], 0, 0))],
        out_specs=pl.BlockSpec((EXP_TB, w), lambda i, e, nu: (i, 0)),
        scratch_shapes=[pltpu.VMEM((D_MODEL, D_EXPERT), BF16),
                        pltpu.VMEM((D_MODEL, D_EXPERT), BF16),
                        pltpu.VMEM((D_EXPERT, D_MODEL), BF16)],
    )
    return pl.pallas_call(
        _expert_body,
        grid_spec=grid_spec,
        out_shape=jax.ShapeDtypeStruct((n_slots, w), jnp.uint32),
        compiler_params=pltpu.CompilerParams(
            dimension_semantics=("arbitrary",), vmem_limit_bytes=48 * 1024 * 1024),
        name="experts",
    )(block_eid, n_used, xb, w_gate, w_up, w_down)


def _combine_body(d0_ref, d1_ref, n0_ref, n1_ref, h1_ref, rg_ref, gfin_ref, yb_ref, o_ref, y_s, sem):
    i = pl.program_id(0)
    tm = h1_ref.shape[0]
    slot = i % 2

    def issue(idx_refs, sl):
        def body(g, carry):
            for u in range(8):
                t = g * 8 + u
                for k in range(TOP_K):
                    pltpu.make_async_copy(yb_ref.at[pl.ds(idx_refs[k][t], 1), :],
                                          y_s.at[sl, k, g, pl.ds(u, 1), :], sem.at[sl]).start()
            return carry

        lax.fori_loop(0, tm // 8, body, 0, unroll=2)

    @pl.when(i == 0)
    def _():
        issue((d0_ref, d1_ref), 0)

    @pl.when(i + 1 < pl.num_programs(0))
    def _():
        issue((n0_ref, n1_ref), 1 - slot)

    _wait_rows(yb_ref, TOP_K * tm, sem.at[slot])
    rg = rg_ref[...]
    g1, g2 = rg[:, 0:1], rg[:, 1:2]
    wy = yb_ref.shape[1]
    lo1, hi1 = _unpack_bf16_pairs(y_s[slot, 0].reshape(tm, wy))
    lo2, hi2 = _unpack_bf16_pairs(y_s[slot, 1].reshape(tm, wy))
    half = D_MODEL // 2
    h2_lo = h1_ref[:, :half] + g1 * lo1 + g2 * lo2
    h2_hi = h1_ref[:, half:] + g1 * hi1 + g2 * hi2
    ms = (jnp.sum(h2_lo * h2_lo, axis=-1, keepdims=True)
          + jnp.sum(h2_hi * h2_hi, axis=-1, keepdims=True)) * (1.0 / D_MODEL)
    inv = lax.rsqrt(ms + EPS)
    o_ref[:, :half] = h2_lo * inv * gfin_ref[:, :half]
    o_ref[:, half:] = h2_hi * inv * gfin_ref[:, half:]


def _combine(dest, h1, rg, g_fin, yb, tm):
    m = h1.shape[0]
    last = m // tm - 1
    return pl.pallas_call(
        _combine_body,
        grid=(m // tm,),
        in_specs=[pl.BlockSpec((tm,), lambda i: (i,), memory_space=pltpu.SMEM),
                  pl.BlockSpec((tm,), lambda i: (i,), memory_space=pltpu.SMEM),
                  pl.BlockSpec((tm,), lambda i: (jnp.minimum(i + 1, last),), memory_space=pltpu.SMEM),
                  pl.BlockSpec((tm,), lambda i: (jnp.minimum(i + 1, last),), memory_space=pltpu.SMEM),
                  pl.BlockSpec((tm, D_MODEL), lambda i: (i, 0)),
                  pl.BlockSpec((tm, LANES), lambda i: (i, 0)),
                  pl.BlockSpec((1, D_MODEL), lambda i: (0, 0)),
                  pl.BlockSpec(memory_space=pl.ANY)],
        out_specs=pl.BlockSpec((tm, D_MODEL), lambda i: (i, 0)),
        out_shape=jax.ShapeDtypeStruct((m, D_MODEL), F32),
        scratch_shapes=[pltpu.VMEM((2, TOP_K, tm // 8, 8, yb.shape[1]), yb.dtype),
                        pltpu.SemaphoreType.DMA((2,))],
        compiler_params=pltpu.CompilerParams(dimension_semantics=("arbitrary",)),
        name="combine",
    )(dest[0], dest[1], dest[0], dest[1], h1, rg, g_fin.reshape(1, -1), yb)


def _moe(h1, a2p, ri, rg, cnt, g_fin, w_gate, w_up, w_down):
    m = h1.shape[0]
    n_blocks = TOP_K * m // EXP_TB + N_EXPERTS
    n_slots = n_blocks * EXP_TB
    counts = cnt[0, N_GROUPS:N_GROUPS + N_EXPERTS].astype(jnp.int32)
    padded = (counts + EXP_TB - 1) // EXP_TB * EXP_TB
    pend = jnp.cumsum(padded)
    base = pend - padded
    rt = jnp.transpose(ri)[:2 * TOP_K]
    sel = rt[None, :TOP_K] == jnp.arange(N_EXPERTS, dtype=jnp.int32)[:, None, None]
    dest = jnp.sum(jnp.where(sel, base[:, None, None], 0), axis=0) + rt[TOP_K:]
    blk_start = jnp.arange(n_blocks, dtype=jnp.int32) * EXP_TB
    block_eid = jnp.minimum(jnp.sum(blk_start[:, None] >= pend[None, :], axis=1),
                            N_EXPERTS - 1).astype(jnp.int32)
    n_used = (pend[-1:] // EXP_TB).astype(jnp.int32)
    lastblk = (pend - EXP_TB).astype(jnp.int32)

    xb = _dispatch(lastblk, padded.astype(jnp.int32), n_used, dest, a2p, n_slots, tm=512)
    yb = _experts(block_eid, n_used, xb, w_gate, w_up, w_down)
    return _combine(dest, h1, rg, g_fin, yb, tm=256)


ND = 2176
NCIRC = 2 * ND
NH = ND // 2
SEQ = 2048
L_TOT = SEQ + N_META
LH = L_TOT // 2
LHP = 1152
LHR = 1040
HY_CT = 256
HY_FB = NH // 2
HY_IB = ((0, 528), (528, 512))
HY_CB = LH // 3
HALO = 8


def _dft_matrices():
    a = 2 * jnp.arange(NH, dtype=jnp.int32) + 1
    t_hi = jnp.arange(LHP // LANES, dtype=jnp.int32) * LANES
    t_lo = jnp.arange(LANES, dtype=jnp.int32)
    valid = ((t_hi[:, None] + t_lo[None, :]) < LH).reshape(1, LHP)

    def cos_sin(m, denom):
        ang = m.astype(F32) * (math.pi / denom)
        return jnp.cos(ang), jnp.sin(ang)

    c_hi, s_hi = cos_sin((a[:, None] * ((4 * t_hi) % (4 * NCIRC))[None, :]) % (4 * NCIRC), 2 * NCIRC)
    c_hi, s_hi = c_hi[:, :, None], s_hi[:, :, None]
    out = []
    for c in (1, 3):
        c_lo, s_lo = cos_sin((a[:, None] * (4 * t_lo + c)[None, :]) % (4 * NCIRC), 2 * NCIRC)
        c_lo, s_lo = c_lo[:, None, :], s_lo[:, None, :]
        cos_phi = jnp.where(valid, (c_hi * c_lo - s_hi * s_lo).reshape(NH, LHP), 0.0)
        sin_phi = jnp.where(valid, (s_hi * c_lo + c_hi * s_lo).reshape(NH, LHP), 0.0)
        out.append(jnp.concatenate([cos_phi, sin_phi if c == 1 else -sin_phi], axis=0).astype(BF16))
    ce, co = out
    return ce, co, ce.T, co.T


def _forward_half_spectrum(ce_ref, co_ref, xe, xo, j0):
    a1 = jnp.dot(ce_ref[j0:j0 + HY_FB, :], xe, preferred_element_type=F32)
    a2 = jnp.dot(ce_ref[NH + j0:NH + j0 + HY_FB, :], xe, preferred_element_type=F32)
    b1 = jnp.dot(co_ref[j0:j0 + HY_FB, :], xo, preferred_element_type=F32)
    b2 = jnp.dot(co_ref[NH + j0:NH + j0 + HY_FB, :], xo, preferred_element_type=F32)
    return a1 + b1, a2 - b2, a2 + b2, a1 - b1


def _spectra_body(s_ref, d_ref, ce_ref, co_ref, rot_ref, hr_ref, hi_ref,
                  stage_s, se_s, so_s, de_s, do_s):
    ct = hr_ref.shape[-1]
    nl = ct // LANES
    for x_ref, xe_s, xo_s in ((s_ref, se_s, so_s), (d_ref, de_s, do_s)):
        for j in range(nl):
            stage_s[j] = x_ref[:, j * LANES:(j + 1) * LANES]
        for par, dst in ((0, xe_s), (1, xo_s)):
            for r0 in range(0, LH, HY_CB):
                dst[r0:r0 + HY_CB, :] = jnp.concatenate(
                    [stage_s.at[j][pl.ds(par + 2 * r0, HY_CB, stride=2), :] for j in range(nl)],
                    axis=1).astype(BF16)
            dst[LH:LHP, :] = jnp.zeros((LHP - LH, ct), BF16)
    for j0 in range(0, NH, HY_FB):
        ar1, _, ar2, _ = _forward_half_spectrum(ce_ref, co_ref, se_s[...], so_s[...], j0)
        _, q1, _, q2 = _forward_half_spectrum(ce_ref, co_ref, de_s[...], do_s[...], j0)
        for r0, ar, q in ((j0, ar1, q1), (NH + j0, ar2, q2)):
            c = rot_ref[r0:r0 + HY_FB, 0:1]
            s = rot_ref[r0:r0 + HY_FB, 1:2]
            hr_ref[0, r0:r0 + HY_FB, :] = c * ar + s * q
            hi_ref[0, r0:r0 + HY_FB, :] = s * ar - c * q


def _spectra(h_fwd, h_bwd, ce, co):
    hbs = jnp.concatenate([h_bwd[1:], jnp.zeros_like(h_bwd[:1])], axis=0)
    s2d = (h_fwd + hbs).reshape(L_TOT, HYENA_ORDER * D_HYENA)
    d2d = (h_fwd - hbs).reshape(L_TOT, HYENA_ORDER * D_HYENA)
    j = jnp.arange(NH, dtype=F32)
    k = jnp.concatenate([j, ND - 1 - j])
    half = (2 * k + 1) * (math.pi / (2 * NCIRC))
    scale = 2.0 / NCIRC
    rot = jnp.stack([jnp.cos(half) * scale, jnp.sin(half) * scale], axis=1)
    rot = jnp.pad(rot, ((0, 0), (0, LANES - 2)))
    n_ord, ct = HYENA_ORDER, HY_CT
    nct = D_HYENA // ct
    col = pl.BlockSpec((L_TOT, ct), lambda o, c: (0, o * nct + c))
    mat = pl.BlockSpec((ND, LHP), lambda o, c: (0, 0))
    out = pl.BlockSpec((1, ND, ct), lambda o, c: (o, 0, c))
    half_bf16 = pltpu.VMEM((LHP, ct), BF16)
    return pl.pallas_call(
        _spectra_body,
        grid=(n_ord, nct),
        in_specs=[col, col, mat, mat, pl.BlockSpec((ND, LANES), lambda o, c: (0, 0))],
        out_specs=[out, out],
        out_shape=[jax.ShapeDtypeStruct((n_ord, ND, D_HYENA), F32)] * 2,
        scratch_shapes=[pltpu.VMEM((ct // LANES, L_TOT, LANES), F32),
                        half_bf16, half_bf16, half_bf16, half_bf16],
        compiler_params=pltpu.CompilerParams(
            dimension_semantics=("arbitrary", "arbitrary"), vmem_limit_bytes=48 * 1024 * 1024),
        name="spectra",
    )(s2d, d2d, ce, co, rot)


def _hyena_body(pv_ref, px1_ref, px2_ref, mv_ref, mx1_ref, mx2_ref,
                wv_ref, wx1_ref, wx2_ref, bv_ref, bx1_ref, bx2_ref, skip_ref,
                ce_ref, co_ref, cet_ref, cot_ref, hr_ref, hi_ref, o_ref,
                stage_s, ze_s, zo_s, g1e_s, g1o_s, g2e_s, g2o_s, zbe_s, zbo_s, we_s, wo_s):
    ct = o_ref.shape[-1]
    nl = ct // LANES

    def strided(off, r0, rows):
        return jnp.concatenate(
            [stage_s.at[j][pl.ds(HALO + off + 2 * r0, rows, stride=2), :] for j in range(nl)], axis=1)

    def short_conv(p_ref, m_ref, w_ref, b_ref, dst_e, dst_o):
        for j in range(nl):
            ln = slice(j * LANES, (j + 1) * LANES)
            stage_s[j, 0:HALO, :] = jnp.zeros((HALO, LANES), F32)
            stage_s[j, HALO:HALO + N_META, :] = m_ref[:, ln].astype(F32)
            stage_s[j, HALO + N_META:HALO + L_TOT, :] = p_ref[0, :, ln].astype(F32)
            stage_s[j, HALO + L_TOT:HALO + L_TOT + HALO, :] = jnp.zeros((HALO, LANES), F32)
        w = w_ref[...]
        b = b_ref[...]
        for r0 in range(0, LH, HY_CB):
            sm1, s0, s1, s2 = (strided(off, r0, HY_CB) for off in (-1, 0, 1, 2))
            dst_e[r0:r0 + HY_CB, :] = b + sm1 * w[0:1] + s0 * w[1:2] + s1 * w[2:3]
            dst_o[r0:r0 + HY_CB, :] = b + s0 * w[0:1] + s1 * w[1:2] + s2 * w[2:3]
        dst_e[LH:LHR, :] = jnp.zeros((LHR - LH, ct), F32)
        dst_o[LH:LHR, :] = jnp.zeros((LHR - LH, ct), F32)

    short_conv(pv_ref, mv_ref, wv_ref, bv_ref, ze_s, zo_s)
    short_conv(px1_ref, mx1_ref, wx1_ref, bx1_ref, g1e_s, g1o_s)
    short_conv(px2_ref, mx2_ref, wx2_ref, bx2_ref, g2e_s, g2o_s)
    zbe_s[LH:LHP, :] = jnp.zeros((LHP - LH, ct), BF16)
    zbo_s[LH:LHP, :] = jnp.zeros((LHP - LH, ct), BF16)

    for n, (ge_s, go_s) in enumerate(((g1e_s, g1o_s), (g2e_s, g2o_s))):
        zbe_s[0:LH, :] = ze_s[0:LH, :].astype(BF16)
        zbo_s[0:LH, :] = zo_s[0:LH, :].astype(BF16)
        for j0 in range(0, NH, HY_FB):
            p1, q1, p2, q2 = _forward_half_spectrum(ce_ref, co_ref, zbe_s[...], zbo_s[...], j0)
            h1r = hr_ref[n, j0:j0 + HY_FB, :]
            h1i = hi_ref[n, j0:j0 + HY_FB, :]
            h2r = hr_ref[n, NH + j0:NH + j0 + HY_FB, :]
            h2i = hi_ref[n, NH + j0:NH + j0 + HY_FB, :]
            yr1 = p1 * h1r + q1 * h1i
            ny1 = q1 * h1r - p1 * h1i
            yr2 = p2 * h2r + q2 * h2i
            ny2 = q2 * h2r - p2 * h2i
            we_s[j0:j0 + HY_FB, :] = (yr1 + ny2).astype(BF16)
            we_s[NH + j0:NH + j0 + HY_FB, :] = (yr2 + ny1).astype(BF16)
            wo_s[j0:j0 + HY_FB, :] = (yr1 - ny2).astype(BF16)
            wo_s[NH + j0:NH + j0 + HY_FB, :] = (yr2 - ny1).astype(BF16)
        skip = skip_ref[n:n + 1, :]
        for parity, (ct_ref, w_s, z_s, g_s) in enumerate(((cet_ref, we_s, ze_s, ge_s),
                                                          (cot_ref, wo_s, zo_s, go_s))):
            for r0, rb in HY_IB:
                y = jnp.dot(ct_ref[r0:r0 + rb, :], w_s[...], preferred_element_type=F32)
                znew = g_s[r0:r0 + rb, :] * (y + z_s[r0:r0 + rb, :] * skip)
                if n == 0:
                    z_s[r0:r0 + rb, :] = znew
                else:
                    n_valid = min(r0 + rb, LH) - r0
                    for j in range(nl):
                        stage_s.at[j][pl.ds(HALO + parity + 2 * r0, n_valid, stride=2), :] = (
                            znew[0:n_valid, j * LANES:(j + 1) * LANES])
    for j in range(nl):
        o_ref[0, :, j * LANES:(j + 1) * LANES] = stage_s[j, HALO + N_META:HALO + L_TOT, :]


def _hyena(p_hy, pm_hy, conv_w, conv_b, skip, mats, hr, hi):
    bsz = p_hy.shape[0]
    nct = D_HYENA // HY_CT
    ct = HY_CT
    ce, co, cet, cot = mats

    def part(j):
        return [pl.BlockSpec((1, SEQ, ct), lambda c, b, j=j: (b, 0, j * nct + c))]

    def mpart(j):
        return [pl.BlockSpec((N_META, ct), lambda c, b, j=j: (0, j * nct + c))]

    def wpart(rows, j):
        return [pl.BlockSpec((rows, ct), lambda c, b, j=j: (0, j * nct + c))]

    spec = pl.BlockSpec((HYENA_ORDER, ND, ct), lambda c, b: (0, 0, c), pipeline_mode=pl.Buffered(1))
    in_specs = (part(0) + part(1) + part(2) + mpart(0) + mpart(1) + mpart(2)
                + wpart(3, 0) + wpart(3, 1) + wpart(3, 2)
                + wpart(1, 0) + wpart(1, 1) + wpart(1, 2)
                + [pl.BlockSpec((HYENA_ORDER, ct), lambda c, b: (0, c)),
                   pl.BlockSpec((ND, LHP), lambda c, b: (0, 0)),
                   pl.BlockSpec((ND, LHP), lambda c, b: (0, 0)),
                   pl.BlockSpec((LHP, ND), lambda c, b: (0, 0)),
                   pl.BlockSpec((LHP, ND), lambda c, b: (0, 0)),
                   spec, spec])
    cb = conv_b.reshape(1, -1)
    f32_half = pltpu.VMEM((LHR, ct), F32)
    return pl.pallas_call(
        _hyena_body,
        grid=(nct, bsz),
        in_specs=in_specs,
        out_specs=pl.BlockSpec((1, SEQ, ct), lambda c, b: (b, 0, c)),
        out_shape=jax.ShapeDtypeStruct((bsz, SEQ, D_HYENA), F32),
        scratch_shapes=[pltpu.VMEM((ct // LANES, L_TOT + 2 * HALO, LANES), F32),
                        f32_half, f32_half, f32_half, f32_half, f32_half, f32_half,
                        pltpu.VMEM((LHP, ct), BF16), pltpu.VMEM((LHP, ct), BF16),
                        pltpu.VMEM((ND, ct), BF16), pltpu.VMEM((ND, ct), BF16)],
        compiler_params=pltpu.CompilerParams(
            dimension_semantics=("arbitrary", "arbitrary"), vmem_limit_bytes=60 * 1024 * 1024),
        name="hyena",
    )(p_hy, p_hy, p_hy, pm_hy, pm_hy, pm_hy, conv_w, conv_w, conv_w, cb, cb, cb, skip,
      ce, co, cet, cot, hr, hi)


N_CHUNKS = SEQ // CHUNK
NT_DIMS = (((1,), (1,)), ((), ()))
TN_DIMS = (((0,), (0,)), ((), ()))
MID_F = CHUNK // 2
MID_B = CHUNK // 2 - 1


HG_G = 8
HG_ROWS = HG_G * CHUNK


def _split2(x):
    hi = x.astype(BF16)
    lo = (x - hi.astype(F32)).astype(BF16)
    return hi, lo


def _chunk_prefix_matrix():
    r = lax.broadcasted_iota(jnp.int32, (HG_ROWS, HG_ROWS), 0)
    c = lax.broadcasted_iota(jnp.int32, (HG_ROWS, HG_ROWS), 1)
    return (((r // CHUNK) == (c // CHUNK)) & (c <= r)).astype(BF16)


def _chunk_rows(x, row):
    return jnp.concatenate(
        [jnp.broadcast_to(x[g * CHUNK + row:g * CHUNK + row + 1, :], (CHUNK, x.shape[1]))
         for g in range(x.shape[0] // CHUNK)], axis=0)


def _hgrn_body(q_ref, ff_ref, fb_ref, i_ref, g_ref, mff_ref, mi_ref,
               lbf_ref, lbb_ref, nw_ref, o_ref,
               tri_s, qe_s, sc_s, ut_s, dec_s, st_s):
    hd = HGRN_HEAD_DIM
    row = lax.broadcasted_iota(jnp.int32, (CHUNK, CHUNK), 0)
    col = lax.broadcasted_iota(jnp.int32, (CHUNK, CHUNK), 1)
    lower = row >= col
    upper = col >= row
    lbf = lbf_ref[...]
    lbb = lbb_ref[...]

    @pl.when((pl.program_id(0) == 0) & (pl.program_id(1) == 0))
    def _():
        tri_s[...] = _chunk_prefix_matrix()

    def forget(logit, lb):
        f = lb + (1.0 - lb) * jax.nn.sigmoid(logit)
        return 1.0 - f, jnp.log(f)

    def prefix_sums(lf):
        s = jnp.dot(tri_s[...], jnp.concatenate(_split2(lf), axis=1), preferred_element_type=F32)
        return s[:, :hd] + s[:, hd:]

    k_m, lf_m = forget(mff_ref[...], lbf)
    pad = jnp.zeros((CHUNK - N_META, hd), F32)
    lf_m = jnp.concatenate([pad, lf_m] * HG_G, axis=0)
    b_m = prefix_sums(lf_m)[0:CHUNK]
    kl_m = jnp.concatenate([pad, k_m], axis=0) * jnp.exp(b_m[CHUNK - 1:CHUNK] - b_m)
    v_m = jnp.concatenate([pad, mi_ref[...]], axis=0).astype(BF16)
    st_meta = lax.dot_general(v_m, kl_m.astype(BF16), TN_DIMS, preferred_element_type=F32)

    def phase_a(j, carry):
        r0 = pl.multiple_of(j * HG_ROWS, HG_ROWS)
        rows = pl.ds(r0, HG_ROWS)
        qv = jax.nn.silu(q_ref[0, rows, :])
        vb = i_ref[0, rows, :].astype(BF16)
        k_f, lf_f = forget(ff_ref[0, rows, :], lbf)
        k_b, lf_b = forget(fb_ref[0, rows, :], lbb)
        b_f = prefix_sums(lf_f)
        p_b = prefix_sums(lf_b)
        bmid_f = _chunk_rows(b_f, MID_F)
        blast_f = _chunk_rows(b_f, CHUNK - 1)
        tot_b = _chunk_rows(p_b, CHUNK - 1)
        c_b = tot_b - p_b + lf_b
        cmid_b = _chunk_rows(c_b, MID_B)
        d_f = b_f - bmid_f
        d_b = c_b - cmid_b
        e_f = jnp.exp(d_f)
        e_b = jnp.exp(d_b)
        qs_f = qv * e_f
        ks_f = k_f / e_f
        qs_b = qv * e_b
        ks_b = k_b / e_b
        qs_fb, ks_fb = qs_f.astype(BF16), ks_f.astype(BF16)
        qs_bb, ks_bb = qs_b.astype(BF16), ks_b.astype(BF16)
        for g in range(HG_G):
            sl = slice(g * CHUNK, (g + 1) * CHUNK)
            r1 = slice(g * CHUNK, g * CHUNK + 1)
            n = j * HG_G + g
            rows_g = pl.ds(pl.multiple_of(r0 + g * CHUNK, CHUNK), CHUNK)
            sc_f = lax.dot_general(qs_fb[sl], ks_fb[sl], NT_DIMS, preferred_element_type=F32)
            sc_b = lax.dot_general(qs_bb[sl], ks_bb[sl], NT_DIMS, preferred_element_type=F32)
            sc_s[n] = (jnp.where(lower, sc_f, 0.0) + jnp.where(upper, sc_b, 0.0)).astype(BF16)
            em_f = jnp.exp(bmid_f[r1])
            el_f = jnp.exp(blast_f[r1] - bmid_f[r1])
            em_b = jnp.exp(cmid_b[r1])
            el_b = jnp.exp(tot_b[r1] - cmid_b[r1])
            qe_s[rows_g, :] = jnp.concatenate(
                [qs_f[sl] * em_f, qs_b[sl] * em_b], axis=1).astype(BF16)
            kl = jnp.concatenate([ks_f[sl] * el_f, ks_b[sl] * el_b], axis=1).astype(BF16)
            ut_s[n] = lax.dot_general(vb[sl], kl, TN_DIMS, preferred_element_type=F32)
            dec_s[n] = jnp.concatenate([jnp.exp(blast_f[r1]), jnp.exp(tot_b[r1])], axis=1)
        return carry

    lax.fori_loop(0, N_CHUNKS // HG_G, phase_a, 0, unroll=True)

    st_f = st_meta
    st_b = jnp.zeros((hd, hd), F32)
    for n in range(N_CHUNKS):
        st_s[n, :, 0:hd] = st_f.astype(BF16)
        st_f = dec_s[n, :, 0:hd] * st_f + ut_s[n, :, 0:hd]
        m = N_CHUNKS - 1 - n
        st_s[m, :, hd:2 * hd] = st_b.astype(BF16)
        st_b = dec_s[m, :, hd:2 * hd] * st_b + ut_s[m, :, hd:2 * hd]

    nw = nw_ref[...]

    def phase_c(j, carry):
        r0 = pl.multiple_of(j * HG_ROWS, HG_ROWS)
        rows = pl.ds(r0, HG_ROWS)
        vb = i_ref[0, rows, :].astype(BF16)
        outs = []
        for g in range(HG_G):
            sl = slice(g * CHUNK, (g + 1) * CHUNK)
            n = j * HG_G + g
            rows_g = pl.ds(pl.multiple_of(r0 + g * CHUNK, CHUNK), CHUNK)
            o = jnp.dot(sc_s[n], vb[sl], preferred_element_type=F32)
            outs.append(o + lax.dot_general(qe_s[rows_g, :], st_s[n], NT_DIMS,
                                            preferred_element_type=F32))
        o = jnp.concatenate(outs, axis=0)
        o = o * lax.rsqrt(jnp.mean(o * o, axis=-1, keepdims=True) + EPS)
        o_ref[0, rows, :] = o * nw * jax.nn.silu(g_ref[0, rows, :])
        return carry

    lax.fori_loop(0, N_CHUNKS // HG_G, phase_c, 0, unroll=True)


def _hgrn(phg_x, phg_m, lb_f, lb_b, norm_w):
    bsz = phg_x.shape[0]
    hd = HGRN_HEAD_DIM
    nh = HGRN_HEADS

    def part(j):
        return pl.BlockSpec((1, SEQ, hd), lambda b, h, j=j: (b, 0, j * nh + h))

    def mpart(j):
        return pl.BlockSpec((N_META, hd), lambda b, h, j=j: (0, j * nh + h))

    vec = pl.BlockSpec((1, hd), lambda b, h: (0, h))
    return pl.pallas_call(
        _hgrn_body,
        grid=(bsz, nh),
        in_specs=[part(0), part(1), part(2), part(3), part(4), mpart(1), mpart(3), vec, vec, vec],
        out_specs=pl.BlockSpec((1, SEQ, hd), lambda b, h: (b, 0, h)),
        out_shape=jax.ShapeDtypeStruct((bsz, SEQ, D_HGRN), F32),
        scratch_shapes=[pltpu.VMEM((HG_ROWS, HG_ROWS), BF16),
                        pltpu.VMEM((SEQ, 2 * hd), BF16),
                        pltpu.VMEM((N_CHUNKS, CHUNK, CHUNK), BF16),
                        pltpu.VMEM((N_CHUNKS, hd, 2 * hd), F32),
                        pltpu.VMEM((N_CHUNKS, 1, 2 * hd), F32),
                        pltpu.VMEM((N_CHUNKS, hd, 2 * hd), BF16)],
        compiler_params=pltpu.CompilerParams(
            dimension_semantics=("arbitrary", "arbitrary"), vmem_limit_bytes=40 * 1024 * 1024),
        name="hgrn",
    )(phg_x, phg_x, phg_x, phg_x, phg_x, phg_m, phg_m,
      lb_f.reshape(1, -1), lb_b.reshape(1, -1), norm_w.reshape(1, -1))


def _hyena_filters(L, w1, b1, w2, b2, w3, freq):
    pos = jnp.arange(L, dtype=F32)
    t = pos / max(L - 1, 1)
    bands = jnp.linspace(1e-4, FILTER_BANDS - 1, FILTER_BANDS, dtype=F32)
    ang = (2.0 * math.pi / L) * pos[:, None] * bands[None, :]
    z = jnp.concatenate([t[:, None], jnp.cos(ang), -jnp.sin(ang)], axis=-1)
    hp = lax.Precision.HIGHEST
    hid = jnp.sin(freq * (jnp.dot(z, w1, precision=hp) + b1))
    hid = jnp.sin(freq * (jnp.dot(hid, w2, precision=hp) + b2))
    filt = jnp.dot(hid, w3, precision=hp).reshape(L, 2, HYENA_ORDER, D_HYENA)
    deltas = jnp.abs(jnp.linspace(math.log(DECAY_TARGET) / SLOW_DECAY_PCT,
                                  math.log(DECAY_TARGET) / FAST_DECAY_PCT, D_HYENA, dtype=F32))
    window = jnp.exp(-t[:, None] * deltas[None, :])
    filt = filt * window[:, None, None, :]
    return filt[:, 0], filt[:, 1]


def kernel(x, meta_tokens, w_in, conv_w, conv_b, filt_w1, filt_b1, filt_w2, filt_b2, filt_w3,
           filt_freq, filt_skip, hyena_norm, lb_fwd, lb_bwd, hgrn_norm, w_out, norm_mix, norm_ffn,
           w_router_group, w_router_expert, w_gate, w_up, w_down, norm_final):
    B, S, D = x.shape
    L = S + N_META
    lbf = jnp.cumsum(jax.nn.softmax(lb_fwd, axis=0), axis=0)[0]
    lbb = jnp.cumsum(jax.nn.softmax(lb_bwd, axis=0), axis=0)[0]

    w_in_b = w_in[0].astype(BF16)
    xf = x.reshape(B * S, D)
    phy_x, phg_x = _inproj(xf, norm_mix[0], w_in_b, tm=512)
    phy_m, phg_m = _inproj(meta_tokens, norm_mix[0], w_in_b, tm=N_META)

    mats = _dft_matrices()
    h_fwd, h_bwd = _hyena_filters(L, filt_w1[0], filt_b1[0], filt_w2[0], filt_b2[0], filt_w3[0],
                                  filt_freq[0])
    hr, hi = _spectra(h_fwd, h_bwd, mats[0], mats[1])
    z_hy = _hyena(phy_x.reshape(B, S, D_HYENA_PROJ), phy_m, conv_w[0], conv_b[0], filt_skip[0],
                  mats, hr, hi).reshape(B * S, D_HYENA)

    y_hg = _hgrn(phg_x.reshape(B, S, 5 * D_HGRN), phg_m, lbf, lbb,
                 hgrn_norm[0]).reshape(B * S, D_HGRN)

    w_r = jnp.concatenate([w_router_group[0], w_router_expert[0].reshape(D, N_EXPERTS),
                           jnp.zeros((D, LANES - N_GROUPS - N_EXPERTS), F32)], axis=1).astype(BF16)
    h1, a2p, ri, rg, cnt = _outproj(z_hy, y_hg, xf, hyena_norm[0], norm_ffn[0],
                                    w_out[0].astype(BF16), w_r, tm=1024)
    out = _moe(h1, a2p, ri, rg, cnt, norm_final, w_gate[0], w_up[0], w_down[0])
    return out.reshape(B, S, D)
```

```python
import functools
import math

import jax
import jax.numpy as jnp
from jax import lax
from jax.experimental import pallas as pl
from jax.experimental.pallas import tpu as pltpu

D_MODEL = 1024
N_META = 16
D_HYENA = 512
D_HGRN = 512
HYENA_ORDER = 2
SHORT_CONV = 3
FILTER_EMB = 33
FILTER_BANDS = 16
DECAY_TARGET = 1e-2
FAST_DECAY_PCT = 0.3
SLOW_DECAY_PCT = 1.5
HGRN_HEAD_DIM = 128
HGRN_HEADS = D_HGRN // HGRN_HEAD_DIM
CHUNK = 64
N_GROUPS = 8
EXPERTS_PER_GROUP = 8
N_EXPERTS = 64
TOP_K = 2
D_EXPERT = 512
D_HYENA_PROJ = 3 * D_HYENA
D_IN_PROJ = D_HYENA_PROJ + 5 * D_HGRN
EPS = 1e-6

F32 = jnp.float32
BF16 = jnp.bfloat16


def _rms(x, gain):
    return x * lax.rsqrt(jnp.mean(x * x, axis=-1, keepdims=True) + EPS) * gain


HI16 = 0xFFFF0000


def _pack_bf16_pairs(x):
    c = x.shape[1] // 2
    bits = lax.bitcast_convert_type(x.astype(BF16).astype(F32), jnp.uint32)
    return (bits[:, :c] >> 16) | (bits[:, c:] & jnp.uint32(HI16))


def _unpack_bf16_pairs(w):
    lo = lax.bitcast_convert_type(w << 16, F32)
    hi = lax.bitcast_convert_type(w & jnp.uint32(HI16), F32)
    return lo, hi


def _inproj_body(x_ref, g_ref, w_ref, hy_ref, hg_ref, *, tn):
    a = _rms(x_ref[...], g_ref[...]).astype(BF16)
    for j in range(D_IN_PROJ // tn):
        acc = jnp.dot(a, w_ref[:, j * tn:(j + 1) * tn], preferred_element_type=F32)
        if j * tn < D_HYENA_PROJ:
            hy_ref[:, j * tn:(j + 1) * tn] = acc.astype(hy_ref.dtype)
        else:
            c0 = j * tn - D_HYENA_PROJ
            hg_ref[:, c0:c0 + tn] = acc.astype(hg_ref.dtype)


def _inproj(x, gain, w_bf16, tm):
    m, d = x.shape
    n_hg = D_IN_PROJ - D_HYENA_PROJ
    return pl.pallas_call(
        functools.partial(_inproj_body, tn=512),
        grid=(m // tm,),
        in_specs=[pl.BlockSpec((tm, d), lambda i: (i, 0)),
                  pl.BlockSpec((1, d), lambda i: (0, 0)),
                  pl.BlockSpec((d, D_IN_PROJ), lambda i: (0, 0))],
        out_specs=[pl.BlockSpec((tm, D_HYENA_PROJ), lambda i: (i, 0)),
                   pl.BlockSpec((tm, n_hg), lambda i: (i, 0))],
        out_shape=[jax.ShapeDtypeStruct((m, D_HYENA_PROJ), BF16),
                   jax.ShapeDtypeStruct((m, n_hg), F32)],
        compiler_params=pltpu.CompilerParams(
            dimension_semantics=("arbitrary",), vmem_limit_bytes=48 * 1024 * 1024),
        name="inproj",
    )(x, gain.reshape(1, d), w_bf16)


LANES = 128
NEG_BIG = -1e30
OP_SUB = 512
ROW_SUB = (D_MODEL // 2) // LANES


def _outproj_body(zhy_ref, yhg_ref, h0_ref, ghy_ref, gffn_ref, wo_ref, wr_ref,
                  h1_ref, a2p_ref, ri_ref, rg_ref, cnt_ref, tri_s, carry_s):
    i = pl.program_id(0)
    tm = h1_ref.shape[0]
    ts = tri_s.shape[0]

    @pl.when(i == 0)
    def _():
        r = lax.broadcasted_iota(jnp.int32, (ts, ts), 0)
        c = lax.broadcasted_iota(jnp.int32, (ts, ts), 1)
        tri_s[...] = (r > c).astype(BF16)
        carry_s[...] = jnp.zeros_like(carry_s)

    lane = lax.broadcasted_iota(jnp.int32, (ts, LANES), 1)
    is_g = lane < N_GROUPS
    carry = carry_s[...]
    for r0 in range(0, tm, ts):
        rows = slice(r0, r0 + ts)
        yhy = _rms(zhy_ref[rows, :], ghy_ref[...]).astype(BF16)
        yhg = yhg_ref[rows, :].astype(BF16)
        acc = jnp.dot(yhy, wo_ref[:D_HYENA, :], preferred_element_type=F32)
        acc = acc + jnp.dot(yhg, wo_ref[D_HYENA:, :], preferred_element_type=F32)
        h1 = h0_ref[rows, :] + acc
        h1_ref[rows, :] = h1
        a2 = _rms(h1, gffn_ref[...])
        a2p_ref[rows] = _pack_bf16_pairs(a2).reshape(ts, ROW_SUB, LANES)
        lg = jnp.dot(a2.astype(BF16), wr_ref[...], preferred_element_type=F32)

        gl = jnp.where(is_g, lg, NEG_BIG)
        gmax = jnp.max(gl, axis=1, keepdims=True)
        gsel = jnp.min(jnp.where(gl == gmax, lane, LANES), axis=1, keepdims=True)
        gden = jnp.sum(jnp.where(is_g, jnp.exp(gl - gmax), 0.0), axis=1, keepdims=True)
        p_group = 1.0 / gden
        in_grp = (lane >= N_GROUPS) & (lane < N_GROUPS + N_EXPERTS) & (
            ((lane - N_GROUPS) >> 3) == gsel)
        el = jnp.where(in_grp, lg, NEG_BIG)
        m1 = jnp.max(el, axis=1, keepdims=True)
        i1 = jnp.min(jnp.where(el == m1, lane, LANES), axis=1, keepdims=True)
        el2 = jnp.where(lane == i1, NEG_BIG, el)
        m2 = jnp.max(el2, axis=1, keepdims=True)
        i2 = jnp.min(jnp.where(el2 == m2, lane, LANES), axis=1, keepdims=True)
        r21 = jnp.exp(m2 - m1)
        gate1 = p_group / (1.0 + r21)
        gate2 = gate1 * r21

        hit1 = lane == i1
        hit2 = lane == i2
        onehot = (hit1 | hit2).astype(BF16)
        pre = jnp.dot(tri_s[...], onehot, preferred_element_type=F32) + carry
        pos1 = jnp.sum(jnp.where(hit1, pre, 0.0), axis=1, keepdims=True).astype(jnp.int32)
        pos2 = jnp.sum(jnp.where(hit2, pre, 0.0), axis=1, keepdims=True).astype(jnp.int32)
        carry = carry + jnp.sum(onehot.astype(F32), axis=0, keepdims=True)

        zero_i = jnp.zeros((ts, LANES), jnp.int32)
        ri_ref[rows, :] = jnp.where(lane == 0, i1 - N_GROUPS,
                          jnp.where(lane == 1, i2 - N_GROUPS,
                          jnp.where(lane == 2, pos1, jnp.where(lane == 3, pos2, zero_i))))
        rg_ref[rows, :] = jnp.where(lane == 0, gate1, jnp.where(lane == 1, gate2, 0.0))
    carry_s[...] = carry
    cnt_ref[...] = carry


def _outproj(zhy, yhg, h0, g_hy, g_ffn, wo_bf16, wr_bf16, tm):
    m = h0.shape[0]
    return pl.pallas_call(
        _outproj_body,
        grid=(m // tm,),
        in_specs=[pl.BlockSpec((tm, D_HYENA), lambda i: (i, 0)),
                  pl.BlockSpec((tm, D_HGRN), lambda i: (i, 0)),
                  pl.BlockSpec((tm, D_MODEL), lambda i: (i, 0)),
                  pl.BlockSpec((1, D_HYENA), lambda i: (0, 0)),
                  pl.BlockSpec((1, D_MODEL), lambda i: (0, 0)),
                  pl.BlockSpec((D_MODEL, D_MODEL), lambda i: (0, 0)),
                  pl.BlockSpec((D_MODEL, LANES), lambda i: (0, 0))],
        out_specs=[pl.BlockSpec((tm, D_MODEL), lambda i: (i, 0)),
                   pl.BlockSpec((tm, ROW_SUB, LANES), lambda i: (i, 0, 0)),
                   pl.BlockSpec((tm, LANES), lambda i: (i, 0)),
                   pl.BlockSpec((tm, LANES), lambda i: (i, 0)),
                   pl.BlockSpec((1, LANES), lambda i: (0, 0))],
        out_shape=[jax.ShapeDtypeStruct((m, D_MODEL), F32),
                   jax.ShapeDtypeStruct((m, ROW_SUB, LANES), jnp.uint32),
                   jax.ShapeDtypeStruct((m, LANES), jnp.int32),
                   jax.ShapeDtypeStruct((m, LANES), F32),
                   jax.ShapeDtypeStruct((1, LANES), F32)],
        scratch_shapes=[pltpu.VMEM((OP_SUB, OP_SUB), BF16), pltpu.VMEM((1, LANES), F32)],
        compiler_params=pltpu.CompilerParams(
            dimension_semantics=("arbitrary",), vmem_limit_bytes=48 * 1024 * 1024),
        name="outproj",
    )(zhy, yhg, h0, g_hy.reshape(1, -1), g_ffn.reshape(1, -1), wo_bf16, wr_bf16)


EXP_TB = 256


def _wait_rows(ref, n_rows, sem):
    pltpu.make_async_copy(ref.at[pl.ds(0, n_rows)], ref.at[pl.ds(n_rows, n_rows)], sem).wait()


def _dispatch_body(lastblk_ref, npad_ref, nused_ref, d0_ref, d1_ref, h1_ref, xb_ref, zero_s, sem_z,
                   sem):
    i = pl.program_id(0)
    tm = d0_ref.shape[0]
    n_blocks = xb_ref.shape[0] // EXP_TB

    @pl.when(i == 0)
    def _():
        zero_s[...] = jnp.zeros_like(zero_s)

        def zero_copy(row0):
            row0 = pl.multiple_of(row0, EXP_TB)
            return pltpu.make_async_copy(zero_s, xb_ref.at[pl.ds(row0, EXP_TB)], sem_z)

        for e in range(N_EXPERTS):
            @pl.when(npad_ref[e] > 0)
            def _():
                zero_copy(lastblk_ref[e]).start()
        for e in range(N_EXPERTS):
            @pl.when(npad_ref[e] > 0)
            def _():
                zero_copy(lastblk_ref[e]).wait()

        def start_tail(b, carry):
            zero_copy(b * EXP_TB).start()
            return carry

        def wait_tail(b, carry):
            zero_copy(b * EXP_TB).wait()
            return carry

        lax.fori_loop(nused_ref[0], n_blocks, start_tail, 0)
        lax.fori_loop(nused_ref[0], n_blocks, wait_tail, 0)

    def issue(t, carry):
        pltpu.make_async_copy(h1_ref.at[t], xb_ref.at[d0_ref[t]], sem).start()
        pltpu.make_async_copy(h1_ref.at[t], xb_ref.at[d1_ref[t]], sem).start()
        return carry

    lax.fori_loop(0, tm, issue, 0, unroll=16)
    _wait_rows(xb_ref, TOP_K * tm, sem)


def _dispatch(lastblk, npad, n_used, dest, rows, n_slots, tm):
    m = rows.shape[0]
    grid_spec = pltpu.PrefetchScalarGridSpec(
        num_scalar_prefetch=3,
        grid=(m // tm,),
        in_specs=[pl.BlockSpec((tm,), lambda i, lb, npd, nu: (i,), memory_space=pltpu.SMEM),
                  pl.BlockSpec((tm,), lambda i, lb, npd, nu: (i,), memory_space=pltpu.SMEM),
                  pl.BlockSpec((tm, ROW_SUB, LANES), lambda i, lb, npd, nu: (i, 0, 0))],
        out_specs=pl.BlockSpec(memory_space=pl.ANY),
        scratch_shapes=[pltpu.VMEM((EXP_TB, ROW_SUB, LANES), rows.dtype),
                        pltpu.SemaphoreType.DMA(()), pltpu.SemaphoreType.DMA(())],
    )
    return pl.pallas_call(
        _dispatch_body,
        grid_spec=grid_spec,
        out_shape=jax.ShapeDtypeStruct((n_slots, ROW_SUB, LANES), rows.dtype),
        compiler_params=pltpu.CompilerParams(dimension_semantics=("arbitrary",)),
        name="dispatch",
    )(lastblk, npad, n_used, dest[0], dest[1], rows)


def _expert_body(eid_ref, nused_ref, xb_ref, wg_ref, wu_ref, wd_ref, o_ref, wg_s, wu_s, wd_s):
    i = pl.program_id(0)
    half = D_MODEL // 2

    @pl.when(i < nused_ref[0])
    def _():
        prev = eid_ref[jnp.maximum(i - 1, 0)]

        @pl.when((i == 0) | (eid_ref[i] != prev))
        def _():
            wg_s[...] = wg_ref[0].astype(BF16)
            wu_s[...] = wu_ref[0].astype(BF16)
            wd_s[...] = wd_ref[0].astype(BF16)

        lo, hi = _unpack_bf16_pairs(xb_ref[...].reshape(EXP_TB, half))
        lo, hi = lo.astype(BF16), hi.astype(BF16)

        def proj(w_s):
            return (jnp.dot(lo, w_s[:half, :], preferred_element_type=F32)
                    + jnp.dot(hi, w_s[half:, :], preferred_element_type=F32))

        g = proj(wg_s)
        u = proj(wu_s)
        hmid = (g * jax.nn.sigmoid(g) * u).astype(BF16)
        y = _pack_bf16_pairs(jnp.dot(hmid, wd_s[...], preferred_element_type=F32))
        o_ref[...] = y.reshape(EXP_TB, ROW_SUB, LANES)

    @pl.when(i >= nused_ref[0])
    def _():
        o_ref[...] = jnp.zeros_like(o_ref)


def _experts(block_eid, n_used, xb, w_gate, w_up, w_down):
    n_slots = xb.shape[0]

    def blk(i, e, nu):
        return jnp.minimum(i, nu[0] - 1)

    grid_spec = pltpu.PrefetchScalarGridSpec(
        num_scalar_prefetch=2,
        grid=(n_slots // EXP_TB,),
        in_specs=[pl.BlockSpec((EXP_TB, ROW_SUB, LANES), lambda i, e, nu: (blk(i, e, nu), 0, 0)),
                  pl.BlockSpec((1, D_MODEL, D_EXPERT), lambda i, e, nu: (e[blk(i, e, nu)], 0, 0)),
                  pl.BlockSpec((1, D_MODEL, D_EXPERT), lambda i, e, nu: (e[blk(i, e, nu)], 0, 0)),
                  pl.BlockSpec((1, D_EXPERT, D_MODEL), lambda i, e, nu: (e[blk(i, e, nu)], 0, 0))],
        out_specs=pl.BlockSpec((EXP_TB, ROW_SUB, LANES), lambda i, e, nu: (i, 0, 0)),
        scratch_shapes=[pltpu.VMEM((D_MODEL, D_EXPERT), BF16),
                        pltpu.VMEM((D_MODEL, D_EXPERT), BF16),
                        pltpu.VMEM((D_EXPERT, D_MODEL), BF16)],
    )
    return pl.pallas_call(
        _expert_body,
        grid_spec=grid_spec,
        out_shape=jax.ShapeDtypeStruct((n_slots, ROW_SUB, LANES), jnp.uint32),
        compiler_params=pltpu.CompilerParams(
            dimension_semantics=("arbitrary",), vmem_limit_bytes=48 * 1024 * 1024),
        name="experts",
    )(block_eid, n_used, xb, w_gate, w_up, w_down)


def _combine_body(d0_ref, d1_ref, n0_ref, n1_ref, h1_ref, rg_ref, gfin_ref, yb_ref, o_ref, y_s, sem):
    i = pl.program_id(0)
    tm = h1_ref.shape[0]
    slot = i % 2

    def issue(idx_refs, sl):
        def body(t, carry):
            for k in range(TOP_K):
                pltpu.make_async_copy(yb_ref.at[idx_refs[k][t]], y_s.at[sl, k, t],
                                      sem.at[sl]).start()
            return carry

        lax.fori_loop(0, tm, body, 0, unroll=16)

    @pl.when(i == 0)
    def _():
        issue((d0_ref, d1_ref), 0)

    @pl.when(i + 1 < pl.num_programs(0))
    def _():
        issue((n0_ref, n1_ref), 1 - slot)

    _wait_rows(yb_ref, TOP_K * tm, sem.at[slot])
    rg = rg_ref[...]
    g1, g2 = rg[:, 0:1], rg[:, 1:2]
    half = D_MODEL // 2
    lo1, hi1 = _unpack_bf16_pairs(y_s[slot, 0].reshape(tm, half))
    lo2, hi2 = _unpack_bf16_pairs(y_s[slot, 1].reshape(tm, half))
    h2_lo = h1_ref[:, :half] + g1 * lo1 + g2 * lo2
    h2_hi = h1_ref[:, half:] + g1 * hi1 + g2 * hi2
    ms = (jnp.sum(h2_lo * h2_lo, axis=-1, keepdims=True)
          + jnp.sum(h2_hi * h2_hi, axis=-1, keepdims=True)) * (1.0 / D_MODEL)
    inv = lax.rsqrt(ms + EPS)
    o_ref[:, :half] = h2_lo * inv * gfin_ref[:, :half]
    o_ref[:, half:] = h2_hi * inv * gfin_ref[:, half:]


def _combine(dest, h1, rg, g_fin, yb, tm):
    m = h1.shape[0]
    last = m // tm - 1
    return pl.pallas_call(
        _combine_body,
        grid=(m // tm,),
        in_specs=[pl.BlockSpec((tm,), lambda i: (i,), memory_space=pltpu.SMEM),
                  pl.BlockSpec((tm,), lambda i: (i,), memory_space=pltpu.SMEM),
                  pl.BlockSpec((tm,), lambda i: (jnp.minimum(i + 1, last),), memory_space=pltpu.SMEM),
                  pl.BlockSpec((tm,), lambda i: (jnp.minimum(i + 1, last),), memory_space=pltpu.SMEM),
                  pl.BlockSpec((tm, D_MODEL), lambda i: (i, 0)),
                  pl.BlockSpec((tm, LANES), lambda i: (i, 0)),
                  pl.BlockSpec((1, D_MODEL), lambda i: (0, 0)),
                  pl.BlockSpec(memory_space=pl.ANY)],
        out_specs=pl.BlockSpec((tm, D_MODEL), lambda i: (i, 0)),
        out_shape=jax.ShapeDtypeStruct((m, D_MODEL), F32),
        scratch_shapes=[pltpu.VMEM((2, TOP_K, tm, ROW_SUB, LANES), yb.dtype),
                        pltpu.SemaphoreType.DMA((2,))],
        compiler_params=pltpu.CompilerParams(dimension_semantics=("arbitrary",)),
        name="combine",
    )(dest[0], dest[1], dest[0], dest[1], h1, rg, g_fin.reshape(1, -1), yb)


def _moe(h1, a2p, ri, rg, cnt, g_fin, w_gate, w_up, w_down):
    m = h1.shape[0]
    n_blocks = TOP_K * m // EXP_TB + N_EXPERTS
    n_slots = n_blocks * EXP_TB
    counts = cnt[0, N_GROUPS:N_GROUPS + N_EXPERTS].astype(jnp.int32)
    padded = (counts + EXP_TB - 1) // EXP_TB * EXP_TB
    pend = jnp.cumsum(padded)
    base = pend - padded
    rt = jnp.transpose(ri)[:2 * TOP_K]
    sel = rt[None, :TOP_K] == jnp.arange(N_EXPERTS, dtype=jnp.int32)[:, None, None]
    dest = jnp.sum(jnp.where(sel, base[:, None, None], 0), axis=0) + rt[TOP_K:]
    blk_start = jnp.arange(n_blocks, dtype=jnp.int32) * EXP_TB
    block_eid = jnp.minimum(jnp.sum(blk_start[:, None] >= pend[None, :], axis=1),
                            N_EXPERTS - 1).astype(jnp.int32)
    n_used = (pend[-1:] // EXP_TB).astype(jnp.int32)
    lastblk = (pend - EXP_TB).astype(jnp.int32)

    xb = _dispatch(lastblk, padded.astype(jnp.int32), n_used, dest, a2p, n_slots, tm=512)
    yb = _experts(block_eid, n_used, xb, w_gate, w_up, w_down)
    return _combine(dest, h1, rg, g_fin, yb, tm=256)


ND = 2176
NCIRC = 2 * ND
NH = ND // 2
SEQ = 2048
L_TOT = SEQ + N_META
LH = L_TOT // 2
LHP = 1152
LHR = 1040
HY_CT = 256
HY_FB = NH // 2
HY_IB = ((0, 528), (528, 512))
HY_CB = LH // 3
HALO = 8


def _dft_matrices():
    a = 2 * jnp.arange(NH, dtype=jnp.int32) + 1
    t_hi = jnp.arange(LHP // LANES, dtype=jnp.int32) * LANES
    t_lo = jnp.arange(LANES, dtype=jnp.int32)
    valid = ((t_hi[:, None] + t_lo[None, :]) < LH).reshape(1, LHP)

    def cos_sin(m, denom):
        ang = m.astype(F32) * (math.pi / denom)
        return jnp.cos(ang), jnp.sin(ang)

    c_hi, s_hi = cos_sin((a[:, None] * ((4 * t_hi) % (4 * NCIRC))[None, :]) % (4 * NCIRC), 2 * NCIRC)
    c_hi, s_hi = c_hi[:, :, None], s_hi[:, :, None]
    out = []
    for c in (1, 3):
        c_lo, s_lo = cos_sin((a[:, None] * (4 * t_lo + c)[None, :]) % (4 * NCIRC), 2 * NCIRC)
        c_lo, s_lo = c_lo[:, None, :], s_lo[:, None, :]
        cos_phi = jnp.where(valid, (c_hi * c_lo - s_hi * s_lo).reshape(NH, LHP), 0.0)
        sin_phi = jnp.where(valid, (s_hi * c_lo + c_hi * s_lo).reshape(NH, LHP), 0.0)
        out.append(jnp.concatenate([cos_phi, sin_phi if c == 1 else -sin_phi], axis=0).astype(BF16))
    ce, co = out
    return ce, co, ce.T, co.T


def _forward_half_spectrum(ce_ref, co_ref, xe, xo, j0):
    a1 = jnp.dot(ce_ref[j0:j0 + HY_FB, :], xe, preferred_element_type=F32)
    a2 = jnp.dot(ce_ref[NH + j0:NH + j0 + HY_FB, :], xe, preferred_element_type=F32)
    b1 = jnp.dot(co_ref[j0:j0 + HY_FB, :], xo, preferred_element_type=F32)
    b2 = jnp.dot(co_ref[NH + j0:NH + j0 + HY_FB, :], xo, preferred_element_type=F32)
    return a1 + b1, a2 - b2, a2 + b2, a1 - b1


def _spectra_body(s_ref, d_ref, ce_ref, co_ref, rot_ref, hr_ref, hi_ref,
                  stage_s, se_s, so_s, de_s, do_s):
    ct = hr_ref.shape[-1]
    nl = ct // LANES
    for x_ref, xe_s, xo_s in ((s_ref, se_s, so_s), (d_ref, de_s, do_s)):
        for j in range(nl):
            stage_s[j] = x_ref[:, j * LANES:(j + 1) * LANES]
        for par, dst in ((0, xe_s), (1, xo_s)):
            for r0 in range(0, LH, HY_CB):
                dst[r0:r0 + HY_CB, :] = jnp.concatenate(
                    [stage_s.at[j][pl.ds(par + 2 * r0, HY_CB, stride=2), :] for j in range(nl)],
                    axis=1).astype(BF16)
            dst[LH:LHP, :] = jnp.zeros((LHP - LH, ct), BF16)
    for j0 in range(0, NH, HY_FB):
        ar1, _, ar2, _ = _forward_half_spectrum(ce_ref, co_ref, se_s[...], so_s[...], j0)
        _, q1, _, q2 = _forward_half_spectrum(ce_ref, co_ref, de_s[...], do_s[...], j0)
        for r0, ar, q in ((j0, ar1, q1), (NH + j0, ar2, q2)):
            c = rot_ref[r0:r0 + HY_FB, 0:1]
            s = rot_ref[r0:r0 + HY_FB, 1:2]
            hr_ref[0, r0:r0 + HY_FB, :] = c * ar + s * q
            hi_ref[0, r0:r0 + HY_FB, :] = s * ar - c * q


def _spectra(h_fwd, h_bwd, ce, co):
    hbs = jnp.concatenate([h_bwd[1:], jnp.zeros_like(h_bwd[:1])], axis=0)
    s2d = (h_fwd + hbs).reshape(L_TOT, HYENA_ORDER * D_HYENA)
    d2d = (h_fwd - hbs).reshape(L_TOT, HYENA_ORDER * D_HYENA)
    j = jnp.arange(NH, dtype=F32)
    k = jnp.concatenate([j, ND - 1 - j])
    half = (2 * k + 1) * (math.pi / (2 * NCIRC))
    scale = 2.0 / NCIRC
    rot = jnp.stack([jnp.cos(half) * scale, jnp.sin(half) * scale], axis=1)
    rot = jnp.pad(rot, ((0, 0), (0, LANES - 2)))
    n_ord, ct = HYENA_ORDER, HY_CT
    nct = D_HYENA // ct
    col = pl.BlockSpec((L_TOT, ct), lambda o, c: (0, o * nct + c))
    mat = pl.BlockSpec((ND, LHP), lambda o, c: (0, 0))
    out = pl.BlockSpec((1, ND, ct), lambda o, c: (o, 0, c))
    half_bf16 = pltpu.VMEM((LHP, ct), BF16)
    return pl.pallas_call(
        _spectra_body,
        grid=(n_ord, nct),
        in_specs=[col, col, mat, mat, pl.BlockSpec((ND, LANES), lambda o, c: (0, 0))],
        out_specs=[out, out],
        out_shape=[jax.ShapeDtypeStruct((n_ord, ND, D_HYENA), F32)] * 2,
        scratch_shapes=[pltpu.VMEM((ct // LANES, L_TOT, LANES), F32),
                        half_bf16, half_bf16, half_bf16, half_bf16],
        compiler_params=pltpu.CompilerParams(
            dimension_semantics=("arbitrary", "arbitrary"), vmem_limit_bytes=48 * 1024 * 1024),
        name="spectra",
    )(s2d, d2d, ce, co, rot)


def _hyena_body(pv_ref, px1_ref, px2_ref, mv_ref, mx1_ref, mx2_ref,
                wv_ref, wx1_ref, wx2_ref, bv_ref, bx1_ref, bx2_ref, skip_ref,
                ce_ref, co_ref, cet_ref, cot_ref, hr_ref, hi_ref, o_ref,
                stage_s, ze_s, zo_s, g1e_s, g1o_s, g2e_s, g2o_s, zbe_s, zbo_s, we_s, wo_s):
    ct = o_ref.shape[-1]
    nl = ct // LANES

    def strided(off, r0, rows):
        return jnp.concatenate(
            [stage_s.at[j][pl.ds(HALO + off + 2 * r0, rows, stride=2), :] for j in range(nl)], axis=1)

    def short_conv(p_ref, m_ref, w_ref, b_ref, dst_e, dst_o):
        for j in range(nl):
            ln = slice(j * LANES, (j + 1) * LANES)
            stage_s[j, 0:HALO, :] = jnp.zeros((HALO, LANES), F32)
            stage_s[j, HALO:HALO + N_META, :] = m_ref[:, ln].astype(F32)
            stage_s[j, HALO + N_META:HALO + L_TOT, :] = p_ref[0, :, ln].astype(F32)
            stage_s[j, HALO + L_TOT:HALO + L_TOT + HALO, :] = jnp.zeros((HALO, LANES), F32)
        w = w_ref[...]
        b = b_ref[...]
        for r0 in range(0, LH, HY_CB):
            sm1, s0, s1, s2 = (strided(off, r0, HY_CB) for off in (-1, 0, 1, 2))
            dst_e[r0:r0 + HY_CB, :] = b + sm1 * w[0:1] + s0 * w[1:2] + s1 * w[2:3]
            dst_o[r0:r0 + HY_CB, :] = b + s0 * w[0:1] + s1 * w[1:2] + s2 * w[2:3]
        dst_e[LH:LHR, :] = jnp.zeros((LHR - LH, ct), F32)
        dst_o[LH:LHR, :] = jnp.zeros((LHR - LH, ct), F32)

    short_conv(pv_ref, mv_ref, wv_ref, bv_ref, ze_s, zo_s)
    short_conv(px1_ref, mx1_ref, wx1_ref, bx1_ref, g1e_s, g1o_s)
    short_conv(px2_ref, mx2_ref, wx2_ref, bx2_ref, g2e_s, g2o_s)
    zbe_s[LH:LHP, :] = jnp.zeros((LHP - LH, ct), BF16)
    zbo_s[LH:LHP, :] = jnp.zeros((LHP - LH, ct), BF16)

    for n, (ge_s, go_s) in enumerate(((g1e_s, g1o_s), (g2e_s, g2o_s))):
        zbe_s[0:LH, :] = ze_s[0:LH, :].astype(BF16)
        zbo_s[0:LH, :] = zo_s[0:LH, :].astype(BF16)
        for j0 in range(0, NH, HY_FB):
            p1, q1, p2, q2 = _forward_half_spectrum(ce_ref, co_ref, zbe_s[...], zbo_s[...], j0)
            h1r = hr_ref[n, j0:j0 + HY_FB, :]
            h1i = hi_ref[n, j0:j0 + HY_FB, :]
            h2r = hr_ref[n, NH + j0:NH + j0 + HY_FB, :]
            h2i = hi_ref[n, NH + j0:NH + j0 + HY_FB, :]
            yr1 = p1 * h1r + q1 * h1i
            ny1 = q1 * h1r - p1 * h1i
            yr2 = p2 * h2r + q2 * h2i
            ny2 = q2 * h2r - p2 * h2i
            we_s[j0:j0 + HY_FB, :] = (yr1 + ny2).astype(BF16)
            we_s[NH + j0:NH + j0 + HY_FB, :] = (yr2 + ny1).astype(BF16)
            wo_s[j0:j0 + HY_FB, :] = (yr1 - ny2).astype(BF16)
            wo_s[NH + j0:NH + j0 + HY_FB, :] = (yr2 - ny1).astype(BF16)
        skip = skip_ref[n:n + 1, :]
        for parity, (ct_ref, w_s, z_s, g_s) in enumerate(((cet_ref, we_s, ze_s, ge_s),
                                                          (cot_ref, wo_s, zo_s, go_s))):
            for r0, rb in HY_IB:
                y = jnp.dot(ct_ref[r0:r0 + rb, :], w_s[...], preferred_element_type=F32)
                znew = g_s[r0:r0 + rb, :] * (y + z_s[r0:r0 + rb, :] * skip)
                if n == 0:
                    z_s[r0:r0 + rb, :] = znew
                else:
                    n_valid = min(r0 + rb, LH) - r0
                    for j in range(nl):
                        stage_s.at[j][pl.ds(HALO + parity + 2 * r0, n_valid, stride=2), :] = (
                            znew[0:n_valid, j * LANES:(j + 1) * LANES])
    for j in range(nl):
        o_ref[0, :, j * LANES:(j + 1) * LANES] = stage_s[j, HALO + N_META:HALO + L_TOT, :]


def _hyena(p_hy, pm_hy, conv_w, conv_b, skip, mats, hr, hi):
    bsz = p_hy.shape[0]
    nct = D_HYENA // HY_CT
    ct = HY_CT
    ce, co, cet, cot = mats

    def part(j):
        return [pl.BlockSpec((1, SEQ, ct), lambda c, b, j=j: (b, 0, j * nct + c))]

    def mpart(j):
        return [pl.BlockSpec((N_META, ct), lambda c, b, j=j: (0, j * nct + c))]

    def wpart(rows, j):
        return [pl.BlockSpec((rows, ct), lambda c, b, j=j: (0, j * nct + c))]

    spec = pl.BlockSpec((HYENA_ORDER, ND, ct), lambda c, b: (0, 0, c), pipeline_mode=pl.Buffered(1))
    in_specs = (part(0) + part(1) + part(2) + mpart(0) + mpart(1) + mpart(2)
                + wpart(3, 0) + wpart(3, 1) + wpart(3, 2)
                + wpart(1, 0) + wpart(1, 1) + wpart(1, 2)
                + [pl.BlockSpec((HYENA_ORDER, ct), lambda c, b: (0, c)),
                   pl.BlockSpec((ND, LHP), lambda c, b: (0, 0)),
                   pl.BlockSpec((ND, LHP), lambda c, b: (0, 0)),
                   pl.BlockSpec((LHP, ND), lambda c, b: (0, 0)),
                   pl.BlockSpec((LHP, ND), lambda c, b: (0, 0)),
                   spec, spec])
    cb = conv_b.reshape(1, -1)
    f32_half = pltpu.VMEM((LHR, ct), F32)
    return pl.pallas_call(
        _hyena_body,
        grid=(nct, bsz),
        in_specs=in_specs,
        out_specs=pl.BlockSpec((1, SEQ, ct), lambda c, b: (b, 0, c)),
        out_shape=jax.ShapeDtypeStruct((bsz, SEQ, D_HYENA), F32),
        scratch_shapes=[pltpu.VMEM((ct // LANES, L_TOT + 2 * HALO, LANES), F32),
                        f32_half, f32_half, f32_half, f32_half, f32_half, f32_half,
                        pltpu.VMEM((LHP, ct), BF16), pltpu.VMEM((LHP, ct), BF16),
                        pltpu.VMEM((ND, ct), BF16), pltpu.VMEM((ND, ct), BF16)],
        compiler_params=pltpu.CompilerParams(
            dimension_semantics=("arbitrary", "arbitrary"), vmem_limit_bytes=60 * 1024 * 1024),
        name="hyena",
    )(p_hy, p_hy, p_hy, pm_hy, pm_hy, pm_hy, conv_w, conv_w, conv_w, cb, cb, cb, skip,
      ce, co, cet, cot, hr, hi)


N_CHUNKS = SEQ // CHUNK
NT_DIMS = (((1,), (1,)), ((), ()))
TN_DIMS = (((0,), (0,)), ((), ()))
MID_F = CHUNK // 2
MID_B = CHUNK // 2 - 1


HG_G = 8
HG_ROWS = HG_G * CHUNK


def _split2(x):
    hi = x.astype(BF16)
    lo = (x - hi.astype(F32)).astype(BF16)
    return hi, lo


def _chunk_prefix_matrix():
    r = lax.broadcasted_iota(jnp.int32, (HG_ROWS, HG_ROWS), 0)
    c = lax.broadcasted_iota(jnp.int32, (HG_ROWS, HG_ROWS), 1)
    return (((r // CHUNK) == (c // CHUNK)) & (c <= r)).astype(BF16)


def _chunk_rows(x, row):
    return jnp.concatenate(
        [jnp.broadcast_to(x[g * CHUNK + row:g * CHUNK + row + 1, :], (CHUNK, x.shape[1]))
         for g in range(x.shape[0] // CHUNK)], axis=0)


def _hgrn_body(q_ref, ff_ref, fb_ref, i_ref, g_ref, mff_ref, mi_ref,
               lbf_ref, lbb_ref, nw_ref, o_ref,
               tri_s, qe_s, sc_s, ut_s, dec_s, st_s):
    hd = HGRN_HEAD_DIM
    row = lax.broadcasted_iota(jnp.int32, (CHUNK, CHUNK), 0)
    col = lax.broadcasted_iota(jnp.int32, (CHUNK, CHUNK), 1)
    lower = row >= col
    upper = col >= row
    lbf = lbf_ref[...]
    lbb = lbb_ref[...]

    @pl.when((pl.program_id(0) == 0) & (pl.program_id(1) == 0))
    def _():
        tri_s[...] = _chunk_prefix_matrix()

    def forget(logit, lb):
        f = lb + (1.0 - lb) * jax.nn.sigmoid(logit)
        return 1.0 - f, jnp.log(f)

    def prefix_sums(lf):
        s = jnp.dot(tri_s[...], jnp.concatenate(_split2(lf), axis=1), preferred_element_type=F32)
        return s[:, :hd] + s[:, hd:]

    k_m, lf_m = forget(mff_ref[...], lbf)
    pad = jnp.zeros((CHUNK - N_META, hd), F32)
    lf_m = jnp.concatenate([pad, lf_m] * HG_G, axis=0)
    b_m = prefix_sums(lf_m)[0:CHUNK]
    kl_m = jnp.concatenate([pad, k_m], axis=0) * jnp.exp(b_m[CHUNK - 1:CHUNK] - b_m)
    v_m = jnp.concatenate([pad, mi_ref[...]], axis=0).astype(BF16)
    st_meta = lax.dot_general(v_m, kl_m.astype(BF16), TN_DIMS, preferred_element_type=F32)

    def phase_a(j, carry):
        r0 = pl.multiple_of(j * HG_ROWS, HG_ROWS)
        rows = pl.ds(r0, HG_ROWS)
        qv = jax.nn.silu(q_ref[0, rows, :])
        vb = i_ref[0, rows, :].astype(BF16)
        k_f, lf_f = forget(ff_ref[0, rows, :], lbf)
        k_b, lf_b = forget(fb_ref[0, rows, :], lbb)
        b_f = prefix_sums(lf_f)
        p_b = prefix_sums(lf_b)
        bmid_f = _chunk_rows(b_f, MID_F)
        blast_f = _chunk_rows(b_f, CHUNK - 1)
        tot_b = _chunk_rows(p_b, CHUNK - 1)
        c_b = tot_b - p_b + lf_b
        cmid_b = _chunk_rows(c_b, MID_B)
        d_f = b_f - bmid_f
        d_b = c_b - cmid_b
        e_f = jnp.exp(d_f)
        e_b = jnp.exp(d_b)
        qs_f = qv * e_f
        ks_f = k_f / e_f
        qs_b = qv * e_b
        ks_b = k_b / e_b
        qs_fb, ks_fb = qs_f.astype(BF16), ks_f.astype(BF16)
        qs_bb, ks_bb = qs_b.astype(BF16), ks_b.astype(BF16)
        for g in range(HG_G):
            sl = slice(g * CHUNK, (g + 1) * CHUNK)
            r1 = slice(g * CHUNK, g * CHUNK + 1)
            n = j * HG_G + g
            rows_g = pl.ds(pl.multiple_of(r0 + g * CHUNK, CHUNK), CHUNK)
            sc_f = lax.dot_general(qs_fb[sl], ks_fb[sl], NT_DIMS, preferred_element_type=F32)
            sc_b = lax.dot_general(qs_bb[sl], ks_bb[sl], NT_DIMS, preferred_element_type=F32)
            sc_s[n] = (jnp.where(lower, sc_f, 0.0) + jnp.where(upper, sc_b, 0.0)).astype(BF16)
            em_f = jnp.exp(bmid_f[r1])
            el_f = jnp.exp(blast_f[r1] - bmid_f[r1])
            em_b = jnp.exp(cmid_b[r1])
            el_b = jnp.exp(tot_b[r1] - cmid_b[r1])
            qe_s[rows_g, :] = jnp.concatenate(
                [qs_f[sl] * em_f, qs_b[sl] * em_b], axis=1).astype(BF16)
            kl = jnp.concatenate([ks_f[sl] * el_f, ks_b[sl] * el_b], axis=1).astype(BF16)
            ut_s[n] = lax.dot_general(vb[sl], kl, TN_DIMS, preferred_element_type=F32)
            dec_s[n] = jnp.concatenate([jnp.exp(blast_f[r1]), jnp.exp(tot_b[r1])], axis=1)
        return carry

    lax.fori_loop(0, N_CHUNKS // HG_G, phase_a, 0, unroll=True)

    st_f = st_meta
    st_b = jnp.zeros((hd, hd), F32)
    for n in range(N_CHUNKS):
        st_s[n, :, 0:hd] = st_f.astype(BF16)
        st_f = dec_s[n, :, 0:hd] * st_f + ut_s[n, :, 0:hd]
        m = N_CHUNKS - 1 - n
        st_s[m, :, hd:2 * hd] = st_b.astype(BF16)
        st_b = dec_s[m, :, hd:2 * hd] * st_b + ut_s[m, :, hd:2 * hd]

    nw = nw_ref[...]

    def phase_c(j, carry):
        r0 = pl.multiple_of(j * HG_ROWS, HG_ROWS)
        rows = pl.ds(r0, HG_ROWS)
        vb = i_ref[0, rows, :].astype(BF16)
        outs = []
        for g in range(HG_G):
            sl = slice(g * CHUNK, (g + 1) * CHUNK)
            n = j * HG_G + g
            rows_g = pl.ds(pl.multiple_of(r0 + g * CHUNK, CHUNK), CHUNK)
            o = jnp.dot(sc_s[n], vb[sl], preferred_element_type=F32)
            outs.append(o + lax.dot_general(qe_s[rows_g, :], st_s[n], NT_DIMS,
                                            preferred_element_type=F32))
        o = jnp.concatenate(outs, axis=0)
        o = o * lax.rsqrt(jnp.mean(o * o, axis=-1, keepdims=True) + EPS)
        o_ref[0, rows, :] = o * nw * jax.nn.silu(g_ref[0, rows, :])
        return carry

    lax.fori_loop(0, N_CHUNKS // HG_G, phase_c, 0, unroll=True)


def _hgrn(phg_x, phg_m, lb_f, lb_b, norm_w):
    bsz = phg_x.shape[0]
    hd = HGRN_HEAD_DIM
    nh = HGRN_HEADS

    def part(j):
        return pl.BlockSpec((1, SEQ, hd), lambda b, h, j=j: (b, 0, j * nh + h))

    def mpart(j):
        return pl.BlockSpec((N_META, hd), lambda b, h, j=j: (0, j * nh + h))

    vec = pl.BlockSpec((1, hd), lambda b, h: (0, h))
    return pl.pallas_call(
        _hgrn_body,
        grid=(bsz, nh),
        in_specs=[part(0), part(1), part(2), part(3), part(4), mpart(1), mpart(3), vec, vec, vec],
        out_specs=pl.BlockSpec((1, SEQ, hd), lambda b, h: (b, 0, h)),
        out_shape=jax.ShapeDtypeStruct((bsz, SEQ, D_HGRN), F32),
        scratch_shapes=[pltpu.VMEM((HG_ROWS, HG_ROWS), BF16),
                        pltpu.VMEM((SEQ, 2 * hd), BF16),
                        pltpu.VMEM((N_CHUNKS, CHUNK, CHUNK), BF16),
                        pltpu.VMEM((N_CHUNKS, hd, 2 * hd), F32),
                        pltpu.VMEM((N_CHUNKS, 1, 2 * hd), F32),
                        pltpu.VMEM((N_CHUNKS, hd, 2 * hd), BF16)],
        compiler_params=pltpu.CompilerParams(
            dimension_semantics=("arbitrary", "arbitrary"), vmem_limit_bytes=40 * 1024 * 1024),
        name="hgrn",
    )(phg_x, phg_x, phg_x, phg_x, phg_x, phg_m, phg_m,
      lb_f.reshape(1, -1), lb_b.reshape(1, -1), norm_w.reshape(1, -1))


def _hyena_filters(L, w1, b1, w2, b2, w3, freq):
    pos = jnp.arange(L, dtype=F32)
    t = pos / max(L - 1, 1)
    bands = jnp.linspace(1e-4, FILTER_BANDS - 1, FILTER_BANDS, dtype=F32)
    ang = (2.0 * math.pi / L) * pos[:, None] * bands[None, :]
    z = jnp.concatenate([t[:, None], jnp.cos(ang), -jnp.sin(ang)], axis=-1)
    hp = lax.Precision.HIGHEST
    hid = jnp.sin(freq * (jnp.dot(z, w1, precision=hp) + b1))
    hid = jnp.sin(freq * (jnp.dot(hid, w2, precision=hp) + b2))
    filt = jnp.dot(hid, w3, precision=hp).reshape(L, 2, HYENA_ORDER, D_HYENA)
    deltas = jnp.abs(jnp.linspace(math.log(DECAY_TARGET) / SLOW_DECAY_PCT,
                                  math.log(DECAY_TARGET) / FAST_DECAY_PCT, D_HYENA, dtype=F32))
    window = jnp.exp(-t[:, None] * deltas[None, :])
    filt = filt * window[:, None, None, :]
    return filt[:, 0], filt[:, 1]


def kernel(x, meta_tokens, w_in, conv_w, conv_b, filt_w1, filt_b1, filt_w2, filt_b2, filt_w3,
           filt_freq, filt_skip, hyena_norm, lb_fwd, lb_bwd, hgrn_norm, w_out, norm_mix, norm_ffn,
           w_router_group, w_router_expert, w_gate, w_up, w_down, norm_final):
    B, S, D = x.shape
    L = S + N_META
    lbf = jnp.cumsum(jax.nn.softmax(lb_fwd, axis=0), axis=0)[0]
    lbb = jnp.cumsum(jax.nn.softmax(lb_bwd, axis=0), axis=0)[0]

    w_in_b = w_in[0].astype(BF16)
    xf = x.reshape(B * S, D)
    phy_x, phg_x = _inproj(xf, norm_mix[0], w_in_b, tm=512)
    phy_m, phg_m = _inproj(meta_tokens, norm_mix[0], w_in_b, tm=N_META)

    mats = _dft_matrices()
    h_fwd, h_bwd = _hyena_filters(L, filt_w1[0], filt_b1[0], filt_w2[0], filt_b2[0], filt_w3[0],
                                  filt_freq[0])
    hr, hi = _spectra(h_fwd, h_bwd, mats[0], mats[1])
    z_hy = _hyena(phy_x.reshape(B, S, D_HYENA_PROJ), phy_m, conv_w[0], conv_b[0], filt_skip[0],
                  mats, hr, hi).reshape(B * S, D_HYENA)

    y_hg = _hgrn(phg_x.reshape(B, S, 5 * D_HGRN), phg_m, lbf, lbb,
                 hgrn_norm[0]).reshape(B * S, D_HGRN)

    w_r = jnp.concatenate([w_router_group[0], w_router_expert[0].reshape(D, N_EXPERTS),
                           jnp.zeros((D, LANES - N_GROUPS - N_EXPERTS), F32)], axis=1).astype(BF16)
    h1, a2p, ri, rg, cnt = _outproj(z_hy, y_hg, xf, hyena_norm[0], norm_ffn[0],
                                    w_out[0].astype(BF16), w_r, tm=1024)
    out = _moe(h1, a2p, ri, rg, cnt, norm_final, w_gate[0], w_up[0], w_down[0])
    return out.reshape(B, S, D)
```

```python
import functools
import math

import jax
import jax.numpy as jnp
from jax import lax
from jax.experimental import pallas as pl
from jax.experimental.pallas import tpu as pltpu

D_MODEL = 1024
N_META = 16
D_HYENA = 512
D_HGRN = 512
HYENA_ORDER = 2
SHORT_CONV = 3
FILTER_EMB = 33
FILTER_BANDS = 16
DECAY_TARGET = 1e-2
FAST_DECAY_PCT = 0.3
SLOW_DECAY_PCT = 1.5
HGRN_HEAD_DIM = 128
HGRN_HEADS = D_HGRN // HGRN_HEAD_DIM
CHUNK = 64
N_GROUPS = 8
EXPERTS_PER_GROUP = 8
N_EXPERTS = 64
TOP_K = 2
D_EXPERT = 512
D_HYENA_PROJ = 3 * D_HYENA
D_IN_PROJ = D_HYENA_PROJ + 5 * D_HGRN
EPS = 1e-6

F32 = jnp.float32
BF16 = jnp.bfloat16


def _rms(x, gain):
    return x * lax.rsqrt(jnp.mean(x * x, axis=-1, keepdims=True) + EPS) * gain


HI16 = 0xFFFF0000


def _pack_bf16_pairs(x):
    c = x.shape[1] // 2
    bits = lax.bitcast_convert_type(x.astype(BF16).astype(F32), jnp.uint32)
    return (bits[:, :c] >> 16) | (bits[:, c:] & jnp.uint32(HI16))


def _unpack_bf16_pairs(w):
    lo = lax.bitcast_convert_type(w << 16, F32)
    hi = lax.bitcast_convert_type(w & jnp.uint32(HI16), F32)
    return lo, hi


def _inproj_body(x_ref, g_ref, w_ref, hy_ref, hg_ref, *, tn):
    a = _rms(x_ref[...], g_ref[...]).astype(BF16)
    for j in range(D_IN_PROJ // tn):
        acc = jnp.dot(a, w_ref[:, j * tn:(j + 1) * tn], preferred_element_type=F32)
        if j * tn < D_HYENA_PROJ:
            hy_ref[:, j * tn:(j + 1) * tn] = acc.astype(hy_ref.dtype)
        else:
            c0 = j * tn - D_HYENA_PROJ
            hg_ref[:, c0:c0 + tn] = acc.astype(hg_ref.dtype)


def _inproj(x, gain, w_bf16, tm):
    m, d = x.shape
    n_hg = D_IN_PROJ - D_HYENA_PROJ
    return pl.pallas_call(
        functools.partial(_inproj_body, tn=512),
        grid=(m // tm,),
        in_specs=[pl.BlockSpec((tm, d), lambda i: (i, 0)),
                  pl.BlockSpec((1, d), lambda i: (0, 0)),
                  pl.BlockSpec((d, D_IN_PROJ), lambda i: (0, 0))],
        out_specs=[pl.BlockSpec((tm, D_HYENA_PROJ), lambda i: (i, 0)),
                   pl.BlockSpec((tm, n_hg), lambda i: (i, 0))],
        out_shape=[jax.ShapeDtypeStruct((m, D_HYENA_PROJ), BF16),
                   jax.ShapeDtypeStruct((m, n_hg), F32)],
        compiler_params=pltpu.CompilerParams(
            dimension_semantics=("arbitrary",), vmem_limit_bytes=48 * 1024 * 1024),
        name="inproj",
    )(x, gain.reshape(1, d), w_bf16)


LANES = 128
NEG_BIG = -1e30
OP_SUB = 512
ROW_SUB = (D_MODEL // 2) // LANES


def _outproj_body(zhy_ref, yhg_ref, h0_ref, ghy_ref, gffn_ref, wo_ref, wr_ref,
                  h1_ref, a2p_ref, ri_ref, rg_ref, cnt_ref, tri_s, carry_s):
    i = pl.program_id(0)
    tm = h1_ref.shape[0]
    ts = tri_s.shape[0]

    @pl.when(i == 0)
    def _():
        r = lax.broadcasted_iota(jnp.int32, (ts, ts), 0)
        c = lax.broadcasted_iota(jnp.int32, (ts, ts), 1)
        tri_s[...] = (r > c).astype(BF16)
        carry_s[...] = jnp.zeros_like(carry_s)

    lane = lax.broadcasted_iota(jnp.int32, (ts, LANES), 1)
    is_g = lane < N_GROUPS
    carry = carry_s[...]
    for r0 in range(0, tm, ts):
        rows = slice(r0, r0 + ts)
        yhy = _rms(zhy_ref[rows, :], ghy_ref[...]).astype(BF16)
        yhg = yhg_ref[rows, :].astype(BF16)
        acc = jnp.dot(yhy, wo_ref[:D_HYENA, :], preferred_element_type=F32)
        acc = acc + jnp.dot(yhg, wo_ref[D_HYENA:, :], preferred_element_type=F32)
        h1 = h0_ref[rows, :] + acc
        h1_ref[rows, :] = h1
        a2 = _rms(h1, gffn_ref[...])
        a2p_ref[rows] = _pack_bf16_pairs(a2).reshape(ts, ROW_SUB, LANES)
        lg = jnp.dot(a2.astype(BF16), wr_ref[...], preferred_element_type=F32)

        gl = jnp.where(is_g, lg, NEG_BIG)
        gmax = jnp.max(gl, axis=1, keepdims=True)
        gsel = jnp.min(jnp.where(gl == gmax, lane, LANES), axis=1, keepdims=True)
        gden = jnp.sum(jnp.where(is_g, jnp.exp(gl - gmax), 0.0), axis=1, keepdims=True)
        p_group = 1.0 / gden
        in_grp = (lane >= N_GROUPS) & (lane < N_GROUPS + N_EXPERTS) & (
            ((lane - N_GROUPS) >> 3) == gsel)
        el = jnp.where(in_grp, lg, NEG_BIG)
        m1 = jnp.max(el, axis=1, keepdims=True)
        i1 = jnp.min(jnp.where(el == m1, lane, LANES), axis=1, keepdims=True)
        el2 = jnp.where(lane == i1, NEG_BIG, el)
        m2 = jnp.max(el2, axis=1, keepdims=True)
        i2 = jnp.min(jnp.where(el2 == m2, lane, LANES), axis=1, keepdims=True)
        r21 = jnp.exp(m2 - m1)
        gate1 = p_group / (1.0 + r21)
        gate2 = gate1 * r21

        hit1 = lane == i1
        hit2 = lane == i2
        onehot = (hit1 | hit2).astype(BF16)
        pre = jnp.dot(tri_s[...], onehot, preferred_element_type=F32) + carry
        pos1 = jnp.sum(jnp.where(hit1, pre, 0.0), axis=1, keepdims=True).astype(jnp.int32)
        pos2 = jnp.sum(jnp.where(hit2, pre, 0.0), axis=1, keepdims=True).astype(jnp.int32)
        carry = carry + jnp.sum(onehot.astype(F32), axis=0, keepdims=True)

        zero_i = jnp.zeros((ts, LANES), jnp.int32)
        ri_ref[rows, :] = jnp.where(lane == 0, i1 - N_GROUPS,
                          jnp.where(lane == 1, i2 - N_GROUPS,
                          jnp.where(lane == 2, pos1, jnp.where(lane == 3, pos2, zero_i))))
        rg_ref[rows, :] = jnp.where(lane == 0, gate1, jnp.where(lane == 1, gate2, 0.0))
    carry_s[...] = carry
    cnt_ref[...] = carry


def _outproj(zhy, yhg, h0, g_hy, g_ffn, wo_bf16, wr_bf16, tm):
    m = h0.shape[0]
    return pl.pallas_call(
        _outproj_body,
        grid=(m // tm,),
        in_specs=[pl.BlockSpec((tm, D_HYENA), lambda i: (i, 0)),
                  pl.BlockSpec((tm, D_HGRN), lambda i: (i, 0)),
                  pl.BlockSpec((tm, D_MODEL), lambda i: (i, 0)),
                  pl.BlockSpec((1, D_HYENA), lambda i: (0, 0)),
                  pl.BlockSpec((1, D_MODEL), lambda i: (0, 0)),
                  pl.BlockSpec((D_MODEL, D_MODEL), lambda i: (0, 0)),
                  pl.BlockSpec((D_MODEL, LANES), lambda i: (0, 0))],
        out_specs=[pl.BlockSpec((tm, D_MODEL), lambda i: (i, 0)),
                   pl.BlockSpec((tm, ROW_SUB, LANES), lambda i: (i, 0, 0)),
                   pl.BlockSpec((tm, LANES), lambda i: (i, 0)),
                   pl.BlockSpec((tm, LANES), lambda i: (i, 0)),
                   pl.BlockSpec((1, LANES), lambda i: (0, 0))],
        out_shape=[jax.ShapeDtypeStruct((m, D_MODEL), F32),
                   jax.ShapeDtypeStruct((m, ROW_SUB, LANES), jnp.uint32),
                   jax.ShapeDtypeStruct((m, LANES), jnp.int32),
                   jax.ShapeDtypeStruct((m, LANES), F32),
                   jax.ShapeDtypeStruct((1, LANES), F32)],
        scratch_shapes=[pltpu.VMEM((OP_SUB, OP_SUB), BF16), pltpu.VMEM((1, LANES), F32)],
        compiler_params=pltpu.CompilerParams(
            dimension_semantics=("arbitrary",), vmem_limit_bytes=48 * 1024 * 1024),
        name="outproj",
    )(zhy, yhg, h0, g_hy.reshape(1, -1), g_ffn.reshape(1, -1), wo_bf16, wr_bf16)


EXP_TB = 512


def _wait_rows(ref, n_rows, sem):
    pltpu.make_async_copy(ref.at[pl.ds(0, n_rows)], ref.at[pl.ds(n_rows, n_rows)], sem).wait()


def _dispatch_body(lastblk_ref, npad_ref, nused_ref, d0_ref, d1_ref, h1_ref, xb_ref, zero_s, sem_z,
                   sem):
    i = pl.program_id(0)
    tm = d0_ref.shape[0]
    n_blocks = xb_ref.shape[0] // EXP_TB

    @pl.when(i == 0)
    def _():
        zero_s[...] = jnp.zeros_like(zero_s)

        def zero_copy(row0):
            row0 = pl.multiple_of(row0, EXP_TB)
            return pltpu.make_async_copy(zero_s, xb_ref.at[pl.ds(row0, EXP_TB)], sem_z)

        for e in range(N_EXPERTS):
            @pl.when(npad_ref[e] > 0)
            def _():
                zero_copy(lastblk_ref[e]).start()
        for e in range(N_EXPERTS):
            @pl.when(npad_ref[e] > 0)
            def _():
                zero_copy(lastblk_ref[e]).wait()

        def start_tail(b, carry):
            zero_copy(b * EXP_TB).start()
            return carry

        def wait_tail(b, carry):
            zero_copy(b * EXP_TB).wait()
            return carry

        lax.fori_loop(nused_ref[0], n_blocks, start_tail, 0)
        lax.fori_loop(nused_ref[0], n_blocks, wait_tail, 0)

    def issue(t, carry):
        pltpu.make_async_copy(h1_ref.at[t], xb_ref.at[d0_ref[t]], sem).start()
        pltpu.make_async_copy(h1_ref.at[t], xb_ref.at[d1_ref[t]], sem).start()
        return carry

    lax.fori_loop(0, tm, issue, 0, unroll=16)
    _wait_rows(xb_ref, TOP_K * tm, sem)


def _dispatch(lastblk, npad, n_used, dest, rows, n_slots, tm):
    m = rows.shape[0]
    grid_spec = pltpu.PrefetchScalarGridSpec(
        num_scalar_prefetch=3,
        grid=(m // tm,),
        in_specs=[pl.BlockSpec((tm,), lambda i, lb, npd, nu: (i,), memory_space=pltpu.SMEM),
                  pl.BlockSpec((tm,), lambda i, lb, npd, nu: (i,), memory_space=pltpu.SMEM),
                  pl.BlockSpec((tm, ROW_SUB, LANES), lambda i, lb, npd, nu: (i, 0, 0))],
        out_specs=pl.BlockSpec(memory_space=pl.ANY),
        scratch_shapes=[pltpu.VMEM((EXP_TB, ROW_SUB, LANES), rows.dtype),
                        pltpu.SemaphoreType.DMA(()), pltpu.SemaphoreType.DMA(())],
    )
    return pl.pallas_call(
        _dispatch_body,
        grid_spec=grid_spec,
        out_shape=jax.ShapeDtypeStruct((n_slots, ROW_SUB, LANES), rows.dtype),
        compiler_params=pltpu.CompilerParams(dimension_semantics=("arbitrary",)),
        name="dispatch",
    )(lastblk, npad, n_used, dest[0], dest[1], rows)


def _expert_body(eid_ref, nused_ref, xb_ref, wg_ref, wu_ref, wd_ref, o_ref, wg_s, wu_s, wd_s):
    i = pl.program_id(0)
    half = D_MODEL // 2

    @pl.when(i < nused_ref[0])
    def _():
        prev = eid_ref[jnp.maximum(i - 1, 0)]

        @pl.when((i == 0) | (eid_ref[i] != prev))
        def _():
            wg_s[...] = wg_ref[0].astype(BF16)
            wu_s[...] = wu_ref[0].astype(BF16)
            wd_s[...] = wd_ref[0].astype(BF16)

        lo, hi = _unpack_bf16_pairs(xb_ref[...].reshape(EXP_TB, half))
        lo, hi = lo.astype(BF16), hi.astype(BF16)

        def proj(w_s):
            return (jnp.dot(lo, w_s[:half, :], preferred_element_type=F32)
                    + jnp.dot(hi, w_s[half:, :], preferred_element_type=F32))

        g = proj(wg_s)
        u = proj(wu_s)
        hmid = (g * jax.nn.sigmoid(g) * u).astype(BF16)
        y = _pack_bf16_pairs(jnp.dot(hmid, wd_s[...], preferred_element_type=F32))
        o_ref[...] = y.reshape(EXP_TB, ROW_SUB, LANES)

    @pl.when(i >= nused_ref[0])
    def _():
        o_ref[...] = jnp.zeros_like(o_ref)


def _experts(block_eid, n_used, xb, w_gate, w_up, w_down):
    n_slots = xb.shape[0]

    def blk(i, e, nu):
        return jnp.minimum(i, nu[0] - 1)

    grid_spec = pltpu.PrefetchScalarGridSpec(
        num_scalar_prefetch=2,
        grid=(n_slots // EXP_TB,),
        in_specs=[pl.BlockSpec((EXP_TB, ROW_SUB, LANES), lambda i, e, nu: (blk(i, e, nu), 0, 0)),
                  pl.BlockSpec((1, D_MODEL, D_EXPERT), lambda i, e, nu: (e[blk(i, e, nu)], 0, 0)),
                  pl.BlockSpec((1, D_MODEL, D_EXPERT), lambda i, e, nu: (e[blk(i, e, nu)], 0, 0)),
                  pl.BlockSpec((1, D_EXPERT, D_MODEL), lambda i, e, nu: (e[blk(i, e, nu)], 0, 0))],
        out_specs=pl.BlockSpec((EXP_TB, ROW_SUB, LANES), lambda i, e, nu: (i, 0, 0)),
        scratch_shapes=[pltpu.VMEM((D_MODEL, D_EXPERT), BF16),
                        pltpu.VMEM((D_MODEL, D_EXPERT), BF16),
                        pltpu.VMEM((D_EXPERT, D_MODEL), BF16)],
    )
    return pl.pallas_call(
        _expert_body,
        grid_spec=grid_spec,
        out_shape=jax.ShapeDtypeStruct((n_slots, ROW_SUB, LANES), jnp.uint32),
        compiler_params=pltpu.CompilerParams(
            dimension_semantics=("arbitrary",), vmem_limit_bytes=48 * 1024 * 1024),
        name="experts",
    )(block_eid, n_used, xb, w_gate, w_up, w_down)


def _combine_body(d0_ref, d1_ref, n0_ref, n1_ref, h1_ref, rg_ref, gfin_ref, yb_ref, o_ref, y_s, sem):
    i = pl.program_id(0)
    tm = h1_ref.shape[0]
    slot = i % 2

    def issue(idx_refs, sl):
        def body(t, carry):
            for k in range(TOP_K):
                pltpu.make_async_copy(yb_ref.at[idx_refs[k][t]], y_s.at[sl, k, t],
                                      sem.at[sl]).start()
            return carry

        lax.fori_loop(0, tm, body, 0, unroll=16)

    @pl.when(i == 0)
    def _():
        issue((d0_ref, d1_ref), 0)

    @pl.when(i + 1 < pl.num_programs(0))
    def _():
        issue((n0_ref, n1_ref), 1 - slot)

    _wait_rows(yb_ref, TOP_K * tm, sem.at[slot])
    rg = rg_ref[...]
    g1, g2 = rg[:, 0:1], rg[:, 1:2]
    half = D_MODEL // 2
    lo1, hi1 = _unpack_bf16_pairs(y_s[slot, 0].reshape(tm, half))
    lo2, hi2 = _unpack_bf16_pairs(y_s[slot, 1].reshape(tm, half))
    h2_lo = h1_ref[:, :half] + g1 * lo1 + g2 * lo2
    h2_hi = h1_ref[:, half:] + g1 * hi1 + g2 * hi2
    ms = (jnp.sum(h2_lo * h2_lo, axis=-1, keepdims=True)
          + jnp.sum(h2_hi * h2_hi, axis=-1, keepdims=True)) * (1.0 / D_MODEL)
    inv = lax.rsqrt(ms + EPS)
    o_ref[:, :half] = h2_lo * inv * gfin_ref[:, :half]
    o_ref[:, half:] = h2_hi * inv * gfin_ref[:, half:]


def _combine(dest, h1, rg, g_fin, yb, tm):
    m = h1.shape[0]
    last = m // tm - 1
    return pl.pallas_call(
        _combine_body,
        grid=(m // tm,),
        in_specs=[pl.BlockSpec((tm,), lambda i: (i,), memory_space=pltpu.SMEM),
                  pl.BlockSpec((tm,), lambda i: (i,), memory_space=pltpu.SMEM),
                  pl.BlockSpec((tm,), lambda i: (jnp.minimum(i + 1, last),), memory_space=pltpu.SMEM),
                  pl.BlockSpec((tm,), lambda i: (jnp.minimum(i + 1, last),), memory_space=pltpu.SMEM),
                  pl.BlockSpec((tm, D_MODEL), lambda i: (i, 0)),
                  pl.BlockSpec((tm, LANES), lambda i: (i, 0)),
                  pl.BlockSpec((1, D_MODEL), lambda i: (0, 0)),
                  pl.BlockSpec(memory_space=pl.ANY)],
        out_specs=pl.BlockSpec((tm, D_MODEL), lambda i: (i, 0)),
        out_shape=jax.ShapeDtypeStruct((m, D_MODEL), F32),
        scratch_shapes=[pltpu.VMEM((2, TOP_K, tm, ROW_SUB, LANES), yb.dtype),
                        pltpu.SemaphoreType.DMA((2,))],
        compiler_params=pltpu.CompilerParams(dimension_semantics=("arbitrary",)),
        name="combine",
    )(dest[0], dest[1], dest[0], dest[1], h1, rg, g_fin.reshape(1, -1), yb)


def _moe(h1, a2p, ri, rg, cnt, g_fin, w_gate, w_up, w_down):
    m = h1.shape[0]
    n_blocks = TOP_K * m // EXP_TB + N_EXPERTS
    n_slots = n_blocks * EXP_TB
    counts = cnt[0, N_GROUPS:N_GROUPS + N_EXPERTS].astype(jnp.int32)
    padded = (counts + EXP_TB - 1) // EXP_TB * EXP_TB
    pend = jnp.cumsum(padded)
    base = pend - padded
    rt = jnp.transpose(ri)[:2 * TOP_K]
    sel = rt[None, :TOP_K] == jnp.arange(N_EXPERTS, dtype=jnp.int32)[:, None, None]
    dest = jnp.sum(jnp.where(sel, base[:, None, None], 0), axis=0) + rt[TOP_K:]
    blk_start = jnp.arange(n_blocks, dtype=jnp.int32) * EXP_TB
    block_eid = jnp.minimum(jnp.sum(blk_start[:, None] >= pend[None, :], axis=1),
                            N_EXPERTS - 1).astype(jnp.int32)
    n_used = (pend[-1:] // EXP_TB).astype(jnp.int32)
    lastblk = (pend - EXP_TB).astype(jnp.int32)

    xb = _dispatch(lastblk, padded.astype(jnp.int32), n_used, dest, a2p, n_slots, tm=512)
    yb = _experts(block_eid, n_used, xb, w_gate, w_up, w_down)
    return _combine(dest, h1, rg, g_fin, yb, tm=256)


ND = 2176
NCIRC = 2 * ND
NH = ND // 2
SEQ = 2048
L_TOT = SEQ + N_META
LH = L_TOT // 2
LHP = 1152
LHR = 1040
HY_CT = 256
HY_FB = NH // 2
HY_IB = ((0, 528), (528, 512))
HY_CB = LH // 3
HALO = 8


def _dft_matrices():
    a = 2 * jnp.arange(NH, dtype=jnp.int32) + 1
    t_hi = jnp.arange(LHP // LANES, dtype=jnp.int32) * LANES
    t_lo = jnp.arange(LANES, dtype=jnp.int32)
    valid = ((t_hi[:, None] + t_lo[None, :]) < LH).reshape(1, LHP)

    def cos_sin(m, denom):
        ang = m.astype(F32) * (math.pi / denom)
        return jnp.cos(ang), jnp.sin(ang)

    c_hi, s_hi = cos_sin((a[:, None] * ((4 * t_hi) % (4 * NCIRC))[None, :]) % (4 * NCIRC), 2 * NCIRC)
    c_hi, s_hi = c_hi[:, :, None], s_hi[:, :, None]
    out = []
    for c in (1, 3):
        c_lo, s_lo = cos_sin((a[:, None] * (4 * t_lo + c)[None, :]) % (4 * NCIRC), 2 * NCIRC)
        c_lo, s_lo = c_lo[:, None, :], s_lo[:, None, :]
        cos_phi = jnp.where(valid, (c_hi * c_lo - s_hi * s_lo).reshape(NH, LHP), 0.0)
        sin_phi = jnp.where(valid, (s_hi * c_lo + c_hi * s_lo).reshape(NH, LHP), 0.0)
        out.append(jnp.concatenate([cos_phi, sin_phi if c == 1 else -sin_phi], axis=0).astype(BF16))
    ce, co = out
    return ce, co, ce.T, co.T


def _forward_half_spectrum(ce_ref, co_ref, xe, xo, j0):
    a1 = jnp.dot(ce_ref[j0:j0 + HY_FB, :], xe, preferred_element_type=F32)
    a2 = jnp.dot(ce_ref[NH + j0:NH + j0 + HY_FB, :], xe, preferred_element_type=F32)
    b1 = jnp.dot(co_ref[j0:j0 + HY_FB, :], xo, preferred_element_type=F32)
    b2 = jnp.dot(co_ref[NH + j0:NH + j0 + HY_FB, :], xo, preferred_element_type=F32)
    return a1 + b1, a2 - b2, a2 + b2, a1 - b1


def _spectra_body(s_ref, d_ref, ce_ref, co_ref, rot_ref, hr_ref, hi_ref,
                  stage_s, se_s, so_s, de_s, do_s):
    ct = hr_ref.shape[-1]
    nl = ct // LANES
    for x_ref, xe_s, xo_s in ((s_ref, se_s, so_s), (d_ref, de_s, do_s)):
        for j in range(nl):
            stage_s[j] = x_ref[:, j * LANES:(j + 1) * LANES]
        for par, dst in ((0, xe_s), (1, xo_s)):
            for r0 in range(0, LH, HY_CB):
                dst[r0:r0 + HY_CB, :] = jnp.concatenate(
                    [stage_s.at[j][pl.ds(par + 2 * r0, HY_CB, stride=2), :] for j in range(nl)],
                    axis=1).astype(BF16)
            dst[LH:LHP, :] = jnp.zeros((LHP - LH, ct), BF16)
    for j0 in range(0, NH, HY_FB):
        ar1, _, ar2, _ = _forward_half_spectrum(ce_ref, co_ref, se_s[...], so_s[...], j0)
        _, q1, _, q2 = _forward_half_spectrum(ce_ref, co_ref, de_s[...], do_s[...], j0)
        for r0, ar, q in ((j0, ar1, q1), (NH + j0, ar2, q2)):
            c = rot_ref[r0:r0 + HY_FB, 0:1]
            s = rot_ref[r0:r0 + HY_FB, 1:2]
            hr_ref[0, r0:r0 + HY_FB, :] = c * ar + s * q
            hi_ref[0, r0:r0 + HY_FB, :] = s * ar - c * q


def _spectra(h_fwd, h_bwd, ce, co):
    hbs = jnp.concatenate([h_bwd[1:], jnp.zeros_like(h_bwd[:1])], axis=0)
    s2d = (h_fwd + hbs).reshape(L_TOT, HYENA_ORDER * D_HYENA)
    d2d = (h_fwd - hbs).reshape(L_TOT, HYENA_ORDER * D_HYENA)
    j = jnp.arange(NH, dtype=F32)
    k = jnp.concatenate([j, ND - 1 - j])
    half = (2 * k + 1) * (math.pi / (2 * NCIRC))
    scale = 2.0 / NCIRC
    rot = jnp.stack([jnp.cos(half) * scale, jnp.sin(half) * scale], axis=1)
    rot = jnp.pad(rot, ((0, 0), (0, LANES - 2)))
    n_ord, ct = HYENA_ORDER, HY_CT
    nct = D_HYENA // ct
    col = pl.BlockSpec((L_TOT, ct), lambda o, c: (0, o * nct + c))
    mat = pl.BlockSpec((ND, LHP), lambda o, c: (0, 0))
    out = pl.BlockSpec((1, ND, ct), lambda o, c: (o, 0, c))
    half_bf16 = pltpu.VMEM((LHP, ct), BF16)
    return pl.pallas_call(
        _spectra_body,
        grid=(n_ord, nct),
        in_specs=[col, col, mat, mat, pl.BlockSpec((ND, LANES), lambda o, c: (0, 0))],
        out_specs=[out, out],
        out_shape=[jax.ShapeDtypeStruct((n_ord, ND, D_HYENA), F32)] * 2,
        scratch_shapes=[pltpu.VMEM((ct // LANES, L_TOT, LANES), F32),
                        half_bf16, half_bf16, half_bf16, half_bf16],
        compiler_params=pltpu.CompilerParams(
            dimension_semantics=("arbitrary", "arbitrary"), vmem_limit_bytes=48 * 1024 * 1024),
        name="spectra",
    )(s2d, d2d, ce, co, rot)


def _hyena_body(pv_ref, px1_ref, px2_ref, mv_ref, mx1_ref, mx2_ref,
                wv_ref, wx1_ref, wx2_ref, bv_ref, bx1_ref, bx2_ref, skip_ref,
                ce_ref, co_ref, cet_ref, cot_ref, hr_ref, hi_ref, o_ref,
                stage_s, ze_s, zo_s, g1e_s, g1o_s, g2e_s, g2o_s, zbe_s, zbo_s, we_s, wo_s):
    ct = o_ref.shape[-1]
    nl = ct // LANES

    def strided(off, r0, rows):
        return jnp.concatenate(
            [stage_s.at[j][pl.ds(HALO + off + 2 * r0, rows, stride=2), :] for j in range(nl)], axis=1)

    def short_conv(p_ref, m_ref, w_ref, b_ref, dst_e, dst_o):
        for j in range(nl):
            ln = slice(j * LANES, (j + 1) * LANES)
            stage_s[j, 0:HALO, :] = jnp.zeros((HALO, LANES), F32)
            stage_s[j, HALO:HALO + N_META, :] = m_ref[:, ln].astype(F32)
            stage_s[j, HALO + N_META:HALO + L_TOT, :] = p_ref[0, :, ln].astype(F32)
            stage_s[j, HALO + L_TOT:HALO + L_TOT + HALO, :] = jnp.zeros((HALO, LANES), F32)
        w = w_ref[...]
        b = b_ref[...]
        for r0 in range(0, LH, HY_CB):
            sm1, s0, s1, s2 = (strided(off, r0, HY_CB) for off in (-1, 0, 1, 2))
            dst_e[r0:r0 + HY_CB, :] = b + sm1 * w[0:1] + s0 * w[1:2] + s1 * w[2:3]
            dst_o[r0:r0 + HY_CB, :] = b + s0 * w[0:1] + s1 * w[1:2] + s2 * w[2:3]
        dst_e[LH:LHR, :] = jnp.zeros((LHR - LH, ct), F32)
        dst_o[LH:LHR, :] = jnp.zeros((LHR - LH, ct), F32)

    short_conv(pv_ref, mv_ref, wv_ref, bv_ref, ze_s, zo_s)
    short_conv(px1_ref, mx1_ref, wx1_ref, bx1_ref, g1e_s, g1o_s)
    short_conv(px2_ref, mx2_ref, wx2_ref, bx2_ref, g2e_s, g2o_s)
    zbe_s[LH:LHP, :] = jnp.zeros((LHP - LH, ct), BF16)
    zbo_s[LH:LHP, :] = jnp.zeros((LHP - LH, ct), BF16)

    for n, (ge_s, go_s) in enumerate(((g1e_s, g1o_s), (g2e_s, g2o_s))):
        zbe_s[0:LH, :] = ze_s[0:LH, :].astype(BF16)
        zbo_s[0:LH, :] = zo_s[0:LH, :].astype(BF16)
        for j0 in range(0, NH, HY_FB):
            p1, q1, p2, q2 = _forward_half_spectrum(ce_ref, co_ref, zbe_s[...], zbo_s[...], j0)
            h1r = hr_ref[n, j0:j0 + HY_FB, :]
            h1i = hi_ref[n, j0:j0 + HY_FB, :]
            h2r = hr_ref[n, NH + j0:NH + j0 + HY_FB, :]
            h2i = hi_ref[n, NH + j0:NH + j0 + HY_FB, :]
            yr1 = p1 * h1r + q1 * h1i
            ny1 = q1 * h1r - p1 * h1i
            yr2 = p2 * h2r + q2 * h2i
            ny2 = q2 * h2r - p2 * h2i
            we_s[j0:j0 + HY_FB, :] = (yr1 + ny2).astype(BF16)
            we_s[NH + j0:NH + j0 + HY_FB, :] = (yr2 + ny1).astype(BF16)
            wo_s[j0:j0 + HY_FB, :] = (yr1 - ny2).astype(BF16)
            wo_s[NH + j0:NH + j0 + HY_FB, :] = (yr2 - ny1).astype(BF16)
        skip = skip_ref[n:n + 1, :]
        for parity, (ct_ref, w_s, z_s, g_s) in enumerate(((cet_ref, we_s, ze_s, ge_s),
                                                          (cot_ref, wo_s, zo_s, go_s))):
            for r0, rb in HY_IB:
                y = jnp.dot(ct_ref[r0:r0 + rb, :], w_s[...], preferred_element_type=F32)
                znew = g_s[r0:r0 + rb, :] * (y + z_s[r0:r0 + rb, :] * skip)
                if n == 0:
                    z_s[r0:r0 + rb, :] = znew
                else:
                    n_valid = min(r0 + rb, LH) - r0
                    for j in range(nl):
                        stage_s.at[j][pl.ds(HALO + parity + 2 * r0, n_valid, stride=2), :] = (
                            znew[0:n_valid, j * LANES:(j + 1) * LANES])
    for j in range(nl):
        o_ref[0, :, j * LANES:(j + 1) * LANES] = stage_s[j, HALO + N_META:HALO + L_TOT, :]


def _hyena(p_hy, pm_hy, conv_w, conv_b, skip, mats, hr, hi):
    bsz = p_hy.shape[0]
    nct = D_HYENA // HY_CT
    ct = HY_CT
    ce, co, cet, cot = mats

    def part(j):
        return [pl.BlockSpec((1, SEQ, ct), lambda c, b, j=j: (b, 0, j * nct + c))]

    def mpart(j):
        return [pl.BlockSpec((N_META, ct), lambda c, b, j=j: (0, j * nct + c))]

    def wpart(rows, j):
        return [pl.BlockSpec((rows, ct), lambda c, b, j=j: (0, j * nct + c))]

    spec = pl.BlockSpec((HYENA_ORDER, ND, ct), lambda c, b: (0, 0, c), pipeline_mode=pl.Buffered(1))
    in_specs = (part(0) + part(1) + part(2) + mpart(0) + mpart(1) + mpart(2)
                + wpart(3, 0) + wpart(3, 1) + wpart(3, 2)
                + wpart(1, 0) + wpart(1, 1) + wpart(1, 2)
                + [pl.BlockSpec((HYENA_ORDER, ct), lambda c, b: (0, c)),
                   pl.BlockSpec((ND, LHP), lambda c, b: (0, 0)),
                   pl.BlockSpec((ND, LHP), lambda c, b: (0, 0)),
                   pl.BlockSpec((LHP, ND), lambda c, b: (0, 0)),
                   pl.BlockSpec((LHP, ND), lambda c, b: (0, 0)),
                   spec, spec])
    cb = conv_b.reshape(1, -1)
    f32_half = pltpu.VMEM((LHR, ct), F32)
    return pl.pallas_call(
        _hyena_body,
        grid=(nct, bsz),
        in_specs=in_specs,
        out_specs=pl.BlockSpec((1, SEQ, ct), lambda c, b: (b, 0, c)),
        out_shape=jax.ShapeDtypeStruct((bsz, SEQ, D_HYENA), F32),
        scratch_shapes=[pltpu.VMEM((ct // LANES, L_TOT + 2 * HALO, LANES), F32),
                        f32_half, f32_half, f32_half, f32_half, f32_half, f32_half,
                        pltpu.VMEM((LHP, ct), BF16), pltpu.VMEM((LHP, ct), BF16),
                        pltpu.VMEM((ND, ct), BF16), pltpu.VMEM((ND, ct), BF16)],
        compiler_params=pltpu.CompilerParams(
            dimension_semantics=("arbitrary", "arbitrary"), vmem_limit_bytes=60 * 1024 * 1024),
        name="hyena",
    )(p_hy, p_hy, p_hy, pm_hy, pm_hy, pm_hy, conv_w, conv_w, conv_w, cb, cb, cb, skip,
      ce, co, cet, cot, hr, hi)


N_CHUNKS = SEQ // CHUNK
NT_DIMS = (((1,), (1,)), ((), ()))
TN_DIMS = (((0,), (0,)), ((), ()))
MID_F = CHUNK // 2
MID_B = CHUNK // 2 - 1


HG_G = 8
HG_ROWS = HG_G * CHUNK


def _split2(x):
    hi = x.astype(BF16)
    lo = (x - hi.astype(F32)).astype(BF16)
    return hi, lo


def _chunk_prefix_matrix():
    r = lax.broadcasted_iota(jnp.int32, (HG_ROWS, HG_ROWS), 0)
    c = lax.broadcasted_iota(jnp.int32, (HG_ROWS, HG_ROWS), 1)
    return (((r // CHUNK) == (c // CHUNK)) & (c <= r)).astype(BF16)


def _chunk_rows(x, row):
    return jnp.concatenate(
        [jnp.broadcast_to(x[g * CHUNK + row:g * CHUNK + row + 1, :], (CHUNK, x.shape[1]))
         for g in range(x.shape[0] // CHUNK)], axis=0)


def _hgrn_body(q_ref, ff_ref, fb_ref, i_ref, g_ref, mff_ref, mi_ref,
               lbf_ref, lbb_ref, nw_ref, o_ref,
               tri_s, qe_s, sc_s, ut_s, dec_s, st_s):
    hd = HGRN_HEAD_DIM
    row = lax.broadcasted_iota(jnp.int32, (CHUNK, CHUNK), 0)
    col = lax.broadcasted_iota(jnp.int32, (CHUNK, CHUNK), 1)
    lower = row >= col
    upper = col >= row
    lbf = lbf_ref[...]
    lbb = lbb_ref[...]

    @pl.when((pl.program_id(0) == 0) & (pl.program_id(1) == 0))
    def _():
        tri_s[...] = _chunk_prefix_matrix()

    def forget(logit, lb):
        f = lb + (1.0 - lb) * jax.nn.sigmoid(logit)
        return 1.0 - f, jnp.log(f)

    def prefix_sums(lf):
        s = jnp.dot(tri_s[...], jnp.concatenate(_split2(lf), axis=1), preferred_element_type=F32)
        return s[:, :hd] + s[:, hd:]

    k_m, lf_m = forget(mff_ref[...], lbf)
    pad = jnp.zeros((CHUNK - N_META, hd), F32)
    lf_m = jnp.concatenate([pad, lf_m] * HG_G, axis=0)
    b_m = prefix_sums(lf_m)[0:CHUNK]
    kl_m = jnp.concatenate([pad, k_m], axis=0) * jnp.exp(b_m[CHUNK - 1:CHUNK] - b_m)
    v_m = jnp.concatenate([pad, mi_ref[...]], axis=0).astype(BF16)
    st_meta = lax.dot_general(v_m, kl_m.astype(BF16), TN_DIMS, preferred_element_type=F32)

    def phase_a(j, carry):
        r0 = pl.multiple_of(j * HG_ROWS, HG_ROWS)
        rows = pl.ds(r0, HG_ROWS)
        qv = jax.nn.silu(q_ref[0, rows, :])
        vb = i_ref[0, rows, :].astype(BF16)
        k_f, lf_f = forget(ff_ref[0, rows, :], lbf)
        k_b, lf_b = forget(fb_ref[0, rows, :], lbb)
        b_f = prefix_sums(lf_f)
        p_b = prefix_sums(lf_b)
        bmid_f = _chunk_rows(b_f, MID_F)
        blast_f = _chunk_rows(b_f, CHUNK - 1)
        tot_b = _chunk_rows(p_b, CHUNK - 1)
        c_b = tot_b - p_b + lf_b
        cmid_b = _chunk_rows(c_b, MID_B)
        d_f = b_f - bmid_f
        d_b = c_b - cmid_b
        e_f = jnp.exp(d_f)
        e_b = jnp.exp(d_b)
        qs_f = qv * e_f
        ks_f = k_f / e_f
        qs_b = qv * e_b
        ks_b = k_b / e_b
        qs_fb, ks_fb = qs_f.astype(BF16), ks_f.astype(BF16)
        qs_bb, ks_bb = qs_b.astype(BF16), ks_b.astype(BF16)
        for g in range(HG_G):
            sl = slice(g * CHUNK, (g + 1) * CHUNK)
            r1 = slice(g * CHUNK, g * CHUNK + 1)
            n = j * HG_G + g
            rows_g = pl.ds(pl.multiple_of(r0 + g * CHUNK, CHUNK), CHUNK)
            sc_f = lax.dot_general(qs_fb[sl], ks_fb[sl], NT_DIMS, preferred_element_type=F32)
            sc_b = lax.dot_general(qs_bb[sl], ks_bb[sl], NT_DIMS, preferred_element_type=F32)
            sc_s[n] = (jnp.where(lower, sc_f, 0.0) + jnp.where(upper, sc_b, 0.0)).astype(BF16)
            em_f = jnp.exp(bmid_f[r1])
            el_f = jnp.exp(blast_f[r1] - bmid_f[r1])
            em_b = jnp.exp(cmid_b[r1])
            el_b = jnp.exp(tot_b[r1] - cmid_b[r1])
            qe_s[rows_g, :] = jnp.concatenate(
                [qs_f[sl] * em_f, qs_b[sl] * em_b], axis=1).astype(BF16)
            kl = jnp.concatenate([ks_f[sl] * el_f, ks_b[sl] * el_b], axis=1).astype(BF16)
            ut_s[n] = lax.dot_general(vb[sl], kl, TN_DIMS, preferred_element_type=F32)
            dec_s[n] = jnp.concatenate([jnp.exp(blast_f[r1]), jnp.exp(tot_b[r1])], axis=1)
        return carry

    lax.fori_loop(0, N_CHUNKS // HG_G, phase_a, 0, unroll=True)

    st_f = st_meta
    st_b = jnp.zeros((hd, hd), F32)
    for n in range(N_CHUNKS):
        st_s[n, :, 0:hd] = st_f.astype(BF16)
        st_f = dec_s[n, :, 0:hd] * st_f + ut_s[n, :, 0:hd]
        m = N_CHUNKS - 1 - n
        st_s[m, :, hd:2 * hd] = st_b.astype(BF16)
        st_b = dec_s[m, :, hd:2 * hd] * st_b + ut_s[m, :, hd:2 * hd]

    nw = nw_ref[...]

    def phase_c(j, carry):
        r0 = pl.multiple_of(j * HG_ROWS, HG_ROWS)
        rows = pl.ds(r0, HG_ROWS)
        vb = i_ref[0, rows, :].astype(BF16)
        outs = []
        for g in range(HG_G):
            sl = slice(g * CHUNK, (g + 1) * CHUNK)
            n = j * HG_G + g
            rows_g = pl.ds(pl.multiple_of(r0 + g * CHUNK, CHUNK), CHUNK)
            o = jnp.dot(sc_s[n], vb[sl], preferred_element_type=F32)
            outs.append(o + lax.dot_general(qe_s[rows_g, :], st_s[n], NT_DIMS,
                                            preferred_element_type=F32))
        o = jnp.concatenate(outs, axis=0)
        o = o * lax.rsqrt(jnp.mean(o * o, axis=-1, keepdims=True) + EPS)
        o_ref[0, rows, :] = o * nw * jax.nn.silu(g_ref[0, rows, :])
        return carry

    lax.fori_loop(0, N_CHUNKS // HG_G, phase_c, 0, unroll=True)


def _hgrn(phg_x, phg_m, lb_f, lb_b, norm_w):
    bsz = phg_x.shape[0]
    hd = HGRN_HEAD_DIM
    nh = HGRN_HEADS

    def part(j):
        return pl.BlockSpec((1, SEQ, hd), lambda b, h, j=j: (b, 0, j * nh + h))

    def mpart(j):
        return pl.BlockSpec((N_META, hd), lambda b, h, j=j: (0, j * nh + h))

    vec = pl.BlockSpec((1, hd), lambda b, h: (0, h))
    return pl.pallas_call(
        _hgrn_body,
        grid=(bsz, nh),
        in_specs=[part(0), part(1), part(2), part(3), part(4), mpart(1), mpart(3), vec, vec, vec],
        out_specs=pl.BlockSpec((1, SEQ, hd), lambda b, h: (b, 0, h)),
        out_shape=jax.ShapeDtypeStruct((bsz, SEQ, D_HGRN), F32),
        scratch_shapes=[pltpu.VMEM((HG_ROWS, HG_ROWS), BF16),
                        pltpu.VMEM((SEQ, 2 * hd), BF16),
                        pltpu.VMEM((N_CHUNKS, CHUNK, CHUNK), BF16),
                        pltpu.VMEM((N_CHUNKS, hd, 2 * hd), F32),
                        pltpu.VMEM((N_CHUNKS, 1, 2 * hd), F32),
                        pltpu.VMEM((N_CHUNKS, hd, 2 * hd), BF16)],
        compiler_params=pltpu.CompilerParams(
            dimension_semantics=("arbitrary", "arbitrary"), vmem_limit_bytes=40 * 1024 * 1024),
        name="hgrn",
    )(phg_x, phg_x, phg_x, phg_x, phg_x, phg_m, phg_m,
      lb_f.reshape(1, -1), lb_b.reshape(1, -1), norm_w.reshape(1, -1))


def _hyena_filters(L, w1, b1, w2, b2, w3, freq):
    pos = jnp.arange(L, dtype=F32)
    t = pos / max(L - 1, 1)
    bands = jnp.linspace(1e-4, FILTER_BANDS - 1, FILTER_BANDS, dtype=F32)
    ang = (2.0 * math.pi / L) * pos[:, None] * bands[None, :]
    z = jnp.concatenate([t[:, None], jnp.cos(ang), -jnp.sin(ang)], axis=-1)
    hp = lax.Precision.HIGHEST
    hid = jnp.sin(freq * (jnp.dot(z, w1, precision=hp) + b1))
    hid = jnp.sin(freq * (jnp.dot(hid, w2, precision=hp) + b2))
    filt = jnp.dot(hid, w3, precision=hp).reshape(L, 2, HYENA_ORDER, D_HYENA)
    deltas = jnp.abs(jnp.linspace(math.log(DECAY_TARGET) / SLOW_DECAY_PCT,
                                  math.log(DECAY_TARGET) / FAST_DECAY_PCT, D_HYENA, dtype=F32))
    window = jnp.exp(-t[:, None] * deltas[None, :])
    filt = filt * window[:, None, None, :]
    return filt[:, 0], filt[:, 1]


def kernel(x, meta_tokens, w_in, conv_w, conv_b, filt_w1, filt_b1, filt_w2, filt_b2, filt_w3,
           filt_freq, filt_skip, hyena_norm, lb_fwd, lb_bwd, hgrn_norm, w_out, norm_mix, norm_ffn,
           w_router_group, w_router_expert, w_gate, w_up, w_down, norm_final):
    B, S, D = x.shape
    L = S + N_META
    lbf = jnp.cumsum(jax.nn.softmax(lb_fwd, axis=0), axis=0)[0]
    lbb = jnp.cumsum(jax.nn.softmax(lb_bwd, axis=0), axis=0)[0]

    w_in_b = w_in[0].astype(BF16)
    xf = x.reshape(B * S, D)
    phy_x, phg_x = _inproj(xf, norm_mix[0], w_in_b, tm=512)
    phy_m, phg_m = _inproj(meta_tokens, norm_mix[0], w_in_b, tm=N_META)

    mats = _dft_matrices()
    h_fwd, h_bwd = _hyena_filters(L, filt_w1[0], filt_b1[0], filt_w2[0], filt_b2[0], filt_w3[0],
                                  filt_freq[0])
    hr, hi = _spectra(h_fwd, h_bwd, mats[0], mats[1])
    z_hy = _hyena(phy_x.reshape(B, S, D_HYENA_PROJ), phy_m, conv_w[0], conv_b[0], filt_skip[0],
                  mats, hr, hi).reshape(B * S, D_HYENA)

    y_hg = _hgrn(phg_x.reshape(B, S, 5 * D_HGRN), phg_m, lbf, lbb,
                 hgrn_norm[0]).reshape(B * S, D_HGRN)

    w_r = jnp.concatenate([w_router_group[0], w_router_expert[0].reshape(D, N_EXPERTS),
                           jnp.zeros((D, LANES - N_GROUPS - N_EXPERTS), F32)], axis=1).astype(BF16)
    h1, a2p, ri, rg, cnt = _outproj(z_hy, y_hg, xf, hyena_norm[0], norm_ffn[0],
                                    w_out[0].astype(BF16), w_r, tm=1024)
    out = _moe(h1, a2p, ri, rg, cnt, norm_final, w_gate[0], w_up[0], w_down[0])
    return out.reshape(B, S, D)
```

```python
import functools
import math

import jax
import jax.numpy as jnp
from jax import lax
from jax.experimental import pallas as pl
from jax.experimental.pallas import tpu as pltpu

D_MODEL = 1024
N_META = 16
D_HYENA = 512
D_HGRN = 512
HYENA_ORDER = 2
SHORT_CONV = 3
FILTER_EMB = 33
FILTER_BANDS = 16
DECAY_TARGET = 1e-2
FAST_DECAY_PCT = 0.3
SLOW_DECAY_PCT = 1.5
HGRN_HEAD_DIM = 128
HGRN_HEADS = D_HGRN // HGRN_HEAD_DIM
CHUNK = 64
N_GROUPS = 8
EXPERTS_PER_GROUP = 8
N_EXPERTS = 64
TOP_K = 2
D_EXPERT = 512
D_HYENA_PROJ = 3 * D_HYENA
D_IN_PROJ = D_HYENA_PROJ + 5 * D_HGRN
EPS = 1e-6

F32 = jnp.float32
BF16 = jnp.bfloat16


def _rms(x, gain):
    return x * lax.rsqrt(jnp.mean(x * x, axis=-1, keepdims=True) + EPS) * gain


HI16 = 0xFFFF0000


def _pack_bf16_pairs(x):
    c = x.shape[1] // 2
    bits = lax.bitcast_convert_type(x.astype(BF16).astype(F32), jnp.uint32)
    return (bits[:, :c] >> 16) | (bits[:, c:] & jnp.uint32(HI16))


def _unpack_bf16_pairs(w):
    lo = lax.bitcast_convert_type(w << 16, F32)
    hi = lax.bitcast_convert_type(w & jnp.uint32(HI16), F32)
    return lo, hi


def _inproj_body(x_ref, g_ref, w_ref, hy_ref, hg_ref, *, tn):
    a = _rms(x_ref[...], g_ref[...]).astype(BF16)
    for j in range(D_IN_PROJ // tn):
        acc = jnp.dot(a, w_ref[:, j * tn:(j + 1) * tn], preferred_element_type=F32)
        if j * tn < D_HYENA_PROJ:
            hy_ref[:, j * tn:(j + 1) * tn] = acc.astype(hy_ref.dtype)
        else:
            c0 = j * tn - D_HYENA_PROJ
            hg_ref[:, c0:c0 + tn] = acc.astype(hg_ref.dtype)


def _inproj(x, gain, w_bf16, tm):
    m, d = x.shape
    n_hg = D_IN_PROJ - D_HYENA_PROJ
    return pl.pallas_call(
        functools.partial(_inproj_body, tn=512),
        grid=(m // tm,),
        in_specs=[pl.BlockSpec((tm, d), lambda i: (i, 0)),
                  pl.BlockSpec((1, d), lambda i: (0, 0)),
                  pl.BlockSpec((d, D_IN_PROJ), lambda i: (0, 0))],
        out_specs=[pl.BlockSpec((tm, D_HYENA_PROJ), lambda i: (i, 0)),
                   pl.BlockSpec((tm, n_hg), lambda i: (i, 0))],
        out_shape=[jax.ShapeDtypeStruct((m, D_HYENA_PROJ), BF16),
                   jax.ShapeDtypeStruct((m, n_hg), F32)],
        compiler_params=pltpu.CompilerParams(
            dimension_semantics=("arbitrary",), vmem_limit_bytes=48 * 1024 * 1024),
        name="inproj",
    )(x, gain.reshape(1, d), w_bf16)


LANES = 128
NEG_BIG = -1e30
OP_SUB = 512
ROW_SUB = (D_MODEL // 2) // LANES


def _outproj_body(zhy_ref, yhg_ref, h0_ref, ghy_ref, gffn_ref, wo_ref, wr_ref,
                  h1_ref, a2p_ref, ri_ref, rg_ref, cnt_ref, tri_s, carry_s):
    i = pl.program_id(0)
    tm = h1_ref.shape[0]
    ts = tri_s.shape[0]

    @pl.when(i == 0)
    def _():
        r = lax.broadcasted_iota(jnp.int32, (ts, ts), 0)
        c = lax.broadcasted_iota(jnp.int32, (ts, ts), 1)
        tri_s[...] = (r > c).astype(BF16)
        carry_s[...] = jnp.zeros_like(carry_s)

    lane = lax.broadcasted_iota(jnp.int32, (ts, LANES), 1)
    is_g = lane < N_GROUPS
    carry = carry_s[...]
    for r0 in range(0, tm, ts):
        rows = slice(r0, r0 + ts)
        yhy = _rms(zhy_ref[rows, :], ghy_ref[...]).astype(BF16)
        yhg = yhg_ref[rows, :].astype(BF16)
        acc = jnp.dot(yhy, wo_ref[:D_HYENA, :], preferred_element_type=F32)
        acc = acc + jnp.dot(yhg, wo_ref[D_HYENA:, :], preferred_element_type=F32)
        h1 = h0_ref[rows, :] + acc
        h1_ref[rows, :] = h1
        a2 = _rms(h1, gffn_ref[...])
        a2p_ref[rows] = _pack_bf16_pairs(a2).reshape(ts, ROW_SUB, LANES)
        lg = jnp.dot(a2.astype(BF16), wr_ref[...], preferred_element_type=F32)

        gl = jnp.where(is_g, lg, NEG_BIG)
        gmax = jnp.max(gl, axis=1, keepdims=True)
        gsel = jnp.min(jnp.where(gl == gmax, lane, LANES), axis=1, keepdims=True)
        gden = jnp.sum(jnp.where(is_g, jnp.exp(gl - gmax), 0.0), axis=1, keepdims=True)
        p_group = 1.0 / gden
        in_grp = (lane >= N_GROUPS) & (lane < N_GROUPS + N_EXPERTS) & (
            ((lane - N_GROUPS) >> 3) == gsel)
        el = jnp.where(in_grp, lg, NEG_BIG)
        m1 = jnp.max(el, axis=1, keepdims=True)
        i1 = jnp.min(jnp.where(el == m1, lane, LANES), axis=1, keepdims=True)
        el2 = jnp.where(lane == i1, NEG_BIG, el)
        m2 = jnp.max(el2, axis=1, keepdims=True)
        i2 = jnp.min(jnp.where(el2 == m2, lane, LANES), axis=1, keepdims=True)
        r21 = jnp.exp(m2 - m1)
        gate1 = p_group / (1.0 + r21)
        gate2 = gate1 * r21

        hit1 = lane == i1
        hit2 = lane == i2
        onehot = (hit1 | hit2).astype(BF16)
        pre = jnp.dot(tri_s[...], onehot, preferred_element_type=F32) + carry
        pos1 = jnp.sum(jnp.where(hit1, pre, 0.0), axis=1, keepdims=True).astype(jnp.int32)
        pos2 = jnp.sum(jnp.where(hit2, pre, 0.0), axis=1, keepdims=True).astype(jnp.int32)
        carry = carry + jnp.sum(onehot.astype(F32), axis=0, keepdims=True)

        zero_i = jnp.zeros((ts, LANES), jnp.int32)
        ri_ref[rows, :] = jnp.where(lane == 0, i1 - N_GROUPS,
                          jnp.where(lane == 1, i2 - N_GROUPS,
                          jnp.where(lane == 2, pos1, jnp.where(lane == 3, pos2, zero_i))))
        rg_ref[rows, :] = jnp.where(lane == 0, gate1, jnp.where(lane == 1, gate2, 0.0))
    carry_s[...] = carry
    cnt_ref[...] = carry


def _outproj(zhy, yhg, h0, g_hy, g_ffn, wo_bf16, wr_bf16, tm):
    m = h0.shape[0]
    return pl.pallas_call(
        _outproj_body,
        grid=(m // tm,),
        in_specs=[pl.BlockSpec((tm, D_HYENA), lambda i: (i, 0)),
                  pl.BlockSpec((tm, D_HGRN), lambda i: (i, 0)),
                  pl.BlockSpec((tm, D_MODEL), lambda i: (i, 0)),
                  pl.BlockSpec((1, D_HYENA), lambda i: (0, 0)),
                  pl.BlockSpec((1, D_MODEL), lambda i: (0, 0)),
                  pl.BlockSpec((D_MODEL, D_MODEL), lambda i: (0, 0)),
                  pl.BlockSpec((D_MODEL, LANES), lambda i: (0, 0))],
        out_specs=[pl.BlockSpec((tm, D_MODEL), lambda i: (i, 0)),
                   pl.BlockSpec((tm, ROW_SUB, LANES), lambda i: (i, 0, 0)),
                   pl.BlockSpec((tm, LANES), lambda i: (i, 0)),
                   pl.BlockSpec((tm, LANES), lambda i: (i, 0)),
                   pl.BlockSpec((1, LANES), lambda i: (0, 0))],
        out_shape=[jax.ShapeDtypeStruct((m, D_MODEL), F32),
                   jax.ShapeDtypeStruct((m, ROW_SUB, LANES), jnp.uint32),
                   jax.ShapeDtypeStruct((m, LANES), jnp.int32),
                   jax.ShapeDtypeStruct((m, LANES), F32),
                   jax.ShapeDtypeStruct((1, LANES), F32)],
        scratch_shapes=[pltpu.VMEM((OP_SUB, OP_SUB), BF16), pltpu.VMEM((1, LANES), F32)],
        compiler_params=pltpu.CompilerParams(
            dimension_semantics=("arbitrary",), vmem_limit_bytes=48 * 1024 * 1024),
        name="outproj",
    )(zhy, yhg, h0, g_hy.reshape(1, -1), g_ffn.reshape(1, -1), wo_bf16, wr_bf16)


EXP_TB = 512


def _wait_rows(ref, n_rows, sem):
    pltpu.make_async_copy(ref.at[pl.ds(0, n_rows)], ref.at[pl.ds(n_rows, n_rows)], sem).wait()


def _dispatch_body(lastblk_ref, npad_ref, nused_ref, d0_ref, d1_ref, h1_ref, xb_ref, zero_s, sem_z,
                   sem):
    i = pl.program_id(0)
    tm = d0_ref.shape[0]
    n_blocks = xb_ref.shape[0] // EXP_TB

    @pl.when(i == 0)
    def _():
        zero_s[...] = jnp.zeros_like(zero_s)

        def zero_copy(row0):
            row0 = pl.multiple_of(row0, EXP_TB)
            return pltpu.make_async_copy(zero_s, xb_ref.at[pl.ds(row0, EXP_TB)], sem_z)

        for e in range(N_EXPERTS):
            @pl.when(npad_ref[e] > 0)
            def _():
                zero_copy(lastblk_ref[e]).start()
        for e in range(N_EXPERTS):
            @pl.when(npad_ref[e] > 0)
            def _():
                zero_copy(lastblk_ref[e]).wait()

        def start_tail(b, carry):
            zero_copy(b * EXP_TB).start()
            return carry

        def wait_tail(b, carry):
            zero_copy(b * EXP_TB).wait()
            return carry

        lax.fori_loop(nused_ref[0], n_blocks, start_tail, 0)
        lax.fori_loop(nused_ref[0], n_blocks, wait_tail, 0)

    def issue(t, carry):
        pltpu.make_async_copy(h1_ref.at[t], xb_ref.at[d0_ref[t]], sem).start(priority=0)
        pltpu.make_async_copy(h1_ref.at[t], xb_ref.at[d1_ref[t]], sem).start(priority=1)
        return carry

    lax.fori_loop(0, tm, issue, 0, unroll=16)
    _wait_rows(xb_ref, TOP_K * tm, sem)


def _dispatch(lastblk, npad, n_used, dest, rows, n_slots, tm):
    m = rows.shape[0]
    grid_spec = pltpu.PrefetchScalarGridSpec(
        num_scalar_prefetch=3,
        grid=(m // tm,),
        in_specs=[pl.BlockSpec((tm,), lambda i, lb, npd, nu: (i,), memory_space=pltpu.SMEM),
                  pl.BlockSpec((tm,), lambda i, lb, npd, nu: (i,), memory_space=pltpu.SMEM),
                  pl.BlockSpec((tm, ROW_SUB, LANES), lambda i, lb, npd, nu: (i, 0, 0))],
        out_specs=pl.BlockSpec(memory_space=pl.ANY),
        scratch_shapes=[pltpu.VMEM((EXP_TB, ROW_SUB, LANES), rows.dtype),
                        pltpu.SemaphoreType.DMA(()), pltpu.SemaphoreType.DMA(())],
    )
    return pl.pallas_call(
        _dispatch_body,
        grid_spec=grid_spec,
        out_shape=jax.ShapeDtypeStruct((n_slots, ROW_SUB, LANES), rows.dtype),
        compiler_params=pltpu.CompilerParams(dimension_semantics=("arbitrary",)),
        name="dispatch",
    )(lastblk, npad, n_used, dest[0], dest[1], rows)


def _expert_body(eid_ref, nused_ref, xb_ref, wg_ref, wu_ref, wd_ref, o_ref, wg_s, wu_s, wd_s):
    i = pl.program_id(0)
    half = D_MODEL // 2

    @pl.when(i < nused_ref[0])
    def _():
        prev = eid_ref[jnp.maximum(i - 1, 0)]

        @pl.when((i == 0) | (eid_ref[i] != prev))
        def _():
            wg_s[...] = wg_ref[0].astype(BF16)
            wu_s[...] = wu_ref[0].astype(BF16)
            wd_s[...] = wd_ref[0].astype(BF16)

        lo, hi = _unpack_bf16_pairs(xb_ref[...].reshape(EXP_TB, half))
        lo, hi = lo.astype(BF16), hi.astype(BF16)

        def proj(w_s):
            return (jnp.dot(lo, w_s[:half, :], preferred_element_type=F32)
                    + jnp.dot(hi, w_s[half:, :], preferred_element_type=F32))

        g = proj(wg_s)
        u = proj(wu_s)
        hmid = (g * jax.nn.sigmoid(g) * u).astype(BF16)
        y = _pack_bf16_pairs(jnp.dot(hmid, wd_s[...], preferred_element_type=F32))
        o_ref[...] = y.reshape(EXP_TB, ROW_SUB, LANES)

    @pl.when(i >= nused_ref[0])
    def _():
        o_ref[...] = jnp.zeros_like(o_ref)


def _experts(block_eid, n_used, xb, w_gate, w_up, w_down):
    n_slots = xb.shape[0]

    def blk(i, e, nu):
        return jnp.minimum(i, nu[0] - 1)

    grid_spec = pltpu.PrefetchScalarGridSpec(
        num_scalar_prefetch=2,
        grid=(n_slots // EXP_TB,),
        in_specs=[pl.BlockSpec((EXP_TB, ROW_SUB, LANES), lambda i, e, nu: (blk(i, e, nu), 0, 0)),
                  pl.BlockSpec((1, D_MODEL, D_EXPERT), lambda i, e, nu: (e[blk(i, e, nu)], 0, 0)),
                  pl.BlockSpec((1, D_MODEL, D_EXPERT), lambda i, e, nu: (e[blk(i, e, nu)], 0, 0)),
                  pl.BlockSpec((1, D_EXPERT, D_MODEL), lambda i, e, nu: (e[blk(i, e, nu)], 0, 0))],
        out_specs=pl.BlockSpec((EXP_TB, ROW_SUB, LANES), lambda i, e, nu: (i, 0, 0)),
        scratch_shapes=[pltpu.VMEM((D_MODEL, D_EXPERT), BF16),
                        pltpu.VMEM((D_MODEL, D_EXPERT), BF16),
                        pltpu.VMEM((D_EXPERT, D_MODEL), BF16)],
    )
    return pl.pallas_call(
        _expert_body,
        grid_spec=grid_spec,
        out_shape=jax.ShapeDtypeStruct((n_slots, ROW_SUB, LANES), jnp.uint32),
        compiler_params=pltpu.CompilerParams(
            dimension_semantics=("arbitrary",), vmem_limit_bytes=48 * 1024 * 1024),
        name="experts",
    )(block_eid, n_used, xb, w_gate, w_up, w_down)


def _combine_body(d0_ref, d1_ref, n0_ref, n1_ref, h1_ref, rg_ref, gfin_ref, yb_ref, o_ref, y_s, sem):
    i = pl.program_id(0)
    tm = h1_ref.shape[0]
    slot = i % 2

    def issue(idx_refs, sl):
        def body(t, carry):
            for k in range(TOP_K):
                pltpu.make_async_copy(yb_ref.at[idx_refs[k][t]], y_s.at[sl, k, t],
                                      sem.at[sl]).start(priority=k)
            return carry

        lax.fori_loop(0, tm, body, 0, unroll=16)

    @pl.when(i == 0)
    def _():
        issue((d0_ref, d1_ref), 0)

    @pl.when(i + 1 < pl.num_programs(0))
    def _():
        issue((n0_ref, n1_ref), 1 - slot)

    _wait_rows(yb_ref, TOP_K * tm, sem.at[slot])
    rg = rg_ref[...]
    g1, g2 = rg[:, 0:1], rg[:, 1:2]
    half = D_MODEL // 2
    lo1, hi1 = _unpack_bf16_pairs(y_s[slot, 0].reshape(tm, half))
    lo2, hi2 = _unpack_bf16_pairs(y_s[slot, 1].reshape(tm, half))
    h2_lo = h1_ref[:, :half] + g1 * lo1 + g2 * lo2
    h2_hi = h1_ref[:, half:] + g1 * hi1 + g2 * hi2
    ms = (jnp.sum(h2_lo * h2_lo, axis=-1, keepdims=True)
          + jnp.sum(h2_hi * h2_hi, axis=-1, keepdims=True)) * (1.0 / D_MODEL)
    inv = lax.rsqrt(ms + EPS)
    o_ref[:, :half] = h2_lo * inv * gfin_ref[:, :half]
    o_ref[:, half:] = h2_hi * inv * gfin_ref[:, half:]


def _combine(dest, h1, rg, g_fin, yb, tm):
    m = h1.shape[0]
    last = m // tm - 1
    return pl.pallas_call(
        _combine_body,
        grid=(m // tm,),
        in_specs=[pl.BlockSpec((tm,), lambda i: (i,), memory_space=pltpu.SMEM),
                  pl.BlockSpec((tm,), lambda i: (i,), memory_space=pltpu.SMEM),
                  pl.BlockSpec((tm,), lambda i: (jnp.minimum(i + 1, last),), memory_space=pltpu.SMEM),
                  pl.BlockSpec((tm,), lambda i: (jnp.minimum(i + 1, last),), memory_space=pltpu.SMEM),
                  pl.BlockSpec((tm, D_MODEL), lambda i: (i, 0)),
                  pl.BlockSpec((tm, LANES), lambda i: (i, 0)),
                  pl.BlockSpec((1, D_MODEL), lambda i: (0, 0)),
                  pl.BlockSpec(memory_space=pl.ANY)],
        out_specs=pl.BlockSpec((tm, D_MODEL), lambda i: (i, 0)),
        out_shape=jax.ShapeDtypeStruct((m, D_MODEL), F32),
        scratch_shapes=[pltpu.VMEM((2, TOP_K, tm, ROW_SUB, LANES), yb.dtype),
                        pltpu.SemaphoreType.DMA((2,))],
        compiler_params=pltpu.CompilerParams(dimension_semantics=("arbitrary",)),
        name="combine",
    )(dest[0], dest[1], dest[0], dest[1], h1, rg, g_fin.reshape(1, -1), yb)


def _moe(h1, a2p, ri, rg, cnt, g_fin, w_gate, w_up, w_down):
    m = h1.shape[0]
    n_blocks = TOP_K * m // EXP_TB + N_EXPERTS
    n_slots = n_blocks * EXP_TB
    counts = cnt[0, N_GROUPS:N_GROUPS + N_EXPERTS].astype(jnp.int32)
    padded = (counts + EXP_TB - 1) // EXP_TB * EXP_TB
    pend = jnp.cumsum(padded)
    base = pend - padded
    rt = jnp.transpose(ri)[:2 * TOP_K]
    sel = rt[None, :TOP_K] == jnp.arange(N_EXPERTS, dtype=jnp.int32)[:, None, None]
    dest = jnp.sum(jnp.where(sel, base[:, None, None], 0), axis=0) + rt[TOP_K:]
    blk_start = jnp.arange(n_blocks, dtype=jnp.int32) * EXP_TB
    block_eid = jnp.minimum(jnp.sum(blk_start[:, None] >= pend[None, :], axis=1),
                            N_EXPERTS - 1).astype(jnp.int32)
    n_used = (pend[-1:] // EXP_TB).astype(jnp.int32)
    lastblk = (pend - EXP_TB).astype(jnp.int32)

    xb = _dispatch(lastblk, padded.astype(jnp.int32), n_used, dest, a2p, n_slots, tm=512)
    yb = _experts(block_eid, n_used, xb, w_gate, w_up, w_down)
    return _combine(dest, h1, rg, g_fin, yb, tm=256)


ND = 2176
NCIRC = 2 * ND
NH = ND // 2
SEQ = 2048
L_TOT = SEQ + N_META
LH = L_TOT // 2
LHP = 1152
LHR = 1040
HY_CT = 256
HY_FB = NH // 2
HY_IB = ((0, 528), (528, 512))
HY_CB = LH // 3
HALO = 8


def _dft_matrices():
    a = 2 * jnp.arange(NH, dtype=jnp.int32) + 1
    t_hi = jnp.arange(LHP // LANES, dtype=jnp.int32) * LANES
    t_lo = jnp.arange(LANES, dtype=jnp.int32)
    valid = ((t_hi[:, None] + t_lo[None, :]) < LH).reshape(1, LHP)

    def cos_sin(m, denom):
        ang = m.astype(F32) * (math.pi / denom)
        return jnp.cos(ang), jnp.sin(ang)

    c_hi, s_hi = cos_sin((a[:, None] * ((4 * t_hi) % (4 * NCIRC))[None, :]) % (4 * NCIRC), 2 * NCIRC)
    c_hi, s_hi = c_hi[:, :, None], s_hi[:, :, None]
    out = []
    for c in (1, 3):
        c_lo, s_lo = cos_sin((a[:, None] * (4 * t_lo + c)[None, :]) % (4 * NCIRC), 2 * NCIRC)
        c_lo, s_lo = c_lo[:, None, :], s_lo[:, None, :]
        cos_phi = jnp.where(valid, (c_hi * c_lo - s_hi * s_lo).reshape(NH, LHP), 0.0)
        sin_phi = jnp.where(valid, (s_hi * c_lo + c_hi * s_lo).reshape(NH, LHP), 0.0)
        out.append(jnp.concatenate([cos_phi, sin_phi if c == 1 else -sin_phi], axis=0).astype(BF16))
    ce, co = out
    return ce, co, ce.T, co.T


def _forward_half_spectrum(ce_ref, co_ref, xe, xo, j0):
    a1 = jnp.dot(ce_ref[j0:j0 + HY_FB, :], xe, preferred_element_type=F32)
    a2 = jnp.dot(ce_ref[NH + j0:NH + j0 + HY_FB, :], xe, preferred_element_type=F32)
    b1 = jnp.dot(co_ref[j0:j0 + HY_FB, :], xo, preferred_element_type=F32)
    b2 = jnp.dot(co_ref[NH + j0:NH + j0 + HY_FB, :], xo, preferred_element_type=F32)
    return a1 + b1, a2 - b2, a2 + b2, a1 - b1


def _spectra_body(s_ref, d_ref, ce_ref, co_ref, rot_ref, hr_ref, hi_ref,
                  stage_s, se_s, so_s, de_s, do_s):
    ct = hr_ref.shape[-1]
    nl = ct // LANES
    for x_ref, xe_s, xo_s in ((s_ref, se_s, so_s), (d_ref, de_s, do_s)):
        for j in range(nl):
            stage_s[j] = x_ref[:, j * LANES:(j + 1) * LANES]
        for par, dst in ((0, xe_s), (1, xo_s)):
            for r0 in range(0, LH, HY_CB):
                dst[r0:r0 + HY_CB, :] = jnp.concatenate(
                    [stage_s.at[j][pl.ds(par + 2 * r0, HY_CB, stride=2), :] for j in range(nl)],
                    axis=1).astype(BF16)
            dst[LH:LHP, :] = jnp.zeros((LHP - LH, ct), BF16)
    for j0 in range(0, NH, HY_FB):
        ar1, _, ar2, _ = _forward_half_spectrum(ce_ref, co_ref, se_s[...], so_s[...], j0)
        _, q1, _, q2 = _forward_half_spectrum(ce_ref, co_ref, de_s[...], do_s[...], j0)
        for r0, ar, q in ((j0, ar1, q1), (NH + j0, ar2, q2)):
            c = rot_ref[r0:r0 + HY_FB, 0:1]
            s = rot_ref[r0:r0 + HY_FB, 1:2]
            hr_ref[0, r0:r0 + HY_FB, :] = c * ar + s * q
            hi_ref[0, r0:r0 + HY_FB, :] = s * ar - c * q


def _spectra(h_fwd, h_bwd, ce, co):
    hbs = jnp.concatenate([h_bwd[1:], jnp.zeros_like(h_bwd[:1])], axis=0)
    s2d = (h_fwd + hbs).reshape(L_TOT, HYENA_ORDER * D_HYENA)
    d2d = (h_fwd - hbs).reshape(L_TOT, HYENA_ORDER * D_HYENA)
    j = jnp.arange(NH, dtype=F32)
    k = jnp.concatenate([j, ND - 1 - j])
    half = (2 * k + 1) * (math.pi / (2 * NCIRC))
    scale = 2.0 / NCIRC
    rot = jnp.stack([jnp.cos(half) * scale, jnp.sin(half) * scale], axis=1)
    rot = jnp.pad(rot, ((0, 0), (0, LANES - 2)))
    n_ord, ct = HYENA_ORDER, HY_CT
    nct = D_HYENA // ct
    col = pl.BlockSpec((L_TOT, ct), lambda o, c: (0, o * nct + c))
    mat = pl.BlockSpec((ND, LHP), lambda o, c: (0, 0))
    out = pl.BlockSpec((1, ND, ct), lambda o, c: (o, 0, c))
    half_bf16 = pltpu.VMEM((LHP, ct), BF16)
    return pl.pallas_call(
        _spectra_body,
        grid=(n_ord, nct),
        in_specs=[col, col, mat, mat, pl.BlockSpec((ND, LANES), lambda o, c: (0, 0))],
        out_specs=[out, out],
        out_shape=[jax.ShapeDtypeStruct((n_ord, ND, D_HYENA), F32)] * 2,
        scratch_shapes=[pltpu.VMEM((ct // LANES, L_TOT, LANES), F32),
                        half_bf16, half_bf16, half_bf16, half_bf16],
        compiler_params=pltpu.CompilerParams(
            dimension_semantics=("arbitrary", "arbitrary"), vmem_limit_bytes=48 * 1024 * 1024),
        name="spectra",
    )(s2d, d2d, ce, co, rot)


def _hyena_body(pv_ref, px1_ref, px2_ref, mv_ref, mx1_ref, mx2_ref,
                wv_ref, wx1_ref, wx2_ref, bv_ref, bx1_ref, bx2_ref, skip_ref,
                ce_ref, co_ref, cet_ref, cot_ref, hr_ref, hi_ref, o_ref,
                stage_s, ze_s, zo_s, g1e_s, g1o_s, g2e_s, g2o_s, zbe_s, zbo_s, we_s, wo_s):
    ct = o_ref.shape[-1]
    nl = ct // LANES

    def strided(off, r0, rows):
        return jnp.concatenate(
            [stage_s.at[j][pl.ds(HALO + off + 2 * r0, rows, stride=2), :] for j in range(nl)], axis=1)

    def short_conv(p_ref, m_ref, w_ref, b_ref, dst_e, dst_o):
        for j in range(nl):
            ln = slice(j * LANES, (j + 1) * LANES)
            stage_s[j, 0:HALO, :] = jnp.zeros((HALO, LANES), F32)
            stage_s[j, HALO:HALO + N_META, :] = m_ref[:, ln].astype(F32)
            stage_s[j, HALO + N_META:HALO + L_TOT, :] = p_ref[0, :, ln].astype(F32)
            stage_s[j, HALO + L_TOT:HALO + L_TOT + HALO, :] = jnp.zeros((HALO, LANES), F32)
        w = w_ref[...]
        b = b_ref[...]
        for r0 in range(0, LH, HY_CB):
            sm1, s0, s1, s2 = (strided(off, r0, HY_CB) for off in (-1, 0, 1, 2))
            dst_e[r0:r0 + HY_CB, :] = b + sm1 * w[0:1] + s0 * w[1:2] + s1 * w[2:3]
            dst_o[r0:r0 + HY_CB, :] = b + s0 * w[0:1] + s1 * w[1:2] + s2 * w[2:3]
        dst_e[LH:LHR, :] = jnp.zeros((LHR - LH, ct), F32)
        dst_o[LH:LHR, :] = jnp.zeros((LHR - LH, ct), F32)

    short_conv(pv_ref, mv_ref, wv_ref, bv_ref, ze_s, zo_s)
    short_conv(px1_ref, mx1_ref, wx1_ref, bx1_ref, g1e_s, g1o_s)
    short_conv(px2_ref, mx2_ref, wx2_ref, bx2_ref, g2e_s, g2o_s)
    zbe_s[LH:LHP, :] = jnp.zeros((LHP - LH, ct), BF16)
    zbo_s[LH:LHP, :] = jnp.zeros((LHP - LH, ct), BF16)

    for n, (ge_s, go_s) in enumerate(((g1e_s, g1o_s), (g2e_s, g2o_s))):
        zbe_s[0:LH, :] = ze_s[0:LH, :].astype(BF16)
        zbo_s[0:LH, :] = zo_s[0:LH, :].astype(BF16)
        for j0 in range(0, NH, HY_FB):
            p1, q1, p2, q2 = _forward_half_spectrum(ce_ref, co_ref, zbe_s[...], zbo_s[...], j0)
            h1r = hr_ref[n, j0:j0 + HY_FB, :]
            h1i = hi_ref[n, j0:j0 + HY_FB, :]
            h2r = hr_ref[n, NH + j0:NH + j0 + HY_FB, :]
            h2i = hi_ref[n, NH + j0:NH + j0 + HY_FB, :]
            yr1 = p1 * h1r + q1 * h1i
            ny1 = q1 * h1r - p1 * h1i
            yr2 = p2 * h2r + q2 * h2i
            ny2 = q2 * h2r - p2 * h2i
            we_s[j0:j0 + HY_FB, :] = (yr1 + ny2).astype(BF16)
            we_s[NH + j0:NH + j0 + HY_FB, :] = (yr2 + ny1).astype(BF16)
            wo_s[j0:j0 + HY_FB, :] = (yr1 - ny2).astype(BF16)
            wo_s[NH + j0:NH + j0 + HY_FB, :] = (yr2 - ny1).astype(BF16)
        skip = skip_ref[n:n + 1, :]
        for parity, (ct_ref, w_s, z_s, g_s) in enumerate(((cet_ref, we_s, ze_s, ge_s),
                                                          (cot_ref, wo_s, zo_s, go_s))):
            for r0, rb in HY_IB:
                y = jnp.dot(ct_ref[r0:r0 + rb, :], w_s[...], preferred_element_type=F32)
                znew = g_s[r0:r0 + rb, :] * (y + z_s[r0:r0 + rb, :] * skip)
                if n == 0:
                    z_s[r0:r0 + rb, :] = znew
                else:
                    n_valid = min(r0 + rb, LH) - r0
                    for j in range(nl):
                        stage_s.at[j][pl.ds(HALO + parity + 2 * r0, n_valid, stride=2), :] = (
                            znew[0:n_valid, j * LANES:(j + 1) * LANES])
    for j in range(nl):
        o_ref[0, :, j * LANES:(j + 1) * LANES] = stage_s[j, HALO + N_META:HALO + L_TOT, :]


def _hyena(p_hy, pm_hy, conv_w, conv_b, skip, mats, hr, hi):
    bsz = p_hy.shape[0]
    nct = D_HYENA // HY_CT
    ct = HY_CT
    ce, co, cet, cot = mats

    def part(j):
        return [pl.BlockSpec((1, SEQ, ct), lambda c, b, j=j: (b, 0, j * nct + c))]

    def mpart(j):
        return [pl.BlockSpec((N_META, ct), lambda c, b, j=j: (0, j * nct + c))]

    def wpart(rows, j):
        return [pl.BlockSpec((rows, ct), lambda c, b, j=j: (0, j * nct + c))]

    spec = pl.BlockSpec((HYENA_ORDER, ND, ct), lambda c, b: (0, 0, c), pipeline_mode=pl.Buffered(1))
    in_specs = (part(0) + part(1) + part(2) + mpart(0) + mpart(1) + mpart(2)
                + wpart(3, 0) + wpart(3, 1) + wpart(3, 2)
                + wpart(1, 0) + wpart(1, 1) + wpart(1, 2)
                + [pl.BlockSpec((HYENA_ORDER, ct), lambda c, b: (0, c)),
                   pl.BlockSpec((ND, LHP), lambda c, b: (0, 0)),
                   pl.BlockSpec((ND, LHP), lambda c, b: (0, 0)),
                   pl.BlockSpec((LHP, ND), lambda c, b: (0, 0)),
                   pl.BlockSpec((LHP, ND), lambda c, b: (0, 0)),
                   spec, spec])
    cb = conv_b.reshape(1, -1)
    f32_half = pltpu.VMEM((LHR, ct), F32)
    return pl.pallas_call(
        _hyena_body,
        grid=(nct, bsz),
        in_specs=in_specs,
        out_specs=pl.BlockSpec((1, SEQ, ct), lambda c, b: (b, 0, c)),
        out_shape=jax.ShapeDtypeStruct((bsz, SEQ, D_HYENA), F32),
        scratch_shapes=[pltpu.VMEM((ct // LANES, L_TOT + 2 * HALO, LANES), F32),
                        f32_half, f32_half, f32_half, f32_half, f32_half, f32_half,
                        pltpu.VMEM((LHP, ct), BF16), pltpu.VMEM((LHP, ct), BF16),
                        pltpu.VMEM((ND, ct), BF16), pltpu.VMEM((ND, ct), BF16)],
        compiler_params=pltpu.CompilerParams(
            dimension_semantics=("arbitrary", "arbitrary"), vmem_limit_bytes=60 * 1024 * 1024),
        name="hyena",
    )(p_hy, p_hy, p_hy, pm_hy, pm_hy, pm_hy, conv_w, conv_w, conv_w, cb, cb, cb, skip,
      ce, co, cet, cot, hr, hi)


N_CHUNKS = SEQ // CHUNK
NT_DIMS = (((1,), (1,)), ((), ()))
TN_DIMS = (((0,), (0,)), ((), ()))
MID_F = CHUNK // 2
MID_B = CHUNK // 2 - 1


HG_G = 8
HG_ROWS = HG_G * CHUNK


def _split2(x):
    hi = x.astype(BF16)
    lo = (x - hi.astype(F32)).astype(BF16)
    return hi, lo


def _chunk_prefix_matrix():
    r = lax.broadcasted_iota(jnp.int32, (HG_ROWS, HG_ROWS), 0)
    c = lax.broadcasted_iota(jnp.int32, (HG_ROWS, HG_ROWS), 1)
    return (((r // CHUNK) == (c // CHUNK)) & (c <= r)).astype(BF16)


def _chunk_rows(x, row):
    return jnp.concatenate(
        [jnp.broadcast_to(x[g * CHUNK + row:g * CHUNK + row + 1, :], (CHUNK, x.shape[1]))
         for g in range(x.shape[0] // CHUNK)], axis=0)


def _hgrn_body(q_ref, ff_ref, fb_ref, i_ref, g_ref, mff_ref, mi_ref,
               lbf_ref, lbb_ref, nw_ref, o_ref,
               tri_s, qe_s, sc_s, ut_s, dec_s, st_s):
    hd = HGRN_HEAD_DIM
    row = lax.broadcasted_iota(jnp.int32, (CHUNK, CHUNK), 0)
    col = lax.broadcasted_iota(jnp.int32, (CHUNK, CHUNK), 1)
    lower = row >= col
    upper = col >= row
    lbf = lbf_ref[...]
    lbb = lbb_ref[...]

    @pl.when((pl.program_id(0) == 0) & (pl.program_id(1) == 0))
    def _():
        tri_s[...] = _chunk_prefix_matrix()

    def forget(logit, lb):
        f = lb + (1.0 - lb) * jax.nn.sigmoid(logit)
        return 1.0 - f, jnp.log(f)

    def prefix_sums(lf):
        s = jnp.dot(tri_s[...], jnp.concatenate(_split2(lf), axis=1), preferred_element_type=F32)
        return s[:, :hd] + s[:, hd:]

    k_m, lf_m = forget(mff_ref[...], lbf)
    pad = jnp.zeros((CHUNK - N_META, hd), F32)
    lf_m = jnp.concatenate([pad, lf_m] * HG_G, axis=0)
    b_m = prefix_sums(lf_m)[0:CHUNK]
    kl_m = jnp.concatenate([pad, k_m], axis=0) * jnp.exp(b_m[CHUNK - 1:CHUNK] - b_m)
    v_m = jnp.concatenate([pad, mi_ref[...]], axis=0).astype(BF16)
    st_meta = lax.dot_general(v_m, kl_m.astype(BF16), TN_DIMS, preferred_element_type=F32)

    def phase_a(j, carry):
        r0 = pl.multiple_of(j * HG_ROWS, HG_ROWS)
        rows = pl.ds(r0, HG_ROWS)
        qv = jax.nn.silu(q_ref[0, rows, :])
        vb = i_ref[0, rows, :].astype(BF16)
        k_f, lf_f = forget(ff_ref[0, rows, :], lbf)
        k_b, lf_b = forget(fb_ref[0, rows, :], lbb)
        b_f = prefix_sums(lf_f)
        p_b = prefix_sums(lf_b)
        bmid_f = _chunk_rows(b_f, MID_F)
        blast_f = _chunk_rows(b_f, CHUNK - 1)
        tot_b = _chunk_rows(p_b, CHUNK - 1)
        c_b = tot_b - p_b + lf_b
        cmid_b = _chunk_rows(c_b, MID_B)
        d_f = b_f - bmid_f
        d_b = c_b - cmid_b
        e_f = jnp.exp(d_f)
        e_b = jnp.exp(d_b)
        qs_f = qv * e_f
        ks_f = k_f / e_f
        qs_b = qv * e_b
        ks_b = k_b / e_b
        qs_fb, ks_fb = qs_f.astype(BF16), ks_f.astype(BF16)
        qs_bb, ks_bb = qs_b.astype(BF16), ks_b.astype(BF16)
        for g in range(HG_G):
            sl = slice(g * CHUNK, (g + 1) * CHUNK)
            r1 = slice(g * CHUNK, g * CHUNK + 1)
            n = j * HG_G + g
            rows_g = pl.ds(pl.multiple_of(r0 + g * CHUNK, CHUNK), CHUNK)
            sc_f = lax.dot_general(qs_fb[sl], ks_fb[sl], NT_DIMS, preferred_element_type=F32)
            sc_b = lax.dot_general(qs_bb[sl], ks_bb[sl], NT_DIMS, preferred_element_type=F32)
            sc_s[n] = (jnp.where(lower, sc_f, 0.0) + jnp.where(upper, sc_b, 0.0)).astype(BF16)
            em_f = jnp.exp(bmid_f[r1])
            el_f = jnp.exp(blast_f[r1] - bmid_f[r1])
            em_b = jnp.exp(cmid_b[r1])
            el_b = jnp.exp(tot_b[r1] - cmid_b[r1])
            qe_s[rows_g, :] = jnp.concatenate(
                [qs_f[sl] * em_f, qs_b[sl] * em_b], axis=1).astype(BF16)
            kl = jnp.concatenate([ks_f[sl] * el_f, ks_b[sl] * el_b], axis=1).astype(BF16)
            ut_s[n] = lax.dot_general(vb[sl], kl, TN_DIMS, preferred_element_type=F32)
            dec_s[n] = jnp.concatenate([jnp.exp(blast_f[r1]), jnp.exp(tot_b[r1])], axis=1)
        return carry

    lax.fori_loop(0, N_CHUNKS // HG_G, phase_a, 0, unroll=True)

    st_f = st_meta
    st_b = jnp.zeros((hd, hd), F32)
    for n in range(N_CHUNKS):
        st_s[n, :, 0:hd] = st_f.astype(BF16)
        st_f = dec_s[n, :, 0:hd] * st_f + ut_s[n, :, 0:hd]
        m = N_CHUNKS - 1 - n
        st_s[m, :, hd:2 * hd] = st_b.astype(BF16)
        st_b = dec_s[m, :, hd:2 * hd] * st_b + ut_s[m, :, hd:2 * hd]

    nw = nw_ref[...]

    def phase_c(j, carry):
        r0 = pl.multiple_of(j * HG_ROWS, HG_ROWS)
        rows = pl.ds(r0, HG_ROWS)
        vb = i_ref[0, rows, :].astype(BF16)
        outs = []
        for g in range(HG_G):
            sl = slice(g * CHUNK, (g + 1) * CHUNK)
            n = j * HG_G + g
            rows_g = pl.ds(pl.multiple_of(r0 + g * CHUNK, CHUNK), CHUNK)
            o = jnp.dot(sc_s[n], vb[sl], preferred_element_type=F32)
            outs.append(o + lax.dot_general(qe_s[rows_g, :], st_s[n], NT_DIMS,
                                            preferred_element_type=F32))
        o = jnp.concatenate(outs, axis=0)
        o = o * lax.rsqrt(jnp.mean(o * o, axis=-1, keepdims=True) + EPS)
        o_ref[0, rows, :] = o * nw * jax.nn.silu(g_ref[0, rows, :])
        return carry

    lax.fori_loop(0, N_CHUNKS // HG_G, phase_c, 0, unroll=True)


def _hgrn(phg_x, phg_m, lb_f, lb_b, norm_w):
    bsz = phg_x.shape[0]
    hd = HGRN_HEAD_DIM
    nh = HGRN_HEADS

    def part(j):
        return pl.BlockSpec((1, SEQ, hd), lambda b, h, j=j: (b, 0, j * nh + h))

    def mpart(j):
        return pl.BlockSpec((N_META, hd), lambda b, h, j=j: (0, j * nh + h))

    vec = pl.BlockSpec((1, hd), lambda b, h: (0, h))
    return pl.pallas_call(
        _hgrn_body,
        grid=(bsz, nh),
        in_specs=[part(0), part(1), part(2), part(3), part(4), mpart(1), mpart(3), vec, vec, vec],
        out_specs=pl.BlockSpec((1, SEQ, hd), lambda b, h: (b, 0, h)),
        out_shape=jax.ShapeDtypeStruct((bsz, SEQ, D_HGRN), F32),
        scratch_shapes=[pltpu.VMEM((HG_ROWS, HG_ROWS), BF16),
                        pltpu.VMEM((SEQ, 2 * hd), BF16),
                        pltpu.VMEM((N_CHUNKS, CHUNK, CHUNK), BF16),
                        pltpu.VMEM((N_CHUNKS, hd, 2 * hd), F32),
                        pltpu.VMEM((N_CHUNKS, 1, 2 * hd), F32),
                        pltpu.VMEM((N_CHUNKS, hd, 2 * hd), BF16)],
        compiler_params=pltpu.CompilerParams(
            dimension_semantics=("arbitrary", "arbitrary"), vmem_limit_bytes=40 * 1024 * 1024),
        name="hgrn",
    )(phg_x, phg_x, phg_x, phg_x, phg_x, phg_m, phg_m,
      lb_f.reshape(1, -1), lb_b.reshape(1, -1), norm_w.reshape(1, -1))


def _hyena_filters(L, w1, b1, w2, b2, w3, freq):
    pos = jnp.arange(L, dtype=F32)
    t = pos / max(L - 1, 1)
    bands = jnp.linspace(1e-4, FILTER_BANDS - 1, FILTER_BANDS, dtype=F32)
    ang = (2.0 * math.pi / L) * pos[:, None] * bands[None, :]
    z = jnp.concatenate([t[:, None], jnp.cos(ang), -jnp.sin(ang)], axis=-1)
    hp = lax.Precision.HIGHEST
    hid = jnp.sin(freq * (jnp.dot(z, w1, precision=hp) + b1))
    hid = jnp.sin(freq * (jnp.dot(hid, w2, precision=hp) + b2))
    filt = jnp.dot(hid, w3, precision=hp).reshape(L, 2, HYENA_ORDER, D_HYENA)
    deltas = jnp.abs(jnp.linspace(math.log(DECAY_TARGET) / SLOW_DECAY_PCT,
                                  math.log(DECAY_TARGET) / FAST_DECAY_PCT, D_HYENA, dtype=F32))
    window = jnp.exp(-t[:, None] * deltas[None, :])
    filt = filt * window[:, None, None, :]
    return filt[:, 0], filt[:, 1]


def kernel(x, meta_tokens, w_in, conv_w, conv_b, filt_w1, filt_b1, filt_w2, filt_b2, filt_w3,
           filt_freq, filt_skip, hyena_norm, lb_fwd, lb_bwd, hgrn_norm, w_out, norm_mix, norm_ffn,
           w_router_group, w_router_expert, w_gate, w_up, w_down, norm_final):
    B, S, D = x.shape
    L = S + N_META
    lbf = jnp.cumsum(jax.nn.softmax(lb_fwd, axis=0), axis=0)[0]
    lbb = jnp.cumsum(jax.nn.softmax(lb_bwd, axis=0), axis=0)[0]

    w_in_b = w_in[0].astype(BF16)
    xf = x.reshape(B * S, D)
    phy_x, phg_x = _inproj(xf, norm_mix[0], w_in_b, tm=512)
    phy_m, phg_m = _inproj(meta_tokens, norm_mix[0], w_in_b, tm=N_META)

    mats = _dft_matrices()
    h_fwd, h_bwd = _hyena_filters(L, filt_w1[0], filt_b1[0], filt_w2[0], filt_b2[0], filt_w3[0],
                                  filt_freq[0])
    hr, hi = _spectra(h_fwd, h_bwd, mats[0], mats[1])
    z_hy = _hyena(phy_x.reshape(B, S, D_HYENA_PROJ), phy_m, conv_w[0], conv_b[0], filt_skip[0],
                  mats, hr, hi).reshape(B * S, D_HYENA)

    y_hg = _hgrn(phg_x.reshape(B, S, 5 * D_HGRN), phg_m, lbf, lbb,
                 hgrn_norm[0]).reshape(B * S, D_HGRN)

    w_r = jnp.concatenate([w_router_group[0], w_router_expert[0].reshape(D, N_EXPERTS),
                           jnp.zeros((D, LANES - N_GROUPS - N_EXPERTS), F32)], axis=1).astype(BF16)
    h1, a2p, ri, rg, cnt = _outproj(z_hy, y_hg, xf, hyena_norm[0], norm_ffn[0],
                                    w_out[0].astype(BF16), w_r, tm=1024)
    out = _moe(h1, a2p, ri, rg, cnt, norm_final, w_gate[0], w_up[0], w_down[0])
    return out.reshape(B, S, D)
```

```python
import functools
import math

import jax
import jax.numpy as jnp
from jax import lax
from jax.experimental import pallas as pl
from jax.experimental.pallas import tpu as pltpu

D_MODEL = 1024
N_META = 16
D_HYENA = 512
D_HGRN = 512
HYENA_ORDER = 2
SHORT_CONV = 3
FILTER_EMB = 33
FILTER_BANDS = 16
DECAY_TARGET = 1e-2
FAST_DECAY_PCT = 0.3
SLOW_DECAY_PCT = 1.5
HGRN_HEAD_DIM = 128
HGRN_HEADS = D_HGRN // HGRN_HEAD_DIM
CHUNK = 64
N_GROUPS = 8
EXPERTS_PER_GROUP = 8
N_EXPERTS = 64
TOP_K = 2
D_EXPERT = 512
D_HYENA_PROJ = 3 * D_HYENA
D_IN_PROJ = D_HYENA_PROJ + 5 * D_HGRN
EPS = 1e-6

F32 = jnp.float32
BF16 = jnp.bfloat16


def _rms(x, gain):
    return x * lax.rsqrt(jnp.mean(x * x, axis=-1, keepdims=True) + EPS) * gain


HI16 = 0xFFFF0000


def _pack_bf16_pairs(x):
    c = x.shape[1] // 2
    bits = lax.bitcast_convert_type(x.astype(BF16).astype(F32), jnp.uint32)
    return (bits[:, :c] >> 16) | (bits[:, c:] & jnp.uint32(HI16))


def _unpack_bf16_pairs(w):
    lo = lax.bitcast_convert_type(w << 16, F32)
    hi = lax.bitcast_convert_type(w & jnp.uint32(HI16), F32)
    return lo, hi


def _inproj_body(x_ref, g_ref, w_ref, hy_ref, hg_ref, *, tn):
    a = _rms(x_ref[...], g_ref[...]).astype(BF16)
    for j in range(D_IN_PROJ // tn):
        acc = jnp.dot(a, w_ref[:, j * tn:(j + 1) * tn], preferred_element_type=F32)
        if j * tn < D_HYENA_PROJ:
            hy_ref[:, j * tn:(j + 1) * tn] = acc.astype(hy_ref.dtype)
        else:
            c0 = j * tn - D_HYENA_PROJ
            hg_ref[:, c0:c0 + tn] = acc.astype(hg_ref.dtype)


def _inproj(x, gain, w_bf16, tm):
    m, d = x.shape
    n_hg = D_IN_PROJ - D_HYENA_PROJ
    return pl.pallas_call(
        functools.partial(_inproj_body, tn=512),
        grid=(m // tm,),
        in_specs=[pl.BlockSpec((tm, d), lambda i: (i, 0)),
                  pl.BlockSpec((1, d), lambda i: (0, 0)),
                  pl.BlockSpec((d, D_IN_PROJ), lambda i: (0, 0))],
        out_specs=[pl.BlockSpec((tm, D_HYENA_PROJ), lambda i: (i, 0)),
                   pl.BlockSpec((tm, n_hg), lambda i: (i, 0))],
        out_shape=[jax.ShapeDtypeStruct((m, D_HYENA_PROJ), BF16),
                   jax.ShapeDtypeStruct((m, n_hg), F32)],
        compiler_params=pltpu.CompilerParams(
            dimension_semantics=("arbitrary",), vmem_limit_bytes=48 * 1024 * 1024),
        name="inproj",
    )(x, gain.reshape(1, d), w_bf16)


LANES = 128
NEG_BIG = -1e30
OP_SUB = 512
ROW_SUB = (D_MODEL // 2) // LANES


def _outproj_body(zhy_ref, yhg_ref, h0_ref, ghy_ref, gffn_ref, wo_ref, wr_ref,
                  h1_ref, a2p_ref, ri_ref, rg_ref, cnt_ref, tri_s, carry_s):
    i = pl.program_id(0)
    tm = h1_ref.shape[0]
    ts = tri_s.shape[0]

    @pl.when(i == 0)
    def _():
        r = lax.broadcasted_iota(jnp.int32, (ts, ts), 0)
        c = lax.broadcasted_iota(jnp.int32, (ts, ts), 1)
        tri_s[...] = (r > c).astype(BF16)
        carry_s[...] = jnp.zeros_like(carry_s)

    lane = lax.broadcasted_iota(jnp.int32, (ts, LANES), 1)
    is_g = lane < N_GROUPS
    carry = carry_s[...]
    for r0 in range(0, tm, ts):
        rows = slice(r0, r0 + ts)
        yhy = _rms(zhy_ref[rows, :], ghy_ref[...]).astype(BF16)
        yhg = yhg_ref[rows, :].astype(BF16)
        acc = jnp.dot(yhy, wo_ref[:D_HYENA, :], preferred_element_type=F32)
        acc = acc + jnp.dot(yhg, wo_ref[D_HYENA:, :], preferred_element_type=F32)
        h1 = h0_ref[rows, :] + acc
        h1_ref[rows, :] = h1
        a2 = _rms(h1, gffn_ref[...])
        a2p_ref[rows] = _pack_bf16_pairs(a2).reshape(ts, ROW_SUB, LANES)
        lg = jnp.dot(a2.astype(BF16), wr_ref[...], preferred_element_type=F32)

        gl = jnp.where(is_g, lg, NEG_BIG)
        gmax = jnp.max(gl, axis=1, keepdims=True)
        gsel = jnp.min(jnp.where(gl == gmax, lane, LANES), axis=1, keepdims=True)
        gden = jnp.sum(jnp.where(is_g, jnp.exp(gl - gmax), 0.0), axis=1, keepdims=True)
        p_group = 1.0 / gden
        in_grp = (lane >= N_GROUPS) & (lane < N_GROUPS + N_EXPERTS) & (
            ((lane - N_GROUPS) >> 3) == gsel)
        el = jnp.where(in_grp, lg, NEG_BIG)
        m1 = jnp.max(el, axis=1, keepdims=True)
        i1 = jnp.min(jnp.where(el == m1, lane, LANES), axis=1, keepdims=True)
        el2 = jnp.where(lane == i1, NEG_BIG, el)
        m2 = jnp.max(el2, axis=1, keepdims=True)
        i2 = jnp.min(jnp.where(el2 == m2, lane, LANES), axis=1, keepdims=True)
        r21 = jnp.exp(m2 - m1)
        gate1 = p_group / (1.0 + r21)
        gate2 = gate1 * r21

        hit1 = lane == i1
        hit2 = lane == i2
        onehot = (hit1 | hit2).astype(BF16)
        pre = jnp.dot(tri_s[...], onehot, preferred_element_type=F32) + carry
        pos1 = jnp.sum(jnp.where(hit1, pre, 0.0), axis=1, keepdims=True).astype(jnp.int32)
        pos2 = jnp.sum(jnp.where(hit2, pre, 0.0), axis=1, keepdims=True).astype(jnp.int32)
        carry = carry + jnp.sum(onehot.astype(F32), axis=0, keepdims=True)

        zero_i = jnp.zeros((ts, LANES), jnp.int32)
        ri_ref[rows, :] = jnp.where(lane == 0, i1 - N_GROUPS,
                          jnp.where(lane == 1, i2 - N_GROUPS,
                          jnp.where(lane == 2, pos1, jnp.where(lane == 3, pos2, zero_i))))
        rg_ref[rows, :] = jnp.where(lane == 0, gate1, jnp.where(lane == 1, gate2, 0.0))
    carry_s[...] = carry
    cnt_ref[...] = carry


def _outproj(zhy, yhg, h0, g_hy, g_ffn, wo_bf16, wr_bf16, tm):
    m = h0.shape[0]
    return pl.pallas_call(
        _outproj_body,
        grid=(m // tm,),
        in_specs=[pl.BlockSpec((tm, D_HYENA), lambda i: (i, 0)),
                  pl.BlockSpec((tm, D_HGRN), lambda i: (i, 0)),
                  pl.BlockSpec((tm, D_MODEL), lambda i: (i, 0)),
                  pl.BlockSpec((1, D_HYENA), lambda i: (0, 0)),
                  pl.BlockSpec((1, D_MODEL), lambda i: (0, 0)),
                  pl.BlockSpec((D_MODEL, D_MODEL), lambda i: (0, 0)),
                  pl.BlockSpec((D_MODEL, LANES), lambda i: (0, 0))],
        out_specs=[pl.BlockSpec((tm, D_MODEL), lambda i: (i, 0)),
                   pl.BlockSpec((tm, ROW_SUB, LANES), lambda i: (i, 0, 0)),
                   pl.BlockSpec((tm, LANES), lambda i: (i, 0)),
                   pl.BlockSpec((tm, LANES), lambda i: (i, 0)),
                   pl.BlockSpec((1, LANES), lambda i: (0, 0))],
        out_shape=[jax.ShapeDtypeStruct((m, D_MODEL), F32),
                   jax.ShapeDtypeStruct((m, ROW_SUB, LANES), jnp.uint32),
                   jax.ShapeDtypeStruct((m, LANES), jnp.int32),
                   jax.ShapeDtypeStruct((m, LANES), F32),
                   jax.ShapeDtypeStruct((1, LANES), F32)],
        scratch_shapes=[pltpu.VMEM((OP_SUB, OP_SUB), BF16), pltpu.VMEM((1, LANES), F32)],
        compiler_params=pltpu.CompilerParams(
            dimension_semantics=("arbitrary",), vmem_limit_bytes=48 * 1024 * 1024),
        name="outproj",
    )(zhy, yhg, h0, g_hy.reshape(1, -1), g_ffn.reshape(1, -1), wo_bf16, wr_bf16)


EXP_TB = 512


def _wait_rows(ref, n_rows, sem):
    pltpu.make_async_copy(ref.at[pl.ds(0, n_rows)], ref.at[pl.ds(n_rows, n_rows)], sem).wait()


def _dispatch_body(lastblk_ref, npad_ref, nused_ref, d0_ref, d1_ref, h1_ref, xb_ref, zero_s, sem_z,
                   sem):
    i = pl.program_id(0)
    tm = d0_ref.shape[0]
    n_blocks = xb_ref.shape[0] // EXP_TB

    @pl.when(i == 0)
    def _():
        zero_s[...] = jnp.zeros_like(zero_s)

        def zero_copy(row0):
            row0 = pl.multiple_of(row0, EXP_TB)
            return pltpu.make_async_copy(zero_s, xb_ref.at[pl.ds(row0, EXP_TB)], sem_z)

        for e in range(N_EXPERTS):
            @pl.when(npad_ref[e] > 0)
            def _():
                zero_copy(lastblk_ref[e]).start()
        for e in range(N_EXPERTS):
            @pl.when(npad_ref[e] > 0)
            def _():
                zero_copy(lastblk_ref[e]).wait()

        def start_tail(b, carry):
            zero_copy(b * EXP_TB).start()
            return carry

        def wait_tail(b, carry):
            zero_copy(b * EXP_TB).wait()
            return carry

        lax.fori_loop(nused_ref[0], n_blocks, start_tail, 0)
        lax.fori_loop(nused_ref[0], n_blocks, wait_tail, 0)

    def issue(t, carry):
        pltpu.make_async_copy(h1_ref.at[t], xb_ref.at[d0_ref[t]], sem).start(priority=0)
        pltpu.make_async_copy(h1_ref.at[t], xb_ref.at[d1_ref[t]], sem).start(priority=1)
        return carry

    lax.fori_loop(0, tm, issue, 0, unroll=16)
    _wait_rows(xb_ref, TOP_K * tm, sem)


def _dispatch(lastblk, npad, n_used, dest, rows, n_slots, tm):
    m = rows.shape[0]
    grid_spec = pltpu.PrefetchScalarGridSpec(
        num_scalar_prefetch=3,
        grid=(m // tm,),
        in_specs=[pl.BlockSpec((tm,), lambda i, lb, npd, nu: (i,), memory_space=pltpu.SMEM),
                  pl.BlockSpec((tm,), lambda i, lb, npd, nu: (i,), memory_space=pltpu.SMEM),
                  pl.BlockSpec((tm, ROW_SUB, LANES), lambda i, lb, npd, nu: (i, 0, 0))],
        out_specs=pl.BlockSpec(memory_space=pl.ANY),
        scratch_shapes=[pltpu.VMEM((EXP_TB, ROW_SUB, LANES), rows.dtype),
                        pltpu.SemaphoreType.DMA(()), pltpu.SemaphoreType.DMA(())],
    )
    return pl.pallas_call(
        _dispatch_body,
        grid_spec=grid_spec,
        out_shape=jax.ShapeDtypeStruct((n_slots, ROW_SUB, LANES), rows.dtype),
        compiler_params=pltpu.CompilerParams(dimension_semantics=("arbitrary",)),
        name="dispatch",
    )(lastblk, npad, n_used, dest[0], dest[1], rows)


def _expert_body(eid_ref, nused_ref, xb_ref, wg_ref, wu_ref, wd_ref, o_ref, wg_s, wu_s, wd_s):
    i = pl.program_id(0)
    half = D_MODEL // 2

    @pl.when(i < nused_ref[0])
    def _():
        prev = eid_ref[jnp.maximum(i - 1, 0)]

        @pl.when((i == 0) | (eid_ref[i] != prev))
        def _():
            wg_s[...] = wg_ref[0].astype(BF16)
            wu_s[...] = wu_ref[0].astype(BF16)
            wd_s[...] = wd_ref[0].astype(BF16)

        lo, hi = _unpack_bf16_pairs(xb_ref[...].reshape(EXP_TB, half))
        lo, hi = lo.astype(BF16), hi.astype(BF16)

        def proj(w_s):
            return (jnp.dot(lo, w_s[:half, :], preferred_element_type=F32)
                    + jnp.dot(hi, w_s[half:, :], preferred_element_type=F32))

        g = proj(wg_s)
        u = proj(wu_s)
        hmid = (g * jax.nn.sigmoid(g) * u).astype(BF16)
        y = _pack_bf16_pairs(jnp.dot(hmid, wd_s[...], preferred_element_type=F32))
        o_ref[...] = y.reshape(EXP_TB, ROW_SUB, LANES)

    @pl.when(i >= nused_ref[0])
    def _():
        o_ref[...] = jnp.zeros_like(o_ref)


def _experts(block_eid, n_used, xb, w_gate, w_up, w_down):
    n_slots = xb.shape[0]

    def blk(i, e, nu):
        return jnp.minimum(i, nu[0] - 1)

    grid_spec = pltpu.PrefetchScalarGridSpec(
        num_scalar_prefetch=2,
        grid=(n_slots // EXP_TB,),
        in_specs=[pl.BlockSpec((EXP_TB, ROW_SUB, LANES), lambda i, e, nu: (blk(i, e, nu), 0, 0)),
                  pl.BlockSpec((1, D_MODEL, D_EXPERT), lambda i, e, nu: (e[blk(i, e, nu)], 0, 0)),
                  pl.BlockSpec((1, D_MODEL, D_EXPERT), lambda i, e, nu: (e[blk(i, e, nu)], 0, 0)),
                  pl.BlockSpec((1, D_EXPERT, D_MODEL), lambda i, e, nu: (e[blk(i, e, nu)], 0, 0))],
        out_specs=pl.BlockSpec((EXP_TB, ROW_SUB, LANES), lambda i, e, nu: (i, 0, 0)),
        scratch_shapes=[pltpu.VMEM((D_MODEL, D_EXPERT), BF16),
                        pltpu.VMEM((D_MODEL, D_EXPERT), BF16),
                        pltpu.VMEM((D_EXPERT, D_MODEL), BF16)],
    )
    return pl.pallas_call(
        _expert_body,
        grid_spec=grid_spec,
        out_shape=jax.ShapeDtypeStruct((n_slots, ROW_SUB, LANES), jnp.uint32),
        compiler_params=pltpu.CompilerParams(
            dimension_semantics=("arbitrary",), vmem_limit_bytes=48 * 1024 * 1024),
        name="experts",
    )(block_eid, n_used, xb, w_gate, w_up, w_down)


def _combine_body(d0_ref, d1_ref, n0_ref, n1_ref, h1_ref, rg_ref, gfin_ref, yb_ref, o_ref, y_s, sem):
    i = pl.program_id(0)
    tm = h1_ref.shape[0]
    slot = i % 2

    def issue(idx_refs, sl):
        def body(t, carry):
            for k in range(TOP_K):
                pltpu.make_async_copy(yb_ref.at[idx_refs[k][t]], y_s.at[sl, k, t],
                                      sem.at[sl]).start(priority=k)
            return carry

        lax.fori_loop(0, tm, body, 0, unroll=16)

    @pl.when(i == 0)
    def _():
        issue((d0_ref, d1_ref), 0)

    @pl.when(i + 1 < pl.num_programs(0))
    def _():
        issue((n0_ref, n1_ref), 1 - slot)

    _wait_rows(yb_ref, TOP_K * tm, sem.at[slot])
    rg = rg_ref[...]
    g1, g2 = rg[:, 0:1], rg[:, 1:2]
    half = D_MODEL // 2
    lo1, hi1 = _unpack_bf16_pairs(y_s[slot, 0].reshape(tm, half))
    lo2, hi2 = _unpack_bf16_pairs(y_s[slot, 1].reshape(tm, half))
    h2_lo = h1_ref[:, :half] + g1 * lo1 + g2 * lo2
    h2_hi = h1_ref[:, half:] + g1 * hi1 + g2 * hi2
    ms = (jnp.sum(h2_lo * h2_lo, axis=-1, keepdims=True)
          + jnp.sum(h2_hi * h2_hi, axis=-1, keepdims=True)) * (1.0 / D_MODEL)
    inv = lax.rsqrt(ms + EPS)
    o_ref[:, :half] = h2_lo * inv * gfin_ref[:, :half]
    o_ref[:, half:] = h2_hi * inv * gfin_ref[:, half:]


def _combine(dest, h1, rg, g_fin, yb, tm):
    m = h1.shape[0]
    last = m // tm - 1
    return pl.pallas_call(
        _combine_body,
        grid=(m // tm,),
        in_specs=[pl.BlockSpec((tm,), lambda i: (i,), memory_space=pltpu.SMEM),
                  pl.BlockSpec((tm,), lambda i: (i,), memory_space=pltpu.SMEM),
                  pl.BlockSpec((tm,), lambda i: (jnp.minimum(i + 1, last),), memory_space=pltpu.SMEM),
                  pl.BlockSpec((tm,), lambda i: (jnp.minimum(i + 1, last),), memory_space=pltpu.SMEM),
                  pl.BlockSpec((tm, D_MODEL), lambda i: (i, 0)),
                  pl.BlockSpec((tm, LANES), lambda i: (i, 0)),
                  pl.BlockSpec((1, D_MODEL), lambda i: (0, 0)),
                  pl.BlockSpec(memory_space=pl.ANY)],
        out_specs=pl.BlockSpec((tm, D_MODEL), lambda i: (i, 0)),
        out_shape=jax.ShapeDtypeStruct((m, D_MODEL), F32),
        scratch_shapes=[pltpu.VMEM((2, TOP_K, tm, ROW_SUB, LANES), yb.dtype),
                        pltpu.SemaphoreType.DMA((2,))],
        compiler_params=pltpu.CompilerParams(dimension_semantics=("arbitrary",)),
        name="combine",
    )(dest[0], dest[1], dest[0], dest[1], h1, rg, g_fin.reshape(1, -1), yb)


def _moe(h1, a2p, ri, rg, cnt, g_fin, w_gate, w_up, w_down):
    m = h1.shape[0]
    n_blocks = TOP_K * m // EXP_TB + N_EXPERTS
    n_slots = n_blocks * EXP_TB
    counts = cnt[0, N_GROUPS:N_GROUPS + N_EXPERTS].astype(jnp.int32)
    padded = (counts + EXP_TB - 1) // EXP_TB * EXP_TB
    pend = jnp.cumsum(padded)
    base = pend - padded
    rt = jnp.transpose(ri)[:2 * TOP_K]
    sel = rt[None, :TOP_K] == jnp.arange(N_EXPERTS, dtype=jnp.int32)[:, None, None]
    dest = jnp.sum(jnp.where(sel, base[:, None, None], 0), axis=0) + rt[TOP_K:]
    blk_start = jnp.arange(n_blocks, dtype=jnp.int32) * EXP_TB
    block_eid = jnp.minimum(jnp.sum(blk_start[:, None] >= pend[None, :], axis=1),
                            N_EXPERTS - 1).astype(jnp.int32)
    n_used = (pend[-1:] // EXP_TB).astype(jnp.int32)
    lastblk = (pend - EXP_TB).astype(jnp.int32)

    xb = _dispatch(lastblk, padded.astype(jnp.int32), n_used, dest, a2p, n_slots, tm=1024)
    yb = _experts(block_eid, n_used, xb, w_gate, w_up, w_down)
    return _combine(dest, h1, rg, g_fin, yb, tm=512)


ND = 2176
NCIRC = 2 * ND
NH = ND // 2
SEQ = 2048
L_TOT = SEQ + N_META
LH = L_TOT // 2
LHP = 1152
LHR = 1040
HY_CT = 256
HY_FB = NH // 2
HY_IB = ((0, 528), (528, 512))
HY_CB = LH // 3
HALO = 8


def _dft_matrices():
    a = 2 * jnp.arange(NH, dtype=jnp.int32) + 1
    t_hi = jnp.arange(LHP // LANES, dtype=jnp.int32) * LANES
    t_lo = jnp.arange(LANES, dtype=jnp.int32)
    valid = ((t_hi[:, None] + t_lo[None, :]) < LH).reshape(1, LHP)

    def cos_sin(m, denom):
        ang = m.astype(F32) * (math.pi / denom)
        return jnp.cos(ang), jnp.sin(ang)

    c_hi, s_hi = cos_sin((a[:, None] * ((4 * t_hi) % (4 * NCIRC))[None, :]) % (4 * NCIRC), 2 * NCIRC)
    c_hi, s_hi = c_hi[:, :, None], s_hi[:, :, None]
    out = []
    for c in (1, 3):
        c_lo, s_lo = cos_sin((a[:, None] * (4 * t_lo + c)[None, :]) % (4 * NCIRC), 2 * NCIRC)
        c_lo, s_lo = c_lo[:, None, :], s_lo[:, None, :]
        cos_phi = jnp.where(valid, (c_hi * c_lo - s_hi * s_lo).reshape(NH, LHP), 0.0)
        sin_phi = jnp.where(valid, (s_hi * c_lo + c_hi * s_lo).reshape(NH, LHP), 0.0)
        out.append(jnp.concatenate([cos_phi, sin_phi if c == 1 else -sin_phi], axis=0).astype(BF16))
    ce, co = out
    return ce, co, ce.T, co.T


def _forward_half_spectrum(ce_ref, co_ref, xe, xo, j0):
    a1 = jnp.dot(ce_ref[j0:j0 + HY_FB, :], xe, preferred_element_type=F32)
    a2 = jnp.dot(ce_ref[NH + j0:NH + j0 + HY_FB, :], xe, preferred_element_type=F32)
    b1 = jnp.dot(co_ref[j0:j0 + HY_FB, :], xo, preferred_element_type=F32)
    b2 = jnp.dot(co_ref[NH + j0:NH + j0 + HY_FB, :], xo, preferred_element_type=F32)
    return a1 + b1, a2 - b2, a2 + b2, a1 - b1


def _spectra_body(s_ref, d_ref, ce_ref, co_ref, rot_ref, hr_ref, hi_ref,
                  stage_s, se_s, so_s, de_s, do_s):
    ct = hr_ref.shape[-1]
    nl = ct // LANES
    for x_ref, xe_s, xo_s in ((s_ref, se_s, so_s), (d_ref, de_s, do_s)):
        for j in range(nl):
            stage_s[j] = x_ref[:, j * LANES:(j + 1) * LANES]
        for par, dst in ((0, xe_s), (1, xo_s)):
            for r0 in range(0, LH, HY_CB):
                dst[r0:r0 + HY_CB, :] = jnp.concatenate(
                    [stage_s.at[j][pl.ds(par + 2 * r0, HY_CB, stride=2), :] for j in range(nl)],
                    axis=1).astype(BF16)
            dst[LH:LHP, :] = jnp.zeros((LHP - LH, ct), BF16)
    for j0 in range(0, NH, HY_FB):
        ar1, _, ar2, _ = _forward_half_spectrum(ce_ref, co_ref, se_s[...], so_s[...], j0)
        _, q1, _, q2 = _forward_half_spectrum(ce_ref, co_ref, de_s[...], do_s[...], j0)
        for r0, ar, q in ((j0, ar1, q1), (NH + j0, ar2, q2)):
            c = rot_ref[r0:r0 + HY_FB, 0:1]
            s = rot_ref[r0:r0 + HY_FB, 1:2]
            hr_ref[0, r0:r0 + HY_FB, :] = c * ar + s * q
            hi_ref[0, r0:r0 + HY_FB, :] = s * ar - c * q


def _spectra(h_fwd, h_bwd, ce, co):
    hbs = jnp.concatenate([h_bwd[1:], jnp.zeros_like(h_bwd[:1])], axis=0)
    s2d = (h_fwd + hbs).reshape(L_TOT, HYENA_ORDER * D_HYENA)
    d2d = (h_fwd - hbs).reshape(L_TOT, HYENA_ORDER * D_HYENA)
    j = jnp.arange(NH, dtype=F32)
    k = jnp.concatenate([j, ND - 1 - j])
    half = (2 * k + 1) * (math.pi / (2 * NCIRC))
    scale = 2.0 / NCIRC
    rot = jnp.stack([jnp.cos(half) * scale, jnp.sin(half) * scale], axis=1)
    rot = jnp.pad(rot, ((0, 0), (0, LANES - 2)))
    n_ord, ct = HYENA_ORDER, HY_CT
    nct = D_HYENA // ct
    col = pl.BlockSpec((L_TOT, ct), lambda o, c: (0, o * nct + c))
    mat = pl.BlockSpec((ND, LHP), lambda o, c: (0, 0))
    out = pl.BlockSpec((1, ND, ct), lambda o, c: (o, 0, c))
    half_bf16 = pltpu.VMEM((LHP, ct), BF16)
    return pl.pallas_call(
        _spectra_body,
        grid=(n_ord, nct),
        in_specs=[col, col, mat, mat, pl.BlockSpec((ND, LANES), lambda o, c: (0, 0))],
        out_specs=[out, out],
        out_shape=[jax.ShapeDtypeStruct((n_ord, ND, D_HYENA), F32)] * 2,
        scratch_shapes=[pltpu.VMEM((ct // LANES, L_TOT, LANES), F32),
                        half_bf16, half_bf16, half_bf16, half_bf16],
        compiler_params=pltpu.CompilerParams(
            dimension_semantics=("arbitrary", "arbitrary"), vmem_limit_bytes=48 * 1024 * 1024),
        name="spectra",
    )(s2d, d2d, ce, co, rot)


def _hyena_body(pv_ref, px1_ref, px2_ref, mv_ref, mx1_ref, mx2_ref,
                wv_ref, wx1_ref, wx2_ref, bv_ref, bx1_ref, bx2_ref, skip_ref,
                ce_ref, co_ref, cet_ref, cot_ref, hr_ref, hi_ref, o_ref,
                stage_s, ze_s, zo_s, g1e_s, g1o_s, g2e_s, g2o_s, zbe_s, zbo_s, we_s, wo_s):
    ct = o_ref.shape[-1]
    nl = ct // LANES

    def strided(off, r0, rows):
        return jnp.concatenate(
            [stage_s.at[j][pl.ds(HALO + off + 2 * r0, rows, stride=2), :] for j in range(nl)], axis=1)

    def short_conv(p_ref, m_ref, w_ref, b_ref, dst_e, dst_o):
        for j in range(nl):
            ln = slice(j * LANES, (j + 1) * LANES)
            stage_s[j, 0:HALO, :] = jnp.zeros((HALO, LANES), F32)
            stage_s[j, HALO:HALO + N_META, :] = m_ref[:, ln].astype(F32)
            stage_s[j, HALO + N_META:HALO + L_TOT, :] = p_ref[0, :, ln].astype(F32)
            stage_s[j, HALO + L_TOT:HALO + L_TOT + HALO, :] = jnp.zeros((HALO, LANES), F32)
        w = w_ref[...]
        b = b_ref[...]
        for r0 in range(0, LH, HY_CB):
            sm1, s0, s1, s2 = (strided(off, r0, HY_CB) for off in (-1, 0, 1, 2))
            dst_e[r0:r0 + HY_CB, :] = b + sm1 * w[0:1] + s0 * w[1:2] + s1 * w[2:3]
            dst_o[r0:r0 + HY_CB, :] = b + s0 * w[0:1] + s1 * w[1:2] + s2 * w[2:3]
        dst_e[LH:LHR, :] = jnp.zeros((LHR - LH, ct), F32)
        dst_o[LH:LHR, :] = jnp.zeros((LHR - LH, ct), F32)

    short_conv(pv_ref, mv_ref, wv_ref, bv_ref, ze_s, zo_s)
    short_conv(px1_ref, mx1_ref, wx1_ref, bx1_ref, g1e_s, g1o_s)
    short_conv(px2_ref, mx2_ref, wx2_ref, bx2_ref, g2e_s, g2o_s)
    zbe_s[LH:LHP, :] = jnp.zeros((LHP - LH, ct), BF16)
    zbo_s[LH:LHP, :] = jnp.zeros((LHP - LH, ct), BF16)

    for n, (ge_s, go_s) in enumerate(((g1e_s, g1o_s), (g2e_s, g2o_s))):
        zbe_s[0:LH, :] = ze_s[0:LH, :].astype(BF16)
        zbo_s[0:LH, :] = zo_s[0:LH, :].astype(BF16)
        for j0 in range(0, NH, HY_FB):
            p1, q1, p2, q2 = _forward_half_spectrum(ce_ref, co_ref, zbe_s[...], zbo_s[...], j0)
            h1r = hr_ref[n, j0:j0 + HY_FB, :]
            h1i = hi_ref[n, j0:j0 + HY_FB, :]
            h2r = hr_ref[n, NH + j0:NH + j0 + HY_FB, :]
            h2i = hi_ref[n, NH + j0:NH + j0 + HY_FB, :]
            yr1 = p1 * h1r + q1 * h1i
            ny1 = q1 * h1r - p1 * h1i
            yr2 = p2 * h2r + q2 * h2i
            ny2 = q2 * h2r - p2 * h2i
            we_s[j0:j0 + HY_FB, :] = (yr1 + ny2).astype(BF16)
            we_s[NH + j0:NH + j0 + HY_FB, :] = (yr2 + ny1).astype(BF16)
            wo_s[j0:j0 + HY_FB, :] = (yr1 - ny2).astype(BF16)
            wo_s[NH + j0:NH + j0 + HY_FB, :] = (yr2 - ny1).astype(BF16)
        skip = skip_ref[n:n + 1, :]
        for parity, (ct_ref, w_s, z_s, g_s) in enumerate(((cet_ref, we_s, ze_s, ge_s),
                                                          (cot_ref, wo_s, zo_s, go_s))):
            for r0, rb in HY_IB:
                y = jnp.dot(ct_ref[r0:r0 + rb, :], w_s[...], preferred_element_type=F32)
                znew = g_s[r0:r0 + rb, :] * (y + z_s[r0:r0 + rb, :] * skip)
                if n == 0:
                    z_s[r0:r0 + rb, :] = znew
                else:
                    n_valid = min(r0 + rb, LH) - r0
                    for j in range(nl):
                        stage_s.at[j][pl.ds(HALO + parity + 2 * r0, n_valid, stride=2), :] = (
                            znew[0:n_valid, j * LANES:(j + 1) * LANES])
    for j in range(nl):
        o_ref[0, :, j * LANES:(j + 1) * LANES] = stage_s[j, HALO + N_META:HALO + L_TOT, :]


def _hyena(p_hy, pm_hy, conv_w, conv_b, skip, mats, hr, hi):
    bsz = p_hy.shape[0]
    nct = D_HYENA // HY_CT
    ct = HY_CT
    ce, co, cet, cot = mats

    def part(j):
        return [pl.BlockSpec((1, SEQ, ct), lambda c, b, j=j: (b, 0, j * nct + c))]

    def mpart(j):
        return [pl.BlockSpec((N_META, ct), lambda c, b, j=j: (0, j * nct + c))]

    def wpart(rows, j):
        return [pl.BlockSpec((rows, ct), lambda c, b, j=j: (0, j * nct + c))]

    spec = pl.BlockSpec((HYENA_ORDER, ND, ct), lambda c, b: (0, 0, c), pipeline_mode=pl.Buffered(1))
    in_specs = (part(0) + part(1) + part(2) + mpart(0) + mpart(1) + mpart(2)
                + wpart(3, 0) + wpart(3, 1) + wpart(3, 2)
                + wpart(1, 0) + wpart(1, 1) + wpart(1, 2)
                + [pl.BlockSpec((HYENA_ORDER, ct), lambda c, b: (0, c)),
                   pl.BlockSpec((ND, LHP), lambda c, b: (0, 0)),
                   pl.BlockSpec((ND, LHP), lambda c, b: (0, 0)),
                   pl.BlockSpec((LHP, ND), lambda c, b: (0, 0)),
                   pl.BlockSpec((LHP, ND), lambda c, b: (0, 0)),
                   spec, spec])
    cb = conv_b.reshape(1, -1)
    f32_half = pltpu.VMEM((LHR, ct), F32)
    return pl.pallas_call(
        _hyena_body,
        grid=(nct, bsz),
        in_specs=in_specs,
        out_specs=pl.BlockSpec((1, SEQ, ct), lambda c, b: (b, 0, c)),
        out_shape=jax.ShapeDtypeStruct((bsz, SEQ, D_HYENA), F32),
        scratch_shapes=[pltpu.VMEM((ct // LANES, L_TOT + 2 * HALO, LANES), F32),
                        f32_half, f32_half, f32_half, f32_half, f32_half, f32_half,
                        pltpu.VMEM((LHP, ct), BF16), pltpu.VMEM((LHP, ct), BF16),
                        pltpu.VMEM((ND, ct), BF16), pltpu.VMEM((ND, ct), BF16)],
        compiler_params=pltpu.CompilerParams(
            dimension_semantics=("arbitrary", "arbitrary"), vmem_limit_bytes=60 * 1024 * 1024),
        name="hyena",
    )(p_hy, p_hy, p_hy, pm_hy, pm_hy, pm_hy, conv_w, conv_w, conv_w, cb, cb, cb, skip,
      ce, co, cet, cot, hr, hi)


N_CHUNKS = SEQ // CHUNK
NT_DIMS = (((1,), (1,)), ((), ()))
TN_DIMS = (((0,), (0,)), ((), ()))
MID_F = CHUNK // 2
MID_B = CHUNK // 2 - 1


HG_G = 8
HG_ROWS = HG_G * CHUNK


def _split2(x):
    hi = x.astype(BF16)
    lo = (x - hi.astype(F32)).astype(BF16)
    return hi, lo


def _chunk_prefix_matrix():
    r = lax.broadcasted_iota(jnp.int32, (HG_ROWS, HG_ROWS), 0)
    c = lax.broadcasted_iota(jnp.int32, (HG_ROWS, HG_ROWS), 1)
    return (((r // CHUNK) == (c // CHUNK)) & (c <= r)).astype(BF16)


def _chunk_rows(x, row):
    return jnp.concatenate(
        [jnp.broadcast_to(x[g * CHUNK + row:g * CHUNK + row + 1, :], (CHUNK, x.shape[1]))
         for g in range(x.shape[0] // CHUNK)], axis=0)


def _hgrn_body(q_ref, ff_ref, fb_ref, i_ref, g_ref, mff_ref, mi_ref,
               lbf_ref, lbb_ref, nw_ref, o_ref,
               tri_s, qe_s, sc_s, ut_s, dec_s, st_s):
    hd = HGRN_HEAD_DIM
    row = lax.broadcasted_iota(jnp.int32, (CHUNK, CHUNK), 0)
    col = lax.broadcasted_iota(jnp.int32, (CHUNK, CHUNK), 1)
    lower = row >= col
    upper = col >= row
    lbf = lbf_ref[...]
    lbb = lbb_ref[...]

    @pl.when((pl.program_id(0) == 0) & (pl.program_id(1) == 0))
    def _():
        tri_s[...] = _chunk_prefix_matrix()

    def forget(logit, lb):
        f = lb + (1.0 - lb) * jax.nn.sigmoid(logit)
        return 1.0 - f, jnp.log(f)

    def prefix_sums(lf):
        s = jnp.dot(tri_s[...], jnp.concatenate(_split2(lf), axis=1), preferred_element_type=F32)
        return s[:, :hd] + s[:, hd:]

    k_m, lf_m = forget(mff_ref[...], lbf)
    pad = jnp.zeros((CHUNK - N_META, hd), F32)
    lf_m = jnp.concatenate([pad, lf_m] * HG_G, axis=0)
    b_m = prefix_sums(lf_m)[0:CHUNK]
    kl_m = jnp.concatenate([pad, k_m], axis=0) * jnp.exp(b_m[CHUNK - 1:CHUNK] - b_m)
    v_m = jnp.concatenate([pad, mi_ref[...]], axis=0).astype(BF16)
    st_meta = lax.dot_general(v_m, kl_m.astype(BF16), TN_DIMS, preferred_element_type=F32)

    def phase_a(j, carry):
        r0 = pl.multiple_of(j * HG_ROWS, HG_ROWS)
        rows = pl.ds(r0, HG_ROWS)
        qv = jax.nn.silu(q_ref[0, rows, :])
        vb = i_ref[0, rows, :].astype(BF16)
        k_f, lf_f = forget(ff_ref[0, rows, :], lbf)
        k_b, lf_b = forget(fb_ref[0, rows, :], lbb)
        b_f = prefix_sums(lf_f)
        p_b = prefix_sums(lf_b)
        bmid_f = _chunk_rows(b_f, MID_F)
        blast_f = _chunk_rows(b_f, CHUNK - 1)
        tot_b = _chunk_rows(p_b, CHUNK - 1)
        c_b = tot_b - p_b + lf_b
        cmid_b = _chunk_rows(c_b, MID_B)
        d_f = b_f - bmid_f
        d_b = c_b - cmid_b
        e_f = jnp.exp(d_f)
        e_b = jnp.exp(d_b)
        qs_f = qv * e_f
        ks_f = k_f / e_f
        qs_b = qv * e_b
        ks_b = k_b / e_b
        qs_fb, ks_fb = qs_f.astype(BF16), ks_f.astype(BF16)
        qs_bb, ks_bb = qs_b.astype(BF16), ks_b.astype(BF16)
        for g in range(HG_G):
            sl = slice(g * CHUNK, (g + 1) * CHUNK)
            r1 = slice(g * CHUNK, g * CHUNK + 1)
            n = j * HG_G + g
            rows_g = pl.ds(pl.multiple_of(r0 + g * CHUNK, CHUNK), CHUNK)
            sc_f = lax.dot_general(qs_fb[sl], ks_fb[sl], NT_DIMS, preferred_element_type=F32)
            sc_b = lax.dot_general(qs_bb[sl], ks_bb[sl], NT_DIMS, preferred_element_type=F32)
            sc_s[n] = (jnp.where(lower, sc_f, 0.0) + jnp.where(upper, sc_b, 0.0)).astype(BF16)
            em_f = jnp.exp(bmid_f[r1])
            el_f = jnp.exp(blast_f[r1] - bmid_f[r1])
            em_b = jnp.exp(cmid_b[r1])
            el_b = jnp.exp(tot_b[r1] - cmid_b[r1])
            qe_s[rows_g, :] = jnp.concatenate(
                [qs_f[sl] * em_f, qs_b[sl] * em_b], axis=1).astype(BF16)
            kl = jnp.concatenate([ks_f[sl] * el_f, ks_b[sl] * el_b], axis=1).astype(BF16)
            ut_s[n] = lax.dot_general(vb[sl], kl, TN_DIMS, preferred_element_type=F32)
            dec_s[n] = jnp.concatenate([jnp.exp(blast_f[r1]), jnp.exp(tot_b[r1])], axis=1)
        return carry

    lax.fori_loop(0, N_CHUNKS // HG_G, phase_a, 0, unroll=True)

    st_f = st_meta
    st_b = jnp.zeros((hd, hd), F32)
    for n in range(N_CHUNKS):
        st_s[n, :, 0:hd] = st_f.astype(BF16)
        st_f = dec_s[n, :, 0:hd] * st_f + ut_s[n, :, 0:hd]
        m = N_CHUNKS - 1 - n
        st_s[m, :, hd:2 * hd] = st_b.astype(BF16)
        st_b = dec_s[m, :, hd:2 * hd] * st_b + ut_s[m, :, hd:2 * hd]

    nw = nw_ref[...]

    def phase_c(j, carry):
        r0 = pl.multiple_of(j * HG_ROWS, HG_ROWS)
        rows = pl.ds(r0, HG_ROWS)
        vb = i_ref[0, rows, :].astype(BF16)
        outs = []
        for g in range(HG_G):
            sl = slice(g * CHUNK, (g + 1) * CHUNK)
            n = j * HG_G + g
            rows_g = pl.ds(pl.multiple_of(r0 + g * CHUNK, CHUNK), CHUNK)
            o = jnp.dot(sc_s[n], vb[sl], preferred_element_type=F32)
            outs.append(o + lax.dot_general(qe_s[rows_g, :], st_s[n], NT_DIMS,
                                            preferred_element_type=F32))
        o = jnp.concatenate(outs, axis=0)
        o = o * lax.rsqrt(jnp.mean(o * o, axis=-1, keepdims=True) + EPS)
        o_ref[0, rows, :] = o * nw * jax.nn.silu(g_ref[0, rows, :])
        return carry

    lax.fori_loop(0, N_CHUNKS // HG_G, phase_c, 0, unroll=True)


def _hgrn(phg_x, phg_m, lb_f, lb_b, norm_w):
    bsz = phg_x.shape[0]
    hd = HGRN_HEAD_DIM
    nh = HGRN_HEADS

    def part(j):
        return pl.BlockSpec((1, SEQ, hd), lambda b, h, j=j: (b, 0, j * nh + h))

    def mpart(j):
        return pl.BlockSpec((N_META, hd), lambda b, h, j=j: (0, j * nh + h))

    vec = pl.BlockSpec((1, hd), lambda b, h: (0, h))
    return pl.pallas_call(
        _hgrn_body,
        grid=(bsz, nh),
        in_specs=[part(0), part(1), part(2), part(3), part(4), mpart(1), mpart(3), vec, vec, vec],
        out_specs=pl.BlockSpec((1, SEQ, hd), lambda b, h: (b, 0, h)),
        out_shape=jax.ShapeDtypeStruct((bsz, SEQ, D_HGRN), F32),
        scratch_shapes=[pltpu.VMEM((HG_ROWS, HG_ROWS), BF16),
                        pltpu.VMEM((SEQ, 2 * hd), BF16),
                        pltpu.VMEM((N_CHUNKS, CHUNK, CHUNK), BF16),
                        pltpu.VMEM((N_CHUNKS, hd, 2 * hd), F32),
                        pltpu.VMEM((N_CHUNKS, 1, 2 * hd), F32),
                        pltpu.VMEM((N_CHUNKS, hd, 2 * hd), BF16)],
        compiler_params=pltpu.CompilerParams(
            dimension_semantics=("arbitrary", "arbitrary"), vmem_limit_bytes=40 * 1024 * 1024),
        name="hgrn",
    )(phg_x, phg_x, phg_x, phg_x, phg_x, phg_m, phg_m,
      lb_f.reshape(1, -1), lb_b.reshape(1, -1), norm_w.reshape(1, -1))


def _hyena_filters(L, w1, b1, w2, b2, w3, freq):
    pos = jnp.arange(L, dtype=F32)
    t = pos / max(L - 1, 1)
    bands = jnp.linspace(1e-4, FILTER_BANDS - 1, FILTER_BANDS, dtype=F32)
    ang = (2.0 * math.pi / L) * pos[:, None] * bands[None, :]
    z = jnp.concatenate([t[:, None], jnp.cos(ang), -jnp.sin(ang)], axis=-1)
    hp = lax.Precision.HIGHEST
    hid = jnp.sin(freq * (jnp.dot(z, w1, precision=hp) + b1))
    hid = jnp.sin(freq * (jnp.dot(hid, w2, precision=hp) + b2))
    filt = jnp.dot(hid, w3, precision=hp).reshape(L, 2, HYENA_ORDER, D_HYENA)
    deltas = jnp.abs(jnp.linspace(math.log(DECAY_TARGET) / SLOW_DECAY_PCT,
                                  math.log(DECAY_TARGET) / FAST_DECAY_PCT, D_HYENA, dtype=F32))
    window = jnp.exp(-t[:, None] * deltas[None, :])
    filt = filt * window[:, None, None, :]
    return filt[:, 0], filt[:, 1]


def kernel(x, meta_tokens, w_in, conv_w, conv_b, filt_w1, filt_b1, filt_w2, filt_b2, filt_w3,
           filt_freq, filt_skip, hyena_norm, lb_fwd, lb_bwd, hgrn_norm, w_out, norm_mix, norm_ffn,
           w_router_group, w_router_expert, w_gate, w_up, w_down, norm_final):
    B, S, D = x.shape
    L = S + N_META
    lbf = jnp.cumsum(jax.nn.softmax(lb_fwd, axis=0), axis=0)[0]
    lbb = jnp.cumsum(jax.nn.softmax(lb_bwd, axis=0), axis=0)[0]

    w_in_b = w_in[0].astype(BF16)
    xf = x.reshape(B * S, D)
    phy_x, phg_x = _inproj(xf, norm_mix[0], w_in_b, tm=512)
    phy_m, phg_m = _inproj(meta_tokens, norm_mix[0], w_in_b, tm=N_META)

    mats = _dft_matrices()
    h_fwd, h_bwd = _hyena_filters(L, filt_w1[0], filt_b1[0], filt_w2[0], filt_b2[0], filt_w3[0],
                                  filt_freq[0])
    hr, hi = _spectra(h_fwd, h_bwd, mats[0], mats[1])
    z_hy = _hyena(phy_x.reshape(B, S, D_HYENA_PROJ), phy_m, conv_w[0], conv_b[0], filt_skip[0],
                  mats, hr, hi).reshape(B * S, D_HYENA)

    y_hg = _hgrn(phg_x.reshape(B, S, 5 * D_HGRN), phg_m, lbf, lbb,
                 hgrn_norm[0]).reshape(B * S, D_HGRN)

    w_r = jnp.concatenate([w_router_group[0], w_router_expert[0].reshape(D, N_EXPERTS),
                           jnp.zeros((D, LANES - N_GROUPS - N_EXPERTS), F32)], axis=1).astype(BF16)
    h1, a2p, ri, rg, cnt = _outproj(z_hy, y_hg, xf, hyena_norm[0], norm_ffn[0],
                                    w_out[0].astype(BF16), w_r, tm=1024)
    out = _moe(h1, a2p, ri, rg, cnt, norm_final, w_gate[0], w_up[0], w_down[0])
    return out.reshape(B, S, D)
```

```python
import functools
import math

import jax
import jax.numpy as jnp
from jax import lax
from jax.experimental import pallas as pl
from jax.experimental.pallas import tpu as pltpu

D_MODEL = 1024
N_META = 16
D_HYENA = 512
D_HGRN = 512
HYENA_ORDER = 2
SHORT_CONV = 3
FILTER_EMB = 33
FILTER_BANDS = 16
DECAY_TARGET = 1e-2
FAST_DECAY_PCT = 0.3
SLOW_DECAY_PCT = 1.5
HGRN_HEAD_DIM = 128
HGRN_HEADS = D_HGRN // HGRN_HEAD_DIM
CHUNK = 64
N_GROUPS = 8
EXPERTS_PER_GROUP = 8
N_EXPERTS = 64
TOP_K = 2
D_EXPERT = 512
D_HYENA_PROJ = 3 * D_HYENA
D_IN_PROJ = D_HYENA_PROJ + 5 * D_HGRN
EPS = 1e-6

F32 = jnp.float32
BF16 = jnp.bfloat16


def _rms(x, gain):
    return x * lax.rsqrt(jnp.mean(x * x, axis=-1, keepdims=True) + EPS) * gain


HI16 = 0xFFFF0000


def _pack_bf16_pairs(x):
    c = x.shape[1] // 2
    bits = lax.bitcast_convert_type(x.astype(BF16).astype(F32), jnp.uint32)
    return (bits[:, :c] >> 16) | (bits[:, c:] & jnp.uint32(HI16))


def _unpack_bf16_pairs(w):
    lo = lax.bitcast_convert_type(w << 16, F32)
    hi = lax.bitcast_convert_type(w & jnp.uint32(HI16), F32)
    return lo, hi


def _inproj_body(x_ref, g_ref, w_ref, hy_ref, hg_ref, *, tn):
    a = _rms(x_ref[...], g_ref[...]).astype(BF16)
    for j in range(D_IN_PROJ // tn):
        acc = jnp.dot(a, w_ref[:, j * tn:(j + 1) * tn], preferred_element_type=F32)
        if j * tn < D_HYENA_PROJ:
            hy_ref[:, j * tn:(j + 1) * tn] = acc.astype(hy_ref.dtype)
        else:
            c0 = j * tn - D_HYENA_PROJ
            hg_ref[:, c0:c0 + tn] = acc.astype(hg_ref.dtype)


def _inproj(x, gain, w_bf16, tm):
    m, d = x.shape
    n_hg = D_IN_PROJ - D_HYENA_PROJ
    return pl.pallas_call(
        functools.partial(_inproj_body, tn=512),
        grid=(m // tm,),
        in_specs=[pl.BlockSpec((tm, d), lambda i: (i, 0)),
                  pl.BlockSpec((1, d), lambda i: (0, 0)),
                  pl.BlockSpec((d, D_IN_PROJ), lambda i: (0, 0))],
        out_specs=[pl.BlockSpec((tm, D_HYENA_PROJ), lambda i: (i, 0)),
                   pl.BlockSpec((tm, n_hg), lambda i: (i, 0))],
        out_shape=[jax.ShapeDtypeStruct((m, D_HYENA_PROJ), BF16),
                   jax.ShapeDtypeStruct((m, n_hg), F32)],
        compiler_params=pltpu.CompilerParams(
            dimension_semantics=("arbitrary",), vmem_limit_bytes=48 * 1024 * 1024),
        name="inproj",
    )(x, gain.reshape(1, d), w_bf16)


LANES = 128
NEG_BIG = -1e30
OP_SUB = 512
ROW_SUB = (D_MODEL // 2) // LANES


def _outproj_body(zhy_ref, yhg_ref, h0_ref, ghy_ref, gffn_ref, wo_ref, wr_ref,
                  h1_ref, a2p_ref, ri_ref, rg_ref, cnt_ref, tri_s, carry_s):
    i = pl.program_id(0)
    tm = h1_ref.shape[0]
    ts = tri_s.shape[0]

    @pl.when(i == 0)
    def _():
        r = lax.broadcasted_iota(jnp.int32, (ts, ts), 0)
        c = lax.broadcasted_iota(jnp.int32, (ts, ts), 1)
        tri_s[...] = (r > c).astype(BF16)
        carry_s[...] = jnp.zeros_like(carry_s)

    lane = lax.broadcasted_iota(jnp.int32, (ts, LANES), 1)
    is_g = lane < N_GROUPS
    carry = carry_s[...]
    for r0 in range(0, tm, ts):
        rows = slice(r0, r0 + ts)
        yhy = _rms(zhy_ref[rows, :], ghy_ref[...]).astype(BF16)
        yhg = yhg_ref[rows, :].astype(BF16)
        acc = jnp.dot(yhy, wo_ref[:D_HYENA, :], preferred_element_type=F32)
        acc = acc + jnp.dot(yhg, wo_ref[D_HYENA:, :], preferred_element_type=F32)
        h1 = h0_ref[rows, :] + acc
        h1_ref[rows, :] = h1
        a2 = _rms(h1, gffn_ref[...])
        a2p_ref[rows] = _pack_bf16_pairs(a2).reshape(ts, ROW_SUB, LANES)
        lg = jnp.dot(a2.astype(BF16), wr_ref[...], preferred_element_type=F32)

        gl = jnp.where(is_g, lg, NEG_BIG)
        gmax = jnp.max(gl, axis=1, keepdims=True)
        gsel = jnp.min(jnp.where(gl == gmax, lane, LANES), axis=1, keepdims=True)
        gden = jnp.sum(jnp.where(is_g, jnp.exp(gl - gmax), 0.0), axis=1, keepdims=True)
        p_group = 1.0 / gden
        in_grp = (lane >= N_GROUPS) & (lane < N_GROUPS + N_EXPERTS) & (
            ((lane - N_GROUPS) >> 3) == gsel)
        el = jnp.where(in_grp, lg, NEG_BIG)
        m1 = jnp.max(el, axis=1, keepdims=True)
        i1 = jnp.min(jnp.where(el == m1, lane, LANES), axis=1, keepdims=True)
        el2 = jnp.where(lane == i1, NEG_BIG, el)
        m2 = jnp.max(el2, axis=1, keepdims=True)
        i2 = jnp.min(jnp.where(el2 == m2, lane, LANES), axis=1, keepdims=True)
        r21 = jnp.exp(m2 - m1)
        gate1 = p_group / (1.0 + r21)
        gate2 = gate1 * r21

        hit1 = lane == i1
        hit2 = lane == i2
        onehot = (hit1 | hit2).astype(BF16)
        pre = jnp.dot(tri_s[...], onehot, preferred_element_type=F32) + carry
        pos1 = jnp.sum(jnp.where(hit1, pre, 0.0), axis=1, keepdims=True).astype(jnp.int32)
        pos2 = jnp.sum(jnp.where(hit2, pre, 0.0), axis=1, keepdims=True).astype(jnp.int32)
        carry = carry + jnp.sum(onehot.astype(F32), axis=0, keepdims=True)

        zero_i = jnp.zeros((ts, LANES), jnp.int32)
        ri = jnp.where(lane == 0, i1 - N_GROUPS,
                       jnp.where(lane == 1, i2 - N_GROUPS,
                                 jnp.where(lane == 2, pos1, jnp.where(lane == 3, pos2, zero_i))))
        ri_ref[:, rows] = jnp.transpose(ri)[:8, :]
        rg_ref[rows, :] = jnp.where(lane == 0, gate1, jnp.where(lane == 1, gate2, 0.0))
    carry_s[...] = carry
    cnt_ref[...] = carry


def _outproj(zhy, yhg, h0, g_hy, g_ffn, wo_bf16, wr_bf16, tm):
    m = h0.shape[0]
    return pl.pallas_call(
        _outproj_body,
        grid=(m // tm,),
        in_specs=[pl.BlockSpec((tm, D_HYENA), lambda i: (i, 0)),
                  pl.BlockSpec((tm, D_HGRN), lambda i: (i, 0)),
                  pl.BlockSpec((tm, D_MODEL), lambda i: (i, 0)),
                  pl.BlockSpec((1, D_HYENA), lambda i: (0, 0)),
                  pl.BlockSpec((1, D_MODEL), lambda i: (0, 0)),
                  pl.BlockSpec((D_MODEL, D_MODEL), lambda i: (0, 0)),
                  pl.BlockSpec((D_MODEL, LANES), lambda i: (0, 0))],
        out_specs=[pl.BlockSpec((tm, D_MODEL), lambda i: (i, 0)),
                   pl.BlockSpec((tm, ROW_SUB, LANES), lambda i: (i, 0, 0)),
                   pl.BlockSpec((8, tm), lambda i: (0, i)),
                   pl.BlockSpec((tm, LANES), lambda i: (i, 0)),
                   pl.BlockSpec((1, LANES), lambda i: (0, 0))],
        out_shape=[jax.ShapeDtypeStruct((m, D_MODEL), F32),
                   jax.ShapeDtypeStruct((m, ROW_SUB, LANES), jnp.uint32),
                   jax.ShapeDtypeStruct((8, m), jnp.int32),
                   jax.ShapeDtypeStruct((m, LANES), F32),
                   jax.ShapeDtypeStruct((1, LANES), F32)],
        scratch_shapes=[pltpu.VMEM((OP_SUB, OP_SUB), BF16), pltpu.VMEM((1, LANES), F32)],
        compiler_params=pltpu.CompilerParams(
            dimension_semantics=("arbitrary",), vmem_limit_bytes=48 * 1024 * 1024),
        name="outproj",
    )(zhy, yhg, h0, g_hy.reshape(1, -1), g_ffn.reshape(1, -1), wo_bf16, wr_bf16)


EXP_TB = 512


def _wait_rows(ref, n_rows, sem):
    pltpu.make_async_copy(ref.at[pl.ds(0, n_rows)], ref.at[pl.ds(n_rows, n_rows)], sem).wait()


def _dispatch_body(lastblk_ref, npad_ref, nused_ref, d0_ref, d1_ref, h1_ref, xb_ref, zero_s, sem_z,
                   sem):
    i = pl.program_id(0)
    tm = d0_ref.shape[0]
    n_blocks = xb_ref.shape[0] // EXP_TB

    @pl.when(i == 0)
    def _():
        zero_s[...] = jnp.zeros_like(zero_s)

        def zero_copy(row0):
            row0 = pl.multiple_of(row0, EXP_TB)
            return pltpu.make_async_copy(zero_s, xb_ref.at[pl.ds(row0, EXP_TB)], sem_z)

        for e in range(N_EXPERTS):
            @pl.when(npad_ref[e] > 0)
            def _():
                zero_copy(lastblk_ref[e]).start()
        for e in range(N_EXPERTS):
            @pl.when(npad_ref[e] > 0)
            def _():
                zero_copy(lastblk_ref[e]).wait()

        def start_tail(b, carry):
            zero_copy(b * EXP_TB).start()
            return carry

        def wait_tail(b, carry):
            zero_copy(b * EXP_TB).wait()
            return carry

        lax.fori_loop(nused_ref[0], n_blocks, start_tail, 0)
        lax.fori_loop(nused_ref[0], n_blocks, wait_tail, 0)

    def issue(t, carry):
        pltpu.make_async_copy(h1_ref.at[t], xb_ref.at[d0_ref[t]], sem).start(priority=0)
        pltpu.make_async_copy(h1_ref.at[t], xb_ref.at[d1_ref[t]], sem).start(priority=1)
        return carry

    lax.fori_loop(0, tm, issue, 0, unroll=16)
    _wait_rows(xb_ref, TOP_K * tm, sem)


def _dispatch(lastblk, npad, n_used, dest, rows, n_slots, tm):
    m = rows.shape[0]
    grid_spec = pltpu.PrefetchScalarGridSpec(
        num_scalar_prefetch=3,
        grid=(m // tm,),
        in_specs=[pl.BlockSpec((tm,), lambda i, lb, npd, nu: (i,), memory_space=pltpu.SMEM),
                  pl.BlockSpec((tm,), lambda i, lb, npd, nu: (i,), memory_space=pltpu.SMEM),
                  pl.BlockSpec((tm, ROW_SUB, LANES), lambda i, lb, npd, nu: (i, 0, 0))],
        out_specs=pl.BlockSpec(memory_space=pl.ANY),
        scratch_shapes=[pltpu.VMEM((EXP_TB, ROW_SUB, LANES), rows.dtype),
                        pltpu.SemaphoreType.DMA(()), pltpu.SemaphoreType.DMA(())],
    )
    return pl.pallas_call(
        _dispatch_body,
        grid_spec=grid_spec,
        out_shape=jax.ShapeDtypeStruct((n_slots, ROW_SUB, LANES), rows.dtype),
        compiler_params=pltpu.CompilerParams(dimension_semantics=("arbitrary",)),
        name="dispatch",
    )(lastblk, npad, n_used, dest[0], dest[1], rows)


def _expert_body(eid_ref, nused_ref, xb_ref, wg_ref, wu_ref, wd_ref, o_ref, wg_s, wu_s, wd_s):
    i = pl.program_id(0)
    half = D_MODEL // 2

    @pl.when(i < nused_ref[0])
    def _():
        prev = eid_ref[jnp.maximum(i - 1, 0)]

        @pl.when((i == 0) | (eid_ref[i] != prev))
        def _():
            wg_s[...] = wg_ref[0].astype(BF16)
            wu_s[...] = wu_ref[0].astype(BF16)
            wd_s[...] = wd_ref[0].astype(BF16)

        lo, hi = _unpack_bf16_pairs(xb_ref[...].reshape(EXP_TB, half))
        lo, hi = lo.astype(BF16), hi.astype(BF16)

        def proj(w_s):
            return (jnp.dot(lo, w_s[:half, :], preferred_element_type=F32)
                    + jnp.dot(hi, w_s[half:, :], preferred_element_type=F32))

        g = proj(wg_s)
        u = proj(wu_s)
        hmid = (g * jax.nn.sigmoid(g) * u).astype(BF16)
        y = _pack_bf16_pairs(jnp.dot(hmid, wd_s[...], preferred_element_type=F32))
        o_ref[...] = y.reshape(EXP_TB, ROW_SUB, LANES)

    @pl.when(i >= nused_ref[0])
    def _():
        o_ref[...] = jnp.zeros_like(o_ref)


def _experts(block_eid, n_used, xb, w_gate, w_up, w_down):
    n_slots = xb.shape[0]

    def blk(i, e, nu):
        return jnp.minimum(i, nu[0] - 1)

    grid_spec = pltpu.PrefetchScalarGridSpec(
        num_scalar_prefetch=2,
        grid=(n_slots // EXP_TB,),
        in_specs=[pl.BlockSpec((EXP_TB, ROW_SUB, LANES), lambda i, e, nu: (blk(i, e, nu), 0, 0)),
                  pl.BlockSpec((1, D_MODEL, D_EXPERT), lambda i, e, nu: (e[blk(i, e, nu)], 0, 0)),
                  pl.BlockSpec((1, D_MODEL, D_EXPERT), lambda i, e, nu: (e[blk(i, e, nu)], 0, 0)),
                  pl.BlockSpec((1, D_EXPERT, D_MODEL), lambda i, e, nu: (e[blk(i, e, nu)], 0, 0))],
        out_specs=pl.BlockSpec((EXP_TB, ROW_SUB, LANES), lambda i, e, nu: (i, 0, 0)),
        scratch_shapes=[pltpu.VMEM((D_MODEL, D_EXPERT), BF16),
                        pltpu.VMEM((D_MODEL, D_EXPERT), BF16),
                        pltpu.VMEM((D_EXPERT, D_MODEL), BF16)],
    )
    return pl.pallas_call(
        _expert_body,
        grid_spec=grid_spec,
        out_shape=jax.ShapeDtypeStruct((n_slots, ROW_SUB, LANES), jnp.uint32),
        compiler_params=pltpu.CompilerParams(
            dimension_semantics=("arbitrary",), vmem_limit_bytes=48 * 1024 * 1024),
        name="experts",
    )(block_eid, n_used, xb, w_gate, w_up, w_down)


def _combine_body(d0_ref, d1_ref, n0_ref, n1_ref, h1_ref, rg_ref, gfin_ref, yb_ref, o_ref, y_s, sem):
    i = pl.program_id(0)
    tm = h1_ref.shape[0]
    slot = i % 2

    def issue(idx_refs, sl):
        def body(t, carry):
            for k in range(TOP_K):
                pltpu.make_async_copy(yb_ref.at[idx_refs[k][t]], y_s.at[sl, k, t],
                                      sem.at[sl]).start(priority=k)
            return carry

        lax.fori_loop(0, tm, body, 0, unroll=16)

    @pl.when(i == 0)
    def _():
        issue((d0_ref, d1_ref), 0)

    @pl.when(i + 1 < pl.num_programs(0))
    def _():
        issue((n0_ref, n1_ref), 1 - slot)

    _wait_rows(yb_ref, TOP_K * tm, sem.at[slot])
    rg = rg_ref[...]
    g1, g2 = rg[:, 0:1], rg[:, 1:2]
    half = D_MODEL // 2
    lo1, hi1 = _unpack_bf16_pairs(y_s[slot, 0].reshape(tm, half))
    lo2, hi2 = _unpack_bf16_pairs(y_s[slot, 1].reshape(tm, half))
    h2_lo = h1_ref[:, :half] + g1 * lo1 + g2 * lo2
    h2_hi = h1_ref[:, half:] + g1 * hi1 + g2 * hi2
    ms = (jnp.sum(h2_lo * h2_lo, axis=-1, keepdims=True)
          + jnp.sum(h2_hi * h2_hi, axis=-1, keepdims=True)) * (1.0 / D_MODEL)
    inv = lax.rsqrt(ms + EPS)
    o_ref[:, :half] = h2_lo * inv * gfin_ref[:, :half]
    o_ref[:, half:] = h2_hi * inv * gfin_ref[:, half:]


def _combine(dest, h1, rg, g_fin, yb, tm):
    m = h1.shape[0]
    last = m // tm - 1
    return pl.pallas_call(
        _combine_body,
        grid=(m // tm,),
        in_specs=[pl.BlockSpec((tm,), lambda i: (i,), memory_space=pltpu.SMEM),
                  pl.BlockSpec((tm,), lambda i: (i,), memory_space=pltpu.SMEM),
                  pl.BlockSpec((tm,), lambda i: (jnp.minimum(i + 1, last),), memory_space=pltpu.SMEM),
                  pl.BlockSpec((tm,), lambda i: (jnp.minimum(i + 1, last),), memory_space=pltpu.SMEM),
                  pl.BlockSpec((tm, D_MODEL), lambda i: (i, 0)),
                  pl.BlockSpec((tm, LANES), lambda i: (i, 0)),
                  pl.BlockSpec((1, D_MODEL), lambda i: (0, 0)),
                  pl.BlockSpec(memory_space=pl.ANY)],
        out_specs=pl.BlockSpec((tm, D_MODEL), lambda i: (i, 0)),
        out_shape=jax.ShapeDtypeStruct((m, D_MODEL), F32),
        scratch_shapes=[pltpu.VMEM((2, TOP_K, tm, ROW_SUB, LANES), yb.dtype),
                        pltpu.SemaphoreType.DMA((2,))],
        compiler_params=pltpu.CompilerParams(dimension_semantics=("arbitrary",)),
        name="combine",
    )(dest[0], dest[1], dest[0], dest[1], h1, rg, g_fin.reshape(1, -1), yb)


def _moe(h1, a2p, ri, rg, cnt, g_fin, w_gate, w_up, w_down):
    m = h1.shape[0]
    n_blocks = TOP_K * m // EXP_TB + N_EXPERTS
    n_slots = n_blocks * EXP_TB
    counts = cnt[0, N_GROUPS:N_GROUPS + N_EXPERTS].astype(jnp.int32)
    padded = (counts + EXP_TB - 1) // EXP_TB * EXP_TB
    pend = jnp.cumsum(padded)
    base = pend - padded
    rt = ri[:2 * TOP_K]
    sel = rt[None, :TOP_K] == jnp.arange(N_EXPERTS, dtype=jnp.int32)[:, None, None]
    dest = jnp.sum(jnp.where(sel, base[:, None, None], 0), axis=0) + rt[TOP_K:]
    blk_start = jnp.arange(n_blocks, dtype=jnp.int32) * EXP_TB
    block_eid = jnp.minimum(jnp.sum(blk_start[:, None] >= pend[None, :], axis=1),
                            N_EXPERTS - 1).astype(jnp.int32)
    n_used = (pend[-1:] // EXP_TB).astype(jnp.int32)
    lastblk = (pend - EXP_TB).astype(jnp.int32)

    xb = _dispatch(lastblk, padded.astype(jnp.int32), n_used, dest, a2p, n_slots, tm=1024)
    yb = _experts(block_eid, n_used, xb, w_gate, w_up, w_down)
    return _combine(dest, h1, rg, g_fin, yb, tm=512)


ND = 2176
NCIRC = 2 * ND
NH = ND // 2
SEQ = 2048
L_TOT = SEQ + N_META
LH = L_TOT // 2
LHP = 1152
LHR = 1040
HY_CT = 256
HY_FB = NH // 2
HY_IB = ((0, 528), (528, 512))
HY_CB = LH // 3
HALO = 8


def _dft_matrices():
    a = 2 * jnp.arange(NH, dtype=jnp.int32) + 1
    t_hi = jnp.arange(LHP // LANES, dtype=jnp.int32) * LANES
    t_lo = jnp.arange(LANES, dtype=jnp.int32)
    valid = ((t_hi[:, None] + t_lo[None, :]) < LH).reshape(1, LHP)

    def cos_sin(m, denom):
        ang = m.astype(F32) * (math.pi / denom)
        return jnp.cos(ang), jnp.sin(ang)

    c_hi, s_hi = cos_sin((a[:, None] * ((4 * t_hi) % (4 * NCIRC))[None, :]) % (4 * NCIRC), 2 * NCIRC)
    c_hi, s_hi = c_hi[:, :, None], s_hi[:, :, None]
    out = []
    for c in (1, 3):
        c_lo, s_lo = cos_sin((a[:, None] * (4 * t_lo + c)[None, :]) % (4 * NCIRC), 2 * NCIRC)
        c_lo, s_lo = c_lo[:, None, :], s_lo[:, None, :]
        cos_phi = jnp.where(valid, (c_hi * c_lo - s_hi * s_lo).reshape(NH, LHP), 0.0)
        sin_phi = jnp.where(valid, (s_hi * c_lo + c_hi * s_lo).reshape(NH, LHP), 0.0)
        out.append(jnp.concatenate([cos_phi, sin_phi if c == 1 else -sin_phi], axis=0).astype(BF16))
    ce, co = out
    return ce, co, ce.T, co.T


def _forward_half_spectrum(ce_ref, co_ref, xe, xo, j0):
    a1 = jnp.dot(ce_ref[j0:j0 + HY_FB, :], xe, preferred_element_type=F32)
    a2 = jnp.dot(ce_ref[NH + j0:NH + j0 + HY_FB, :], xe, preferred_element_type=F32)
    b1 = jnp.dot(co_ref[j0:j0 + HY_FB, :], xo, preferred_element_type=F32)
    b2 = jnp.dot(co_ref[NH + j0:NH + j0 + HY_FB, :], xo, preferred_element_type=F32)
    return a1 + b1, a2 - b2, a2 + b2, a1 - b1


def _spectra_body(s_ref, d_ref, ce_ref, co_ref, rot_ref, hr_ref, hi_ref,
                  stage_s, se_s, so_s, de_s, do_s):
    ct = hr_ref.shape[-1]
    nl = ct // LANES
    for x_ref, xe_s, xo_s in ((s_ref, se_s, so_s), (d_ref, de_s, do_s)):
        for j in range(nl):
            stage_s[j] = x_ref[:, j * LANES:(j + 1) * LANES]
        for par, dst in ((0, xe_s), (1, xo_s)):
            for r0 in range(0, LH, HY_CB):
                dst[r0:r0 + HY_CB, :] = jnp.concatenate(
                    [stage_s.at[j][pl.ds(par + 2 * r0, HY_CB, stride=2), :] for j in range(nl)],
                    axis=1).astype(BF16)
            dst[LH:LHP, :] = jnp.zeros((LHP - LH, ct), BF16)
    for j0 in range(0, NH, HY_FB):
        ar1, _, ar2, _ = _forward_half_spectrum(ce_ref, co_ref, se_s[...], so_s[...], j0)
        _, q1, _, q2 = _forward_half_spectrum(ce_ref, co_ref, de_s[...], do_s[...], j0)
        for r0, ar, q in ((j0, ar1, q1), (NH + j0, ar2, q2)):
            c = rot_ref[r0:r0 + HY_FB, 0:1]
            s = rot_ref[r0:r0 + HY_FB, 1:2]
            hr_ref[0, r0:r0 + HY_FB, :] = c * ar + s * q
            hi_ref[0, r0:r0 + HY_FB, :] = s * ar - c * q


def _spectra(h_fwd, h_bwd, ce, co):
    hbs = jnp.concatenate([h_bwd[1:], jnp.zeros_like(h_bwd[:1])], axis=0)
    s2d = (h_fwd + hbs).reshape(L_TOT, HYENA_ORDER * D_HYENA)
    d2d = (h_fwd - hbs).reshape(L_TOT, HYENA_ORDER * D_HYENA)
    j = jnp.arange(NH, dtype=F32)
    k = jnp.concatenate([j, ND - 1 - j])
    half = (2 * k + 1) * (math.pi / (2 * NCIRC))
    scale = 2.0 / NCIRC
    rot = jnp.stack([jnp.cos(half) * scale, jnp.sin(half) * scale], axis=1)
    rot = jnp.pad(rot, ((0, 0), (0, LANES - 2)))
    n_ord, ct = HYENA_ORDER, HY_CT
    nct = D_HYENA // ct
    col = pl.BlockSpec((L_TOT, ct), lambda o, c: (0, o * nct + c))
    mat = pl.BlockSpec((ND, LHP), lambda o, c: (0, 0))
    out = pl.BlockSpec((1, ND, ct), lambda o, c: (o, 0, c))
    half_bf16 = pltpu.VMEM((LHP, ct), BF16)
    return pl.pallas_call(
        _spectra_body,
        grid=(n_ord, nct),
        in_specs=[col, col, mat, mat, pl.BlockSpec((ND, LANES), lambda o, c: (0, 0))],
        out_specs=[out, out],
        out_shape=[jax.ShapeDtypeStruct((n_ord, ND, D_HYENA), F32)] * 2,
        scratch_shapes=[pltpu.VMEM((ct // LANES, L_TOT, LANES), F32),
                        half_bf16, half_bf16, half_bf16, half_bf16],
        compiler_params=pltpu.CompilerParams(
            dimension_semantics=("arbitrary", "arbitrary"), vmem_limit_bytes=48 * 1024 * 1024),
        name="spectra",
    )(s2d, d2d, ce, co, rot)


def _hyena_body(pv_ref, px1_ref, px2_ref, mv_ref, mx1_ref, mx2_ref,
                wv_ref, wx1_ref, wx2_ref, bv_ref, bx1_ref, bx2_ref, skip_ref,
                ce_ref, co_ref, cet_ref, cot_ref, hr_ref, hi_ref, o_ref,
                stage_s, ze_s, zo_s, g1e_s, g1o_s, g2e_s, g2o_s, zbe_s, zbo_s, we_s, wo_s):
    ct = o_ref.shape[-1]
    nl = ct // LANES

    def strided(off, r0, rows):
        return jnp.concatenate(
            [stage_s.at[j][pl.ds(HALO + off + 2 * r0, rows, stride=2), :] for j in range(nl)], axis=1)

    def short_conv(p_ref, m_ref, w_ref, b_ref, dst_e, dst_o):
        for j in range(nl):
            ln = slice(j * LANES, (j + 1) * LANES)
            stage_s[j, 0:HALO, :] = jnp.zeros((HALO, LANES), F32)
            stage_s[j, HALO:HALO + N_META, :] = m_ref[:, ln].astype(F32)
            stage_s[j, HALO + N_META:HALO + L_TOT, :] = p_ref[0, :, ln].astype(F32)
            stage_s[j, HALO + L_TOT:HALO + L_TOT + HALO, :] = jnp.zeros((HALO, LANES), F32)
        w = w_ref[...]
        b = b_ref[...]
        for r0 in range(0, LH, HY_CB):
            sm1, s0, s1, s2 = (strided(off, r0, HY_CB) for off in (-1, 0, 1, 2))
            dst_e[r0:r0 + HY_CB, :] = b + sm1 * w[0:1] + s0 * w[1:2] + s1 * w[2:3]
            dst_o[r0:r0 + HY_CB, :] = b + s0 * w[0:1] + s1 * w[1:2] + s2 * w[2:3]
        dst_e[LH:LHR, :] = jnp.zeros((LHR - LH, ct), F32)
        dst_o[LH:LHR, :] = jnp.zeros((LHR - LH, ct), F32)

    short_conv(pv_ref, mv_ref, wv_ref, bv_ref, ze_s, zo_s)
    short_conv(px1_ref, mx1_ref, wx1_ref, bx1_ref, g1e_s, g1o_s)
    short_conv(px2_ref, mx2_ref, wx2_ref, bx2_ref, g2e_s, g2o_s)
    zbe_s[LH:LHP, :] = jnp.zeros((LHP - LH, ct), BF16)
    zbo_s[LH:LHP, :] = jnp.zeros((LHP - LH, ct), BF16)

    for n, (ge_s, go_s) in enumerate(((g1e_s, g1o_s), (g2e_s, g2o_s))):
        zbe_s[0:LH, :] = ze_s[0:LH, :].astype(BF16)
        zbo_s[0:LH, :] = zo_s[0:LH, :].astype(BF16)
        for j0 in range(0, NH, HY_FB):
            p1, q1, p2, q2 = _forward_half_spectrum(ce_ref, co_ref, zbe_s[...], zbo_s[...], j0)
            h1r = hr_ref[n, j0:j0 + HY_FB, :]
            h1i = hi_ref[n, j0:j0 + HY_FB, :]
            h2r = hr_ref[n, NH + j0:NH + j0 + HY_FB, :]
            h2i = hi_ref[n, NH + j0:NH + j0 + HY_FB, :]
            yr1 = p1 * h1r + q1 * h1i
            ny1 = q1 * h1r - p1 * h1i
            yr2 = p2 * h2r + q2 * h2i
            ny2 = q2 * h2r - p2 * h2i
            we_s[j0:j0 + HY_FB, :] = (yr1 + ny2).astype(BF16)
            we_s[NH + j0:NH + j0 + HY_FB, :] = (yr2 + ny1).astype(BF16)
            wo_s[j0:j0 + HY_FB, :] = (yr1 - ny2).astype(BF16)
            wo_s[NH + j0:NH + j0 + HY_FB, :] = (yr2 - ny1).astype(BF16)
        skip = skip_ref[n:n + 1, :]
        for parity, (ct_ref, w_s, z_s, g_s) in enumerate(((cet_ref, we_s, ze_s, ge_s),
                                                          (cot_ref, wo_s, zo_s, go_s))):
            for r0, rb in HY_IB:
                y = jnp.dot(ct_ref[r0:r0 + rb, :], w_s[...], preferred_element_type=F32)
                znew = g_s[r0:r0 + rb, :] * (y + z_s[r0:r0 + rb, :] * skip)
                if n == 0:
                    z_s[r0:r0 + rb, :] = znew
                else:
                    n_valid = min(r0 + rb, LH) - r0
                    for j in range(nl):
                        stage_s.at[j][pl.ds(HALO + parity + 2 * r0, n_valid, stride=2), :] = (
                            znew[0:n_valid, j * LANES:(j + 1) * LANES])
    for j in range(nl):
        o_ref[0, :, j * LANES:(j + 1) * LANES] = stage_s[j, HALO + N_META:HALO + L_TOT, :]


def _hyena(p_hy, pm_hy, conv_w, conv_b, skip, mats, hr, hi):
    bsz = p_hy.shape[0]
    nct = D_HYENA // HY_CT
    ct = HY_CT
    ce, co, cet, cot = mats

    def part(j):
        return [pl.BlockSpec((1, SEQ, ct), lambda c, b, j=j: (b, 0, j * nct + c))]

    def mpart(j):
        return [pl.BlockSpec((N_META, ct), lambda c, b, j=j: (0, j * nct + c))]

    def wpart(rows, j):
        return [pl.BlockSpec((rows, ct), lambda c, b, j=j: (0, j * nct + c))]

    spec = pl.BlockSpec((HYENA_ORDER, ND, ct), lambda c, b: (0, 0, c), pipeline_mode=pl.Buffered(1))
    in_specs = (part(0) + part(1) + part(2) + mpart(0) + mpart(1) + mpart(2)
                + wpart(3, 0) + wpart(3, 1) + wpart(3, 2)
                + wpart(1, 0) + wpart(1, 1) + wpart(1, 2)
                + [pl.BlockSpec((HYENA_ORDER, ct), lambda c, b: (0, c)),
                   pl.BlockSpec((ND, LHP), lambda c, b: (0, 0)),
                   pl.BlockSpec((ND, LHP), lambda c, b: (0, 0)),
                   pl.BlockSpec((LHP, ND), lambda c, b: (0, 0)),
                   pl.BlockSpec((LHP, ND), lambda c, b: (0, 0)),
                   spec, spec])
    cb = conv_b.reshape(1, -1)
    f32_half = pltpu.VMEM((LHR, ct), F32)
    return pl.pallas_call(
        _hyena_body,
        grid=(nct, bsz),
        in_specs=in_specs,
        out_specs=pl.BlockSpec((1, SEQ, ct), lambda c, b: (b, 0, c)),
        out_shape=jax.ShapeDtypeStruct((bsz, SEQ, D_HYENA), F32),
        scratch_shapes=[pltpu.VMEM((ct // LANES, L_TOT + 2 * HALO, LANES), F32),
                        f32_half, f32_half, f32_half, f32_half, f32_half, f32_half,
                        pltpu.VMEM((LHP, ct), BF16), pltpu.VMEM((LHP, ct), BF16),
                        pltpu.VMEM((ND, ct), BF16), pltpu.VMEM((ND, ct), BF16)],
        compiler_params=pltpu.CompilerParams(
            dimension_semantics=("arbitrary", "arbitrary"), vmem_limit_bytes=60 * 1024 * 1024),
        name="hyena",
    )(p_hy, p_hy, p_hy, pm_hy, pm_hy, pm_hy, conv_w, conv_w, conv_w, cb, cb, cb, skip,
      ce, co, cet, cot, hr, hi)


N_CHUNKS = SEQ // CHUNK
NT_DIMS = (((1,), (1,)), ((), ()))
TN_DIMS = (((0,), (0,)), ((), ()))
MID_F = CHUNK // 2
MID_B = CHUNK // 2 - 1


HG_G = 8
HG_ROWS = HG_G * CHUNK


def _split2(x):
    hi = x.astype(BF16)
    lo = (x - hi.astype(F32)).astype(BF16)
    return hi, lo


def _chunk_prefix_matrix():
    r = lax.broadcasted_iota(jnp.int32, (HG_ROWS, HG_ROWS), 0)
    c = lax.broadcasted_iota(jnp.int32, (HG_ROWS, HG_ROWS), 1)
    return (((r // CHUNK) == (c // CHUNK)) & (c <= r)).astype(BF16)


def _chunk_rows(x, row):
    return jnp.concatenate(
        [jnp.broadcast_to(x[g * CHUNK + row:g * CHUNK + row + 1, :], (CHUNK, x.shape[1]))
         for g in range(x.shape[0] // CHUNK)], axis=0)


def _hgrn_body(q_ref, ff_ref, fb_ref, i_ref, g_ref, mff_ref, mi_ref,
               lbf_ref, lbb_ref, nw_ref, o_ref,
               tri_s, qe_s, sc_s, ut_s, dec_s, st_s):
    hd = HGRN_HEAD_DIM
    row = lax.broadcasted_iota(jnp.int32, (CHUNK, CHUNK), 0)
    col = lax.broadcasted_iota(jnp.int32, (CHUNK, CHUNK), 1)
    lower = row >= col
    upper = col >= row
    lbf = lbf_ref[...]
    lbb = lbb_ref[...]

    @pl.when((pl.program_id(0) == 0) & (pl.program_id(1) == 0))
    def _():
        tri_s[...] = _chunk_prefix_matrix()

    def forget(logit, lb):
        f = lb + (1.0 - lb) * jax.nn.sigmoid(logit)
        return 1.0 - f, jnp.log(f)

    def prefix_sums(lf):
        s = jnp.dot(tri_s[...], jnp.concatenate(_split2(lf), axis=1), preferred_element_type=F32)
        return s[:, :hd] + s[:, hd:]

    k_m, lf_m = forget(mff_ref[...], lbf)
    pad = jnp.zeros((CHUNK - N_META, hd), F32)
    lf_m = jnp.concatenate([pad, lf_m] * HG_G, axis=0)
    b_m = prefix_sums(lf_m)[0:CHUNK]
    kl_m = jnp.concatenate([pad, k_m], axis=0) * jnp.exp(b_m[CHUNK - 1:CHUNK] - b_m)
    v_m = jnp.concatenate([pad, mi_ref[...]], axis=0).astype(BF16)
    st_meta = lax.dot_general(v_m, kl_m.astype(BF16), TN_DIMS, preferred_element_type=F32)

    def phase_a(j, carry):
        r0 = pl.multiple_of(j * HG_ROWS, HG_ROWS)
        rows = pl.ds(r0, HG_ROWS)
        qv = jax.nn.silu(q_ref[0, rows, :])
        vb = i_ref[0, rows, :].astype(BF16)
        k_f, lf_f = forget(ff_ref[0, rows, :], lbf)
        k_b, lf_b = forget(fb_ref[0, rows, :], lbb)
        b_f = prefix_sums(lf_f)
        p_b = prefix_sums(lf_b)
        bmid_f = _chunk_rows(b_f, MID_F)
        blast_f = _chunk_rows(b_f, CHUNK - 1)
        tot_b = _chunk_rows(p_b, CHUNK - 1)
        c_b = tot_b - p_b + lf_b
        cmid_b = _chunk_rows(c_b, MID_B)
        d_f = b_f - bmid_f
        d_b = c_b - cmid_b
        e_f = jnp.exp(d_f)
        e_b = jnp.exp(d_b)
        qs_f = qv * e_f
        ks_f = k_f / e_f
        qs_b = qv * e_b
        ks_b = k_b / e_b
        qs_fb, ks_fb = qs_f.astype(BF16), ks_f.astype(BF16)
        qs_bb, ks_bb = qs_b.astype(BF16), ks_b.astype(BF16)
        for g in range(HG_G):
            sl = slice(g * CHUNK, (g + 1) * CHUNK)
            r1 = slice(g * CHUNK, g * CHUNK + 1)
            n = j * HG_G + g
            rows_g = pl.ds(pl.multiple_of(r0 + g * CHUNK, CHUNK), CHUNK)
            sc_f = lax.dot_general(qs_fb[sl], ks_fb[sl], NT_DIMS, preferred_element_type=F32)
            sc_b = lax.dot_general(qs_bb[sl], ks_bb[sl], NT_DIMS, preferred_element_type=F32)
            sc_s[n] = (jnp.where(lower, sc_f, 0.0) + jnp.where(upper, sc_b, 0.0)).astype(BF16)
            em_f = jnp.exp(bmid_f[r1])
            el_f = jnp.exp(blast_f[r1] - bmid_f[r1])
            em_b = jnp.exp(cmid_b[r1])
            el_b = jnp.exp(tot_b[r1] - cmid_b[r1])
            qe_s[rows_g, :] = jnp.concatenate(
                [qs_f[sl] * em_f, qs_b[sl] * em_b], axis=1).astype(BF16)
            kl = jnp.concatenate([ks_f[sl] * el_f, ks_b[sl] * el_b], axis=1).astype(BF16)
            ut_s[n] = lax.dot_general(vb[sl], kl, TN_DIMS, preferred_element_type=F32)
            dec_s[n] = jnp.concatenate([jnp.exp(blast_f[r1]), jnp.exp(tot_b[r1])], axis=1)
        return carry

    lax.fori_loop(0, N_CHUNKS // HG_G, phase_a, 0, unroll=True)

    st_f = st_meta
    st_b = jnp.zeros((hd, hd), F32)
    for n in range(N_CHUNKS):
        st_s[n, :, 0:hd] = st_f.astype(BF16)
        st_f = dec_s[n, :, 0:hd] * st_f + ut_s[n, :, 0:hd]
        m = N_CHUNKS - 1 - n
        st_s[m, :, hd:2 * hd] = st_b.astype(BF16)
        st_b = dec_s[m, :, hd:2 * hd] * st_b + ut_s[m, :, hd:2 * hd]

    nw = nw_ref[...]

    def phase_c(j, carry):
        r0 = pl.multiple_of(j * HG_ROWS, HG_ROWS)
        rows = pl.ds(r0, HG_ROWS)
        vb = i_ref[0, rows, :].astype(BF16)
        outs = []
        for g in range(HG_G):
            sl = slice(g * CHUNK, (g + 1) * CHUNK)
            n = j * HG_G + g
            rows_g = pl.ds(pl.multiple_of(r0 + g * CHUNK, CHUNK), CHUNK)
            o = jnp.dot(sc_s[n], vb[sl], preferred_element_type=F32)
            outs.append(o + lax.dot_general(qe_s[rows_g, :], st_s[n], NT_DIMS,
                                            preferred_element_type=F32))
        o = jnp.concatenate(outs, axis=0)
        o = o * lax.rsqrt(jnp.mean(o * o, axis=-1, keepdims=True) + EPS)
        o_ref[0, rows, :] = o * nw * jax.nn.silu(g_ref[0, rows, :])
        return carry

    lax.fori_loop(0, N_CHUNKS // HG_G, phase_c, 0, unroll=True)


def _hgrn(phg_x, phg_m, lb_f, lb_b, norm_w):
    bsz = phg_x.shape[0]
    hd = HGRN_HEAD_DIM
    nh = HGRN_HEADS

    def part(j):
        return pl.BlockSpec((1, SEQ, hd), lambda b, h, j=j: (b, 0, j * nh + h))

    def mpart(j):
        return pl.BlockSpec((N_META, hd), lambda b, h, j=j: (0, j * nh + h))

    vec = pl.BlockSpec((1, hd), lambda b, h: (0, h))
    return pl.pallas_call(
        _hgrn_body,
        grid=(bsz, nh),
        in_specs=[part(0), part(1), part(2), part(3), part(4), mpart(1), mpart(3), vec, vec, vec],
        out_specs=pl.BlockSpec((1, SEQ, hd), lambda b, h: (b, 0, h)),
        out_shape=jax.ShapeDtypeStruct((bsz, SEQ, D_HGRN), F32),
        scratch_shapes=[pltpu.VMEM((HG_ROWS, HG_ROWS), BF16),
                        pltpu.VMEM((SEQ, 2 * hd), BF16),
                        pltpu.VMEM((N_CHUNKS, CHUNK, CHUNK), BF16),
                        pltpu.VMEM((N_CHUNKS, hd, 2 * hd), F32),
                        pltpu.VMEM((N_CHUNKS, 1, 2 * hd), F32),
                        pltpu.VMEM((N_CHUNKS, hd, 2 * hd), BF16)],
        compiler_params=pltpu.CompilerParams(
            dimension_semantics=("arbitrary", "arbitrary"), vmem_limit_bytes=40 * 1024 * 1024),
        name="hgrn",
    )(phg_x, phg_x, phg_x, phg_x, phg_x, phg_m, phg_m,
      lb_f.reshape(1, -1), lb_b.reshape(1, -1), norm_w.reshape(1, -1))


def _hyena_filters(L, w1, b1, w2, b2, w3, freq):
    pos = jnp.arange(L, dtype=F32)
    t = pos / max(L - 1, 1)
    bands = jnp.linspace(1e-4, FILTER_BANDS - 1, FILTER_BANDS, dtype=F32)
    ang = (2.0 * math.pi / L) * pos[:, None] * bands[None, :]
    z = jnp.concatenate([t[:, None], jnp.cos(ang), -jnp.sin(ang)], axis=-1)
    hp = lax.Precision.HIGHEST
    hid = jnp.sin(freq * (jnp.dot(z, w1, precision=hp) + b1))
    hid = jnp.sin(freq * (jnp.dot(hid, w2, precision=hp) + b2))
    filt = jnp.dot(hid, w3, precision=hp).reshape(L, 2, HYENA_ORDER, D_HYENA)
    deltas = jnp.abs(jnp.linspace(math.log(DECAY_TARGET) / SLOW_DECAY_PCT,
                                  math.log(DECAY_TARGET) / FAST_DECAY_PCT, D_HYENA, dtype=F32))
    window = jnp.exp(-t[:, None] * deltas[None, :])
    filt = filt * window[:, None, None, :]
    return filt[:, 0], filt[:, 1]


def kernel(x, meta_tokens, w_in, conv_w, conv_b, filt_w1, filt_b1, filt_w2, filt_b2, filt_w3,
           filt_freq, filt_skip, hyena_norm, lb_fwd, lb_bwd, hgrn_norm, w_out, norm_mix, norm_ffn,
           w_router_group, w_router_expert, w_gate, w_up, w_down, norm_final):
    B, S, D = x.shape
    L = S + N_META
    lbf = jnp.cumsum(jax.nn.softmax(lb_fwd, axis=0), axis=0)[0]
    lbb = jnp.cumsum(jax.nn.softmax(lb_bwd, axis=0), axis=0)[0]

    w_in_b = w_in[0].astype(BF16)
    xf = x.reshape(B * S, D)
    phy_x, phg_x = _inproj(xf, norm_mix[0], w_in_b, tm=512)
    phy_m, phg_m = _inproj(meta_tokens, norm_mix[0], w_in_b, tm=N_META)

    mats = _dft_matrices()
    h_fwd, h_bwd = _hyena_filters(L, filt_w1[0], filt_b1[0], filt_w2[0], filt_b2[0], filt_w3[0],
                                  filt_freq[0])
    hr, hi = _spectra(h_fwd, h_bwd, mats[0], mats[1])
    z_hy = _hyena(phy_x.reshape(B, S, D_HYENA_PROJ), phy_m, conv_w[0], conv_b[0], filt_skip[0],
                  mats, hr, hi).reshape(B * S, D_HYENA)

    y_hg = _hgrn(phg_x.reshape(B, S, 5 * D_HGRN), phg_m, lbf, lbb,
                 hgrn_norm[0]).reshape(B * S, D_HGRN)

    w_r = jnp.concatenate([w_router_group[0], w_router_expert[0].reshape(D, N_EXPERTS),
                           jnp.zeros((D, LANES - N_GROUPS - N_EXPERTS), F32)], axis=1).astype(BF16)
    h1, a2p, ri, rg, cnt = _outproj(z_hy, y_hg, xf, hyena_norm[0], norm_ffn[0],
                                    w_out[0].astype(BF16), w_r, tm=1024)
    out = _moe(h1, a2p, ri, rg, cnt, norm_final, w_gate[0], w_up[0], w_down[0])
    return out.reshape(B, S, D)
```

```python
import functools
import math

import jax
import jax.numpy as jnp
from jax import lax
from jax.experimental import pallas as pl
from jax.experimental.pallas import tpu as pltpu

D_MODEL = 1024
N_META = 16
D_HYENA = 512
D_HGRN = 512
HYENA_ORDER = 2
SHORT_CONV = 3
FILTER_EMB = 33
FILTER_BANDS = 16
DECAY_TARGET = 1e-2
FAST_DECAY_PCT = 0.3
SLOW_DECAY_PCT = 1.5
HGRN_HEAD_DIM = 128
HGRN_HEADS = D_HGRN // HGRN_HEAD_DIM
CHUNK = 64
N_GROUPS = 8
EXPERTS_PER_GROUP = 8
N_EXPERTS = 64
TOP_K = 2
D_EXPERT = 512
D_HYENA_PROJ = 3 * D_HYENA
D_IN_PROJ = D_HYENA_PROJ + 5 * D_HGRN
EPS = 1e-6

F32 = jnp.float32
BF16 = jnp.bfloat16


def _rms(x, gain):
    return x * lax.rsqrt(jnp.mean(x * x, axis=-1, keepdims=True) + EPS) * gain


HI16 = 0xFFFF0000


def _pack_bf16_pairs(x):
    c = x.shape[1] // 2
    bits = lax.bitcast_convert_type(x.astype(BF16).astype(F32), jnp.uint32)
    return (bits[:, :c] >> 16) | (bits[:, c:] & jnp.uint32(HI16))


def _unpack_bf16_pairs(w):
    lo = lax.bitcast_convert_type(w << 16, F32)
    hi = lax.bitcast_convert_type(w & jnp.uint32(HI16), F32)
    return lo, hi


def _inproj_body(x_ref, g_ref, w_ref, hy_ref, hg_ref, *, tn):
    a = _rms(x_ref[...], g_ref[...]).astype(BF16)
    for j in range(D_IN_PROJ // tn):
        acc = jnp.dot(a, w_ref[:, j * tn:(j + 1) * tn].astype(BF16), preferred_element_type=F32)
        if j * tn < D_HYENA_PROJ:
            hy_ref[:, j * tn:(j + 1) * tn] = acc.astype(hy_ref.dtype)
        else:
            c0 = j * tn - D_HYENA_PROJ
            hg_ref[:, c0:c0 + tn] = acc.astype(hg_ref.dtype)


def _inproj(x, gain, w, tm):
    m, d = x.shape
    n_hg = D_IN_PROJ - D_HYENA_PROJ
    return pl.pallas_call(
        functools.partial(_inproj_body, tn=512),
        grid=(m // tm,),
        in_specs=[pl.BlockSpec((tm, d), lambda i: (i, 0)),
                  pl.BlockSpec((1, d), lambda i: (0, 0)),
                  pl.BlockSpec((d, D_IN_PROJ), lambda i: (0, 0))],
        out_specs=[pl.BlockSpec((tm, D_HYENA_PROJ), lambda i: (i, 0)),
                   pl.BlockSpec((tm, n_hg), lambda i: (i, 0))],
        out_shape=[jax.ShapeDtypeStruct((m, D_HYENA_PROJ), BF16),
                   jax.ShapeDtypeStruct((m, n_hg), F32)],
        compiler_params=pltpu.CompilerParams(
            dimension_semantics=("arbitrary",), vmem_limit_bytes=48 * 1024 * 1024),
        name="inproj",
    )(x, gain.reshape(1, d), w)


LANES = 128
NEG_BIG = -1e30
OP_SUB = 512
ROW_SUB = (D_MODEL // 2) // LANES


def _outproj_body(zhy_ref, yhg_ref, h0_ref, ghy_ref, gffn_ref, wo_ref, wr_ref,
                  h1_ref, a2p_ref, ri_ref, rg_ref, cnt_ref, tri_s, carry_s):
    i = pl.program_id(0)
    tm = h1_ref.shape[0]
    ts = tri_s.shape[0]

    @pl.when(i == 0)
    def _():
        r = lax.broadcasted_iota(jnp.int32, (ts, ts), 0)
        c = lax.broadcasted_iota(jnp.int32, (ts, ts), 1)
        tri_s[...] = (r > c).astype(BF16)
        carry_s[...] = jnp.zeros_like(carry_s)

    lane = lax.broadcasted_iota(jnp.int32, (ts, LANES), 1)
    is_g = lane < N_GROUPS
    carry = carry_s[...]
    for r0 in range(0, tm, ts):
        rows = slice(r0, r0 + ts)
        yhy = _rms(zhy_ref[rows, :], ghy_ref[...]).astype(BF16)
        yhg = yhg_ref[rows, :].astype(BF16)
        acc = jnp.dot(yhy, wo_ref[:D_HYENA, :], preferred_element_type=F32)
        acc = acc + jnp.dot(yhg, wo_ref[D_HYENA:, :], preferred_element_type=F32)
        h1 = h0_ref[rows, :] + acc
        h1_ref[rows, :] = h1
        a2 = _rms(h1, gffn_ref[...])
        a2p_ref[rows] = _pack_bf16_pairs(a2).reshape(ts, ROW_SUB, LANES)
        lg = jnp.dot(a2.astype(BF16), wr_ref[...], preferred_element_type=F32)

        gl = jnp.where(is_g, lg, NEG_BIG)
        gmax = jnp.max(gl, axis=1, keepdims=True)
        gsel = jnp.min(jnp.where(gl == gmax, lane, LANES), axis=1, keepdims=True)
        gden = jnp.sum(jnp.where(is_g, jnp.exp(gl - gmax), 0.0), axis=1, keepdims=True)
        p_group = 1.0 / gden
        in_grp = (lane >= N_GROUPS) & (lane < N_GROUPS + N_EXPERTS) & (
            ((lane - N_GROUPS) >> 3) == gsel)
        el = jnp.where(in_grp, lg, NEG_BIG)
        m1 = jnp.max(el, axis=1, keepdims=True)
        i1 = jnp.min(jnp.where(el == m1, lane, LANES), axis=1, keepdims=True)
        el2 = jnp.where(lane == i1, NEG_BIG, el)
        m2 = jnp.max(el2, axis=1, keepdims=True)
        i2 = jnp.min(jnp.where(el2 == m2, lane, LANES), axis=1, keepdims=True)
        r21 = jnp.exp(m2 - m1)
        gate1 = p_group / (1.0 + r21)
        gate2 = gate1 * r21

        hit1 = lane == i1
        hit2 = lane == i2
        onehot = (hit1 | hit2).astype(BF16)
        pre = jnp.dot(tri_s[...], onehot, preferred_element_type=F32) + carry
        pos1 = jnp.sum(jnp.where(hit1, pre, 0.0), axis=1, keepdims=True).astype(jnp.int32)
        pos2 = jnp.sum(jnp.where(hit2, pre, 0.0), axis=1, keepdims=True).astype(jnp.int32)
        carry = carry + jnp.sum(onehot.astype(F32), axis=0, keepdims=True)

        zero_i = jnp.zeros((ts, LANES), jnp.int32)
        ri = jnp.where(lane == 0, i1 - N_GROUPS,
                       jnp.where(lane == 1, i2 - N_GROUPS,
                                 jnp.where(lane == 2, pos1, jnp.where(lane == 3, pos2, zero_i))))
        ri_ref[:, rows] = jnp.transpose(ri)[:8, :]
        rg_ref[rows, :] = jnp.where(lane == 0, gate1, jnp.where(lane == 1, gate2, 0.0))
    carry_s[...] = carry
    cnt_ref[...] = carry


def _outproj(zhy, yhg, h0, g_hy, g_ffn, wo_bf16, wr_bf16, tm):
    m = h0.shape[0]
    return pl.pallas_call(
        _outproj_body,
        grid=(m // tm,),
        in_specs=[pl.BlockSpec((tm, D_HYENA), lambda i: (i, 0)),
                  pl.BlockSpec((tm, D_HGRN), lambda i: (i, 0)),
                  pl.BlockSpec((tm, D_MODEL), lambda i: (i, 0)),
                  pl.BlockSpec((1, D_HYENA), lambda i: (0, 0)),
                  pl.BlockSpec((1, D_MODEL), lambda i: (0, 0)),
                  pl.BlockSpec((D_MODEL, D_MODEL), lambda i: (0, 0)),
                  pl.BlockSpec((D_MODEL, LANES), lambda i: (0, 0))],
        out_specs=[pl.BlockSpec((tm, D_MODEL), lambda i: (i, 0)),
                   pl.BlockSpec((tm, ROW_SUB, LANES), lambda i: (i, 0, 0)),
                   pl.BlockSpec((8, tm), lambda i: (0, i)),
                   pl.BlockSpec((tm, LANES), lambda i: (i, 0)),
                   pl.BlockSpec((1, LANES), lambda i: (0, 0))],
        out_shape=[jax.ShapeDtypeStruct((m, D_MODEL), F32),
                   jax.ShapeDtypeStruct((m, ROW_SUB, LANES), jnp.uint32),
                   jax.ShapeDtypeStruct((8, m), jnp.int32),
                   jax.ShapeDtypeStruct((m, LANES), F32),
                   jax.ShapeDtypeStruct((1, LANES), F32)],
        scratch_shapes=[pltpu.VMEM((OP_SUB, OP_SUB), BF16), pltpu.VMEM((1, LANES), F32)],
        compiler_params=pltpu.CompilerParams(
            dimension_semantics=("arbitrary",), vmem_limit_bytes=48 * 1024 * 1024),
        name="outproj",
    )(zhy, yhg, h0, g_hy.reshape(1, -1), g_ffn.reshape(1, -1), wo_bf16, wr_bf16)


EXP_TB = 512


def _wait_rows(ref, n_rows, sem):
    pltpu.make_async_copy(ref.at[pl.ds(0, n_rows)], ref.at[pl.ds(n_rows, n_rows)], sem).wait()


def _dispatch_body(lastblk_ref, npad_ref, nused_ref, d0_ref, d1_ref, h1_ref, xb_ref, zero_s, sem_z,
                   sem):
    i = pl.program_id(0)
    tm = d0_ref.shape[0]
    n_blocks = xb_ref.shape[0] // EXP_TB

    @pl.when(i == 0)
    def _():
        zero_s[...] = jnp.zeros_like(zero_s)

        def zero_copy(row0):
            row0 = pl.multiple_of(row0, EXP_TB)
            return pltpu.make_async_copy(zero_s, xb_ref.at[pl.ds(row0, EXP_TB)], sem_z)

        for e in range(N_EXPERTS):
            @pl.when(npad_ref[e] > 0)
            def _():
                zero_copy(lastblk_ref[e]).start()
        for e in range(N_EXPERTS):
            @pl.when(npad_ref[e] > 0)
            def _():
                zero_copy(lastblk_ref[e]).wait()

        def start_tail(b, carry):
            zero_copy(b * EXP_TB).start()
            return carry

        def wait_tail(b, carry):
            zero_copy(b * EXP_TB).wait()
            return carry

        lax.fori_loop(nused_ref[0], n_blocks, start_tail, 0)
        lax.fori_loop(nused_ref[0], n_blocks, wait_tail, 0)

    def issue(t, carry):
        pltpu.make_async_copy(h1_ref.at[t], xb_ref.at[d0_ref[t]], sem).start(priority=0)
        pltpu.make_async_copy(h1_ref.at[t], xb_ref.at[d1_ref[t]], sem).start(priority=1)
        return carry

    lax.fori_loop(0, tm, issue, 0, unroll=16)
    _wait_rows(xb_ref, TOP_K * tm, sem)


def _dispatch(lastblk, npad, n_used, dest, rows, n_slots, tm):
    m = rows.shape[0]
    grid_spec = pltpu.PrefetchScalarGridSpec(
        num_scalar_prefetch=3,
        grid=(m // tm,),
        in_specs=[pl.BlockSpec((tm,), lambda i, lb, npd, nu: (i,), memory_space=pltpu.SMEM),
                  pl.BlockSpec((tm,), lambda i, lb, npd, nu: (i,), memory_space=pltpu.SMEM),
                  pl.BlockSpec((tm, ROW_SUB, LANES), lambda i, lb, npd, nu: (i, 0, 0))],
        out_specs=pl.BlockSpec(memory_space=pl.ANY),
        scratch_shapes=[pltpu.VMEM((EXP_TB, ROW_SUB, LANES), rows.dtype),
                        pltpu.SemaphoreType.DMA(()), pltpu.SemaphoreType.DMA(())],
    )
    return pl.pallas_call(
        _dispatch_body,
        grid_spec=grid_spec,
        out_shape=jax.ShapeDtypeStruct((n_slots, ROW_SUB, LANES), rows.dtype),
        compiler_params=pltpu.CompilerParams(dimension_semantics=("arbitrary",)),
        name="dispatch",
    )(lastblk, npad, n_used, dest[0], dest[1], rows)


def _expert_body(eid_ref, nused_ref, xb_ref, wg_ref, wu_ref, wd_ref, o_ref, wg_s, wu_s, wd_s):
    i = pl.program_id(0)
    half = D_MODEL // 2

    @pl.when(i < nused_ref[0])
    def _():
        prev = eid_ref[jnp.maximum(i - 1, 0)]

        @pl.when((i == 0) | (eid_ref[i] != prev))
        def _():
            wg_s[...] = wg_ref[0].astype(BF16)
            wu_s[...] = wu_ref[0].astype(BF16)
            wd_s[...] = wd_ref[0].astype(BF16)

        lo, hi = _unpack_bf16_pairs(xb_ref[...].reshape(EXP_TB, half))
        lo, hi = lo.astype(BF16), hi.astype(BF16)

        def proj(w_s):
            return (jnp.dot(lo, w_s[:half, :], preferred_element_type=F32)
                    + jnp.dot(hi, w_s[half:, :], preferred_element_type=F32))

        g = proj(wg_s)
        u = proj(wu_s)
        hmid = (g * jax.nn.sigmoid(g) * u).astype(BF16)
        y = _pack_bf16_pairs(jnp.dot(hmid, wd_s[...], preferred_element_type=F32))
        o_ref[...] = y.reshape(EXP_TB, ROW_SUB, LANES)

    @pl.when(i >= nused_ref[0])
    def _():
        o_ref[...] = jnp.zeros_like(o_ref)


def _experts(block_eid, n_used, xb, w_gate, w_up, w_down):
    n_slots = xb.shape[0]

    def blk(i, e, nu):
        return jnp.minimum(i, nu[0] - 1)

    grid_spec = pltpu.PrefetchScalarGridSpec(
        num_scalar_prefetch=2,
        grid=(n_slots // EXP_TB,),
        in_specs=[pl.BlockSpec((EXP_TB, ROW_SUB, LANES), lambda i, e, nu: (blk(i, e, nu), 0, 0)),
                  pl.BlockSpec((1, D_MODEL, D_EXPERT), lambda i, e, nu: (e[blk(i, e, nu)], 0, 0)),
                  pl.BlockSpec((1, D_MODEL, D_EXPERT), lambda i, e, nu: (e[blk(i, e, nu)], 0, 0)),
                  pl.BlockSpec((1, D_EXPERT, D_MODEL), lambda i, e, nu: (e[blk(i, e, nu)], 0, 0))],
        out_specs=pl.BlockSpec((EXP_TB, ROW_SUB, LANES), lambda i, e, nu: (i, 0, 0)),
        scratch_shapes=[pltpu.VMEM((D_MODEL, D_EXPERT), BF16),
                        pltpu.VMEM((D_MODEL, D_EXPERT), BF16),
                        pltpu.VMEM((D_EXPERT, D_MODEL), BF16)],
    )
    return pl.pallas_call(
        _expert_body,
        grid_spec=grid_spec,
        out_shape=jax.ShapeDtypeStruct((n_slots, ROW_SUB, LANES), jnp.uint32),
        compiler_params=pltpu.CompilerParams(
            dimension_semantics=("arbitrary",), vmem_limit_bytes=48 * 1024 * 1024),
        name="experts",
    )(block_eid, n_used, xb, w_gate, w_up, w_down)


def _combine_body(d0_ref, d1_ref, n0_ref, n1_ref, h1_ref, rg_ref, gfin_ref, yb_ref, o_ref, y_s, sem):
    i = pl.program_id(0)
    tm = h1_ref.shape[0]
    slot = i % 2

    def issue(idx_refs, sl):
        def body(t, carry):
            for k in range(TOP_K):
                pltpu.make_async_copy(yb_ref.at[idx_refs[k][t]], y_s.at[sl, k, t],
                                      sem.at[sl]).start(priority=k)
            return carry

        lax.fori_loop(0, tm, body, 0, unroll=16)

    @pl.when(i == 0)
    def _():
        issue((d0_ref, d1_ref), 0)

    @pl.when(i + 1 < pl.num_programs(0))
    def _():
        issue((n0_ref, n1_ref), 1 - slot)

    _wait_rows(yb_ref, TOP_K * tm, sem.at[slot])
    rg = rg_ref[...]
    g1, g2 = rg[:, 0:1], rg[:, 1:2]
    half = D_MODEL // 2
    lo1, hi1 = _unpack_bf16_pairs(y_s[slot, 0].reshape(tm, half))
    lo2, hi2 = _unpack_bf16_pairs(y_s[slot, 1].reshape(tm, half))
    h2_lo = h1_ref[:, :half] + g1 * lo1 + g2 * lo2
    h2_hi = h1_ref[:, half:] + g1 * hi1 + g2 * hi2
    ms = (jnp.sum(h2_lo * h2_lo, axis=-1, keepdims=True)
          + jnp.sum(h2_hi * h2_hi, axis=-1, keepdims=True)) * (1.0 / D_MODEL)
    inv = lax.rsqrt(ms + EPS)
    o_ref[:, :half] = h2_lo * inv * gfin_ref[:, :half]
    o_ref[:, half:] = h2_hi * inv * gfin_ref[:, half:]


def _combine(dest, h1, rg, g_fin, yb, tm):
    m = h1.shape[0]
    last = m // tm - 1
    return pl.pallas_call(
        _combine_body,
        grid=(m // tm,),
        in_specs=[pl.BlockSpec((tm,), lambda i: (i,), memory_space=pltpu.SMEM),
                  pl.BlockSpec((tm,), lambda i: (i,), memory_space=pltpu.SMEM),
                  pl.BlockSpec((tm,), lambda i: (jnp.minimum(i + 1, last),), memory_space=pltpu.SMEM),
                  pl.BlockSpec((tm,), lambda i: (jnp.minimum(i + 1, last),), memory_space=pltpu.SMEM),
                  pl.BlockSpec((tm, D_MODEL), lambda i: (i, 0)),
                  pl.BlockSpec((tm, LANES), lambda i: (i, 0)),
                  pl.BlockSpec((1, D_MODEL), lambda i: (0, 0)),
                  pl.BlockSpec(memory_space=pl.ANY)],
        out_specs=pl.BlockSpec((tm, D_MODEL), lambda i: (i, 0)),
        out_shape=jax.ShapeDtypeStruct((m, D_MODEL), F32),
        scratch_shapes=[pltpu.VMEM((2, TOP_K, tm, ROW_SUB, LANES), yb.dtype),
                        pltpu.SemaphoreType.DMA((2,))],
        compiler_params=pltpu.CompilerParams(dimension_semantics=("arbitrary",)),
        name="combine",
    )(dest[0], dest[1], dest[0], dest[1], h1, rg, g_fin.reshape(1, -1), yb)


def _moe(h1, a2p, ri, rg, cnt, g_fin, w_gate, w_up, w_down):
    m = h1.shape[0]
    n_blocks = TOP_K * m // EXP_TB + N_EXPERTS
    n_slots = n_blocks * EXP_TB
    counts = cnt[0, N_GROUPS:N_GROUPS + N_EXPERTS].astype(jnp.int32)
    padded = (counts + EXP_TB - 1) // EXP_TB * EXP_TB
    pend = jnp.cumsum(padded)
    base = pend - padded
    rt = ri[:2 * TOP_K]
    sel = rt[None, :TOP_K] == jnp.arange(N_EXPERTS, dtype=jnp.int32)[:, None, None]
    dest = jnp.sum(jnp.where(sel, base[:, None, None], 0), axis=0) + rt[TOP_K:]
    blk_start = jnp.arange(n_blocks, dtype=jnp.int32) * EXP_TB
    block_eid = jnp.minimum(jnp.sum(blk_start[:, None] >= pend[None, :], axis=1),
                            N_EXPERTS - 1).astype(jnp.int32)
    n_used = (pend[-1:] // EXP_TB).astype(jnp.int32)
    lastblk = (pend - EXP_TB).astype(jnp.int32)

    xb = _dispatch(lastblk, padded.astype(jnp.int32), n_used, dest, a2p, n_slots, tm=1024)
    yb = _experts(block_eid, n_used, xb, w_gate, w_up, w_down)
    return _combine(dest, h1, rg, g_fin, yb, tm=512)


ND = 2176
NCIRC = 2 * ND
NH = ND // 2
SEQ = 2048
L_TOT = SEQ + N_META
LH = L_TOT // 2
LHP = 1152
LHR = 1040
HY_CT = 256
HY_FB = NH // 2
HY_IB = ((0, 528), (528, 512))
HY_CB = LH // 3
HALO = 8


def _dft_matrices():
    a = 2 * jnp.arange(NH, dtype=jnp.int32) + 1
    t_hi = jnp.arange(LHP // LANES, dtype=jnp.int32) * LANES
    t_lo = jnp.arange(LANES, dtype=jnp.int32)
    valid = ((t_hi[:, None] + t_lo[None, :]) < LH).reshape(1, LHP)

    def cos_sin(m, denom):
        ang = m.astype(F32) * (math.pi / denom)
        return jnp.cos(ang), jnp.sin(ang)

    c_hi, s_hi = cos_sin((a[:, None] * ((4 * t_hi) % (4 * NCIRC))[None, :]) % (4 * NCIRC), 2 * NCIRC)
    c_hi, s_hi = c_hi[:, :, None], s_hi[:, :, None]
    out = []
    for c in (1, 3):
        c_lo, s_lo = cos_sin((a[:, None] * (4 * t_lo + c)[None, :]) % (4 * NCIRC), 2 * NCIRC)
        c_lo, s_lo = c_lo[:, None, :], s_lo[:, None, :]
        cos_phi = jnp.where(valid, (c_hi * c_lo - s_hi * s_lo).reshape(NH, LHP), 0.0)
        sin_phi = jnp.where(valid, (s_hi * c_lo + c_hi * s_lo).reshape(NH, LHP), 0.0)
        out.append(jnp.concatenate([cos_phi, sin_phi if c == 1 else -sin_phi], axis=0).astype(BF16))
    ce, co = out
    return ce, co, ce.T, co.T


def _forward_half_spectrum(ce_ref, co_ref, xe, xo, j0):
    a1 = jnp.dot(ce_ref[j0:j0 + HY_FB, :], xe, preferred_element_type=F32)
    a2 = jnp.dot(ce_ref[NH + j0:NH + j0 + HY_FB, :], xe, preferred_element_type=F32)
    b1 = jnp.dot(co_ref[j0:j0 + HY_FB, :], xo, preferred_element_type=F32)
    b2 = jnp.dot(co_ref[NH + j0:NH + j0 + HY_FB, :], xo, preferred_element_type=F32)
    return a1 + b1, a2 - b2, a2 + b2, a1 - b1


def _spectra_body(s_ref, d_ref, ce_ref, co_ref, rot_ref, hr_ref, hi_ref,
                  stage_s, se_s, so_s, de_s, do_s):
    ct = hr_ref.shape[-1]
    nl = ct // LANES
    for x_ref, xe_s, xo_s in ((s_ref, se_s, so_s), (d_ref, de_s, do_s)):
        for j in range(nl):
            stage_s[j] = x_ref[:, j * LANES:(j + 1) * LANES]
        for par, dst in ((0, xe_s), (1, xo_s)):
            for r0 in range(0, LH, HY_CB):
                dst[r0:r0 + HY_CB, :] = jnp.concatenate(
                    [stage_s.at[j][pl.ds(par + 2 * r0, HY_CB, stride=2), :] for j in range(nl)],
                    axis=1).astype(BF16)
            dst[LH:LHP, :] = jnp.zeros((LHP - LH, ct), BF16)
    for j0 in range(0, NH, HY_FB):
        ar1, _, ar2, _ = _forward_half_spectrum(ce_ref, co_ref, se_s[...], so_s[...], j0)
        _, q1, _, q2 = _forward_half_spectrum(ce_ref, co_ref, de_s[...], do_s[...], j0)
        for r0, ar, q in ((j0, ar1, q1), (NH + j0, ar2, q2)):
            c = rot_ref[r0:r0 + HY_FB, 0:1]
            s = rot_ref[r0:r0 + HY_FB, 1:2]
            hr_ref[0, r0:r0 + HY_FB, :] = c * ar + s * q
            hi_ref[0, r0:r0 + HY_FB, :] = s * ar - c * q


def _spectra(h_fwd, h_bwd, ce, co):
    hbs = jnp.concatenate([h_bwd[1:], jnp.zeros_like(h_bwd[:1])], axis=0)
    s2d = (h_fwd + hbs).reshape(L_TOT, HYENA_ORDER * D_HYENA)
    d2d = (h_fwd - hbs).reshape(L_TOT, HYENA_ORDER * D_HYENA)
    j = jnp.arange(NH, dtype=F32)
    k = jnp.concatenate([j, ND - 1 - j])
    half = (2 * k + 1) * (math.pi / (2 * NCIRC))
    scale = 2.0 / NCIRC
    rot = jnp.stack([jnp.cos(half) * scale, jnp.sin(half) * scale], axis=1)
    rot = jnp.pad(rot, ((0, 0), (0, LANES - 2)))
    n_ord, ct = HYENA_ORDER, HY_CT
    nct = D_HYENA // ct
    col = pl.BlockSpec((L_TOT, ct), lambda o, c: (0, o * nct + c))
    mat = pl.BlockSpec((ND, LHP), lambda o, c: (0, 0))
    out = pl.BlockSpec((1, ND, ct), lambda o, c: (o, 0, c))
    half_bf16 = pltpu.VMEM((LHP, ct), BF16)
    return pl.pallas_call(
        _spectra_body,
        grid=(n_ord, nct),
        in_specs=[col, col, mat, mat, pl.BlockSpec((ND, LANES), lambda o, c: (0, 0))],
        out_specs=[out, out],
        out_shape=[jax.ShapeDtypeStruct((n_ord, ND, D_HYENA), F32)] * 2,
        scratch_shapes=[pltpu.VMEM((ct // LANES, L_TOT, LANES), F32),
                        half_bf16, half_bf16, half_bf16, half_bf16],
        compiler_params=pltpu.CompilerParams(
            dimension_semantics=("arbitrary", "arbitrary"), vmem_limit_bytes=48 * 1024 * 1024),
        name="spectra",
    )(s2d, d2d, ce, co, rot)


def _hyena_body(pv_ref, px1_ref, px2_ref, mv_ref, mx1_ref, mx2_ref,
                wv_ref, wx1_ref, wx2_ref, bv_ref, bx1_ref, bx2_ref, skip_ref,
                ce_ref, co_ref, cet_ref, cot_ref, hr_ref, hi_ref, o_ref,
                stage_s, ze_s, zo_s, g1e_s, g1o_s, g2e_s, g2o_s, zbe_s, zbo_s, we_s, wo_s):
    ct = o_ref.shape[-1]
    nl = ct // LANES

    def strided(off, r0, rows):
        return jnp.concatenate(
            [stage_s.at[j][pl.ds(HALO + off + 2 * r0, rows, stride=2), :] for j in range(nl)], axis=1)

    def short_conv(p_ref, m_ref, w_ref, b_ref, dst_e, dst_o):
        for j in range(nl):
            ln = slice(j * LANES, (j + 1) * LANES)
            stage_s[j, 0:HALO, :] = jnp.zeros((HALO, LANES), F32)
            stage_s[j, HALO:HALO + N_META, :] = m_ref[:, ln].astype(F32)
            stage_s[j, HALO + N_META:HALO + L_TOT, :] = p_ref[0, :, ln].astype(F32)
            stage_s[j, HALO + L_TOT:HALO + L_TOT + HALO, :] = jnp.zeros((HALO, LANES), F32)
        w = w_ref[...]
        b = b_ref[...]
        for r0 in range(0, LH, HY_CB):
            sm1, s0, s1, s2 = (strided(off, r0, HY_CB) for off in (-1, 0, 1, 2))
            dst_e[r0:r0 + HY_CB, :] = b + sm1 * w[0:1] + s0 * w[1:2] + s1 * w[2:3]
            dst_o[r0:r0 + HY_CB, :] = b + s0 * w[0:1] + s1 * w[1:2] + s2 * w[2:3]
        dst_e[LH:LHR, :] = jnp.zeros((LHR - LH, ct), F32)
        dst_o[LH:LHR, :] = jnp.zeros((LHR - LH, ct), F32)

    short_conv(pv_ref, mv_ref, wv_ref, bv_ref, ze_s, zo_s)
    short_conv(px1_ref, mx1_ref, wx1_ref, bx1_ref, g1e_s, g1o_s)
    short_conv(px2_ref, mx2_ref, wx2_ref, bx2_ref, g2e_s, g2o_s)
    zbe_s[LH:LHP, :] = jnp.zeros((LHP - LH, ct), BF16)
    zbo_s[LH:LHP, :] = jnp.zeros((LHP - LH, ct), BF16)

    for n, (ge_s, go_s) in enumerate(((g1e_s, g1o_s), (g2e_s, g2o_s))):
        zbe_s[0:LH, :] = ze_s[0:LH, :].astype(BF16)
        zbo_s[0:LH, :] = zo_s[0:LH, :].astype(BF16)
        for j0 in range(0, NH, HY_FB):
            p1, q1, p2, q2 = _forward_half_spectrum(ce_ref, co_ref, zbe_s[...], zbo_s[...], j0)
            h1r = hr_ref[n, j0:j0 + HY_FB, :]
            h1i = hi_ref[n, j0:j0 + HY_FB, :]
            h2r = hr_ref[n, NH + j0:NH + j0 + HY_FB, :]
            h2i = hi_ref[n, NH + j0:NH + j0 + HY_FB, :]
            yr1 = p1 * h1r + q1 * h1i
            ny1 = q1 * h1r - p1 * h1i
            yr2 = p2 * h2r + q2 * h2i
            ny2 = q2 * h2r - p2 * h2i
            we_s[j0:j0 + HY_FB, :] = (yr1 + ny2).astype(BF16)
            we_s[NH + j0:NH + j0 + HY_FB, :] = (yr2 + ny1).astype(BF16)
            wo_s[j0:j0 + HY_FB, :] = (yr1 - ny2).astype(BF16)
            wo_s[NH + j0:NH + j0 + HY_FB, :] = (yr2 - ny1).astype(BF16)
        skip = skip_ref[n:n + 1, :]
        for parity, (ct_ref, w_s, z_s, g_s) in enumerate(((cet_ref, we_s, ze_s, ge_s),
                                                          (cot_ref, wo_s, zo_s, go_s))):
            for r0, rb in HY_IB:
                y = jnp.dot(ct_ref[r0:r0 + rb, :], w_s[...], preferred_element_type=F32)
                znew = g_s[r0:r0 + rb, :] * (y + z_s[r0:r0 + rb, :] * skip)
                if n == 0:
                    z_s[r0:r0 + rb, :] = znew
                else:
                    n_valid = min(r0 + rb, LH) - r0
                    for j in range(nl):
                        stage_s.at[j][pl.ds(HALO + parity + 2 * r0, n_valid, stride=2), :] = (
                            znew[0:n_valid, j * LANES:(j + 1) * LANES])
    for j in range(nl):
        o_ref[0, :, j * LANES:(j + 1) * LANES] = stage_s[j, HALO + N_META:HALO + L_TOT, :]


def _hyena(p_hy, pm_hy, conv_w, conv_b, skip, mats, hr, hi):
    bsz = p_hy.shape[0]
    nct = D_HYENA // HY_CT
    ct = HY_CT
    ce, co, cet, cot = mats

    def part(j):
        return [pl.BlockSpec((1, SEQ, ct), lambda c, b, j=j: (b, 0, j * nct + c))]

    def mpart(j):
        return [pl.BlockSpec((N_META, ct), lambda c, b, j=j: (0, j * nct + c))]

    def wpart(rows, j):
        return [pl.BlockSpec((rows, ct), lambda c, b, j=j: (0, j * nct + c))]

    spec = pl.BlockSpec((HYENA_ORDER, ND, ct), lambda c, b: (0, 0, c), pipeline_mode=pl.Buffered(1))
    in_specs = (part(0) + part(1) + part(2) + mpart(0) + mpart(1) + mpart(2)
                + wpart(3, 0) + wpart(3, 1) + wpart(3, 2)
                + wpart(1, 0) + wpart(1, 1) + wpart(1, 2)
                + [pl.BlockSpec((HYENA_ORDER, ct), lambda c, b: (0, c)),
                   pl.BlockSpec((ND, LHP), lambda c, b: (0, 0)),
                   pl.BlockSpec((ND, LHP), lambda c, b: (0, 0)),
                   pl.BlockSpec((LHP, ND), lambda c, b: (0, 0)),
                   pl.BlockSpec((LHP, ND), lambda c, b: (0, 0)),
                   spec, spec])
    cb = conv_b.reshape(1, -1)
    f32_half = pltpu.VMEM((LHR, ct), F32)
    return pl.pallas_call(
        _hyena_body,
        grid=(nct, bsz),
        in_specs=in_specs,
        out_specs=pl.BlockSpec((1, SEQ, ct), lambda c, b: (b, 0, c)),
        out_shape=jax.ShapeDtypeStruct((bsz, SEQ, D_HYENA), F32),
        scratch_shapes=[pltpu.VMEM((ct // LANES, L_TOT + 2 * HALO, LANES), F32),
                        f32_half, f32_half, f32_half, f32_half, f32_half, f32_half,
                        pltpu.VMEM((LHP, ct), BF16), pltpu.VMEM((LHP, ct), BF16),
                        pltpu.VMEM((ND, ct), BF16), pltpu.VMEM((ND, ct), BF16)],
        compiler_params=pltpu.CompilerParams(
            dimension_semantics=("arbitrary", "arbitrary"), vmem_limit_bytes=60 * 1024 * 1024),
        name="hyena",
    )(p_hy, p_hy, p_hy, pm_hy, pm_hy, pm_hy, conv_w, conv_w, conv_w, cb, cb, cb, skip,
      ce, co, cet, cot, hr, hi)


N_CHUNKS = SEQ // CHUNK
NT_DIMS = (((1,), (1,)), ((), ()))
TN_DIMS = (((0,), (0,)), ((), ()))
MID_F = CHUNK // 2
MID_B = CHUNK // 2 - 1


HG_G = 8
HG_ROWS = HG_G * CHUNK


def _split2(x):
    hi = x.astype(BF16)
    lo = (x - hi.astype(F32)).astype(BF16)
    return hi, lo


def _chunk_prefix_matrix():
    r = lax.broadcasted_iota(jnp.int32, (HG_ROWS, HG_ROWS), 0)
    c = lax.broadcasted_iota(jnp.int32, (HG_ROWS, HG_ROWS), 1)
    return (((r // CHUNK) == (c // CHUNK)) & (c <= r)).astype(BF16)


def _chunk_rows(x, row):
    return jnp.concatenate(
        [jnp.broadcast_to(x[g * CHUNK + row:g * CHUNK + row + 1, :], (CHUNK, x.shape[1]))
         for g in range(x.shape[0] // CHUNK)], axis=0)


def _hgrn_body(q_ref, ff_ref, fb_ref, i_ref, g_ref, mff_ref, mi_ref,
               lbf_ref, lbb_ref, nw_ref, o_ref,
               tri_s, qe_s, sc_s, ut_s, dec_s, st_s):
    hd = HGRN_HEAD_DIM
    row = lax.broadcasted_iota(jnp.int32, (CHUNK, CHUNK), 0)
    col = lax.broadcasted_iota(jnp.int32, (CHUNK, CHUNK), 1)
    lower = row >= col
    upper = col >= row
    lbf = lbf_ref[...]
    lbb = lbb_ref[...]

    @pl.when((pl.program_id(0) == 0) & (pl.program_id(1) == 0))
    def _():
        tri_s[...] = _chunk_prefix_matrix()

    def forget(logit, lb):
        f = lb + (1.0 - lb) * jax.nn.sigmoid(logit)
        return 1.0 - f, jnp.log(f)

    def prefix_sums(lf):
        s = jnp.dot(tri_s[...], jnp.concatenate(_split2(lf), axis=1), preferred_element_type=F32)
        return s[:, :hd] + s[:, hd:]

    k_m, lf_m = forget(mff_ref[...], lbf)
    pad = jnp.zeros((CHUNK - N_META, hd), F32)
    lf_m = jnp.concatenate([pad, lf_m] * HG_G, axis=0)
    b_m = prefix_sums(lf_m)[0:CHUNK]
    kl_m = jnp.concatenate([pad, k_m], axis=0) * jnp.exp(b_m[CHUNK - 1:CHUNK] - b_m)
    v_m = jnp.concatenate([pad, mi_ref[...]], axis=0).astype(BF16)
    st_meta = lax.dot_general(v_m, kl_m.astype(BF16), TN_DIMS, preferred_element_type=F32)

    def phase_a(j, carry):
        r0 = pl.multiple_of(j * HG_ROWS, HG_ROWS)
        rows = pl.ds(r0, HG_ROWS)
        qv = jax.nn.silu(q_ref[0, rows, :])
        vb = i_ref[0, rows, :].astype(BF16)
        k_f, lf_f = forget(ff_ref[0, rows, :], lbf)
        k_b, lf_b = forget(fb_ref[0, rows, :], lbb)
        b_f = prefix_sums(lf_f)
        p_b = prefix_sums(lf_b)
        bmid_f = _chunk_rows(b_f, MID_F)
        blast_f = _chunk_rows(b_f, CHUNK - 1)
        tot_b = _chunk_rows(p_b, CHUNK - 1)
        c_b = tot_b - p_b + lf_b
        cmid_b = _chunk_rows(c_b, MID_B)
        d_f = b_f - bmid_f
        d_b = c_b - cmid_b
        e_f = jnp.exp(d_f)
        e_b = jnp.exp(d_b)
        qs_f = qv * e_f
        ks_f = k_f / e_f
        qs_b = qv * e_b
        ks_b = k_b / e_b
        qs_fb, ks_fb = qs_f.astype(BF16), ks_f.astype(BF16)
        qs_bb, ks_bb = qs_b.astype(BF16), ks_b.astype(BF16)
        for g in range(HG_G):
            sl = slice(g * CHUNK, (g + 1) * CHUNK)
            r1 = slice(g * CHUNK, g * CHUNK + 1)
            n = j * HG_G + g
            rows_g = pl.ds(pl.multiple_of(r0 + g * CHUNK, CHUNK), CHUNK)
            sc_f = lax.dot_general(qs_fb[sl], ks_fb[sl], NT_DIMS, preferred_element_type=F32)
            sc_b = lax.dot_general(qs_bb[sl], ks_bb[sl], NT_DIMS, preferred_element_type=F32)
            sc_s[n] = (jnp.where(lower, sc_f, 0.0) + jnp.where(upper, sc_b, 0.0)).astype(BF16)
            em_f = jnp.exp(bmid_f[r1])
            el_f = jnp.exp(blast_f[r1] - bmid_f[r1])
            em_b = jnp.exp(cmid_b[r1])
            el_b = jnp.exp(tot_b[r1] - cmid_b[r1])
            qe_s[rows_g, :] = jnp.concatenate(
                [qs_f[sl] * em_f, qs_b[sl] * em_b], axis=1).astype(BF16)
            kl = jnp.concatenate([ks_f[sl] * el_f, ks_b[sl] * el_b], axis=1).astype(BF16)
            ut_s[n] = lax.dot_general(vb[sl], kl, TN_DIMS, preferred_element_type=F32)
            dec_s[n] = jnp.concatenate([jnp.exp(blast_f[r1]), jnp.exp(tot_b[r1])], axis=1)
        return carry

    lax.fori_loop(0, N_CHUNKS // HG_G, phase_a, 0, unroll=True)

    st_f = st_meta
    st_b = jnp.zeros((hd, hd), F32)
    for n in range(N_CHUNKS):
        st_s[n, :, 0:hd] = st_f.astype(BF16)
        st_f = dec_s[n, :, 0:hd] * st_f + ut_s[n, :, 0:hd]
        m = N_CHUNKS - 1 - n
        st_s[m, :, hd:2 * hd] = st_b.astype(BF16)
        st_b = dec_s[m, :, hd:2 * hd] * st_b + ut_s[m, :, hd:2 * hd]

    nw = nw_ref[...]

    def phase_c(j, carry):
        r0 = pl.multiple_of(j * HG_ROWS, HG_ROWS)
        rows = pl.ds(r0, HG_ROWS)
        vb = i_ref[0, rows, :].astype(BF16)
        outs = []
        for g in range(HG_G):
            sl = slice(g * CHUNK, (g + 1) * CHUNK)
            n = j * HG_G + g
            rows_g = pl.ds(pl.multiple_of(r0 + g * CHUNK, CHUNK), CHUNK)
            o = jnp.dot(sc_s[n], vb[sl], preferred_element_type=F32)
            outs.append(o + lax.dot_general(qe_s[rows_g, :], st_s[n], NT_DIMS,
                                            preferred_element_type=F32))
        o = jnp.concatenate(outs, axis=0)
        o = o * lax.rsqrt(jnp.mean(o * o, axis=-1, keepdims=True) + EPS)
        o_ref[0, rows, :] = o * nw * jax.nn.silu(g_ref[0, rows, :])
        return carry

    lax.fori_loop(0, N_CHUNKS // HG_G, phase_c, 0, unroll=True)


def _hgrn(phg_x, phg_m, lb_f, lb_b, norm_w):
    bsz = phg_x.shape[0]
    hd = HGRN_HEAD_DIM
    nh = HGRN_HEADS

    def part(j):
        return pl.BlockSpec((1, SEQ, hd), lambda b, h, j=j: (b, 0, j * nh + h))

    def mpart(j):
        return pl.BlockSpec((N_META, hd), lambda b, h, j=j: (0, j * nh + h))

    vec = pl.BlockSpec((1, hd), lambda b, h: (0, h))
    return pl.pallas_call(
        _hgrn_body,
        grid=(bsz, nh),
        in_specs=[part(0), part(1), part(2), part(3), part(4), mpart(1), mpart(3), vec, vec, vec],
        out_specs=pl.BlockSpec((1, SEQ, hd), lambda b, h: (b, 0, h)),
        out_shape=jax.ShapeDtypeStruct((bsz, SEQ, D_HGRN), F32),
        scratch_shapes=[pltpu.VMEM((HG_ROWS, HG_ROWS), BF16),
                        pltpu.VMEM((SEQ, 2 * hd), BF16),
                        pltpu.VMEM((N_CHUNKS, CHUNK, CHUNK), BF16),
                        pltpu.VMEM((N_CHUNKS, hd, 2 * hd), F32),
                        pltpu.VMEM((N_CHUNKS, 1, 2 * hd), F32),
                        pltpu.VMEM((N_CHUNKS, hd, 2 * hd), BF16)],
        compiler_params=pltpu.CompilerParams(
            dimension_semantics=("arbitrary", "arbitrary"), vmem_limit_bytes=40 * 1024 * 1024),
        name="hgrn",
    )(phg_x, phg_x, phg_x, phg_x, phg_x, phg_m, phg_m,
      lb_f.reshape(1, -1), lb_b.reshape(1, -1), norm_w.reshape(1, -1))


def _hyena_filters(L, w1, b1, w2, b2, w3, freq):
    pos = jnp.arange(L, dtype=F32)
    t = pos / max(L - 1, 1)
    bands = jnp.linspace(1e-4, FILTER_BANDS - 1, FILTER_BANDS, dtype=F32)
    ang = (2.0 * math.pi / L) * pos[:, None] * bands[None, :]
    z = jnp.concatenate([t[:, None], jnp.cos(ang), -jnp.sin(ang)], axis=-1)
    hp = lax.Precision.HIGHEST
    hid = jnp.sin(freq * (jnp.dot(z, w1, precision=hp) + b1))
    hid = jnp.sin(freq * (jnp.dot(hid, w2, precision=hp) + b2))
    filt = jnp.dot(hid, w3, precision=hp).reshape(L, 2, HYENA_ORDER, D_HYENA)
    deltas = jnp.abs(jnp.linspace(math.log(DECAY_TARGET) / SLOW_DECAY_PCT,
                                  math.log(DECAY_TARGET) / FAST_DECAY_PCT, D_HYENA, dtype=F32))
    window = jnp.exp(-t[:, None] * deltas[None, :])
    filt = filt * window[:, None, None, :]
    return filt[:, 0], filt[:, 1]


def kernel(x, meta_tokens, w_in, conv_w, conv_b, filt_w1, filt_b1, filt_w2, filt_b2, filt_w3,
           filt_freq, filt_skip, hyena_norm, lb_fwd, lb_bwd, hgrn_norm, w_out, norm_mix, norm_ffn,
           w_router_group, w_router_expert, w_gate, w_up, w_down, norm_final):
    B, S, D = x.shape
    L = S + N_META
    lbf = jnp.cumsum(jax.nn.softmax(lb_fwd, axis=0), axis=0)[0]
    lbb = jnp.cumsum(jax.nn.softmax(lb_bwd, axis=0), axis=0)[0]

    xf = x.reshape(B * S, D)
    phy_x, phg_x = _inproj(xf, norm_mix[0], w_in[0], tm=512)
    phy_m, phg_m = _inproj(meta_tokens, norm_mix[0], w_in[0], tm=N_META)

    mats = _dft_matrices()
    h_fwd, h_bwd = _hyena_filters(L, filt_w1[0], filt_b1[0], filt_w2[0], filt_b2[0], filt_w3[0],
                                  filt_freq[0])
    hr, hi = _spectra(h_fwd, h_bwd, mats[0], mats[1])
    z_hy = _hyena(phy_x.reshape(B, S, D_HYENA_PROJ), phy_m, conv_w[0], conv_b[0], filt_skip[0],
                  mats, hr, hi).reshape(B * S, D_HYENA)

    y_hg = _hgrn(phg_x.reshape(B, S, 5 * D_HGRN), phg_m, lbf, lbb,
                 hgrn_norm[0]).reshape(B * S, D_HGRN)

    w_r = jnp.concatenate([w_router_group[0], w_router_expert[0].reshape(D, N_EXPERTS),
                           jnp.zeros((D, LANES - N_GROUPS - N_EXPERTS), F32)], axis=1).astype(BF16)
    h1, a2p, ri, rg, cnt = _outproj(z_hy, y_hg, xf, hyena_norm[0], norm_ffn[0],
                                    w_out[0].astype(BF16), w_r, tm=1024)
    out = _moe(h1, a2p, ri, rg, cnt, norm_final, w_gate[0], w_up[0], w_down[0])
    return out.reshape(B, S, D)
```

```python
import functools
import math

import jax
import jax.numpy as jnp
from jax import lax
from jax.experimental import pallas as pl
from jax.experimental.pallas import tpu as pltpu

D_MODEL = 1024
N_META = 16
D_HYENA = 512
D_HGRN = 512
HYENA_ORDER = 2
SHORT_CONV = 3
FILTER_EMB = 33
FILTER_BANDS = 16
DECAY_TARGET = 1e-2
FAST_DECAY_PCT = 0.3
SLOW_DECAY_PCT = 1.5
HGRN_HEAD_DIM = 128
HGRN_HEADS = D_HGRN // HGRN_HEAD_DIM
CHUNK = 64
N_GROUPS = 8
EXPERTS_PER_GROUP = 8
N_EXPERTS = 64
TOP_K = 2
D_EXPERT = 512
D_HYENA_PROJ = 3 * D_HYENA
D_IN_PROJ = D_HYENA_PROJ + 5 * D_HGRN
EPS = 1e-6

F32 = jnp.float32
BF16 = jnp.bfloat16


def _rms(x, gain):
    return x * lax.rsqrt(jnp.mean(x * x, axis=-1, keepdims=True) + EPS) * gain


HI16 = 0xFFFF0000


def _pack_bf16_pairs(x):
    c = x.shape[1] // 2
    bits = lax.bitcast_convert_type(x.astype(BF16).astype(F32), jnp.uint32)
    return (bits[:, :c] >> 16) | (bits[:, c:] & jnp.uint32(HI16))


def _unpack_bf16_pairs(w):
    lo = lax.bitcast_convert_type(w << 16, F32)
    hi = lax.bitcast_convert_type(w & jnp.uint32(HI16), F32)
    return lo, hi


def _inproj_body(x_ref, g_ref, w_ref, hy_ref, hg_ref, *, tn):
    a = _rms(x_ref[...], g_ref[...]).astype(BF16)
    for j in range(D_IN_PROJ // tn):
        acc = jnp.dot(a, w_ref[:, j * tn:(j + 1) * tn].astype(BF16), preferred_element_type=F32)
        if j * tn < D_HYENA_PROJ:
            hy_ref[:, j * tn:(j + 1) * tn] = acc.astype(hy_ref.dtype)
        else:
            c0 = j * tn - D_HYENA_PROJ
            hg_ref[:, c0:c0 + tn] = acc.astype(hg_ref.dtype)


def _inproj(x, gain, w, tm):
    m, d = x.shape
    n_hg = D_IN_PROJ - D_HYENA_PROJ
    return pl.pallas_call(
        functools.partial(_inproj_body, tn=512),
        grid=(m // tm,),
        in_specs=[pl.BlockSpec((tm, d), lambda i: (i, 0)),
                  pl.BlockSpec((1, d), lambda i: (0, 0)),
                  pl.BlockSpec((d, D_IN_PROJ), lambda i: (0, 0))],
        out_specs=[pl.BlockSpec((tm, D_HYENA_PROJ), lambda i: (i, 0)),
                   pl.BlockSpec((tm, n_hg), lambda i: (i, 0))],
        out_shape=[jax.ShapeDtypeStruct((m, D_HYENA_PROJ), BF16),
                   jax.ShapeDtypeStruct((m, n_hg), F32)],
        compiler_params=pltpu.CompilerParams(
            dimension_semantics=("arbitrary",), vmem_limit_bytes=48 * 1024 * 1024),
        name="inproj",
    )(x, gain.reshape(1, d), w)


LANES = 128
NEG_BIG = -1e30
OP_SUB = 512
ROW_SUB = (D_MODEL // 2) // LANES


def _outproj_body(zhy_ref, yhg_ref, h0_ref, ghy_ref, gffn_ref, wo_ref, wr_ref,
                  h1_ref, a2p_ref, ri_ref, rg_ref, cnt_ref, tri_s, carry_s):
    i = pl.program_id(0)
    tm = h1_ref.shape[0]
    ts = tri_s.shape[0]

    @pl.when(i == 0)
    def _():
        r = lax.broadcasted_iota(jnp.int32, (ts, ts), 0)
        c = lax.broadcasted_iota(jnp.int32, (ts, ts), 1)
        tri_s[...] = (r > c).astype(BF16)
        carry_s[...] = jnp.zeros_like(carry_s)

    lane = lax.broadcasted_iota(jnp.int32, (ts, LANES), 1)
    is_g = lane < N_GROUPS
    carry = carry_s[...]
    for r0 in range(0, tm, ts):
        rows = slice(r0, r0 + ts)
        yhy = _rms(zhy_ref[rows, :], ghy_ref[...]).astype(BF16)
        yhg = yhg_ref[rows, :].astype(BF16)
        acc = jnp.dot(yhy, wo_ref[:D_HYENA, :], preferred_element_type=F32)
        acc = acc + jnp.dot(yhg, wo_ref[D_HYENA:, :], preferred_element_type=F32)
        h1 = h0_ref[rows, :] + acc
        h1_ref[rows, :] = h1
        a2 = _rms(h1, gffn_ref[...])
        a2p_ref[rows] = _pack_bf16_pairs(a2).reshape(ts, ROW_SUB, LANES)
        lg = jnp.dot(a2.astype(BF16), wr_ref[...], preferred_element_type=F32)

        gl = jnp.where(is_g, lg, NEG_BIG)
        gmax = jnp.max(gl, axis=1, keepdims=True)
        gsel = jnp.min(jnp.where(gl == gmax, lane, LANES), axis=1, keepdims=True)
        gden = jnp.sum(jnp.where(is_g, jnp.exp(gl - gmax), 0.0), axis=1, keepdims=True)
        p_group = 1.0 / gden
        in_grp = (lane >= N_GROUPS) & (lane < N_GROUPS + N_EXPERTS) & (
            ((lane - N_GROUPS) >> 3) == gsel)
        el = jnp.where(in_grp, lg, NEG_BIG)
        m1 = jnp.max(el, axis=1, keepdims=True)
        i1 = jnp.min(jnp.where(el == m1, lane, LANES), axis=1, keepdims=True)
        el2 = jnp.where(lane == i1, NEG_BIG, el)
        m2 = jnp.max(el2, axis=1, keepdims=True)
        i2 = jnp.min(jnp.where(el2 == m2, lane, LANES), axis=1, keepdims=True)
        r21 = jnp.exp(m2 - m1)
        gate1 = p_group / (1.0 + r21)
        gate2 = gate1 * r21

        hit1 = lane == i1
        hit2 = lane == i2
        onehot = (hit1 | hit2).astype(BF16)
        pre = jnp.dot(tri_s[...], onehot, preferred_element_type=F32) + carry
        pos1 = jnp.sum(jnp.where(hit1, pre, 0.0), axis=1, keepdims=True).astype(jnp.int32)
        pos2 = jnp.sum(jnp.where(hit2, pre, 0.0), axis=1, keepdims=True).astype(jnp.int32)
        carry = carry + jnp.sum(onehot.astype(F32), axis=0, keepdims=True)

        zero_i = jnp.zeros((ts, LANES), jnp.int32)
        ri = jnp.where(lane == 0, i1 - N_GROUPS,
                       jnp.where(lane == 1, i2 - N_GROUPS,
                                 jnp.where(lane == 2, pos1, jnp.where(lane == 3, pos2, zero_i))))
        ri_ref[:, rows] = jnp.transpose(ri)[:8, :]
        rg_ref[rows, :] = jnp.where(lane == 0, gate1, jnp.where(lane == 1, gate2, 0.0))
    carry_s[...] = carry
    cnt_ref[...] = carry


def _outproj(zhy, yhg, h0, g_hy, g_ffn, wo_bf16, wr_bf16, tm):
    m = h0.shape[0]
    return pl.pallas_call(
        _outproj_body,
        grid=(m // tm,),
        in_specs=[pl.BlockSpec((tm, D_HYENA), lambda i: (i, 0)),
                  pl.BlockSpec((tm, D_HGRN), lambda i: (i, 0)),
                  pl.BlockSpec((tm, D_MODEL), lambda i: (i, 0)),
                  pl.BlockSpec((1, D_HYENA), lambda i: (0, 0)),
                  pl.BlockSpec((1, D_MODEL), lambda i: (0, 0)),
                  pl.BlockSpec((D_MODEL, D_MODEL), lambda i: (0, 0)),
                  pl.BlockSpec((D_MODEL, LANES), lambda i: (0, 0))],
        out_specs=[pl.BlockSpec((tm, D_MODEL), lambda i: (i, 0)),
                   pl.BlockSpec((tm, ROW_SUB, LANES), lambda i: (i, 0, 0)),
                   pl.BlockSpec((8, tm), lambda i: (0, i)),
                   pl.BlockSpec((tm, LANES), lambda i: (i, 0)),
                   pl.BlockSpec((1, LANES), lambda i: (0, 0))],
        out_shape=[jax.ShapeDtypeStruct((m, D_MODEL), F32),
                   jax.ShapeDtypeStruct((m, ROW_SUB, LANES), jnp.uint32),
                   jax.ShapeDtypeStruct((8, m), jnp.int32),
                   jax.ShapeDtypeStruct((m, LANES), F32),
                   jax.ShapeDtypeStruct((1, LANES), F32)],
        scratch_shapes=[pltpu.VMEM((OP_SUB, OP_SUB), BF16), pltpu.VMEM((1, LANES), F32)],
        compiler_params=pltpu.CompilerParams(
            dimension_semantics=("arbitrary",), vmem_limit_bytes=48 * 1024 * 1024),
        name="outproj",
    )(zhy, yhg, h0, g_hy.reshape(1, -1), g_ffn.reshape(1, -1), wo_bf16, wr_bf16)


EXP_TB = 512


def _wait_rows(ref, n_rows, sem):
    pltpu.make_async_copy(ref.at[pl.ds(0, n_rows)], ref.at[pl.ds(n_rows, n_rows)], sem).wait()


def _dispatch_body(lastblk_ref, npad_ref, nused_ref, d0_ref, d1_ref, h1_ref, xb_ref, zero_s, sem_z,
                   sem):
    i = pl.program_id(0)
    tm = d0_ref.shape[0]
    n_blocks = xb_ref.shape[0] // EXP_TB

    @pl.when(i == 0)
    def _():
        zero_s[...] = jnp.zeros_like(zero_s)

        def zero_copy(row0):
            row0 = pl.multiple_of(row0, EXP_TB)
            return pltpu.make_async_copy(zero_s, xb_ref.at[pl.ds(row0, EXP_TB)], sem_z)

        for e in range(N_EXPERTS):
            @pl.when(npad_ref[e] > 0)
            def _():
                zero_copy(lastblk_ref[e]).start()
        for e in range(N_EXPERTS):
            @pl.when(npad_ref[e] > 0)
            def _():
                zero_copy(lastblk_ref[e]).wait()

        def start_tail(b, carry):
            zero_copy(b * EXP_TB).start()
            return carry

        def wait_tail(b, carry):
            zero_copy(b * EXP_TB).wait()
            return carry

        lax.fori_loop(nused_ref[0], n_blocks, start_tail, 0)
        lax.fori_loop(nused_ref[0], n_blocks, wait_tail, 0)

    def issue(t, carry):
        pltpu.make_async_copy(h1_ref.at[t], xb_ref.at[d0_ref[t]], sem).start(priority=0)
        pltpu.make_async_copy(h1_ref.at[t], xb_ref.at[d1_ref[t]], sem).start(priority=1)
        return carry

    lax.fori_loop(0, tm, issue, 0, unroll=16)
    _wait_rows(xb_ref, TOP_K * tm, sem)


def _dispatch(lastblk, npad, n_used, dest, rows, n_slots, tm):
    m = rows.shape[0]
    grid_spec = pltpu.PrefetchScalarGridSpec(
        num_scalar_prefetch=3,
        grid=(m // tm,),
        in_specs=[pl.BlockSpec((tm,), lambda i, lb, npd, nu: (i,), memory_space=pltpu.SMEM),
                  pl.BlockSpec((tm,), lambda i, lb, npd, nu: (i,), memory_space=pltpu.SMEM),
                  pl.BlockSpec((tm, ROW_SUB, LANES), lambda i, lb, npd, nu: (i, 0, 0))],
        out_specs=pl.BlockSpec(memory_space=pl.ANY),
        scratch_shapes=[pltpu.VMEM((EXP_TB, ROW_SUB, LANES), rows.dtype),
                        pltpu.SemaphoreType.DMA(()), pltpu.SemaphoreType.DMA(())],
    )
    return pl.pallas_call(
        _dispatch_body,
        grid_spec=grid_spec,
        out_shape=jax.ShapeDtypeStruct((n_slots, ROW_SUB, LANES), rows.dtype),
        compiler_params=pltpu.CompilerParams(dimension_semantics=("arbitrary",)),
        name="dispatch",
    )(lastblk, npad, n_used, dest[0], dest[1], rows)


def _expert_body(eid_ref, nused_ref, xb_ref, wg_ref, wu_ref, wd_ref, o_ref, wg_s, wu_s, wd_s):
    i = pl.program_id(0)
    half = D_MODEL // 2

    @pl.when(i < nused_ref[0])
    def _():
        prev = eid_ref[jnp.maximum(i - 1, 0)]

        @pl.when((i == 0) | (eid_ref[i] != prev))
        def _():
            wg_s[...] = wg_ref[0].astype(BF16)
            wu_s[...] = wu_ref[0].astype(BF16)
            wd_s[...] = wd_ref[0].astype(BF16)

        lo, hi = _unpack_bf16_pairs(xb_ref[...].reshape(EXP_TB, half))
        lo, hi = lo.astype(BF16), hi.astype(BF16)

        def proj(w_s):
            return (jnp.dot(lo, w_s[:half, :], preferred_element_type=F32)
                    + jnp.dot(hi, w_s[half:, :], preferred_element_type=F32))

        g = proj(wg_s)
        u = proj(wu_s)
        hmid = (g * jax.nn.sigmoid(g) * u).astype(BF16)
        y = _pack_bf16_pairs(jnp.dot(hmid, wd_s[...], preferred_element_type=F32))
        o_ref[...] = y.reshape(EXP_TB, ROW_SUB, LANES)

    @pl.when(i >= nused_ref[0])
    def _():
        o_ref[...] = jnp.zeros_like(o_ref)


def _experts(block_eid, n_used, xb, w_gate, w_up, w_down):
    n_slots = xb.shape[0]

    def blk(i, e, nu):
        return jnp.minimum(i, nu[0] - 1)

    grid_spec = pltpu.PrefetchScalarGridSpec(
        num_scalar_prefetch=2,
        grid=(n_slots // EXP_TB,),
        in_specs=[pl.BlockSpec((EXP_TB, ROW_SUB, LANES), lambda i, e, nu: (blk(i, e, nu), 0, 0)),
                  pl.BlockSpec((1, D_MODEL, D_EXPERT), lambda i, e, nu: (e[blk(i, e, nu)], 0, 0)),
                  pl.BlockSpec((1, D_MODEL, D_EXPERT), lambda i, e, nu: (e[blk(i, e, nu)], 0, 0)),
                  pl.BlockSpec((1, D_EXPERT, D_MODEL), lambda i, e, nu: (e[blk(i, e, nu)], 0, 0))],
        out_specs=pl.BlockSpec((EXP_TB, ROW_SUB, LANES), lambda i, e, nu: (i, 0, 0)),
        scratch_shapes=[pltpu.VMEM((D_MODEL, D_EXPERT), BF16),
                        pltpu.VMEM((D_MODEL, D_EXPERT), BF16),
                        pltpu.VMEM((D_EXPERT, D_MODEL), BF16)],
    )
    return pl.pallas_call(
        _expert_body,
        grid_spec=grid_spec,
        out_shape=jax.ShapeDtypeStruct((n_slots, ROW_SUB, LANES), jnp.uint32),
        compiler_params=pltpu.CompilerParams(
            dimension_semantics=("arbitrary",), vmem_limit_bytes=48 * 1024 * 1024),
        name="experts",
    )(block_eid, n_used, xb, w_gate, w_up, w_down)


def _combine_body(d0_ref, d1_ref, n0_ref, n1_ref, h1_ref, rg_ref, gfin_ref, yb_ref, o_ref, y_s, sem):
    i = pl.program_id(0)
    tm = h1_ref.shape[0]
    slot = i % 2

    def issue(idx_refs, sl):
        def body(t, carry):
            for k in range(TOP_K):
                pltpu.make_async_copy(yb_ref.at[idx_refs[k][t]], y_s.at[sl, k, t],
                                      sem.at[sl]).start(priority=k)
            return carry

        lax.fori_loop(0, tm, body, 0, unroll=16)

    @pl.when(i == 0)
    def _():
        issue((d0_ref, d1_ref), 0)

    @pl.when(i + 1 < pl.num_programs(0))
    def _():
        issue((n0_ref, n1_ref), 1 - slot)

    _wait_rows(yb_ref, TOP_K * tm, sem.at[slot])
    rg = rg_ref[...]
    g1, g2 = rg[:, 0:1], rg[:, 1:2]
    half = D_MODEL // 2
    lo1, hi1 = _unpack_bf16_pairs(y_s[slot, 0].reshape(tm, half))
    lo2, hi2 = _unpack_bf16_pairs(y_s[slot, 1].reshape(tm, half))
    h2_lo = h1_ref[:, :half] + g1 * lo1 + g2 * lo2
    h2_hi = h1_ref[:, half:] + g1 * hi1 + g2 * hi2
    ms = (jnp.sum(h2_lo * h2_lo, axis=-1, keepdims=True)
          + jnp.sum(h2_hi * h2_hi, axis=-1, keepdims=True)) * (1.0 / D_MODEL)
    inv = lax.rsqrt(ms + EPS)
    o_ref[:, :half] = h2_lo * inv * gfin_ref[:, :half]
    o_ref[:, half:] = h2_hi * inv * gfin_ref[:, half:]


def _combine(dest, h1, rg, g_fin, yb, tm):
    m = h1.shape[0]
    last = m // tm - 1
    return pl.pallas_call(
        _combine_body,
        grid=(m // tm,),
        in_specs=[pl.BlockSpec((tm,), lambda i: (i,), memory_space=pltpu.SMEM),
                  pl.BlockSpec((tm,), lambda i: (i,), memory_space=pltpu.SMEM),
                  pl.BlockSpec((tm,), lambda i: (jnp.minimum(i + 1, last),), memory_space=pltpu.SMEM),
                  pl.BlockSpec((tm,), lambda i: (jnp.minimum(i + 1, last),), memory_space=pltpu.SMEM),
                  pl.BlockSpec((tm, D_MODEL), lambda i: (i, 0)),
                  pl.BlockSpec((tm, LANES), lambda i: (i, 0)),
                  pl.BlockSpec((1, D_MODEL), lambda i: (0, 0)),
                  pl.BlockSpec(memory_space=pl.ANY)],
        out_specs=pl.BlockSpec((tm, D_MODEL), lambda i: (i, 0)),
        out_shape=jax.ShapeDtypeStruct((m, D_MODEL), F32),
        scratch_shapes=[pltpu.VMEM((2, TOP_K, tm, ROW_SUB, LANES), yb.dtype),
                        pltpu.SemaphoreType.DMA((2,))],
        compiler_params=pltpu.CompilerParams(dimension_semantics=("arbitrary",)),
        name="combine",
    )(dest[0], dest[1], dest[0], dest[1], h1, rg, g_fin.reshape(1, -1), yb)


def _moe(h1, a2p, ri, rg, cnt, g_fin, w_gate, w_up, w_down):
    m = h1.shape[0]
    n_blocks = TOP_K * m // EXP_TB + N_EXPERTS
    n_slots = n_blocks * EXP_TB
    counts = cnt[0, N_GROUPS:N_GROUPS + N_EXPERTS].astype(jnp.int32)
    padded = (counts + EXP_TB - 1) // EXP_TB * EXP_TB
    pend = jnp.cumsum(padded)
    base = pend - padded
    rt = ri[:2 * TOP_K]
    sel = rt[None, :TOP_K] == jnp.arange(N_EXPERTS, dtype=jnp.int32)[:, None, None]
    dest = jnp.sum(jnp.where(sel, base[:, None, None], 0), axis=0) + rt[TOP_K:]
    blk_start = jnp.arange(n_blocks, dtype=jnp.int32) * EXP_TB
    block_eid = jnp.minimum(jnp.sum(blk_start[:, None] >= pend[None, :], axis=1),
                            N_EXPERTS - 1).astype(jnp.int32)
    n_used = (pend[-1:] // EXP_TB).astype(jnp.int32)
    lastblk = (pend - EXP_TB).astype(jnp.int32)

    xb = _dispatch(lastblk, padded.astype(jnp.int32), n_used, dest, a2p, n_slots, tm=1024)
    yb = _experts(block_eid, n_used, xb, w_gate, w_up, w_down)
    return _combine(dest, h1, rg, g_fin, yb, tm=512)


ND = 2176
NCIRC = 2 * ND
NH = ND // 2
SEQ = 2048
L_TOT = SEQ + N_META
LH = L_TOT // 2
LHP = 1152
LHR = 1040
HY_CT = 256
HY_FB = NH // 2
HY_IB = ((0, 528), (528, 512))
HY_CB = LH // 3
HALO = 8


def _dft_matrices():
    a = 2 * jnp.arange(NH, dtype=jnp.int32) + 1
    t_hi = jnp.arange(LHP // LANES, dtype=jnp.int32) * LANES
    t_lo = jnp.arange(LANES, dtype=jnp.int32)
    valid = ((t_hi[:, None] + t_lo[None, :]) < LH).reshape(1, LHP)

    def cos_sin(m, denom):
        ang = m.astype(F32) * (math.pi / denom)
        return jnp.cos(ang), jnp.sin(ang)

    c_hi, s_hi = cos_sin((a[:, None] * ((4 * t_hi) % (4 * NCIRC))[None, :]) % (4 * NCIRC), 2 * NCIRC)
    c_hi, s_hi = c_hi[:, :, None], s_hi[:, :, None]
    out = []
    for c in (1, 3):
        c_lo, s_lo = cos_sin((a[:, None] * (4 * t_lo + c)[None, :]) % (4 * NCIRC), 2 * NCIRC)
        c_lo, s_lo = c_lo[:, None, :], s_lo[:, None, :]
        cos_phi = jnp.where(valid, (c_hi * c_lo - s_hi * s_lo).reshape(NH, LHP), 0.0)
        sin_phi = jnp.where(valid, (s_hi * c_lo + c_hi * s_lo).reshape(NH, LHP), 0.0)
        out.append(jnp.concatenate([cos_phi, sin_phi if c == 1 else -sin_phi], axis=0).astype(BF16))
    ce, co = out
    return ce, co, ce.T, co.T


def _forward_half_spectrum(ce_ref, co_ref, xe, xo, j0):
    a1 = jnp.dot(ce_ref[j0:j0 + HY_FB, :], xe, preferred_element_type=F32)
    a2 = jnp.dot(ce_ref[NH + j0:NH + j0 + HY_FB, :], xe, preferred_element_type=F32)
    b1 = jnp.dot(co_ref[j0:j0 + HY_FB, :], xo, preferred_element_type=F32)
    b2 = jnp.dot(co_ref[NH + j0:NH + j0 + HY_FB, :], xo, preferred_element_type=F32)
    return a1 + b1, a2 - b2, a2 + b2, a1 - b1


def _spectra_body(hf_ref, hb_ref, ce_ref, co_ref, rot_ref, hr_ref, hi_ref,
                  stf_s, stb_s, se_s, so_s, de_s, do_s):
    ct = hr_ref.shape[-1]
    nl = ct // LANES
    for j in range(nl):
        stf_s[j] = hf_ref[:, j * LANES:(j + 1) * LANES]
        stb_s[j, 0:L_TOT, :] = hb_ref[:, j * LANES:(j + 1) * LANES]
        stb_s[j, L_TOT:L_TOT + HALO, :] = jnp.zeros((HALO, LANES), F32)

    def lags(st, off, r0):
        return jnp.concatenate(
            [st.at[j][pl.ds(off + 2 * r0, HY_CB, stride=2), :] for j in range(nl)], axis=1)

    for r0 in range(0, LH, HY_CB):
        fe, fo = lags(stf_s, 0, r0), lags(stf_s, 1, r0)
        be, bo = lags(stb_s, 1, r0), lags(stb_s, 2, r0)
        se_s[r0:r0 + HY_CB, :] = (fe + be).astype(BF16)
        so_s[r0:r0 + HY_CB, :] = (fo + bo).astype(BF16)
        de_s[r0:r0 + HY_CB, :] = (fe - be).astype(BF16)
        do_s[r0:r0 + HY_CB, :] = (fo - bo).astype(BF16)
    for dst in (se_s, so_s, de_s, do_s):
        dst[LH:LHP, :] = jnp.zeros((LHP - LH, ct), BF16)
    for j0 in range(0, NH, HY_FB):
        ar1, _, ar2, _ = _forward_half_spectrum(ce_ref, co_ref, se_s[...], so_s[...], j0)
        _, q1, _, q2 = _forward_half_spectrum(ce_ref, co_ref, de_s[...], do_s[...], j0)
        for r0, ar, q in ((j0, ar1, q1), (NH + j0, ar2, q2)):
            c = rot_ref[r0:r0 + HY_FB, 0:1]
            s = rot_ref[r0:r0 + HY_FB, 1:2]
            hr_ref[0, r0:r0 + HY_FB, :] = c * ar + s * q
            hi_ref[0, r0:r0 + HY_FB, :] = s * ar - c * q


def _spectra(h_fwd, h_bwd, ce, co):
    hf2d = h_fwd.reshape(L_TOT, HYENA_ORDER * D_HYENA)
    hb2d = h_bwd.reshape(L_TOT, HYENA_ORDER * D_HYENA)
    j = jnp.arange(NH, dtype=F32)
    k = jnp.concatenate([j, ND - 1 - j])
    half = (2 * k + 1) * (math.pi / (2 * NCIRC))
    scale = 2.0 / NCIRC
    rot = jnp.stack([jnp.cos(half) * scale, jnp.sin(half) * scale], axis=1)
    rot = jnp.pad(rot, ((0, 0), (0, LANES - 2)))
    n_ord, ct = HYENA_ORDER, HY_CT
    nct = D_HYENA // ct
    col = pl.BlockSpec((L_TOT, ct), lambda o, c: (0, o * nct + c))
    mat = pl.BlockSpec((ND, LHP), lambda o, c: (0, 0))
    out = pl.BlockSpec((1, ND, ct), lambda o, c: (o, 0, c))
    half_bf16 = pltpu.VMEM((LHP, ct), BF16)
    return pl.pallas_call(
        _spectra_body,
        grid=(n_ord, nct),
        in_specs=[col, col, mat, mat, pl.BlockSpec((ND, LANES), lambda o, c: (0, 0))],
        out_specs=[out, out],
        out_shape=[jax.ShapeDtypeStruct((n_ord, ND, D_HYENA), F32)] * 2,
        scratch_shapes=[pltpu.VMEM((ct // LANES, L_TOT, LANES), F32),
                        pltpu.VMEM((ct // LANES, L_TOT + HALO, LANES), F32),
                        half_bf16, half_bf16, half_bf16, half_bf16],
        compiler_params=pltpu.CompilerParams(
            dimension_semantics=("arbitrary", "arbitrary"), vmem_limit_bytes=48 * 1024 * 1024),
        name="spectra",
    )(hf2d, hb2d, ce, co, rot)


def _hyena_body(pv_ref, px1_ref, px2_ref, mv_ref, mx1_ref, mx2_ref,
                wv_ref, wx1_ref, wx2_ref, bv_ref, bx1_ref, bx2_ref, skip_ref,
                ce_ref, co_ref, cet_ref, cot_ref, hr_ref, hi_ref, o_ref,
                stage_s, ze_s, zo_s, g1e_s, g1o_s, g2e_s, g2o_s, zbe_s, zbo_s, we_s, wo_s):
    ct = o_ref.shape[-1]
    nl = ct // LANES

    def strided(off, r0, rows):
        return jnp.concatenate(
            [stage_s.at[j][pl.ds(HALO + off + 2 * r0, rows, stride=2), :] for j in range(nl)], axis=1)

    def short_conv(p_ref, m_ref, w_ref, b_ref, dst_e, dst_o):
        for j in range(nl):
            ln = slice(j * LANES, (j + 1) * LANES)
            stage_s[j, 0:HALO, :] = jnp.zeros((HALO, LANES), F32)
            stage_s[j, HALO:HALO + N_META, :] = m_ref[:, ln].astype(F32)
            stage_s[j, HALO + N_META:HALO + L_TOT, :] = p_ref[0, :, ln].astype(F32)
            stage_s[j, HALO + L_TOT:HALO + L_TOT + HALO, :] = jnp.zeros((HALO, LANES), F32)
        w = w_ref[...]
        b = b_ref[...]
        for r0 in range(0, LH, HY_CB):
            sm1, s0, s1, s2 = (strided(off, r0, HY_CB) for off in (-1, 0, 1, 2))
            dst_e[r0:r0 + HY_CB, :] = b + sm1 * w[0:1] + s0 * w[1:2] + s1 * w[2:3]
            dst_o[r0:r0 + HY_CB, :] = b + s0 * w[0:1] + s1 * w[1:2] + s2 * w[2:3]
        dst_e[LH:LHR, :] = jnp.zeros((LHR - LH, ct), F32)
        dst_o[LH:LHR, :] = jnp.zeros((LHR - LH, ct), F32)

    short_conv(pv_ref, mv_ref, wv_ref, bv_ref, ze_s, zo_s)
    short_conv(px1_ref, mx1_ref, wx1_ref, bx1_ref, g1e_s, g1o_s)
    short_conv(px2_ref, mx2_ref, wx2_ref, bx2_ref, g2e_s, g2o_s)
    zbe_s[LH:LHP, :] = jnp.zeros((LHP - LH, ct), BF16)
    zbo_s[LH:LHP, :] = jnp.zeros((LHP - LH, ct), BF16)

    for n, (ge_s, go_s) in enumerate(((g1e_s, g1o_s), (g2e_s, g2o_s))):
        zbe_s[0:LH, :] = ze_s[0:LH, :].astype(BF16)
        zbo_s[0:LH, :] = zo_s[0:LH, :].astype(BF16)
        for j0 in range(0, NH, HY_FB):
            p1, q1, p2, q2 = _forward_half_spectrum(ce_ref, co_ref, zbe_s[...], zbo_s[...], j0)
            h1r = hr_ref[n, j0:j0 + HY_FB, :]
            h1i = hi_ref[n, j0:j0 + HY_FB, :]
            h2r = hr_ref[n, NH + j0:NH + j0 + HY_FB, :]
            h2i = hi_ref[n, NH + j0:NH + j0 + HY_FB, :]
            yr1 = p1 * h1r + q1 * h1i
            ny1 = q1 * h1r - p1 * h1i
            yr2 = p2 * h2r + q2 * h2i
            ny2 = q2 * h2r - p2 * h2i
            we_s[j0:j0 + HY_FB, :] = (yr1 + ny2).astype(BF16)
            we_s[NH + j0:NH + j0 + HY_FB, :] = (yr2 + ny1).astype(BF16)
            wo_s[j0:j0 + HY_FB, :] = (yr1 - ny2).astype(BF16)
            wo_s[NH + j0:NH + j0 + HY_FB, :] = (yr2 - ny1).astype(BF16)
        skip = skip_ref[n:n + 1, :]
        for parity, (ct_ref, w_s, z_s, g_s) in enumerate(((cet_ref, we_s, ze_s, ge_s),
                                                          (cot_ref, wo_s, zo_s, go_s))):
            for r0, rb in HY_IB:
                y = jnp.dot(ct_ref[r0:r0 + rb, :], w_s[...], preferred_element_type=F32)
                znew = g_s[r0:r0 + rb, :] * (y + z_s[r0:r0 + rb, :] * skip)
                if n == 0:
                    z_s[r0:r0 + rb, :] = znew
                else:
                    n_valid = min(r0 + rb, LH) - r0
                    for j in range(nl):
                        stage_s.at[j][pl.ds(HALO + parity + 2 * r0, n_valid, stride=2), :] = (
                            znew[0:n_valid, j * LANES:(j + 1) * LANES])
    for j in range(nl):
        o_ref[0, :, j * LANES:(j + 1) * LANES] = stage_s[j, HALO + N_META:HALO + L_TOT, :]


def _hyena(p_hy, pm_hy, conv_w, conv_b, skip, mats, hr, hi):
    bsz = p_hy.shape[0]
    nct = D_HYENA // HY_CT
    ct = HY_CT
    ce, co, cet, cot = mats

    def part(j):
        return [pl.BlockSpec((1, SEQ, ct), lambda c, b, j=j: (b, 0, j * nct + c))]

    def mpart(j):
        return [pl.BlockSpec((N_META, ct), lambda c, b, j=j: (0, j * nct + c))]

    def wpart(rows, j):
        return [pl.BlockSpec((rows, ct), lambda c, b, j=j: (0, j * nct + c))]

    spec = pl.BlockSpec((HYENA_ORDER, ND, ct), lambda c, b: (0, 0, c), pipeline_mode=pl.Buffered(1))
    in_specs = (part(0) + part(1) + part(2) + mpart(0) + mpart(1) + mpart(2)
                + wpart(3, 0) + wpart(3, 1) + wpart(3, 2)
                + wpart(1, 0) + wpart(1, 1) + wpart(1, 2)
                + [pl.BlockSpec((HYENA_ORDER, ct), lambda c, b: (0, c)),
                   pl.BlockSpec((ND, LHP), lambda c, b: (0, 0)),
                   pl.BlockSpec((ND, LHP), lambda c, b: (0, 0)),
                   pl.BlockSpec((LHP, ND), lambda c, b: (0, 0)),
                   pl.BlockSpec((LHP, ND), lambda c, b: (0, 0)),
                   spec, spec])
    cb = conv_b.reshape(1, -1)
    f32_half = pltpu.VMEM((LHR, ct), F32)
    return pl.pallas_call(
        _hyena_body,
        grid=(nct, bsz),
        in_specs=in_specs,
        out_specs=pl.BlockSpec((1, SEQ, ct), lambda c, b: (b, 0, c)),
        out_shape=jax.ShapeDtypeStruct((bsz, SEQ, D_HYENA), F32),
        scratch_shapes=[pltpu.VMEM((ct // LANES, L_TOT + 2 * HALO, LANES), F32),
                        f32_half, f32_half, f32_half, f32_half, f32_half, f32_half,
                        pltpu.VMEM((LHP, ct), BF16), pltpu.VMEM((LHP, ct), BF16),
                        pltpu.VMEM((ND, ct), BF16), pltpu.VMEM((ND, ct), BF16)],
        compiler_params=pltpu.CompilerParams(
            dimension_semantics=("arbitrary", "arbitrary"), vmem_limit_bytes=60 * 1024 * 1024),
        name="hyena",
    )(p_hy, p_hy, p_hy, pm_hy, pm_hy, pm_hy, conv_w, conv_w, conv_w, cb, cb, cb, skip,
      ce, co, cet, cot, hr, hi)


N_CHUNKS = SEQ // CHUNK
NT_DIMS = (((1,), (1,)), ((), ()))
TN_DIMS = (((0,), (0,)), ((), ()))
MID_F = CHUNK // 2
MID_B = CHUNK // 2 - 1


HG_G = 8
HG_ROWS = HG_G * CHUNK


def _split2(x):
    hi = x.astype(BF16)
    lo = (x - hi.astype(F32)).astype(BF16)
    return hi, lo


def _chunk_prefix_matrix():
    r = lax.broadcasted_iota(jnp.int32, (HG_ROWS, HG_ROWS), 0)
    c = lax.broadcasted_iota(jnp.int32, (HG_ROWS, HG_ROWS), 1)
    return (((r // CHUNK) == (c // CHUNK)) & (c <= r)).astype(BF16)


def _chunk_rows(x, row):
    return jnp.concatenate(
        [jnp.broadcast_to(x[g * CHUNK + row:g * CHUNK + row + 1, :], (CHUNK, x.shape[1]))
         for g in range(x.shape[0] // CHUNK)], axis=0)


def _hgrn_body(q_ref, ff_ref, fb_ref, i_ref, g_ref, mff_ref, mi_ref,
               lbf_ref, lbb_ref, nw_ref, o_ref,
               tri_s, qe_s, sc_s, ut_s, dec_s, st_s):
    hd = HGRN_HEAD_DIM
    row = lax.broadcasted_iota(jnp.int32, (CHUNK, CHUNK), 0)
    col = lax.broadcasted_iota(jnp.int32, (CHUNK, CHUNK), 1)
    lower = row >= col
    upper = col >= row
    lbf = lbf_ref[...]
    lbb = lbb_ref[...]

    @pl.when((pl.program_id(0) == 0) & (pl.program_id(1) == 0))
    def _():
        tri_s[...] = _chunk_prefix_matrix()

    def forget(logit, lb):
        f = lb + (1.0 - lb) * jax.nn.sigmoid(logit)
        return 1.0 - f, jnp.log(f)

    def prefix_sums(lf):
        s = jnp.dot(tri_s[...], jnp.concatenate(_split2(lf), axis=1), preferred_element_type=F32)
        return s[:, :hd] + s[:, hd:]

    k_m, lf_m = forget(mff_ref[...], lbf)
    pad = jnp.zeros((CHUNK - N_META, hd), F32)
    lf_m = jnp.concatenate([pad, lf_m] * HG_G, axis=0)
    b_m = prefix_sums(lf_m)[0:CHUNK]
    kl_m = jnp.concatenate([pad, k_m], axis=0) * jnp.exp(b_m[CHUNK - 1:CHUNK] - b_m)
    v_m = jnp.concatenate([pad, mi_ref[...]], axis=0).astype(BF16)
    st_meta = lax.dot_general(v_m, kl_m.astype(BF16), TN_DIMS, preferred_element_type=F32)

    def phase_a(j, carry):
        r0 = pl.multiple_of(j * HG_ROWS, HG_ROWS)
        rows = pl.ds(r0, HG_ROWS)
        qv = jax.nn.silu(q_ref[0, rows, :])
        vb = i_ref[0, rows, :].astype(BF16)
        k_f, lf_f = forget(ff_ref[0, rows, :], lbf)
        k_b, lf_b = forget(fb_ref[0, rows, :], lbb)
        b_f = prefix_sums(lf_f)
        p_b = prefix_sums(lf_b)
        bmid_f = _chunk_rows(b_f, MID_F)
        blast_f = _chunk_rows(b_f, CHUNK - 1)
        tot_b = _chunk_rows(p_b, CHUNK - 1)
        c_b = tot_b - p_b + lf_b
        cmid_b = _chunk_rows(c_b, MID_B)
        d_f = b_f - bmid_f
        d_b = c_b - cmid_b
        e_f = jnp.exp(d_f)
        e_b = jnp.exp(d_b)
        qs_f = qv * e_f
        ks_f = k_f / e_f
        qs_b = qv * e_b
        ks_b = k_b / e_b
        qs_fb, ks_fb = qs_f.astype(BF16), ks_f.astype(BF16)
        qs_bb, ks_bb = qs_b.astype(BF16), ks_b.astype(BF16)
        for g in range(HG_G):
            sl = slice(g * CHUNK, (g + 1) * CHUNK)
            r1 = slice(g * CHUNK, g * CHUNK + 1)
            n = j * HG_G + g
            rows_g = pl.ds(pl.multiple_of(r0 + g * CHUNK, CHUNK), CHUNK)
            sc_f = lax.dot_general(qs_fb[sl], ks_fb[sl], NT_DIMS, preferred_element_type=F32)
            sc_b = lax.dot_general(qs_bb[sl], ks_bb[sl], NT_DIMS, preferred_element_type=F32)
            sc_s[n] = (jnp.where(lower, sc_f, 0.0) + jnp.where(upper, sc_b, 0.0)).astype(BF16)
            em_f = jnp.exp(bmid_f[r1])
            el_f = jnp.exp(blast_f[r1] - bmid_f[r1])
            em_b = jnp.exp(cmid_b[r1])
            el_b = jnp.exp(tot_b[r1] - cmid_b[r1])
            qe_s[rows_g, :] = jnp.concatenate(
                [qs_f[sl] * em_f, qs_b[sl] * em_b], axis=1).astype(BF16)
            kl = jnp.concatenate([ks_f[sl] * el_f, ks_b[sl] * el_b], axis=1).astype(BF16)
            ut_s[n] = lax.dot_general(vb[sl], kl, TN_DIMS, preferred_element_type=F32)
            dec_s[n] = jnp.concatenate([jnp.exp(blast_f[r1]), jnp.exp(tot_b[r1])], axis=1)
        return carry

    lax.fori_loop(0, N_CHUNKS // HG_G, phase_a, 0, unroll=True)

    st_f = st_meta
    st_b = jnp.zeros((hd, hd), F32)
    for n in range(N_CHUNKS):
        st_s[n, :, 0:hd] = st_f.astype(BF16)
        st_f = dec_s[n, :, 0:hd] * st_f + ut_s[n, :, 0:hd]
        m = N_CHUNKS - 1 - n
        st_s[m, :, hd:2 * hd] = st_b.astype(BF16)
        st_b = dec_s[m, :, hd:2 * hd] * st_b + ut_s[m, :, hd:2 * hd]

    nw = nw_ref[...]

    def phase_c(j, carry):
        r0 = pl.multiple_of(j * HG_ROWS, HG_ROWS)
        rows = pl.ds(r0, HG_ROWS)
        vb = i_ref[0, rows, :].astype(BF16)
        outs = []
        for g in range(HG_G):
            sl = slice(g * CHUNK, (g + 1) * CHUNK)
            n = j * HG_G + g
            rows_g = pl.ds(pl.multiple_of(r0 + g * CHUNK, CHUNK), CHUNK)
            o = jnp.dot(sc_s[n], vb[sl], preferred_element_type=F32)
            outs.append(o + lax.dot_general(qe_s[rows_g, :], st_s[n], NT_DIMS,
                                            preferred_element_type=F32))
        o = jnp.concatenate(outs, axis=0)
        o = o * lax.rsqrt(jnp.mean(o * o, axis=-1, keepdims=True) + EPS)
        o_ref[0, rows, :] = o * nw * jax.nn.silu(g_ref[0, rows, :])
        return carry

    lax.fori_loop(0, N_CHUNKS // HG_G, phase_c, 0, unroll=True)


def _hgrn(phg_x, phg_m, lb_f, lb_b, norm_w):
    bsz = phg_x.shape[0]
    hd = HGRN_HEAD_DIM
    nh = HGRN_HEADS

    def part(j):
        return pl.BlockSpec((1, SEQ, hd), lambda b, h, j=j: (b, 0, j * nh + h))

    def mpart(j):
        return pl.BlockSpec((N_META, hd), lambda b, h, j=j: (0, j * nh + h))

    vec = pl.BlockSpec((1, hd), lambda b, h: (0, h))
    return pl.pallas_call(
        _hgrn_body,
        grid=(bsz, nh),
        in_specs=[part(0), part(1), part(2), part(3), part(4), mpart(1), mpart(3), vec, vec, vec],
        out_specs=pl.BlockSpec((1, SEQ, hd), lambda b, h: (b, 0, h)),
        out_shape=jax.ShapeDtypeStruct((bsz, SEQ, D_HGRN), F32),
        scratch_shapes=[pltpu.VMEM((HG_ROWS, HG_ROWS), BF16),
                        pltpu.VMEM((SEQ, 2 * hd), BF16),
                        pltpu.VMEM((N_CHUNKS, CHUNK, CHUNK), BF16),
                        pltpu.VMEM((N_CHUNKS, hd, 2 * hd), F32),
                        pltpu.VMEM((N_CHUNKS, 1, 2 * hd), F32),
                        pltpu.VMEM((N_CHUNKS, hd, 2 * hd), BF16)],
        compiler_params=pltpu.CompilerParams(
            dimension_semantics=("arbitrary", "arbitrary"), vmem_limit_bytes=40 * 1024 * 1024),
        name="hgrn",
    )(phg_x, phg_x, phg_x, phg_x, phg_x, phg_m, phg_m,
      lb_f.reshape(1, -1), lb_b.reshape(1, -1), norm_w.reshape(1, -1))


def _hyena_filters(L, w1, b1, w2, b2, w3, freq):
    pos = jnp.arange(L, dtype=F32)
    t = pos / max(L - 1, 1)
    bands = jnp.linspace(1e-4, FILTER_BANDS - 1, FILTER_BANDS, dtype=F32)
    ang = (2.0 * math.pi / L) * pos[:, None] * bands[None, :]
    z = jnp.concatenate([t[:, None], jnp.cos(ang), -jnp.sin(ang)], axis=-1)
    hp = lax.Precision.HIGHEST
    hid = jnp.sin(freq * (jnp.dot(z, w1, precision=hp) + b1))
    hid = jnp.sin(freq * (jnp.dot(hid, w2, precision=hp) + b2))
    filt = jnp.dot(hid, w3, precision=hp).reshape(L, 2, HYENA_ORDER, D_HYENA)
    deltas = jnp.abs(jnp.linspace(math.log(DECAY_TARGET) / SLOW_DECAY_PCT,
                                  math.log(DECAY_TARGET) / FAST_DECAY_PCT, D_HYENA, dtype=F32))
    window = jnp.exp(-t[:, None] * deltas[None, :])
    filt = filt * window[:, None, None, :]
    return filt[:, 0], filt[:, 1]


def kernel(x, meta_tokens, w_in, conv_w, conv_b, filt_w1, filt_b1, filt_w2, filt_b2, filt_w3,
           filt_freq, filt_skip, hyena_norm, lb_fwd, lb_bwd, hgrn_norm, w_out, norm_mix, norm_ffn,
           w_router_group, w_router_expert, w_gate, w_up, w_down, norm_final):
    B, S, D = x.shape
    L = S + N_META
    lbf = jnp.cumsum(jax.nn.softmax(lb_fwd, axis=0), axis=0)[0]
    lbb = jnp.cumsum(jax.nn.softmax(lb_bwd, axis=0), axis=0)[0]

    xf = x.reshape(B * S, D)
    phy_x, phg_x = _inproj(xf, norm_mix[0], w_in[0], tm=512)
    phy_m, phg_m = _inproj(meta_tokens, norm_mix[0], w_in[0], tm=N_META)

    mats = _dft_matrices()
    h_fwd, h_bwd = _hyena_filters(L, filt_w1[0], filt_b1[0], filt_w2[0], filt_b2[0], filt_w3[0],
                                  filt_freq[0])
    hr, hi = _spectra(h_fwd, h_bwd, mats[0], mats[1])
    z_hy = _hyena(phy_x.reshape(B, S, D_HYENA_PROJ), phy_m, conv_w[0], conv_b[0], filt_skip[0],
                  mats, hr, hi).reshape(B * S, D_HYENA)

    y_hg = _hgrn(phg_x.reshape(B, S, 5 * D_HGRN), phg_m, lbf, lbb,
                 hgrn_norm[0]).reshape(B * S, D_HGRN)

    w_r = jnp.concatenate([w_router_group[0], w_router_expert[0].reshape(D, N_EXPERTS),
                           jnp.zeros((D, LANES - N_GROUPS - N_EXPERTS), F32)], axis=1).astype(BF16)
    h1, a2p, ri, rg, cnt = _outproj(z_hy, y_hg, xf, hyena_norm[0], norm_ffn[0],
                                    w_out[0].astype(BF16), w_r, tm=1024)
    out = _moe(h1, a2p, ri, rg, cnt, norm_final, w_gate[0], w_up[0], w_down[0])
    return out.reshape(B, S, D)
```
